```python
import math
import jax, jax.numpy as jnp
from jax import lax
import numpy as np

D_MODEL = 2048
BATCH = 4
SEQ = 2048
DEPTH = 1

HEAD_DIM = 128
FOX_HEADS = 8
FOX_WIDTH = FOX_HEADS * HEAD_DIM
NSA_HEADS = 8
NSA_KV_GROUPS = 2
NSA_HPG = NSA_HEADS // NSA_KV_GROUPS
NSA_WIDTH = NSA_HEADS * HEAD_DIM
NSA_KV_WIDTH = NSA_KV_GROUPS * HEAD_DIM
N_NSA_BRANCHES = 3
CMP_LEN = 32
CMP_STRIDE = 16
CMP_HIDDEN = 256
SEL_LEN = 64
SEL_TOPK = 8
WINDOW = 512
REL_BUCKETS = 32
REL_MAX_DIST = 128
Q_BLOCK = 128
DEEPNORM_ALPHA = (2 * DEPTH) ** 0.25
DEEPNORM_BETA = (8 * DEPTH) ** -0.25
LN_EPS = 1e-5
NEG = -1e30

COL_LAYOUT = (
    ("fox_q", FOX_WIDTH, 1.0),
    ("fox_k", FOX_WIDTH, 1.0),
    ("fox_v", FOX_WIDTH, DEEPNORM_BETA),
    ("fox_f", FOX_HEADS, 1.0),
    ("fox_z", FOX_WIDTH, 1.0),
    ("nsa_q", NSA_WIDTH, 1.0),
    ("nsa_k_cmp", NSA_KV_WIDTH, 1.0),
    ("nsa_v_cmp", NSA_KV_WIDTH, DEEPNORM_BETA),
    ("nsa_k_sel", NSA_KV_WIDTH, 1.0),
    ("nsa_v_sel", NSA_KV_WIDTH, DEEPNORM_BETA),
    ("nsa_k_win", NSA_KV_WIDTH, 1.0),
    ("nsa_v_win", NSA_KV_WIDTH, DEEPNORM_BETA),
    ("nsa_gate", NSA_HEADS * N_NSA_BRANCHES, 1.0),
    ("nsa_z", NSA_WIDTH, 1.0),
    ("merge_a", D_MODEL, 1.0),
    ("merge_b", D_MODEL, 1.0),
)
IN_COLS = sum(c[1] for c in COL_LAYOUT)

kernel_name = "fox_nsa_gated_hybrid_deepnorm"


def layer_norm(z, g, b):
    zf = z.astype(jnp.float32)
    mu = jnp.mean(zf, axis=-1, keepdims=True)
    var = jnp.mean(jnp.square(zf - mu), axis=-1, keepdims=True)
    return ((zf - mu) * lax.rsqrt(var + LN_EPS) * g + b).astype(z.dtype)


def rel_bucket(dist):
    n = jnp.maximum(dist, 0)
    exact = REL_BUCKETS // 2
    large = exact + (jnp.log(jnp.maximum(n, 1).astype(jnp.float32) / exact)
                     / math.log(REL_MAX_DIST / exact) * (REL_BUCKETS - exact)).astype(jnp.int32)
    return jnp.where(n < exact, n, jnp.minimum(large, REL_BUCKETS - 1))


def fox_attention(q, k, v, log_f):
    B, S, H, dh = q.shape
    nq = S // Q_BLOCK
    c = jnp.cumsum(log_f, axis=1).transpose(0, 2, 1)
    qb = q.reshape(B, nq, Q_BLOCK, H, dh).transpose(1, 0, 2, 3, 4)
    cb = c.reshape(B, H, nq, Q_BLOCK).transpose(2, 0, 1, 3)
    s_pos = jnp.arange(S)
    scale = dh ** -0.5

    def block(args):
        qc, cc, q0 = args
        t_pos = q0 + jnp.arange(Q_BLOCK)
        s = jnp.einsum('bqhd,bshd->bhqs', qc, k).astype(jnp.float32) * scale
        s = s + cc[..., None] - c[:, :, None, :]
        s = jnp.where(s_pos[None, :] <= t_pos[:, None], s, NEG)
        p = jax.nn.softmax(s, axis=-1).astype(v.dtype)
        return jnp.einsum('bhqs,bshd->bqhd', p, v)

    o = lax.map(block, (qb, cb, jnp.arange(nq) * Q_BLOCK))
    return o.transpose(1, 0, 2, 3, 4).reshape(B, S, H, dh)


def nsa_attention(q, kc_raw, vc_raw, ks, vs, kw, vw, gates,
                  cmp_pos_k, cmp_pos_v, cmp_wk1, cmp_wk2, cmp_wv1, cmp_wv2, rel_bias):
    B, S, H, dh = q.shape
    G, HPG = NSA_KV_GROUPS, NSA_HPG
    scale = dh ** -0.5
    qg = q.reshape(B, S, G, HPG, dh)
    t_pos = jnp.arange(S)

    n_cmp = (S - CMP_LEN) // CMP_STRIDE + 1
    cmp_start = jnp.arange(n_cmp) * CMP_STRIDE
    blk_idx = cmp_start[:, None] + jnp.arange(CMP_LEN)[None, :]

    def compress(raw, pos, w1, w2):
        blocks = raw[:, blk_idx] + pos[None, None, :, None, :]
        flat = blocks.transpose(0, 1, 3, 2, 4).reshape(B, n_cmp, G, CMP_LEN * dh)
        return jax.nn.gelu(flat @ w1) @ w2

    k_cmp = compress(kc_raw, cmp_pos_k, cmp_wk1, cmp_wk2)
    v_cmp = compress(vc_raw, cmp_pos_v, cmp_wv1, cmp_wv2)
    blk_end = cmp_start + CMP_LEN - 1
    cmask = blk_end[None, :] <= t_pos[:, None]
    cbias = rel_bias[rel_bucket(t_pos[:, None] - blk_end[None, :])]
    cbias = cbias.transpose(2, 0, 1).reshape(G, HPG, S, n_cmp)
    sc = jnp.einsum('btghd,bcgd->bghtc', qg, k_cmp).astype(jnp.float32) * scale + cbias
    sc = jnp.where(cmask, sc, NEG)
    p_cmp = jax.nn.softmax(sc, axis=-1) * cmask
    o_cmp = jnp.einsum('bghtc,bcgd->btghd', p_cmp.astype(v_cmp.dtype), v_cmp).reshape(B, S, H, dh)

    n_sel = S // SEL_LEN
    sel_start = jnp.arange(n_sel) * SEL_LEN
    overlap = ((cmp_start[:, None] < sel_start[None, :] + SEL_LEN)
               & (cmp_start[:, None] + CMP_LEN > sel_start[None, :])).astype(jnp.float32)
    imp = jnp.einsum('bgtc,cj->bgtj', jnp.sum(p_cmp, axis=2), overlap)
    cur = t_pos // SEL_LEN
    j = jnp.arange(n_sel)
    forced = (j[None, :] == 0) | (j[None, :] == cur[:, None]) | (j[None, :] == cur[:, None] - 1)
    valid = sel_start[None, :] <= t_pos[:, None]
    imp = jnp.where(valid, jnp.where(forced, -NEG, imp), NEG)
    k_top = min(SEL_TOPK, n_sel)
    _, sel_idx = lax.top_k(imp, k_top)

    nq = S // Q_BLOCK
    qb = qg.reshape(B, nq, Q_BLOCK, G, HPG, dh).transpose(1, 0, 2, 3, 4, 5)
    ib = sel_idx.reshape(B, G, nq, Q_BLOCK, k_top).transpose(2, 0, 1, 3, 4)
    ks_blk = ks.reshape(B, n_sel, SEL_LEN, G, dh).transpose(0, 3, 1, 2, 4)
    vs_blk = vs.reshape(B, n_sel, SEL_LEN, G, dh).transpose(0, 3, 1, 2, 4)
    kw_pad = jnp.pad(kw, ((0, 0), (WINDOW, 0), (0, 0), (0, 0)))
    vw_pad = jnp.pad(vw, ((0, 0), (WINDOW, 0), (0, 0), (0, 0)))
    win_len = WINDOW + Q_BLOCK
    qi = jnp.arange(Q_BLOCK)
    kj = jnp.arange(win_len)
    wdist = WINDOW + qi[:, None] - kj[None, :]
    wband = (wdist >= 0) & (wdist < WINDOW)
    wbias = rel_bias[rel_bucket(wdist)].transpose(2, 0, 1).reshape(G, HPG, Q_BLOCK, win_len)
    tbl = rel_bias.T.reshape(G, HPG, REL_BUCKETS)
    gather_blocks = jax.vmap(jax.vmap(lambda blk, idx: blk[idx]))
    group_bias = jax.vmap(jax.vmap(lambda tg, bk: tg[:, bk]), in_axes=(None, 0))

    def block(args):
        qc, ic, q0 = args
        t = q0 + qi
        kg = gather_blocks(ks_blk, ic)
        vg = gather_blocks(vs_blk, ic)
        spos = ic[..., None] * SEL_LEN + jnp.arange(SEL_LEN)
        sdist = t[None, None, :, None, None] - spos
        ss = (jnp.einsum('bqghd,bgqkld->bghqkl', qc, kg).astype(jnp.float32) * scale
              + group_bias(tbl, rel_bucket(sdist)))
        ss = jnp.where((sdist >= 0)[:, :, None], ss, NEG)
        ps = jax.nn.softmax(ss.reshape(ss.shape[:4] + (-1,)), axis=-1).reshape(ss.shape).astype(vg.dtype)
        o_s = jnp.einsum('bghqkl,bgqkld->bqghd', ps, vg)
        kwc = lax.dynamic_slice_in_dim(kw_pad, q0, win_len, axis=1)
        vwc = lax.dynamic_slice_in_dim(vw_pad, q0, win_len, axis=1)
        sw = jnp.einsum('bqghd,bsgd->bghqs', qc, kwc).astype(jnp.float32) * scale + wbias
        wmask = wband & (q0 - WINDOW + kj >= 0)[None, :]
        sw = jnp.where(wmask, sw, NEG)
        pw = jax.nn.softmax(sw, axis=-1).astype(vwc.dtype)
        o_w = jnp.einsum('bghqs,bsgd->bqghd', pw, vwc)
        return o_s, o_w

    o_sel, o_win = lax.map(block, (qb, ib, jnp.arange(nq) * Q_BLOCK))
    o_sel = o_sel.transpose(1, 0, 2, 3, 4, 5).reshape(B, S, H, dh)
    o_win = o_win.transpose(1, 0, 2, 3, 4, 5).reshape(B, S, H, dh)
    return gates[..., 0:1] * o_cmp + gates[..., 1:2] * o_sel + gates[..., 2:3] * o_win


def hybrid_layer(x, w_in, b_f, cmp_pos_k, cmp_pos_v, cmp_wk1, cmp_wk2, cmp_wv1, cmp_wv2,
                 w_a, w_b, w_o, ln_g, ln_b, rel_bias):
    B, S, _ = x.shape
    h = x @ w_in
    offsets = list(np.cumsum([c[1] for c in COL_LAYOUT])[:-1])
    (fq, fk, fv, ff, fz, nq, nkc, nvc, nks, nvs, nkw, nvw, ng, nz, ga, gb) = jnp.split(h, offsets, axis=-1)
    heads_a = lambda t: t.reshape(B, S, FOX_HEADS, HEAD_DIM)
    kv_b = lambda t: t.reshape(B, S, NSA_KV_GROUPS, HEAD_DIM)

    log_f = jax.nn.log_sigmoid((ff + b_f).astype(jnp.float32))
    o_a = fox_attention(heads_a(fq), heads_a(fk), heads_a(fv), log_f).reshape(B, S, FOX_WIDTH)
    y_a = (o_a * jax.nn.silu(fz)) @ w_a

    gates = jax.nn.sigmoid(ng.reshape(B, S, NSA_HEADS, N_NSA_BRANCHES))
    o_b = nsa_attention(nq.reshape(B, S, NSA_HEADS, HEAD_DIM), kv_b(nkc), kv_b(nvc), kv_b(nks), kv_b(nvs),
                        kv_b(nkw), kv_b(nvw), gates, cmp_pos_k, cmp_pos_v, cmp_wk1, cmp_wk2,
                        cmp_wv1, cmp_wv2, rel_bias).reshape(B, S, NSA_WIDTH)
    y_b = (o_b * jax.nn.silu(nz)) @ w_b

    merged = jax.nn.sigmoid(ga) * y_a + jax.nn.sigmoid(gb) * y_b
    return layer_norm(DEEPNORM_ALPHA * x + merged @ w_o, ln_g, ln_b)


def setup_inputs(seed: int = 0) -> dict:
    key = jax.random.key(seed)
    ks = jax.random.split(key, 16)
    f32 = jnp.float32
    col_scale = jnp.concatenate([jnp.full((c[1],), c[2], f32) for c in COL_LAYOUT])
    kdim = CMP_LEN * HEAD_DIM
    return {
        "x": jax.random.normal(ks[0], (BATCH, SEQ, D_MODEL), f32),
        "w_in": jax.random.normal(ks[1], (DEPTH, D_MODEL, IN_COLS), f32) * D_MODEL ** -0.5 * col_scale,
        "b_f": 3.0 + 0.1 * jax.random.normal(ks[2], (DEPTH, FOX_HEADS), f32),
        "cmp_pos_k": 0.1 * jax.random.normal(ks[3], (DEPTH, CMP_LEN, HEAD_DIM), f32),
        "cmp_pos_v": 0.1 * jax.random.normal(ks[4], (DEPTH, CMP_LEN, HEAD_DIM), f32),
        "cmp_wk1": jax.random.normal(ks[5], (DEPTH, kdim, CMP_HIDDEN), f32) * kdim ** -0.5,
        "cmp_wk2": jax.random.normal(ks[6], (DEPTH, CMP_HIDDEN, HEAD_DIM), f32) * CMP_HIDDEN ** -0.5,
        "cmp_wv1": jax.random.normal(ks[7], (DEPTH, kdim, CMP_HIDDEN), f32) * kdim ** -0.5,
        "cmp_wv2": jax.random.normal(ks[8], (DEPTH, CMP_HIDDEN, HEAD_DIM), f32) * CMP_HIDDEN ** -0.5,
        "w_a": jax.random.normal(ks[9], (DEPTH, FOX_WIDTH, D_MODEL), f32) * FOX_WIDTH ** -0.5 * DEEPNORM_BETA,
        "w_b": jax.random.normal(ks[10], (DEPTH, NSA_WIDTH, D_MODEL), f32) * NSA_WIDTH ** -0.5 * DEEPNORM_BETA,
        "w_o": jax.random.normal(ks[11], (DEPTH, D_MODEL, D_MODEL), f32) * D_MODEL ** -0.5 * DEEPNORM_BETA,
        "ln_g": 1.0 + 0.02 * jax.random.normal(ks[12], (DEPTH, D_MODEL), f32),
        "ln_b": 0.02 * jax.random.normal(ks[13], (DEPTH, D_MODEL), f32),
        "rel_bias": 0.5 * jax.random.normal(ks[14], (REL_BUCKETS, NSA_HEADS), f32),
    }


def reference(x, w_in, b_f, cmp_pos_k, cmp_pos_v, cmp_wk1, cmp_wk2, cmp_wv1, cmp_wv2,
              w_a, w_b, w_o, ln_g, ln_b, rel_bias):
    for layer in range(DEPTH):
        x = hybrid_layer(x, w_in[layer], b_f[layer], cmp_pos_k[layer], cmp_pos_v[layer],
                         cmp_wk1[layer], cmp_wk2[layer], cmp_wv1[layer], cmp_wv2[layer],
                         w_a[layer], w_b[layer], w_o[layer], ln_g[layer], ln_b[layer], rel_bias)
    return x
```

```python
import functools
import math

import jax
import jax.numpy as jnp
import numpy as np
from jax import lax
from jax.experimental import pallas as pl
from jax.experimental.pallas import tpu as pltpu

F32 = jnp.float32
BF16 = jnp.bfloat16

D_MODEL = 2048
HEAD_DIM = 128
N_HEADS = 8
WIDTH = N_HEADS * HEAD_DIM
KV_GROUPS = 2
HEADS_PER_GROUP = N_HEADS // KV_GROUPS
KV_WIDTH = KV_GROUPS * HEAD_DIM
N_BRANCHES = 3
CMP_LEN = 32
CMP_STRIDE = 16
CMP_HIDDEN = 256
SEL_LEN = 64
SEL_TOPK = 8
WINDOW = 512
REL_BUCKETS = 32
REL_MAX_DIST = 128
LN_EPS = 1e-5
NEG = -1e30
SCALE = HEAD_DIM ** -0.5

LANES = 128
VMEM_LIMIT = 56 * 1024 * 1024

_REF_LAYOUT = (
    ("fox_q", WIDTH), ("fox_k", WIDTH), ("fox_v", WIDTH), ("fox_f", N_HEADS), ("fox_z", WIDTH),
    ("nsa_q", WIDTH), ("nsa_k_cmp", KV_WIDTH), ("nsa_v_cmp", KV_WIDTH), ("nsa_k_sel", KV_WIDTH),
    ("nsa_v_sel", KV_WIDTH), ("nsa_k_win", KV_WIDTH), ("nsa_v_win", KV_WIDTH),
    ("nsa_gate", N_HEADS * N_BRANCHES), ("nsa_z", WIDTH), ("merge_a", D_MODEL), ("merge_b", D_MODEL),
)
_REF_OFF = {}
_o = 0
for _n, _w in _REF_LAYOUT:
    _REF_OFF[_n] = (_o, _w)
    _o += _w

_MAIN_ORDER = ("fox_q", "fox_k", "fox_v", "fox_z", "nsa_q", "nsa_z", "merge_a", "merge_b",
               "nsa_k_cmp", "nsa_v_cmp", "nsa_k_sel", "nsa_v_sel", "nsa_k_win", "nsa_v_win")
_MAIN_OFF = {}
_o = 0
for _n in _MAIN_ORDER:
    _MAIN_OFF[_n] = _o
    _o += _REF_OFF[_n][1]
MAIN_COLS = _o
GATE_LANE0 = N_HEADS

TQ = 256
TK = TQ


def _dot(a, b):
    return jnp.dot(a, b, preferred_element_type=F32)


def _dot_nt(a, b):
    return lax.dot_general(a, b, (((1,), (1,)), ((), ())), preferred_element_type=F32)


def _sigmoid(x):
    return 1.0 / (1.0 + jnp.exp(-x))


def _proj_kernel(x_ref, w_ref, ws_ref, o_ref, os_ref, xb_ref):
    @pl.when(pl.program_id(1) == 0)
    def _():
        xb_ref[...] = x_ref[...].astype(BF16)
        os_ref[...] = _dot(xb_ref[...], ws_ref[...])

    o_ref[...] = _dot(xb_ref[...], w_ref[...]).astype(o_ref.dtype)


def _proj(x2, w_main, w_small, tm=1024, tn=512):
    m, k = x2.shape
    n = w_main.shape[1]
    return pl.pallas_call(
        _proj_kernel,
        grid=(m // tm, n // tn),
        in_specs=[
            pl.BlockSpec((tm, k), lambda i, j: (i, 0)),
            pl.BlockSpec((k, tn), lambda i, j: (0, j)),
            pl.BlockSpec((k, LANES), lambda i, j: (0, 0)),
        ],
        out_specs=[
            pl.BlockSpec((tm, tn), lambda i, j: (i, j)),
            pl.BlockSpec((tm, LANES), lambda i, j: (i, 0)),
        ],
        out_shape=[jax.ShapeDtypeStruct((m, n), BF16), jax.ShapeDtypeStruct((m, LANES), F32)],
        scratch_shapes=[pltpu.VMEM((tm, k), BF16)],
        compiler_params=pltpu.CompilerParams(
            dimension_semantics=("parallel", "arbitrary"), vmem_limit_bytes=VMEM_LIMIT),
        name="proj",
    )(x2, w_main, w_small)


_CUM_CHUNK = 256


def _split3(x):
    hi = x.astype(BF16)
    r1 = x - hi.astype(F32)
    mid = r1.astype(BF16)
    lo = (r1 - mid.astype(F32)).astype(BF16)
    return hi, mid, lo


def _gate_kernel(hs_ref, bf_ref, c_ref, g_ref):
    hs = hs_ref[...]
    g_ref[...] = _sigmoid(hs)
    z = hs + bf_ref[...]
    logf = jnp.minimum(z, 0.0) - jnp.log1p(jnp.exp(-jnp.abs(z)))
    n = _CUM_CHUNK
    tri = (lax.broadcasted_iota(jnp.int32, (n, n), 1)
           <= lax.broadcasted_iota(jnp.int32, (n, n), 0)).astype(BF16)
    carry = jnp.zeros((1, LANES), F32)
    for blk in range(hs.shape[0] // n):
        hi, mid, lo = _split3(logf[blk * n:(blk + 1) * n])
        cb = _dot(tri, hi) + _dot(tri, mid) + _dot(tri, lo) + carry
        c_ref[blk * n:(blk + 1) * n, :] = cb
        carry = cb[n - 1:n, :]


def _gates(h_small, bf_row, batch, seq):
    return pl.pallas_call(
        _gate_kernel,
        grid=(batch,),
        in_specs=[pl.BlockSpec((seq, LANES), lambda b: (b, 0)),
                  pl.BlockSpec((1, LANES), lambda b: (0, 0))],
        out_specs=[pl.BlockSpec((seq, LANES), lambda b: (b, 0)),
                   pl.BlockSpec((seq, LANES), lambda b: (b, 0))],
        out_shape=[jax.ShapeDtypeStruct(h_small.shape, F32), jax.ShapeDtypeStruct(h_small.shape, F32)],
        compiler_params=pltpu.CompilerParams(dimension_semantics=("parallel",)),
        name="gates",
    )(h_small, bf_row)


_HALF = CMP_LEN // 2
_CHUNK_COLS = _HALF * 2 * KV_WIDTH


def _gelu_tanh(x):
    return 0.5 * x * (1.0 + jnp.tanh(math.sqrt(2.0 / math.pi) * (x + 0.044715 * (x * x * x))))


def _compress_kernel(r_ref, pk_ref, pv_ref, w1k_ref, w2k_ref, w1v_ref, w2v_ref, kc_ref, vc_ref):
    n_chunks = r_ref.shape[1]
    for kv, (pos_ref, w1_ref, w2_ref, out_ref) in enumerate(
            ((pk_ref, w1k_ref, w2k_ref, kc_ref), (pv_ref, w1v_ref, w2v_ref, vc_ref))):
        for g in range(KV_GROUPS):
            first = jnp.zeros((n_chunks, CMP_HIDDEN), F32)
            second = jnp.zeros((n_chunks, CMP_HIDDEN), F32)
            for l in range(_HALF):
                col = l * 2 * KV_WIDTH + kv * KV_WIDTH + g * HEAD_DIM
                a = r_ref[0, :, col:col + HEAD_DIM].astype(F32)
                a1 = (a + pos_ref[l:l + 1, :]).astype(BF16)
                a2 = (a + pos_ref[_HALF + l:_HALF + l + 1, :]).astype(BF16)
                first += _dot(a1, w1_ref[l * HEAD_DIM:(l + 1) * HEAD_DIM, :])
                second += _dot(a2, w1_ref[(_HALF + l) * HEAD_DIM:(_HALF + l + 1) * HEAD_DIM, :])
            hid = first + pltpu.roll(second, n_chunks - 1, 0)
            out = _dot(_gelu_tanh(hid).astype(BF16), w2_ref[...])
            out_ref[0, :, g * HEAD_DIM:(g + 1) * HEAD_DIM] = out.astype(out_ref.dtype)


def _compress(r, pos_k, pos_v, w1k, w2k, w1v, w2v):
    batch, n_chunks, cols = r.shape
    full = lambda shape: pl.BlockSpec(shape, lambda b: (0,) * len(shape))
    return pl.pallas_call(
        _compress_kernel,
        grid=(batch,),
        in_specs=[pl.BlockSpec((1, n_chunks, cols), lambda b: (b, 0, 0)),
                  full(pos_k.shape), full(pos_v.shape),
                  full(w1k.shape), full(w2k.shape), full(w1v.shape), full(w2v.shape)],
        out_specs=[pl.BlockSpec((1, n_chunks, KV_WIDTH), lambda b: (b, 0, 0)),
                   pl.BlockSpec((1, n_chunks, KV_WIDTH), lambda b: (b, 0, 0))],
        out_shape=[jax.ShapeDtypeStruct((batch, n_chunks, KV_WIDTH), BF16),
                   jax.ShapeDtypeStruct((batch, n_chunks, KV_WIDTH), BF16)],
        compiler_params=pltpu.CompilerParams(
            dimension_semantics=("parallel",), vmem_limit_bytes=VMEM_LIMIT),
        name="compress",
    )(r, pos_k, pos_v, w1k, w2k, w1v, w2v)


def _softmax_init(m_ref, l_ref, acc_ref):
    m_ref[...] = jnp.full(m_ref.shape, NEG, F32)
    l_ref[...] = jnp.zeros(l_ref.shape, F32)
    acc_ref[...] = jnp.zeros(acc_ref.shape, F32)


def _softmax_update(s, v, m_ref, l_ref, acc_ref):
    m_prev = m_ref[...]
    m_new = jnp.maximum(m_prev, jnp.max(s, axis=1, keepdims=True))
    alpha = jnp.exp(m_prev - m_new)
    p = jnp.exp(s - m_new)
    l_ref[...] = alpha * l_ref[...] + jnp.sum(p, axis=1, keepdims=True)
    acc_ref[...] = alpha * acc_ref[...] + _dot(p.astype(BF16), v)
    m_ref[...] = m_new


def _fox_kernel(q_ref, k_ref, v_ref, z_ref, cc_ref, cr_ref, o_ref, m_ref, l_ref, acc_ref):
    i = pl.program_id(1)
    causal = (lax.broadcasted_iota(jnp.int32, (TQ, TK), 1)
              <= lax.broadcasted_iota(jnp.int32, (TQ, TK), 0))
    for h in range(N_HEADS):
        lo = h * HEAD_DIM
        q = q_ref[:, lo:lo + HEAD_DIM]
        ct = cc_ref[:, h:h + 1]
        _softmax_init(m_ref, l_ref, acc_ref)

        def step(j, masked):
            off = pl.multiple_of(j * TK, TK)
            k = k_ref[pl.ds(off, TK), lo:lo + HEAD_DIM]
            v = v_ref[pl.ds(off, TK), lo:lo + HEAD_DIM]
            s = _dot_nt(q, k) * SCALE + (ct - cr_ref[0, j, h:h + 1, :])
            if masked:
                s = jnp.where(causal, s, NEG)
            _softmax_update(s, v, m_ref, l_ref, acc_ref)

        def body(j, carry):
            step(j, False)
            return carry

        lax.fori_loop(0, i, body, 0)
        step(i, True)
        o = acc_ref[...] / l_ref[...]
        z = z_ref[:, lo:lo + HEAD_DIM].astype(F32)
        o_ref[:, lo:lo + HEAD_DIM] = (o * (z * _sigmoid(z))).astype(o_ref.dtype)


def _fox(h_main, c_col, c_row, batch, seq):
    nq = seq // TQ
    blk = lambda name: _MAIN_OFF[name] // WIDTH
    return pl.pallas_call(
        _fox_kernel,
        grid=(batch, nq),
        in_specs=[
            pl.BlockSpec((TQ, WIDTH), lambda b, i: (b * nq + i, blk("fox_q"))),
            pl.BlockSpec((seq, WIDTH), lambda b, i: (b, blk("fox_k"))),
            pl.BlockSpec((seq, WIDTH), lambda b, i: (b, blk("fox_v"))),
            pl.BlockSpec((TQ, WIDTH), lambda b, i: (b * nq + i, blk("fox_z"))),
            pl.BlockSpec((TQ, LANES), lambda b, i: (b * nq + i, 0)),
            pl.BlockSpec((1, seq // TK, N_HEADS, TK), lambda b, i: (b, 0, 0, 0)),
        ],
        out_specs=pl.BlockSpec((TQ, WIDTH), lambda b, i: (b * nq + i, 0)),
        out_shape=jax.ShapeDtypeStruct((batch * seq, WIDTH), BF16),
        scratch_shapes=[pltpu.VMEM((TQ, 1), F32), pltpu.VMEM((TQ, 1), F32),
                        pltpu.VMEM((TQ, HEAD_DIM), F32)],
        compiler_params=pltpu.CompilerParams(
            dimension_semantics=("parallel", "arbitrary"), vmem_limit_bytes=VMEM_LIMIT),
        name="fox",
    )(h_main, h_main, h_main, h_main, c_col, c_row)


N_SEL_ROWS = 32


def _nsa_kernel(q_ref, ks_ref, vs_ref, kw_ref, vw_ref, z_ref, kc_ref, vc_ref, cb_ref, t_ref,
                g_ref, e_ref, ov_ref, o_ref, m_ref, l_ref, acc_ref, mb_ref, oc_ref):
    i = pl.program_id(1)
    nk = mb_ref.shape[0]
    t0 = i * TQ
    cmaskf = (lax.broadcasted_iota(jnp.int32, (TQ, LANES), 1) * CMP_STRIDE + (CMP_LEN - 1)
              <= t0 + lax.broadcasted_iota(jnp.int32, (TQ, LANES), 0)).astype(F32)
    j_t = lax.broadcasted_iota(jnp.int32, (N_SEL_ROWS, TQ), 0)
    t_t = t0 + lax.broadcasted_iota(jnp.int32, (N_SEL_ROWS, TQ), 1)
    cur_t = t_t // SEL_LEN
    forced_t = (j_t == 0) | (j_t == cur_t) | (j_t == cur_t - 1)
    valid_t = j_t * SEL_LEN <= t_t

    for g in range(KV_GROUPS):
        glo = g * HEAD_DIM
        kc = kc_ref[0, :, glo:glo + HEAD_DIM]
        vc = vc_ref[0, :, glo:glo + HEAD_DIM]

        psum = jnp.zeros((TQ, LANES), F32)
        for hh in range(HEADS_PER_GROUP):
            h = g * HEADS_PER_GROUP + hh
            q = q_ref[:, h * HEAD_DIM:(h + 1) * HEAD_DIM]
            sc = _dot_nt(q, kc) * SCALE + cb_ref[h]
            e = jnp.exp(sc - jnp.max(sc, axis=1, keepdims=True))
            p = e / jnp.sum(e, axis=1, keepdims=True) * cmaskf
            oc_ref[:, hh * HEAD_DIM:(hh + 1) * HEAD_DIM] = _dot(p.astype(BF16), vc)
            psum = psum + p

        p_hi = psum.astype(BF16)
        p_lo = (psum - p_hi.astype(F32)).astype(BF16)
        imp_t = _dot_nt(ov_ref[...], p_hi) + _dot_nt(ov_ref[...], p_lo)
        x = jnp.where(valid_t, jnp.where(forced_t, -NEG, imp_t[:N_SEL_ROWS]), NEG)
        cnt = jnp.zeros((N_SEL_ROWS, TQ), F32)
        for jp in range(N_SEL_ROWS):
            row = x[jp:jp + 1, :]
            beats = (row > x) | ((row == x) & (j_t > jp))
            cnt = cnt + jnp.where(beats, 1.0, 0.0)
        sel_t = jnp.where(cnt < SEL_TOPK, 1.0, 0.0)
        sel_t = jnp.concatenate([sel_t, jnp.zeros((LANES - N_SEL_ROWS, TQ), F32)], axis=0)
        sel = sel_t.T.astype(BF16)
        for j in range(nk):
            @pl.when(j <= i)
            def _(j=j):
                hit = _dot(sel, e_ref[:, j * TK:(j + 1) * TK])
                mb_ref[j] = (hit - 1.0) * (-NEG)

        for hh in range(HEADS_PER_GROUP):
            h = g * HEADS_PER_GROUP + hh
            lo = h * HEAD_DIM
            q = q_ref[:, lo:lo + HEAD_DIM]

            _softmax_init(m_ref, l_ref, acc_ref)

            def sel_body(j, carry):
                off = pl.multiple_of(j * TK, TK)
                k = ks_ref[pl.ds(off, TK), glo:glo + HEAD_DIM]
                v = vs_ref[pl.ds(off, TK), glo:glo + HEAD_DIM]
                s = _dot_nt(q, k) * SCALE + t_ref[h, jnp.minimum(i - j, 2)] + mb_ref[j]
                _softmax_update(s, v, m_ref, l_ref, acc_ref)
                return carry

            lax.fori_loop(0, i + 1, sel_body, 0)
            o_sel = acc_ref[...] / l_ref[...]

            _softmax_init(m_ref, l_ref, acc_ref)

            def win_body(dd, carry):
                off = pl.multiple_of((i - dd) * TK, TK)
                k = kw_ref[pl.ds(off, TK), glo:glo + HEAD_DIM]
                v = vw_ref[pl.ds(off, TK), glo:glo + HEAD_DIM]
                s = _dot_nt(q, k) * SCALE + t_ref[h, jnp.where(dd == 2, 3, dd)]
                _softmax_update(s, v, m_ref, l_ref, acc_ref)
                return carry

            lax.fori_loop(0, jnp.minimum(i, 2) + 1, win_body, 0)
            o_win = acc_ref[...] / l_ref[...]

            gl = GATE_LANE0 + h * N_BRANCHES
            ob = (g_ref[:, gl:gl + 1] * oc_ref[:, hh * HEAD_DIM:(hh + 1) * HEAD_DIM]
                  + g_ref[:, gl + 1:gl + 2] * o_sel + g_ref[:, gl + 2:gl + 3] * o_win)
            z = z_ref[:, lo:lo + HEAD_DIM].astype(F32)
            o_ref[:, lo:lo + HEAD_DIM] = (ob * (z * _sigmoid(z))).astype(o_ref.dtype)


def _nsa(h_main, k_cmp, v_cmp, cbias, t_tiles, gates, e_mat, ov_t, batch, seq):
    nq = seq // TQ
    nk = seq // TK
    wblk = lambda name: _MAIN_OFF[name] // WIDTH
    kvblk = lambda name: _MAIN_OFF[name] // KV_WIDTH
    full = lambda shape: pl.BlockSpec(shape, lambda b, i: (0,) * len(shape))
    kv_spec = lambda name: pl.BlockSpec((seq, KV_WIDTH), lambda b, i: (b, kvblk(name)))
    n_chunks = k_cmp.shape[1]
    return pl.pallas_call(
        _nsa_kernel,
        grid=(batch, nq),
        in_specs=[
            pl.BlockSpec((TQ, WIDTH), lambda b, i: (b * nq + i, wblk("nsa_q"))),
            kv_spec("nsa_k_sel"), kv_spec("nsa_v_sel"), kv_spec("nsa_k_win"), kv_spec("nsa_v_win"),
            pl.BlockSpec((TQ, WIDTH), lambda b, i: (b * nq + i, wblk("nsa_z"))),
            pl.BlockSpec((1, n_chunks, KV_WIDTH), lambda b, i: (b, 0, 0)),
            pl.BlockSpec((1, n_chunks, KV_WIDTH), lambda b, i: (b, 0, 0)),
            pl.BlockSpec((N_HEADS, TQ, LANES), lambda b, i: (0, i, 0)),
            full(t_tiles.shape),
            pl.BlockSpec((TQ, LANES), lambda b, i: (b * nq + i, 0)),
            full(e_mat.shape), full(ov_t.shape),
        ],
        out_specs=pl.BlockSpec((TQ, WIDTH), lambda b, i: (b * nq + i, 0)),
        out_shape=jax.ShapeDtypeStruct((batch * seq, WIDTH), BF16),
        scratch_shapes=[pltpu.VMEM((TQ, 1), F32), pltpu.VMEM((TQ, 1), F32),
                        pltpu.VMEM((TQ, HEAD_DIM), F32),
                        pltpu.VMEM((nk, TQ, TK), F32),
                        pltpu.VMEM((TQ, HEADS_PER_GROUP * HEAD_DIM), F32)],
        compiler_params=pltpu.CompilerParams(
            dimension_semantics=("parallel", "arbitrary"), vmem_limit_bytes=VMEM_LIMIT),
        name="nsa",
    )(h_main, h_main, h_main, h_main, h_main, h_main, k_cmp, v_cmp, cbias, t_tiles, gates,
      e_mat, ov_t)


def _out_kernel(alpha, ua_ref, ub_ref, ga_ref, gb_ref, x_ref, wa_ref, wb_ref, wo_ref, lg_ref, lb_ref,
                o_ref):
    ya = _dot(ua_ref[...], wa_ref[...])
    yb = _dot(ub_ref[...], wb_ref[...])
    merged = _sigmoid(ga_ref[...].astype(F32)) * ya + _sigmoid(gb_ref[...].astype(F32)) * yb
    y = alpha * x_ref[...] + _dot(merged.astype(BF16), wo_ref[...])
    mu = jnp.mean(y, axis=-1, keepdims=True)
    d = y - mu
    var = jnp.mean(d * d, axis=-1, keepdims=True)
    o_ref[...] = d * lax.rsqrt(var + LN_EPS) * lg_ref[...] + lb_ref[...]


def _out(u_a, u_b, h_main, x2, w_a, w_b, w_o, ln_g, ln_b, alpha, tm=256):
    m = x2.shape[0]
    mblk = lambda name: _MAIN_OFF[name] // D_MODEL
    const = lambda shape: pl.BlockSpec(shape, lambda i: (0, 0), pipeline_mode=pl.Buffered(1))
    return pl.pallas_call(
        functools.partial(_out_kernel, alpha),
        grid=(m // tm,),
        in_specs=[
            pl.BlockSpec((tm, WIDTH), lambda i: (i, 0)),
            pl.BlockSpec((tm, WIDTH), lambda i: (i, 0)),
            pl.BlockSpec((tm, D_MODEL), lambda i: (i, mblk("merge_a"))),
            pl.BlockSpec((tm, D_MODEL), lambda i: (i, mblk("merge_b"))),
            pl.BlockSpec((tm, D_MODEL), lambda i: (i, 0)),
            const(w_a.shape), const(w_b.shape), const(w_o.shape),
            const(ln_g.shape), const(ln_b.shape),
        ],
        out_specs=pl.BlockSpec((tm, D_MODEL), lambda i: (i, 0)),
        out_shape=jax.ShapeDtypeStruct((m, D_MODEL), F32),
        compiler_params=pltpu.CompilerParams(
            dimension_semantics=("parallel",), vmem_limit_bytes=VMEM_LIMIT),
        name="out",
    )(u_a, u_b, h_main, h_main, x2, w_a, w_b, w_o, ln_g, ln_b)


def _bucket_np(dist):
    n = np.maximum(dist, 0)
    exact = REL_BUCKETS // 2
    large = exact + (np.log(np.maximum(n, 1).astype(np.float32) / exact)
                     / math.log(REL_MAX_DIST / exact) * (REL_BUCKETS - exact)).astype(np.int32)
    return np.where(n < exact, n, np.minimum(large, REL_BUCKETS - 1)).astype(np.int32)


@functools.lru_cache(maxsize=None)
def _static_tables(seq):
    r = np.arange(TQ)[:, None]
    c = np.arange(TK)[None, :]
    tile_idx = np.stack([_bucket_np(r - c), _bucket_np(TQ + r - c), _bucket_np(2 * TQ + r - c),
                         _bucket_np(2 * TQ + r - c)])
    tile_ok = np.stack([c <= r, np.ones((TQ, TK), bool), np.ones((TQ, TK), bool),
                        (2 * TQ + r - c) < WINDOW])
    t = np.arange(seq)[:, None]
    cblk = np.arange(LANES)[None, :]
    blk_end = cblk * CMP_STRIDE + CMP_LEN - 1
    n_cmp = (seq - CMP_LEN) // CMP_STRIDE + 1
    cmp_idx = _bucket_np(t - blk_end)
    cmp_ok = (blk_end <= t) & (cblk < n_cmp)
    e_mat = (np.arange(LANES)[:, None] == (np.arange(seq)[None, :] // SEL_LEN)).astype(np.float32)
    cs = (np.arange(LANES) * CMP_STRIDE)[None, :]
    ss = (np.arange(LANES) * SEL_LEN)[:, None]
    ov_t = ((cs < ss + SEL_LEN) & (cs + CMP_LEN > ss)
            & (np.arange(LANES)[None, :] < n_cmp) & (np.arange(LANES)[:, None] < seq // SEL_LEN))
    return tile_idx, tile_ok, cmp_idx, cmp_ok, e_mat, ov_t.astype(np.float32)


def _layer(x, w_in, b_f, cmp_pos_k, cmp_pos_v, cmp_wk1, cmp_wk2, cmp_wv1, cmp_wv2,
           w_a, w_b, w_o, ln_g, ln_b, rel_bias, alpha):
    batch, seq, d_model = x.shape
    assert d_model == D_MODEL and seq % TQ == 0 and 2 * TQ == WINDOW
    assert seq // SEL_LEN == N_SEL_ROWS and seq // CMP_STRIDE == LANES
    x2 = x.reshape(batch * seq, d_model)

    cols = lambda name: w_in[:, _REF_OFF[name][0]:_REF_OFF[name][0] + _REF_OFF[name][1]]
    w_main = jnp.concatenate([cols(n) for n in _MAIN_ORDER], axis=1).astype(BF16)
    n_small = N_HEADS + N_HEADS * N_BRANCHES
    w_small = jnp.concatenate(
        [cols("fox_f"), cols("nsa_gate"), jnp.zeros((d_model, LANES - n_small), F32)], axis=1).astype(BF16)
    bf_row = jnp.concatenate([b_f.astype(F32), jnp.zeros((LANES - N_HEADS,), F32)]).reshape(1, LANES)

    h_main, h_small = _proj(x2, w_main, w_small)
    c_col, gates = _gates(h_small, bf_row, batch, seq)
    c_row = (c_col[:, :N_HEADS].reshape(batch, seq // TK, TK, N_HEADS).transpose(0, 1, 3, 2))

    u_a = _fox(h_main, c_col, c_row, batch, seq)

    c0 = _MAIN_OFF["nsa_k_cmp"]
    r = h_main[:, c0:c0 + 2 * KV_WIDTH].reshape(batch, seq // _HALF, _CHUNK_COLS)
    k_cmp, v_cmp = _compress(r, cmp_pos_k, cmp_pos_v, cmp_wk1.astype(BF16), cmp_wk2.astype(BF16),
                             cmp_wv1.astype(BF16), cmp_wv2.astype(BF16))

    tile_idx, tile_ok, cmp_idx, cmp_ok, e_mat, ov_t = _static_tables(seq)
    table = rel_bias.T.astype(F32)
    t_tiles = jnp.where(tile_ok[None], jnp.take(table, tile_idx, axis=1), NEG)
    cbias = jnp.where(cmp_ok[None], jnp.take(table, cmp_idx, axis=1), NEG)
    u_b = _nsa(h_main, k_cmp, v_cmp, cbias, t_tiles, gates, jnp.asarray(e_mat, BF16),
               jnp.asarray(ov_t, BF16), batch, seq)

    out = _out(u_a, u_b, h_main, x2, w_a.astype(BF16), w_b.astype(BF16), w_o.astype(BF16),
               ln_g.reshape(1, d_model), ln_b.reshape(1, d_model), alpha)
    return out.reshape(batch, seq, d_model)


def kernel(x, w_in, b_f, cmp_pos_k, cmp_pos_v, cmp_wk1, cmp_wk2, cmp_wv1, cmp_wv2,
           w_a, w_b, w_o, ln_g, ln_b, rel_bias):
    depth = w_in.shape[0]
    alpha = (2 * depth) ** 0.25
    for layer in range(depth):
        x = _layer(x, w_in[layer], b_f[layer], cmp_pos_k[layer], cmp_pos_v[layer], cmp_wk1[layer],
                   cmp_wk2[layer], cmp_wv1[layer], cmp_wv2[layer], w_a[layer], w_b[layer], w_o[layer],
                   ln_g[layer], ln_b[layer], rel_bias, alpha)
    return x
```

```python
import functools
import math

import jax
import jax.numpy as jnp
import numpy as np
from jax import lax
from jax.experimental import pallas as pl
from jax.experimental.pallas import tpu as pltpu

F32 = jnp.float32
BF16 = jnp.bfloat16

D_MODEL = 2048
HEAD_DIM = 128
N_HEADS = 8
WIDTH = N_HEADS * HEAD_DIM
KV_GROUPS = 2
HEADS_PER_GROUP = N_HEADS // KV_GROUPS
KV_WIDTH = KV_GROUPS * HEAD_DIM
N_BRANCHES = 3
CMP_LEN = 32
CMP_STRIDE = 16
CMP_HIDDEN = 256
SEL_LEN = 64
SEL_TOPK = 8
WINDOW = 512
REL_BUCKETS = 32
REL_MAX_DIST = 128
LN_EPS = 1e-5
NEG = -1e30
SCALE = HEAD_DIM ** -0.5

LANES = 128
VMEM_LIMIT = 56 * 1024 * 1024

_REF_LAYOUT = (
    ("fox_q", WIDTH), ("fox_k", WIDTH), ("fox_v", WIDTH), ("fox_f", N_HEADS), ("fox_z", WIDTH),
    ("nsa_q", WIDTH), ("nsa_k_cmp", KV_WIDTH), ("nsa_v_cmp", KV_WIDTH), ("nsa_k_sel", KV_WIDTH),
    ("nsa_v_sel", KV_WIDTH), ("nsa_k_win", KV_WIDTH), ("nsa_v_win", KV_WIDTH),
    ("nsa_gate", N_HEADS * N_BRANCHES), ("nsa_z", WIDTH), ("merge_a", D_MODEL), ("merge_b", D_MODEL),
)
_REF_OFF = {}
_o = 0
for _n, _w in _REF_LAYOUT:
    _REF_OFF[_n] = (_o, _w)
    _o += _w

_MAIN_ORDER = ("fox_q", "fox_k", "fox_v", "fox_z", "nsa_q", "nsa_z", "merge_a", "merge_b",
               "nsa_k_cmp", "nsa_v_cmp", "nsa_k_sel", "nsa_v_sel", "nsa_k_win", "nsa_v_win")
_MAIN_OFF = {}
_o = 0
for _n in _MAIN_ORDER:
    _MAIN_OFF[_n] = _o
    _o += _REF_OFF[_n][1]
MAIN_COLS = _o
GATE_LANE0 = N_HEADS

TQ = 256
TK = TQ


def _dot(a, b):
    return jnp.dot(a, b, preferred_element_type=F32)


def _dot_nt(a, b):
    return lax.dot_general(a, b, (((1,), (1,)), ((), ())), preferred_element_type=F32)


def _sigmoid(x):
    return 1.0 / (1.0 + jnp.exp(-x))


def _proj_kernel(x_ref, w_ref, ws_ref, o_ref, os_ref, xb_ref):
    @pl.when(pl.program_id(1) == 0)
    def _():
        xb_ref[...] = x_ref[...].astype(BF16)
        os_ref[...] = _dot(xb_ref[...], ws_ref[...])

    o_ref[...] = _dot(xb_ref[...], w_ref[...]).astype(o_ref.dtype)


def _proj(x2, w_main, w_small, tm=1024, tn=512):
    m, k = x2.shape
    n = w_main.shape[1]
    return pl.pallas_call(
        _proj_kernel,
        grid=(m // tm, n // tn),
        in_specs=[
            pl.BlockSpec((tm, k), lambda i, j: (i, 0)),
            pl.BlockSpec((k, tn), lambda i, j: (0, j)),
            pl.BlockSpec((k, LANES), lambda i, j: (0, 0)),
        ],
        out_specs=[
            pl.BlockSpec((tm, tn), lambda i, j: (i, j)),
            pl.BlockSpec((tm, LANES), lambda i, j: (i, 0)),
        ],
        out_shape=[jax.ShapeDtypeStruct((m, n), BF16), jax.ShapeDtypeStruct((m, LANES), F32)],
        scratch_shapes=[pltpu.VMEM((tm, k), BF16)],
        compiler_params=pltpu.CompilerParams(
            dimension_semantics=("parallel", "arbitrary"), vmem_limit_bytes=VMEM_LIMIT),
        name="proj",
    )(x2, w_main, w_small)


_CUM_CHUNK = 256


def _split3(x):
    hi = x.astype(BF16)
    r1 = x - hi.astype(F32)
    mid = r1.astype(BF16)
    lo = (r1 - mid.astype(F32)).astype(BF16)
    return hi, mid, lo


def _gate_kernel(hs_ref, bf_ref, c_ref, g_ref):
    hs = hs_ref[...]
    g_ref[...] = _sigmoid(hs)
    z = hs + bf_ref[...]
    logf = jnp.minimum(z, 0.0) - jnp.log1p(jnp.exp(-jnp.abs(z)))
    n = _CUM_CHUNK
    tri = (lax.broadcasted_iota(jnp.int32, (n, n), 1)
           <= lax.broadcasted_iota(jnp.int32, (n, n), 0)).astype(BF16)
    carry = jnp.zeros((1, LANES), F32)
    for blk in range(hs.shape[0] // n):
        hi, mid, lo = _split3(logf[blk * n:(blk + 1) * n])
        cb = _dot(tri, hi) + _dot(tri, mid) + _dot(tri, lo) + carry
        c_ref[blk * n:(blk + 1) * n, :] = cb
        carry = cb[n - 1:n, :]


def _gates(h_small, bf_row, batch, seq):
    return pl.pallas_call(
        _gate_kernel,
        grid=(batch,),
        in_specs=[pl.BlockSpec((seq, LANES), lambda b: (b, 0)),
                  pl.BlockSpec((1, LANES), lambda b: (0, 0))],
        out_specs=[pl.BlockSpec((seq, LANES), lambda b: (b, 0)),
                   pl.BlockSpec((seq, LANES), lambda b: (b, 0))],
        out_shape=[jax.ShapeDtypeStruct(h_small.shape, F32), jax.ShapeDtypeStruct(h_small.shape, F32)],
        compiler_params=pltpu.CompilerParams(dimension_semantics=("parallel",)),
        name="gates",
    )(h_small, bf_row)


_HALF = CMP_LEN // 2
_CHUNK_COLS = _HALF * 2 * KV_WIDTH


def _gelu_tanh(x):
    return 0.5 * x * (1.0 + jnp.tanh(math.sqrt(2.0 / math.pi) * (x + 0.044715 * (x * x * x))))


def _compress_kernel(r_ref, pk_ref, pv_ref, w1k_ref, w2k_ref, w1v_ref, w2v_ref, kc_ref, vc_ref):
    n_chunks = r_ref.shape[1]
    for kv, (pos_ref, w1_ref, w2_ref, out_ref) in enumerate(
            ((pk_ref, w1k_ref, w2k_ref, kc_ref), (pv_ref, w1v_ref, w2v_ref, vc_ref))):
        for g in range(KV_GROUPS):
            first = jnp.zeros((n_chunks, CMP_HIDDEN), F32)
            second = jnp.zeros((n_chunks, CMP_HIDDEN), F32)
            for l in range(_HALF):
                col = l * 2 * KV_WIDTH + kv * KV_WIDTH + g * HEAD_DIM
                a = r_ref[0, :, col:col + HEAD_DIM].astype(F32)
                a1 = (a + pos_ref[l:l + 1, :]).astype(BF16)
                a2 = (a + pos_ref[_HALF + l:_HALF + l + 1, :]).astype(BF16)
                first += _dot(a1, w1_ref[l * HEAD_DIM:(l + 1) * HEAD_DIM, :])
                second += _dot(a2, w1_ref[(_HALF + l) * HEAD_DIM:(_HALF + l + 1) * HEAD_DIM, :])
            hid = first + pltpu.roll(second, n_chunks - 1, 0)
            out = _dot(_gelu_tanh(hid).astype(BF16), w2_ref[...])
            out_ref[0, :, g * HEAD_DIM:(g + 1) * HEAD_DIM] = out.astype(out_ref.dtype)


def _compress(r, pos_k, pos_v, w1k, w2k, w1v, w2v):
    batch, n_chunks, cols = r.shape
    full = lambda shape: pl.BlockSpec(shape, lambda b: (0,) * len(shape))
    return pl.pallas_call(
        _compress_kernel,
        grid=(batch,),
        in_specs=[pl.BlockSpec((1, n_chunks, cols), lambda b: (b, 0, 0)),
                  full(pos_k.shape), full(pos_v.shape),
                  full(w1k.shape), full(w2k.shape), full(w1v.shape), full(w2v.shape)],
        out_specs=[pl.BlockSpec((1, n_chunks, KV_WIDTH), lambda b: (b, 0, 0)),
                   pl.BlockSpec((1, n_chunks, KV_WIDTH), lambda b: (b, 0, 0))],
        out_shape=[jax.ShapeDtypeStruct((batch, n_chunks, KV_WIDTH), BF16),
                   jax.ShapeDtypeStruct((batch, n_chunks, KV_WIDTH), BF16)],
        compiler_params=pltpu.CompilerParams(
            dimension_semantics=("parallel",), vmem_limit_bytes=VMEM_LIMIT),
        name="compress",
    )(r, pos_k, pos_v, w1k, w2k, w1v, w2v)


def _softmax_init(m_ref, l_ref, acc_ref):
    m_ref[...] = jnp.full(m_ref.shape, NEG, F32)
    l_ref[...] = jnp.zeros(l_ref.shape, F32)
    acc_ref[...] = jnp.zeros(acc_ref.shape, F32)


def _softmax_update(s, v, m_ref, l_ref, acc_ref):
    m_prev = m_ref[...]
    m_new = jnp.maximum(m_prev, jnp.max(s, axis=1, keepdims=True))
    alpha = jnp.exp(m_prev - m_new)
    p = jnp.exp(s - m_new)
    l_ref[...] = alpha * l_ref[...] + jnp.sum(p, axis=1, keepdims=True)
    acc_ref[...] = alpha * acc_ref[...] + _dot(p.astype(BF16), v)
    m_ref[...] = m_new


def _fox_kernel(q_ref, k_ref, v_ref, z_ref, cc_ref, cr_ref, o_ref, m_ref, l_ref, acc_ref):
    i = pl.program_id(1)
    causal = (lax.broadcasted_iota(jnp.int32, (TQ, TK), 1)
              <= lax.broadcasted_iota(jnp.int32, (TQ, TK), 0))
    for h in range(N_HEADS):
        lo = h * HEAD_DIM
        q = q_ref[:, lo:lo + HEAD_DIM]
        ct = cc_ref[:, h:h + 1]
        _softmax_init(m_ref, l_ref, acc_ref)

        def step(j, masked):
            off = pl.multiple_of(j * TK, TK)
            k = k_ref[pl.ds(off, TK), lo:lo + HEAD_DIM]
            v = v_ref[pl.ds(off, TK), lo:lo + HEAD_DIM]
            s = _dot_nt(q, k) * SCALE + (ct - cr_ref[0, j, h:h + 1, :])
            if masked:
                s = jnp.where(causal, s, NEG)
            _softmax_update(s, v, m_ref, l_ref, acc_ref)

        def body(j, carry):
            step(j, False)
            return carry

        lax.fori_loop(0, i, body, 0)
        step(i, True)
        o = acc_ref[...] / l_ref[...]
        z = z_ref[:, lo:lo + HEAD_DIM].astype(F32)
        o_ref[:, lo:lo + HEAD_DIM] = (o * (z * _sigmoid(z))).astype(o_ref.dtype)


def _fox(h_main, c_col, c_row, batch, seq):
    nq = seq // TQ
    blk = lambda name: _MAIN_OFF[name] // WIDTH
    return pl.pallas_call(
        _fox_kernel,
        grid=(batch, nq),
        in_specs=[
            pl.BlockSpec((TQ, WIDTH), lambda b, i: (b * nq + i, blk("fox_q"))),
            pl.BlockSpec((seq, WIDTH), lambda b, i: (b, blk("fox_k"))),
            pl.BlockSpec((seq, WIDTH), lambda b, i: (b, blk("fox_v"))),
            pl.BlockSpec((TQ, WIDTH), lambda b, i: (b * nq + i, blk("fox_z"))),
            pl.BlockSpec((TQ, LANES), lambda b, i: (b * nq + i, 0)),
            pl.BlockSpec((1, seq // TK, N_HEADS, TK), lambda b, i: (b, 0, 0, 0)),
        ],
        out_specs=pl.BlockSpec((TQ, WIDTH), lambda b, i: (b * nq + i, 0)),
        out_shape=jax.ShapeDtypeStruct((batch * seq, WIDTH), BF16),
        scratch_shapes=[pltpu.VMEM((TQ, 1), F32), pltpu.VMEM((TQ, 1), F32),
                        pltpu.VMEM((TQ, HEAD_DIM), F32)],
        compiler_params=pltpu.CompilerParams(
            dimension_semantics=("parallel", "arbitrary"), vmem_limit_bytes=VMEM_LIMIT),
        name="fox",
    )(h_main, h_main, h_main, h_main, c_col, c_row)


N_SEL_ROWS = 32
MASK_BUCKET = REL_BUCKETS


def _bias_lookup(tab_ref, h, idx):
    row = jnp.broadcast_to(tab_ref[h:h + 1, :], idx.shape)
    return jnp.take_along_axis(row, idx, axis=1, mode="promise_in_bounds")


def _nsa_kernel(q_ref, ks_ref, vs_ref, kw_ref, vw_ref, z_ref, kc_ref, vc_ref, cidx_ref, tidx_ref,
                tab_ref, g_ref, e_ref, ov_ref, o_ref, m_ref, l_ref, acc_ref, mb_ref, oc_ref, t_ref):
    i = pl.program_id(1)
    nk = mb_ref.shape[0]
    t0 = i * TQ

    @pl.when((pl.program_id(0) == 0) & (i == 0))
    def _():
        for h in range(N_HEADS):
            for d in range(tidx_ref.shape[0]):
                for half in range(TK // LANES):
                    cs = slice(half * LANES, (half + 1) * LANES)
                    t_ref[h, d, :, cs] = _bias_lookup(tab_ref, h, tidx_ref[d, :, cs])

    cmaskf = (lax.broadcasted_iota(jnp.int32, (TQ, LANES), 1) * CMP_STRIDE + (CMP_LEN - 1)
              <= t0 + lax.broadcasted_iota(jnp.int32, (TQ, LANES), 0)).astype(F32)
    j_t = lax.broadcasted_iota(jnp.int32, (N_SEL_ROWS, TQ), 0)
    t_t = t0 + lax.broadcasted_iota(jnp.int32, (N_SEL_ROWS, TQ), 1)
    cur_t = t_t // SEL_LEN
    forced_t = (j_t == 0) | (j_t == cur_t) | (j_t == cur_t - 1)
    valid_t = j_t * SEL_LEN <= t_t

    for g in range(KV_GROUPS):
        glo = g * HEAD_DIM
        kc = kc_ref[0, :, glo:glo + HEAD_DIM]
        vc = vc_ref[0, :, glo:glo + HEAD_DIM]

        psum = jnp.zeros((TQ, LANES), F32)
        for hh in range(HEADS_PER_GROUP):
            h = g * HEADS_PER_GROUP + hh
            q = q_ref[:, h * HEAD_DIM:(h + 1) * HEAD_DIM]
            sc = _dot_nt(q, kc) * SCALE + _bias_lookup(tab_ref, h, cidx_ref[...])
            e = jnp.exp(sc - jnp.max(sc, axis=1, keepdims=True))
            p = e / jnp.sum(e, axis=1, keepdims=True) * cmaskf
            oc_ref[:, hh * HEAD_DIM:(hh + 1) * HEAD_DIM] = _dot(p.astype(BF16), vc)
            psum = psum + p

        p_hi = psum.astype(BF16)
        p_lo = (psum - p_hi.astype(F32)).astype(BF16)
        imp_t = _dot_nt(ov_ref[...], p_hi) + _dot_nt(ov_ref[...], p_lo)
        x = jnp.where(valid_t, jnp.where(forced_t, -NEG, imp_t[:N_SEL_ROWS]), NEG)
        cnt = jnp.zeros((N_SEL_ROWS, TQ), F32)
        for jp in range(N_SEL_ROWS):
            row = x[jp:jp + 1, :]
            beats = (row > x) | ((row == x) & (j_t > jp))
            cnt = cnt + jnp.where(beats, 1.0, 0.0)
        sel_t = jnp.where(cnt < SEL_TOPK, 1.0, 0.0)
        sel_t = jnp.concatenate([sel_t, jnp.zeros((LANES - N_SEL_ROWS, TQ), F32)], axis=0)
        sel = sel_t.T.astype(BF16)
        for j in range(nk):
            @pl.when(j <= i)
            def _(j=j):
                hit = _dot(sel, e_ref[:, j * TK:(j + 1) * TK])
                mb_ref[j] = (hit - 1.0) * (-NEG)

        for hh in range(HEADS_PER_GROUP):
            h = g * HEADS_PER_GROUP + hh
            lo = h * HEAD_DIM
            q = q_ref[:, lo:lo + HEAD_DIM]

            _softmax_init(m_ref, l_ref, acc_ref)

            def sel_body(j, carry):
                off = pl.multiple_of(j * TK, TK)
                k = ks_ref[pl.ds(off, TK), glo:glo + HEAD_DIM]
                v = vs_ref[pl.ds(off, TK), glo:glo + HEAD_DIM]
                s = _dot_nt(q, k) * SCALE + t_ref[h, jnp.minimum(i - j, 2)] + mb_ref[j]
                _softmax_update(s, v, m_ref, l_ref, acc_ref)
                return carry

            lax.fori_loop(0, i + 1, sel_body, 0)
            o_sel = acc_ref[...] / l_ref[...]

            _softmax_init(m_ref, l_ref, acc_ref)

            def win_body(dd, carry):
                off = pl.multiple_of((i - dd) * TK, TK)
                k = kw_ref[pl.ds(off, TK), glo:glo + HEAD_DIM]
                v = vw_ref[pl.ds(off, TK), glo:glo + HEAD_DIM]
                s = _dot_nt(q, k) * SCALE + t_ref[h, jnp.where(dd == 2, 3, dd)]
                _softmax_update(s, v, m_ref, l_ref, acc_ref)
                return carry

            lax.fori_loop(0, jnp.minimum(i, 2) + 1, win_body, 0)
            o_win = acc_ref[...] / l_ref[...]

            gl = GATE_LANE0 + h * N_BRANCHES
            ob = (g_ref[:, gl:gl + 1] * oc_ref[:, hh * HEAD_DIM:(hh + 1) * HEAD_DIM]
                  + g_ref[:, gl + 1:gl + 2] * o_sel + g_ref[:, gl + 2:gl + 3] * o_win)
            z = z_ref[:, lo:lo + HEAD_DIM].astype(F32)
            o_ref[:, lo:lo + HEAD_DIM] = (ob * (z * _sigmoid(z))).astype(o_ref.dtype)


def _nsa(h_main, k_cmp, v_cmp, cmp_idx, tile_idx, table, gates, e_mat, ov_t, batch, seq):
    nq = seq // TQ
    nk = seq // TK
    wblk = lambda name: _MAIN_OFF[name] // WIDTH
    kvblk = lambda name: _MAIN_OFF[name] // KV_WIDTH
    full = lambda shape: pl.BlockSpec(shape, lambda b, i: (0,) * len(shape))
    kv_spec = lambda name: pl.BlockSpec((seq, KV_WIDTH), lambda b, i: (b, kvblk(name)))
    n_chunks = k_cmp.shape[1]
    return pl.pallas_call(
        _nsa_kernel,
        grid=(batch, nq),
        in_specs=[
            pl.BlockSpec((TQ, WIDTH), lambda b, i: (b * nq + i, wblk("nsa_q"))),
            kv_spec("nsa_k_sel"), kv_spec("nsa_v_sel"), kv_spec("nsa_k_win"), kv_spec("nsa_v_win"),
            pl.BlockSpec((TQ, WIDTH), lambda b, i: (b * nq + i, wblk("nsa_z"))),
            pl.BlockSpec((1, n_chunks, KV_WIDTH), lambda b, i: (b, 0, 0)),
            pl.BlockSpec((1, n_chunks, KV_WIDTH), lambda b, i: (b, 0, 0)),
            pl.BlockSpec((TQ, LANES), lambda b, i: (i, 0)),
            full(tile_idx.shape), full(table.shape),
            pl.BlockSpec((TQ, LANES), lambda b, i: (b * nq + i, 0)),
            full(e_mat.shape), full(ov_t.shape),
        ],
        out_specs=pl.BlockSpec((TQ, WIDTH), lambda b, i: (b * nq + i, 0)),
        out_shape=jax.ShapeDtypeStruct((batch * seq, WIDTH), BF16),
        scratch_shapes=[pltpu.VMEM((TQ, 1), F32), pltpu.VMEM((TQ, 1), F32),
                        pltpu.VMEM((TQ, HEAD_DIM), F32),
                        pltpu.VMEM((nk, TQ, TK), F32),
                        pltpu.VMEM((TQ, HEADS_PER_GROUP * HEAD_DIM), F32),
                        pltpu.VMEM((N_HEADS,) + tile_idx.shape, F32)],
        compiler_params=pltpu.CompilerParams(
            dimension_semantics=("arbitrary", "arbitrary"), vmem_limit_bytes=VMEM_LIMIT),
        name="nsa",
    )(h_main, h_main, h_main, h_main, h_main, h_main, k_cmp, v_cmp, cmp_idx, tile_idx, table, gates,
      e_mat, ov_t)


def _out_kernel(alpha, ua_ref, ub_ref, ga_ref, gb_ref, x_ref, wa_ref, wb_ref, wo_ref, lg_ref, lb_ref,
                o_ref):
    ya = _dot(ua_ref[...], wa_ref[...])
    yb = _dot(ub_ref[...], wb_ref[...])
    merged = _sigmoid(ga_ref[...].astype(F32)) * ya + _sigmoid(gb_ref[...].astype(F32)) * yb
    y = alpha * x_ref[...] + _dot(merged.astype(BF16), wo_ref[...])
    mu = jnp.mean(y, axis=-1, keepdims=True)
    d = y - mu
    var = jnp.mean(d * d, axis=-1, keepdims=True)
    o_ref[...] = d * lax.rsqrt(var + LN_EPS) * lg_ref[...] + lb_ref[...]


def _out(u_a, u_b, h_main, x2, w_a, w_b, w_o, ln_g, ln_b, alpha, tm=256):
    m = x2.shape[0]
    mblk = lambda name: _MAIN_OFF[name] // D_MODEL
    const = lambda shape: pl.BlockSpec(shape, lambda i: (0, 0), pipeline_mode=pl.Buffered(1))
    return pl.pallas_call(
        functools.partial(_out_kernel, alpha),
        grid=(m // tm,),
        in_specs=[
            pl.BlockSpec((tm, WIDTH), lambda i: (i, 0)),
            pl.BlockSpec((tm, WIDTH), lambda i: (i, 0)),
            pl.BlockSpec((tm, D_MODEL), lambda i: (i, mblk("merge_a"))),
            pl.BlockSpec((tm, D_MODEL), lambda i: (i, mblk("merge_b"))),
            pl.BlockSpec((tm, D_MODEL), lambda i: (i, 0)),
            const(w_a.shape), const(w_b.shape), const(w_o.shape),
            const(ln_g.shape), const(ln_b.shape),
        ],
        out_specs=pl.BlockSpec((tm, D_MODEL), lambda i: (i, 0)),
        out_shape=jax.ShapeDtypeStruct((m, D_MODEL), F32),
        compiler_params=pltpu.CompilerParams(
            dimension_semantics=("parallel",), vmem_limit_bytes=VMEM_LIMIT),
        name="out",
    )(u_a, u_b, h_main, h_main, x2, w_a, w_b, w_o, ln_g, ln_b)


def _bucket_np(dist):
    n = np.maximum(dist, 0)
    exact = REL_BUCKETS // 2
    large = exact + (np.log(np.maximum(n, 1).astype(np.float32) / exact)
                     / math.log(REL_MAX_DIST / exact) * (REL_BUCKETS - exact)).astype(np.int32)
    return np.where(n < exact, n, np.minimum(large, REL_BUCKETS - 1)).astype(np.int32)


@functools.lru_cache(maxsize=None)
def _static_tables(seq):
    r = np.arange(TQ)[:, None]
    c = np.arange(TK)[None, :]
    tile_idx = np.stack([_bucket_np(r - c), _bucket_np(TQ + r - c), _bucket_np(2 * TQ + r - c),
                         _bucket_np(2 * TQ + r - c)])
    tile_ok = np.stack([c <= r, np.ones((TQ, TK), bool), np.ones((TQ, TK), bool),
                        (2 * TQ + r - c) < WINDOW])
    tile_idx = np.where(tile_ok, tile_idx, MASK_BUCKET).astype(np.int32)
    t = np.arange(seq)[:, None]
    cblk = np.arange(LANES)[None, :]
    blk_end = cblk * CMP_STRIDE + CMP_LEN - 1
    n_cmp = (seq - CMP_LEN) // CMP_STRIDE + 1
    cmp_idx = np.where((blk_end <= t) & (cblk < n_cmp), _bucket_np(t - blk_end), MASK_BUCKET).astype(np.int32)
    e_mat = (np.arange(LANES)[:, None] == (np.arange(seq)[None, :] // SEL_LEN)).astype(np.float32)
    cs = (np.arange(LANES) * CMP_STRIDE)[None, :]
    ss = (np.arange(LANES) * SEL_LEN)[:, None]
    ov_t = ((cs < ss + SEL_LEN) & (cs + CMP_LEN > ss)
            & (np.arange(LANES)[None, :] < n_cmp) & (np.arange(LANES)[:, None] < seq // SEL_LEN))
    return tile_idx, cmp_idx, e_mat, ov_t.astype(np.float32)


def _layer(x, w_in, b_f, cmp_pos_k, cmp_pos_v, cmp_wk1, cmp_wk2, cmp_wv1, cmp_wv2,
           w_a, w_b, w_o, ln_g, ln_b, rel_bias, alpha):
    batch, seq, d_model = x.shape
    assert d_model == D_MODEL and seq % TQ == 0 and 2 * TQ == WINDOW
    assert seq // SEL_LEN == N_SEL_ROWS and seq // CMP_STRIDE == LANES
    x2 = x.reshape(batch * seq, d_model)

    cols = lambda name: w_in[:, _REF_OFF[name][0]:_REF_OFF[name][0] + _REF_OFF[name][1]]
    w_main = jnp.concatenate([cols(n) for n in _MAIN_ORDER], axis=1).astype(BF16)
    n_small = N_HEADS + N_HEADS * N_BRANCHES
    w_small = jnp.concatenate(
        [cols("fox_f"), cols("nsa_gate"), jnp.zeros((d_model, LANES - n_small), F32)], axis=1).astype(BF16)
    bf_row = jnp.concatenate([b_f.astype(F32), jnp.zeros((LANES - N_HEADS,), F32)]).reshape(1, LANES)

    h_main, h_small = _proj(x2, w_main, w_small)
    c_col, gates = _gates(h_small, bf_row, batch, seq)
    c_row = (c_col[:, :N_HEADS].reshape(batch, seq // TK, TK, N_HEADS).transpose(0, 1, 3, 2))

    u_a = _fox(h_main, c_col, c_row, batch, seq)

    c0 = _MAIN_OFF["nsa_k_cmp"]
    r = h_main[:, c0:c0 + 2 * KV_WIDTH].reshape(batch, seq // _HALF, _CHUNK_COLS)
    k_cmp, v_cmp = _compress(r, cmp_pos_k, cmp_pos_v, cmp_wk1.astype(BF16), cmp_wk2.astype(BF16),
                             cmp_wv1.astype(BF16), cmp_wv2.astype(BF16))

    tile_idx, cmp_idx, e_mat, ov_t = _static_tables(seq)
    table = jnp.concatenate(
        [rel_bias.T.astype(F32), jnp.full((N_HEADS, 1), NEG, F32),
         jnp.zeros((N_HEADS, LANES - REL_BUCKETS - 1), F32)], axis=1)
    u_b = _nsa(h_main, k_cmp, v_cmp, jnp.asarray(cmp_idx), jnp.asarray(tile_idx), table, gates,
               jnp.asarray(e_mat, BF16), jnp.asarray(ov_t, BF16), batch, seq)

    out = _out(u_a, u_b, h_main, x2, w_a.astype(BF16), w_b.astype(BF16), w_o.astype(BF16),
               ln_g.reshape(1, d_model), ln_b.reshape(1, d_model), alpha)
    return out.reshape(batch, seq, d_model)


def kernel(x, w_in, b_f, cmp_pos_k, cmp_pos_v, cmp_wk1, cmp_wk2, cmp_wv1, cmp_wv2,
           w_a, w_b, w_o, ln_g, ln_b, rel_bias):
    depth = w_in.shape[0]
    alpha = (2 * depth) ** 0.25
    for layer in range(depth):
        x = _layer(x, w_in[layer], b_f[layer], cmp_pos_k[layer], cmp_pos_v[layer], cmp_wk1[layer],
                   cmp_wk2[layer], cmp_wv1[layer], cmp_wv2[layer], w_a[layer], w_b[layer], w_o[layer],
                   ln_g[layer], ln_b[layer], rel_bias, alpha)
    return x
```

```python
import functools
import math

import jax
import jax.numpy as jnp
import numpy as np
from jax import lax
from jax.experimental import pallas as pl
from jax.experimental.pallas import tpu as pltpu

F32 = jnp.float32
BF16 = jnp.bfloat16

D_MODEL = 2048
HEAD_DIM = 128
N_HEADS = 8
WIDTH = N_HEADS * HEAD_DIM
KV_GROUPS = 2
HEADS_PER_GROUP = N_HEADS // KV_GROUPS
KV_WIDTH = KV_GROUPS * HEAD_DIM
N_BRANCHES = 3
CMP_LEN = 32
CMP_STRIDE = 16
CMP_HIDDEN = 256
SEL_LEN = 64
SEL_TOPK = 8
WINDOW = 512
REL_BUCKETS = 32
REL_MAX_DIST = 128
LN_EPS = 1e-5
NEG = -1e30
LOG2E = math.log2(math.e)
Q_PRESCALE = HEAD_DIM ** -0.5 * LOG2E

LANES = 128
VMEM_LIMIT = 56 * 1024 * 1024

_REF_LAYOUT = (
    ("fox_q", WIDTH), ("fox_k", WIDTH), ("fox_v", WIDTH), ("fox_f", N_HEADS), ("fox_z", WIDTH),
    ("nsa_q", WIDTH), ("nsa_k_cmp", KV_WIDTH), ("nsa_v_cmp", KV_WIDTH), ("nsa_k_sel", KV_WIDTH),
    ("nsa_v_sel", KV_WIDTH), ("nsa_k_win", KV_WIDTH), ("nsa_v_win", KV_WIDTH),
    ("nsa_gate", N_HEADS * N_BRANCHES), ("nsa_z", WIDTH), ("merge_a", D_MODEL), ("merge_b", D_MODEL),
)
_REF_OFF = {}
_o = 0
for _n, _w in _REF_LAYOUT:
    _REF_OFF[_n] = (_o, _w)
    _o += _w

_MAIN_ORDER = ("fox_q", "fox_k", "fox_v", "fox_z", "nsa_q", "nsa_z", "merge_a", "merge_b",
               "nsa_k_cmp", "nsa_v_cmp", "nsa_k_sel", "nsa_v_sel", "nsa_k_win", "nsa_v_win")
_QUERY_COLS = ("fox_q", "nsa_q")
_MAIN_OFF = {}
_o = 0
for _n in _MAIN_ORDER:
    _MAIN_OFF[_n] = _o
    _o += _REF_OFF[_n][1]
MAIN_COLS = _o
GATE_LANE0 = N_HEADS

TQ = 256
TK = TQ
FOX_HEAD_GROUP = 4


def _dot(a, b):
    return jnp.dot(a, b, preferred_element_type=F32)


def _dot_nt(a, b):
    return lax.dot_general(a, b, (((1,), (1,)), ((), ())), preferred_element_type=F32)


def _sigmoid(x):
    return 1.0 / (1.0 + jnp.exp(-x))


def _proj_kernel(x_ref, w_ref, ws_ref, o_ref, os_ref, xb_ref):
    @pl.when(pl.program_id(1) == 0)
    def _():
        xb_ref[...] = x_ref[...].astype(BF16)
        os_ref[...] = _dot(xb_ref[...], ws_ref[...])

    o_ref[...] = _dot(xb_ref[...], w_ref[...]).astype(o_ref.dtype)


def _proj(x2, w_main, w_small, tm=1024, tn=512):
    m, k = x2.shape
    n = w_main.shape[1]
    return pl.pallas_call(
        _proj_kernel,
        grid=(m // tm, n // tn),
        in_specs=[
            pl.BlockSpec((tm, k), lambda i, j: (i, 0)),
            pl.BlockSpec((k, tn), lambda i, j: (0, j)),
            pl.BlockSpec((k, LANES), lambda i, j: (0, 0)),
        ],
        out_specs=[
            pl.BlockSpec((tm, tn), lambda i, j: (i, j)),
            pl.BlockSpec((tm, LANES), lambda i, j: (i, 0)),
        ],
        out_shape=[jax.ShapeDtypeStruct((m, n), BF16), jax.ShapeDtypeStruct((m, LANES), F32)],
        scratch_shapes=[pltpu.VMEM((tm, k), BF16)],
        compiler_params=pltpu.CompilerParams(
            dimension_semantics=("parallel", "arbitrary"), vmem_limit_bytes=VMEM_LIMIT),
        name="proj",
    )(x2, w_main, w_small)


_CUM_CHUNK = 256


def _split3(x):
    hi = x.astype(BF16)
    r1 = x - hi.astype(F32)
    mid = r1.astype(BF16)
    lo = (r1 - mid.astype(F32)).astype(BF16)
    return hi, mid, lo


def _gate_kernel(hs_ref, bf_ref, c_ref, g_ref):
    hs = hs_ref[...]
    g_ref[...] = _sigmoid(hs)
    z = hs + bf_ref[...]
    logf = jnp.minimum(z, 0.0) - jnp.log1p(jnp.exp(-jnp.abs(z)))
    n = _CUM_CHUNK
    tri = (lax.broadcasted_iota(jnp.int32, (n, n), 1)
           <= lax.broadcasted_iota(jnp.int32, (n, n), 0)).astype(BF16)
    carry = jnp.zeros((1, LANES), F32)
    for blk in range(hs.shape[0] // n):
        hi, mid, lo = _split3(logf[blk * n:(blk + 1) * n])
        cb = _dot(tri, hi) + _dot(tri, mid) + _dot(tri, lo) + carry
        c_ref[blk * n:(blk + 1) * n, :] = cb * LOG2E
        carry = cb[n - 1:n, :]


def _gates(h_small, bf_row, batch, seq):
    return pl.pallas_call(
        _gate_kernel,
        grid=(batch,),
        in_specs=[pl.BlockSpec((seq, LANES), lambda b: (b, 0)),
                  pl.BlockSpec((1, LANES), lambda b: (0, 0))],
        out_specs=[pl.BlockSpec((seq, LANES), lambda b: (b, 0)),
                   pl.BlockSpec((seq, LANES), lambda b: (b, 0))],
        out_shape=[jax.ShapeDtypeStruct(h_small.shape, F32), jax.ShapeDtypeStruct(h_small.shape, F32)],
        compiler_params=pltpu.CompilerParams(dimension_semantics=("parallel",)),
        name="gates",
    )(h_small, bf_row)


_HALF = CMP_LEN // 2
_CHUNK_COLS = _HALF * 2 * KV_WIDTH


def _gelu_tanh(x):
    return 0.5 * x * (1.0 + jnp.tanh(math.sqrt(2.0 / math.pi) * (x + 0.044715 * (x * x * x))))


def _compress_kernel(r_ref, pk_ref, pv_ref, w1k_ref, w2k_ref, w1v_ref, w2v_ref, kc_ref, vc_ref):
    n_chunks = r_ref.shape[1]
    for kv, (pos_ref, w1_ref, w2_ref, out_ref) in enumerate(
            ((pk_ref, w1k_ref, w2k_ref, kc_ref), (pv_ref, w1v_ref, w2v_ref, vc_ref))):
        for g in range(KV_GROUPS):
            first = jnp.zeros((n_chunks, CMP_HIDDEN), F32)
            second = jnp.zeros((n_chunks, CMP_HIDDEN), F32)
            for l in range(_HALF):
                col = l * 2 * KV_WIDTH + kv * KV_WIDTH + g * HEAD_DIM
                a = r_ref[0, :, col:col + HEAD_DIM].astype(F32)
                a1 = (a + pos_ref[l:l + 1, :]).astype(BF16)
                a2 = (a + pos_ref[_HALF + l:_HALF + l + 1, :]).astype(BF16)
                first += _dot(a1, w1_ref[l * HEAD_DIM:(l + 1) * HEAD_DIM, :])
                second += _dot(a2, w1_ref[(_HALF + l) * HEAD_DIM:(_HALF + l + 1) * HEAD_DIM, :])
            hid = first + pltpu.roll(second, n_chunks - 1, 0)
            out = _dot(_gelu_tanh(hid).astype(BF16), w2_ref[...])
            out_ref[0, :, g * HEAD_DIM:(g + 1) * HEAD_DIM] = out.astype(out_ref.dtype)


def _compress(r, pos_k, pos_v, w1k, w2k, w1v, w2v):
    batch, n_chunks, cols = r.shape
    full = lambda shape: pl.BlockSpec(shape, lambda b: (0,) * len(shape))
    return pl.pallas_call(
        _compress_kernel,
        grid=(batch,),
        in_specs=[pl.BlockSpec((1, n_chunks, cols), lambda b: (b, 0, 0)),
                  full(pos_k.shape), full(pos_v.shape),
                  full(w1k.shape), full(w2k.shape), full(w1v.shape), full(w2v.shape)],
        out_specs=[pl.BlockSpec((1, n_chunks, KV_WIDTH), lambda b: (b, 0, 0)),
                   pl.BlockSpec((1, n_chunks, KV_WIDTH), lambda b: (b, 0, 0))],
        out_shape=[jax.ShapeDtypeStruct((batch, n_chunks, KV_WIDTH), BF16),
                   jax.ShapeDtypeStruct((batch, n_chunks, KV_WIDTH), BF16)],
        compiler_params=pltpu.CompilerParams(
            dimension_semantics=("parallel",), vmem_limit_bytes=VMEM_LIMIT),
        name="compress",
    )(r, pos_k, pos_v, w1k, w2k, w1v, w2v)


def _softmax_init(m_ref, l_ref, acc_ref):
    m_ref[...] = jnp.full(m_ref.shape, NEG, F32)
    l_ref[...] = jnp.zeros(l_ref.shape, F32)
    acc_ref[...] = jnp.zeros(acc_ref.shape, F32)


def _lane_tile(x, n):
    return x if n == 1 else jnp.concatenate([x] * n, axis=1)


def _softmax_update(s, v, row_const, m_ref, l_ref, acc_ref, rows=slice(None)):
    n = s.shape[1] // LANES
    m_prev = m_ref[rows, :]
    m_cur = jnp.max(s, axis=1, keepdims=True)
    if row_const is not None:
        m_cur = m_cur + row_const
    m_new = jnp.maximum(m_prev, m_cur)
    alpha = jnp.exp2(m_prev - m_new)
    shift = m_new if row_const is None else m_new - row_const
    p = jnp.exp2(s - _lane_tile(shift, n))
    p_lanes = p[:, :LANES]
    for t in range(1, n):
        p_lanes = p_lanes + p[:, t * LANES:(t + 1) * LANES]
    l_ref[rows, :] = alpha * l_ref[rows, :] + p_lanes
    acc_ref[rows, :] = alpha * acc_ref[rows, :] + _dot(p.astype(BF16), v)
    m_ref[rows, :] = m_new


def _softmax_finish(l_ref, acc_ref, rows=slice(None)):
    return acc_ref[rows, :] / jnp.sum(l_ref[rows, :], axis=1, keepdims=True)


def _fox_kernel(q_ref, k_ref, v_ref, z_ref, cc_ref, cr_ref, o_ref, ct_ref, m_ref, l_ref, acc_ref):
    i = pl.program_id(1)
    causal = (lax.broadcasted_iota(jnp.int32, (TQ, TK), 1)
              <= lax.broadcasted_iota(jnp.int32, (TQ, TK), 0))
    for hg in range(N_HEADS // FOX_HEAD_GROUP):
        heads = tuple(range(hg * FOX_HEAD_GROUP, (hg + 1) * FOX_HEAD_GROUP))
        _softmax_init(m_ref, l_ref, acc_ref)
        for n, h in enumerate(heads):
            ct_ref[n * TQ:(n + 1) * TQ, :] = jnp.broadcast_to(cc_ref[:, h:h + 1], (TQ, LANES))

        def step(j, masked):
            off = pl.multiple_of(j * TK, TK)
            for n, h in enumerate(heads):
                lo = h * HEAD_DIM
                rows = slice(n * TQ, (n + 1) * TQ)
                k = k_ref[pl.ds(off, TK), lo:lo + HEAD_DIM]
                v = v_ref[pl.ds(off, TK), lo:lo + HEAD_DIM]
                s = _dot_nt(q_ref[:, lo:lo + HEAD_DIM], k) - cr_ref[0, j, h:h + 1, :]
                if masked:
                    s = jnp.where(causal, s, NEG)
                _softmax_update(s, v, ct_ref[rows, :], m_ref, l_ref, acc_ref, rows)

        def body(j, carry):
            step(j, False)
            return carry

        lax.fori_loop(0, i, body, 0)
        step(i, True)
        for n, h in enumerate(heads):
            lo = h * HEAD_DIM
            o = _softmax_finish(l_ref, acc_ref, slice(n * TQ, (n + 1) * TQ))
            z = z_ref[:, lo:lo + HEAD_DIM].astype(F32)
            o_ref[:, lo:lo + HEAD_DIM] = (o * (z * _sigmoid(z))).astype(o_ref.dtype)


def _fox(h_main, c_col, c_row, batch, seq):
    nq = seq // TQ
    blk = lambda name: _MAIN_OFF[name] // WIDTH
    stat = pltpu.VMEM((FOX_HEAD_GROUP * TQ, LANES), F32)
    return pl.pallas_call(
        _fox_kernel,
        grid=(batch, nq),
        in_specs=[
            pl.BlockSpec((TQ, WIDTH), lambda b, i: (b * nq + i, blk("fox_q"))),
            pl.BlockSpec((seq, WIDTH), lambda b, i: (b, blk("fox_k"))),
            pl.BlockSpec((seq, WIDTH), lambda b, i: (b, blk("fox_v"))),
            pl.BlockSpec((TQ, WIDTH), lambda b, i: (b * nq + i, blk("fox_z"))),
            pl.BlockSpec((TQ, LANES), lambda b, i: (b * nq + i, 0)),
            pl.BlockSpec((1, seq // TK, N_HEADS, TK), lambda b, i: (b, 0, 0, 0)),
        ],
        out_specs=pl.BlockSpec((TQ, WIDTH), lambda b, i: (b * nq + i, 0)),
        out_shape=jax.ShapeDtypeStruct((batch * seq, WIDTH), BF16),
        scratch_shapes=[stat, stat, stat, stat],
        compiler_params=pltpu.CompilerParams(
            dimension_semantics=("parallel", "arbitrary"), vmem_limit_bytes=VMEM_LIMIT),
        name="fox",
    )(h_main, h_main, h_main, h_main, c_col, c_row)


N_SEL_ROWS = 32
MASK_BUCKET = REL_BUCKETS
GROUP_ROWS = HEADS_PER_GROUP * TQ
TILE_DIAG, TILE_PREV, TILE_WIN2 = 0, 1, 2


def _bias_lookup(tab_ref, h, idx):
    row = jnp.broadcast_to(tab_ref[h:h + 1, :], idx.shape)
    return jnp.take_along_axis(row, idx, axis=1, mode="promise_in_bounds")


def _nsa_kernel(q_ref, ks_ref, vs_ref, kw_ref, vw_ref, z_ref, kc_ref, vc_ref, cidx_ref, tidx_ref,
                tab_ref, g_ref, e_ref, ov_ref, o_ref,
                qg_ref, m_ref, l_ref, acc_ref, mb_ref, oc_ref, os_ref, t_ref, far_ref):
    i = pl.program_id(1)
    nk = mb_ref.shape[0]
    t0 = i * TQ

    @pl.when((pl.program_id(0) == 0) & (i == 0))
    def _():
        for h in range(N_HEADS):
            for d in range(tidx_ref.shape[0]):
                for half in range(TK // LANES):
                    cs = slice(half * LANES, (half + 1) * LANES)
                    t_ref[h, d, :, cs] = _bias_lookup(tab_ref, h, tidx_ref[d, :, cs])
            far_ref[h * TQ:(h + 1) * TQ, :] = jnp.broadcast_to(
                tab_ref[h:h + 1, REL_BUCKETS - 1:REL_BUCKETS], (TQ, LANES))

    cmaskf = (lax.broadcasted_iota(jnp.int32, (TQ, LANES), 1) * CMP_STRIDE + (CMP_LEN - 1)
              <= t0 + lax.broadcasted_iota(jnp.int32, (TQ, LANES), 0)).astype(F32)
    j_t = lax.broadcasted_iota(jnp.int32, (N_SEL_ROWS, TQ), 0)
    t_t = t0 + lax.broadcasted_iota(jnp.int32, (N_SEL_ROWS, TQ), 1)
    cur_t = t_t // SEL_LEN
    forced_t = (j_t == 0) | (j_t == cur_t) | (j_t == cur_t - 1)
    valid_t = j_t * SEL_LEN <= t_t

    for g in range(KV_GROUPS):
        glo = g * HEAD_DIM
        heads = tuple(range(g * HEADS_PER_GROUP, (g + 1) * HEADS_PER_GROUP))
        grows = slice(g * GROUP_ROWS, (g + 1) * GROUP_ROWS)
        for n, h in enumerate(heads):
            qg_ref[n * TQ:(n + 1) * TQ, :] = q_ref[:, h * HEAD_DIM:(h + 1) * HEAD_DIM]

        cidx = cidx_ref[...]
        cbias = jnp.concatenate([_bias_lookup(tab_ref, h, cidx) for h in heads], axis=0)
        sc = _dot_nt(qg_ref[...], kc_ref[0, :, glo:glo + HEAD_DIM]) + cbias
        e = jnp.exp2(sc - jnp.max(sc, axis=1, keepdims=True))
        p = e / jnp.sum(e, axis=1, keepdims=True) * jnp.concatenate([cmaskf] * HEADS_PER_GROUP, axis=0)
        oc_ref[...] = _dot(p.astype(BF16), vc_ref[0, :, glo:glo + HEAD_DIM])
        psum = p[:TQ]
        for n in range(1, HEADS_PER_GROUP):
            psum = psum + p[n * TQ:(n + 1) * TQ]

        p_hi = psum.astype(BF16)
        p_lo = (psum - p_hi.astype(F32)).astype(BF16)
        imp_t = _dot_nt(ov_ref[...], p_hi) + _dot_nt(ov_ref[...], p_lo)
        x = jnp.where(valid_t, jnp.where(forced_t, -NEG, imp_t[:N_SEL_ROWS]), NEG)
        cnt = jnp.zeros((N_SEL_ROWS, TQ), F32)
        for jp in range(N_SEL_ROWS):
            row = x[jp:jp + 1, :]
            beats = (row > x) | ((row == x) & (j_t > jp))
            cnt = cnt + jnp.where(beats, 1.0, 0.0)
        unsel_t = jnp.where(cnt < SEL_TOPK, 0.0, 1.0)
        unsel_t = jnp.concatenate([unsel_t, jnp.zeros((LANES - N_SEL_ROWS, TQ), F32)], axis=0)
        unsel = unsel_t.T.astype(BF16)
        for j in range(nk):
            @pl.when(j <= i)
            def _(j=j):
                mb_ref[j] = _dot(unsel, e_ref[:, j * TK:(j + 1) * TK])

        def tile_bias(d, extra=None):
            tiles = [t_ref[h, d] if extra is None else t_ref[h, d] + extra for h in heads]
            return jnp.concatenate(tiles, axis=0)

        _softmax_init(m_ref, l_ref, acc_ref)

        def sel_far(j, carry):
            off = pl.multiple_of(j * TK, TK)
            s = (_dot_nt(qg_ref[...], ks_ref[pl.ds(off, TK), glo:glo + HEAD_DIM])
                 + jnp.concatenate([mb_ref[j]] * HEADS_PER_GROUP, axis=0))
            _softmax_update(s, vs_ref[pl.ds(off, TK), glo:glo + HEAD_DIM], far_ref[grows, :],
                            m_ref, l_ref, acc_ref)
            return carry

        lax.fori_loop(0, i - 1, sel_far, 0)

        def sel_near(j, d):
            off = pl.multiple_of(j * TK, TK)
            s = (_dot_nt(qg_ref[...], ks_ref[pl.ds(off, TK), glo:glo + HEAD_DIM])
                 + tile_bias(d, mb_ref[j]))
            _softmax_update(s, vs_ref[pl.ds(off, TK), glo:glo + HEAD_DIM], None, m_ref, l_ref, acc_ref)

        @pl.when(i >= 1)
        def _():
            sel_near(i - 1, TILE_PREV)

        sel_near(i, TILE_DIAG)
        os_ref[...] = _softmax_finish(l_ref, acc_ref)

        _softmax_init(m_ref, l_ref, acc_ref)

        def win_step(j, d):
            off = pl.multiple_of(j * TK, TK)
            s = _dot_nt(qg_ref[...], kw_ref[pl.ds(off, TK), glo:glo + HEAD_DIM]) + tile_bias(d)
            _softmax_update(s, vw_ref[pl.ds(off, TK), glo:glo + HEAD_DIM], None, m_ref, l_ref, acc_ref)

        win_step(i, TILE_DIAG)

        @pl.when(i >= 1)
        def _():
            win_step(i - 1, TILE_PREV)

        @pl.when(i >= 2)
        def _():
            win_step(i - 2, TILE_WIN2)

        o_win = _softmax_finish(l_ref, acc_ref)

        for n, h in enumerate(heads):
            rows = slice(n * TQ, (n + 1) * TQ)
            lo = h * HEAD_DIM
            gl = GATE_LANE0 + h * N_BRANCHES
            ob = (g_ref[:, gl:gl + 1] * oc_ref[rows, :] + g_ref[:, gl + 1:gl + 2] * os_ref[rows, :]
                  + g_ref[:, gl + 2:gl + 3] * o_win[rows])
            z = z_ref[:, lo:lo + HEAD_DIM].astype(F32)
            o_ref[:, lo:lo + HEAD_DIM] = (ob * (z * _sigmoid(z))).astype(o_ref.dtype)


def _nsa(h_main, k_cmp, v_cmp, cmp_idx, tile_idx, table, gates, e_mat, ov_t, batch, seq):
    nq = seq // TQ
    nk = seq // TK
    wblk = lambda name: _MAIN_OFF[name] // WIDTH
    kvblk = lambda name: _MAIN_OFF[name] // KV_WIDTH
    full = lambda shape: pl.BlockSpec(shape, lambda b, i: (0,) * len(shape))
    kv_spec = lambda name: pl.BlockSpec((seq, KV_WIDTH), lambda b, i: (b, kvblk(name)))
    n_chunks = k_cmp.shape[1]
    stat = pltpu.VMEM((GROUP_ROWS, LANES), F32)
    return pl.pallas_call(
        _nsa_kernel,
        grid=(batch, nq),
        in_specs=[
            pl.BlockSpec((TQ, WIDTH), lambda b, i: (b * nq + i, wblk("nsa_q"))),
            kv_spec("nsa_k_sel"), kv_spec("nsa_v_sel"), kv_spec("nsa_k_win"), kv_spec("nsa_v_win"),
            pl.BlockSpec((TQ, WIDTH), lambda b, i: (b * nq + i, wblk("nsa_z"))),
            pl.BlockSpec((1, n_chunks, KV_WIDTH), lambda b, i: (b, 0, 0)),
            pl.BlockSpec((1, n_chunks, KV_WIDTH), lambda b, i: (b, 0, 0)),
            pl.BlockSpec((TQ, LANES), lambda b, i: (i, 0)),
            full(tile_idx.shape), full(table.shape),
            pl.BlockSpec((TQ, LANES), lambda b, i: (b * nq + i, 0)),
            full(e_mat.shape), full(ov_t.shape),
        ],
        out_specs=pl.BlockSpec((TQ, WIDTH), lambda b, i: (b * nq + i, 0)),
        out_shape=jax.ShapeDtypeStruct((batch * seq, WIDTH), BF16),
        scratch_shapes=[pltpu.VMEM((GROUP_ROWS, HEAD_DIM), BF16),
                        stat, stat, stat,
                        pltpu.VMEM((nk, TQ, TK), F32),
                        stat, stat,
                        pltpu.VMEM((N_HEADS,) + tile_idx.shape, F32),
                        pltpu.VMEM((N_HEADS * TQ, LANES), F32)],
        compiler_params=pltpu.CompilerParams(
            dimension_semantics=("arbitrary", "arbitrary"), vmem_limit_bytes=VMEM_LIMIT),
        name="nsa",
    )(h_main, h_main, h_main, h_main, h_main, h_main, k_cmp, v_cmp, cmp_idx, tile_idx, table, gates,
      e_mat, ov_t)


def _out_kernel(alpha, ua_ref, ub_ref, ga_ref, gb_ref, x_ref, wa_ref, wb_ref, wo_ref, lg_ref, lb_ref,
                o_ref):
    ya = _dot(ua_ref[...], wa_ref[...])
    yb = _dot(ub_ref[...], wb_ref[...])
    merged = _sigmoid(ga_ref[...].astype(F32)) * ya + _sigmoid(gb_ref[...].astype(F32)) * yb
    y = alpha * x_ref[...] + _dot(merged.astype(BF16), wo_ref[...])
    mu = jnp.mean(y, axis=-1, keepdims=True)
    d = y - mu
    var = jnp.mean(d * d, axis=-1, keepdims=True)
    o_ref[...] = d * lax.rsqrt(var + LN_EPS) * lg_ref[...] + lb_ref[...]


def _out(u_a, u_b, h_main, x2, w_a, w_b, w_o, ln_g, ln_b, alpha, tm=256):
    m = x2.shape[0]
    mblk = lambda name: _MAIN_OFF[name] // D_MODEL
    const = lambda shape: pl.BlockSpec(shape, lambda i: (0, 0), pipeline_mode=pl.Buffered(1))
    return pl.pallas_call(
        functools.partial(_out_kernel, alpha),
        grid=(m // tm,),
        in_specs=[
            pl.BlockSpec((tm, WIDTH), lambda i: (i, 0)),
            pl.BlockSpec((tm, WIDTH), lambda i: (i, 0)),
            pl.BlockSpec((tm, D_MODEL), lambda i: (i, mblk("merge_a"))),
            pl.BlockSpec((tm, D_MODEL), lambda i: (i, mblk("merge_b"))),
            pl.BlockSpec((tm, D_MODEL), lambda i: (i, 0)),
            const(w_a.shape), const(w_b.shape), const(w_o.shape),
            const(ln_g.shape), const(ln_b.shape),
        ],
        out_specs=pl.BlockSpec((tm, D_MODEL), lambda i: (i, 0)),
        out_shape=jax.ShapeDtypeStruct((m, D_MODEL), F32),
        compiler_params=pltpu.CompilerParams(
            dimension_semantics=("parallel",), vmem_limit_bytes=VMEM_LIMIT),
        name="out",
    )(u_a, u_b, h_main, h_main, x2, w_a, w_b, w_o, ln_g, ln_b)


def _bucket_np(dist):
    n = np.maximum(dist, 0)
    exact = REL_BUCKETS // 2
    large = exact + (np.log(np.maximum(n, 1).astype(np.float32) / exact)
                     / math.log(REL_MAX_DIST / exact) * (REL_BUCKETS - exact)).astype(np.int32)
    return np.where(n < exact, n, np.minimum(large, REL_BUCKETS - 1)).astype(np.int32)


@functools.lru_cache(maxsize=None)
def _static_tables(seq):
    r = np.arange(TQ)[:, None]
    c = np.arange(TK)[None, :]
    tile_idx = np.stack([_bucket_np(r - c), _bucket_np(TQ + r - c), _bucket_np(2 * TQ + r - c)])
    tile_ok = np.stack([c <= r, np.ones((TQ, TK), bool), (2 * TQ + r - c) < WINDOW])
    tile_idx = np.where(tile_ok, tile_idx, MASK_BUCKET).astype(np.int32)
    t = np.arange(seq)[:, None]
    cblk = np.arange(LANES)[None, :]
    blk_end = cblk * CMP_STRIDE + CMP_LEN - 1
    n_cmp = (seq - CMP_LEN) // CMP_STRIDE + 1
    cmp_idx = np.where((blk_end <= t) & (cblk < n_cmp), _bucket_np(t - blk_end), MASK_BUCKET).astype(np.int32)
    e_mat = np.where(np.arange(LANES)[:, None] == (np.arange(seq)[None, :] // SEL_LEN), NEG, 0.0)
    cs = (np.arange(LANES) * CMP_STRIDE)[None, :]
    ss = (np.arange(LANES) * SEL_LEN)[:, None]
    ov_t = ((cs < ss + SEL_LEN) & (cs + CMP_LEN > ss)
            & (np.arange(LANES)[None, :] < n_cmp) & (np.arange(LANES)[:, None] < seq // SEL_LEN))
    return tile_idx, cmp_idx, e_mat.astype(np.float32), ov_t.astype(np.float32)


def _layer(x, w_in, b_f, cmp_pos_k, cmp_pos_v, cmp_wk1, cmp_wk2, cmp_wv1, cmp_wv2,
           w_a, w_b, w_o, ln_g, ln_b, rel_bias, alpha):
    batch, seq, d_model = x.shape
    assert d_model == D_MODEL and seq % TQ == 0 and 2 * TQ == WINDOW and REL_MAX_DIST <= TQ
    assert seq // SEL_LEN == N_SEL_ROWS and seq // CMP_STRIDE == LANES
    x2 = x.reshape(batch * seq, d_model)

    def cols(name):
        w = w_in[:, _REF_OFF[name][0]:_REF_OFF[name][0] + _REF_OFF[name][1]]
        return w * Q_PRESCALE if name in _QUERY_COLS else w

    w_main = jnp.concatenate([cols(n) for n in _MAIN_ORDER], axis=1).astype(BF16)
    n_small = N_HEADS + N_HEADS * N_BRANCHES
    w_small = jnp.concatenate(
        [cols("fox_f"), cols("nsa_gate"), jnp.zeros((d_model, LANES - n_small), F32)], axis=1).astype(BF16)
    bf_row = jnp.concatenate([b_f.astype(F32), jnp.zeros((LANES - N_HEADS,), F32)]).reshape(1, LANES)

    h_main, h_small = _proj(x2, w_main, w_small)
    c_col, gates = _gates(h_small, bf_row, batch, seq)
    c_row = (c_col[:, :N_HEADS].reshape(batch, seq // TK, TK, N_HEADS).transpose(0, 1, 3, 2))

    u_a = _fox(h_main, c_col, c_row, batch, seq)

    c0 = _MAIN_OFF["nsa_k_cmp"]
    r = h_main[:, c0:c0 + 2 * KV_WIDTH].reshape(batch, seq // _HALF, _CHUNK_COLS)
    k_cmp, v_cmp = _compress(r, cmp_pos_k, cmp_pos_v, cmp_wk1.astype(BF16), cmp_wk2.astype(BF16),
                             cmp_wv1.astype(BF16), cmp_wv2.astype(BF16))

    tile_idx, cmp_idx, e_mat, ov_t = _static_tables(seq)
    table = jnp.concatenate(
        [rel_bias.T.astype(F32) * LOG2E, jnp.full((N_HEADS, 1), NEG, F32),
         jnp.zeros((N_HEADS, LANES - REL_BUCKETS - 1), F32)], axis=1)
    u_b = _nsa(h_main, k_cmp, v_cmp, jnp.asarray(cmp_idx), jnp.asarray(tile_idx), table, gates,
               jnp.asarray(e_mat, BF16), jnp.asarray(ov_t, BF16), batch, seq)

    out = _out(u_a, u_b, h_main, x2, w_a.astype(BF16), w_b.astype(BF16), w_o.astype(BF16),
               ln_g.reshape(1, d_model), ln_b.reshape(1, d_model), alpha)
    return out.reshape(batch, seq, d_model)


def kernel(x, w_in, b_f, cmp_pos_k, cmp_pos_v, cmp_wk1, cmp_wk2, cmp_wv1, cmp_wv2,
           w_a, w_b, w_o, ln_g, ln_b, rel_bias):
    depth = w_in.shape[0]
    alpha = (2 * depth) ** 0.25
    for layer in range(depth):
        x = _layer(x, w_in[layer], b_f[layer], cmp_pos_k[layer], cmp_pos_v[layer], cmp_wk1[layer],
                   cmp_wk2[layer], cmp_wv1[layer], cmp_wv2[layer], w_a[layer], w_b[layer], w_o[layer],
                   ln_g[layer], ln_b[layer], rel_bias, alpha)
    return x
```

```python
import functools
import math

import jax
import jax.numpy as jnp
import numpy as np
from jax import lax
from jax.experimental import pallas as pl
from jax.experimental.pallas import tpu as pltpu

F32 = jnp.float32
BF16 = jnp.bfloat16

D_MODEL = 2048
HEAD_DIM = 128
N_HEADS = 8
WIDTH = N_HEADS * HEAD_DIM
KV_GROUPS = 2
HEADS_PER_GROUP = N_HEADS // KV_GROUPS
KV_WIDTH = KV_GROUPS * HEAD_DIM
N_BRANCHES = 3
CMP_LEN = 32
CMP_STRIDE = 16
CMP_HIDDEN = 256
SEL_LEN = 64
SEL_TOPK = 8
WINDOW = 512
REL_BUCKETS = 32
REL_MAX_DIST = 128
LN_EPS = 1e-5
NEG = -1e30
LOG2E = math.log2(math.e)
Q_PRESCALE = HEAD_DIM ** -0.5 * LOG2E

LANES = 128
SUBLANES = 8
VMEM_LIMIT = 56 * 1024 * 1024

_REF_LAYOUT = (
    ("fox_q", WIDTH), ("fox_k", WIDTH), ("fox_v", WIDTH), ("fox_f", N_HEADS), ("fox_z", WIDTH),
    ("nsa_q", WIDTH), ("nsa_k_cmp", KV_WIDTH), ("nsa_v_cmp", KV_WIDTH), ("nsa_k_sel", KV_WIDTH),
    ("nsa_v_sel", KV_WIDTH), ("nsa_k_win", KV_WIDTH), ("nsa_v_win", KV_WIDTH),
    ("nsa_gate", N_HEADS * N_BRANCHES), ("nsa_z", WIDTH), ("merge_a", D_MODEL), ("merge_b", D_MODEL),
)
_REF_OFF = {}
_o = 0
for _n, _w in _REF_LAYOUT:
    _REF_OFF[_n] = (_o, _w)
    _o += _w

_MAIN_ORDER = ("fox_q", "fox_k", "fox_v", "fox_z", "nsa_q", "nsa_z", "merge_a", "merge_b",
               "nsa_k_cmp", "nsa_v_cmp", "nsa_k_sel", "nsa_v_sel", "nsa_k_win", "nsa_v_win")
_QUERY_COLS = ("fox_q", "nsa_q")
_MAIN_OFF = {}
_o = 0
for _n in _MAIN_ORDER:
    _MAIN_OFF[_n] = _o
    _o += _REF_OFF[_n][1]
MAIN_COLS = _o
GATE_LANE0 = N_HEADS

TQ = 256
TK = TQ
FOX_HEAD_GROUP = 4


def _dot(a, b):
    return jnp.dot(a, b, preferred_element_type=F32)


def _dot_nt(a, b):
    return lax.dot_general(a, b, (((1,), (1,)), ((), ())), preferred_element_type=F32)


def _sigmoid(x):
    return 1.0 / (1.0 + jnp.exp(-x))


PROJ_TN = 512


def _proj_tiles():
    rows, is_query = [], []
    for name in _MAIN_ORDER:
        off, width = _REF_OFF[name]
        start = _MAIN_OFF[name]
        for c in range(start, start + width):
            if c % PROJ_TN == 0:
                assert (off + c - start) % SUBLANES == 0
                rows.append((off + c - start) // SUBLANES)
                is_query.append(int(name in _QUERY_COLS))
    return np.asarray(rows, np.int32), np.asarray(is_query, np.int32)


def _proj_kernel(rows_ref, isq_ref, x_ref, wt_ref, wst_ref, o_ref, os_ref, xb_ref):
    j = pl.program_id(1)

    @pl.when(j == 0)
    def _():
        xb_ref[...] = x_ref[...].astype(BF16)
        os_ref[...] = _dot_nt(xb_ref[...], wst_ref[...].astype(BF16))

    scale = jnp.where(isq_ref[j] == 1, Q_PRESCALE, 1.0)
    o_ref[...] = (_dot_nt(xb_ref[...], wt_ref[...].astype(BF16)) * scale).astype(o_ref.dtype)


def _proj(x2, w_t, w_small_t, tm=1024):
    m, k = x2.shape
    rows, is_query = _proj_tiles()
    assert MAIN_COLS % PROJ_TN == 0 and len(rows) == MAIN_COLS // PROJ_TN
    grid_spec = pltpu.PrefetchScalarGridSpec(
        num_scalar_prefetch=2,
        grid=(m // tm, len(rows)),
        in_specs=[
            pl.BlockSpec((tm, k), lambda i, j, rows, isq: (i, 0)),
            pl.BlockSpec((pl.Element(PROJ_TN), pl.Element(k)),
                         lambda i, j, rows, isq: (rows[j] * SUBLANES, 0)),
            pl.BlockSpec((LANES, k), lambda i, j, rows, isq: (0, 0)),
        ],
        out_specs=[
            pl.BlockSpec((tm, PROJ_TN), lambda i, j, rows, isq: (i, j)),
            pl.BlockSpec((tm, LANES), lambda i, j, rows, isq: (i, 0)),
        ],
        scratch_shapes=[pltpu.VMEM((tm, k), BF16)],
    )
    return pl.pallas_call(
        _proj_kernel,
        grid_spec=grid_spec,
        out_shape=[jax.ShapeDtypeStruct((m, MAIN_COLS), BF16), jax.ShapeDtypeStruct((m, LANES), F32)],
        compiler_params=pltpu.CompilerParams(
            dimension_semantics=("parallel", "arbitrary"), vmem_limit_bytes=VMEM_LIMIT),
        name="proj",
    )(jnp.asarray(rows), jnp.asarray(is_query), x2, w_t, w_small_t)


_CUM_CHUNK = 256


def _split3(x):
    hi = x.astype(BF16)
    r1 = x - hi.astype(F32)
    mid = r1.astype(BF16)
    lo = (r1 - mid.astype(F32)).astype(BF16)
    return hi, mid, lo


def _gate_kernel(hs_ref, bf_ref, c_ref, g_ref):
    hs = hs_ref[...]
    g_ref[...] = _sigmoid(hs)
    z = hs + bf_ref[...]
    logf = jnp.minimum(z, 0.0) - jnp.log1p(jnp.exp(-jnp.abs(z)))
    n = _CUM_CHUNK
    tri = (lax.broadcasted_iota(jnp.int32, (n, n), 1)
           <= lax.broadcasted_iota(jnp.int32, (n, n), 0)).astype(BF16)
    carry = jnp.zeros((1, LANES), F32)
    for blk in range(hs.shape[0] // n):
        hi, mid, lo = _split3(logf[blk * n:(blk + 1) * n])
        cb = _dot(tri, hi) + _dot(tri, mid) + _dot(tri, lo) + carry
        c_ref[blk * n:(blk + 1) * n, :] = cb * LOG2E
        carry = cb[n - 1:n, :]


def _gates(h_small, bf_row, batch, seq):
    return pl.pallas_call(
        _gate_kernel,
        grid=(batch,),
        in_specs=[pl.BlockSpec((seq, LANES), lambda b: (b, 0)),
                  pl.BlockSpec((1, LANES), lambda b: (0, 0))],
        out_specs=[pl.BlockSpec((seq, LANES), lambda b: (b, 0)),
                   pl.BlockSpec((seq, LANES), lambda b: (b, 0))],
        out_shape=[jax.ShapeDtypeStruct(h_small.shape, F32), jax.ShapeDtypeStruct(h_small.shape, F32)],
        compiler_params=pltpu.CompilerParams(dimension_semantics=("parallel",)),
        name="gates",
    )(h_small, bf_row)


_HALF = CMP_LEN // 2
_CHUNK_COLS = _HALF * 2 * KV_WIDTH


def _gelu_tanh(x):
    return 0.5 * x * (1.0 + jnp.tanh(math.sqrt(2.0 / math.pi) * (x + 0.044715 * (x * x * x))))


def _compress_kernel(r_ref, pk_ref, pv_ref, w1k_ref, w2k_ref, w1v_ref, w2v_ref, kc_ref, vc_ref):
    n_chunks = r_ref.shape[1]
    for kv, (pos_ref, w1_ref, w2_ref, out_ref) in enumerate(
            ((pk_ref, w1k_ref, w2k_ref, kc_ref), (pv_ref, w1v_ref, w2v_ref, vc_ref))):
        for g in range(KV_GROUPS):
            first = jnp.zeros((n_chunks, CMP_HIDDEN), F32)
            second = jnp.zeros((n_chunks, CMP_HIDDEN), F32)
            for l in range(_HALF):
                col = l * 2 * KV_WIDTH + kv * KV_WIDTH + g * HEAD_DIM
                a = r_ref[0, :, col:col + HEAD_DIM].astype(F32)
                a1 = (a + pos_ref[l:l + 1, :]).astype(BF16)
                a2 = (a + pos_ref[_HALF + l:_HALF + l + 1, :]).astype(BF16)
                first += _dot(a1, w1_ref[l * HEAD_DIM:(l + 1) * HEAD_DIM, :])
                second += _dot(a2, w1_ref[(_HALF + l) * HEAD_DIM:(_HALF + l + 1) * HEAD_DIM, :])
            hid = first + pltpu.roll(second, n_chunks - 1, 0)
            out = _dot(_gelu_tanh(hid).astype(BF16), w2_ref[...])
            out_ref[0, :, g * HEAD_DIM:(g + 1) * HEAD_DIM] = out.astype(out_ref.dtype)


def _compress(r, pos_k, pos_v, w1k, w2k, w1v, w2v):
    batch, n_chunks, cols = r.shape
    full = lambda shape: pl.BlockSpec(shape, lambda b: (0,) * len(shape))
    return pl.pallas_call(
        _compress_kernel,
        grid=(batch,),
        in_specs=[pl.BlockSpec((1, n_chunks, cols), lambda b: (b, 0, 0)),
                  full(pos_k.shape), full(pos_v.shape),
                  full(w1k.shape), full(w2k.shape), full(w1v.shape), full(w2v.shape)],
        out_specs=[pl.BlockSpec((1, n_chunks, KV_WIDTH), lambda b: (b, 0, 0)),
                   pl.BlockSpec((1, n_chunks, KV_WIDTH), lambda b: (b, 0, 0))],
        out_shape=[jax.ShapeDtypeStruct((batch, n_chunks, KV_WIDTH), BF16),
                   jax.ShapeDtypeStruct((batch, n_chunks, KV_WIDTH), BF16)],
        compiler_params=pltpu.CompilerParams(
            dimension_semantics=("parallel",), vmem_limit_bytes=VMEM_LIMIT),
        name="compress",
    )(r, pos_k, pos_v, w1k, w2k, w1v, w2v)


def _softmax_init(m_ref, l_ref, acc_ref):
    m_ref[...] = jnp.full(m_ref.shape, NEG, F32)
    l_ref[...] = jnp.zeros(l_ref.shape, F32)
    acc_ref[...] = jnp.zeros(acc_ref.shape, F32)


def _lane_tile(x, n):
    return x if n == 1 else jnp.concatenate([x] * n, axis=1)


def _softmax_update(s, v, row_const, m_ref, l_ref, acc_ref, rows=slice(None)):
    n = s.shape[1] // LANES
    m_prev = m_ref[rows, :]
    m_cur = jnp.max(s, axis=1, keepdims=True)
    if row_const is not None:
        m_cur = m_cur + row_const
    m_new = jnp.maximum(m_prev, m_cur)
    alpha = jnp.exp2(m_prev - m_new)
    shift = m_new if row_const is None else m_new - row_const
    p = jnp.exp2(s - _lane_tile(shift, n))
    p_lanes = p[:, :LANES]
    for t in range(1, n):
        p_lanes = p_lanes + p[:, t * LANES:(t + 1) * LANES]
    l_ref[rows, :] = alpha * l_ref[rows, :] + p_lanes
    acc_ref[rows, :] = alpha * acc_ref[rows, :] + _dot(p.astype(BF16), v)
    m_ref[rows, :] = m_new


def _softmax_finish(l_ref, acc_ref, rows=slice(None)):
    return acc_ref[rows, :] / jnp.sum(l_ref[rows, :], axis=1, keepdims=True)


def _fox_kernel(q_ref, k_ref, v_ref, z_ref, cc_ref, cr_ref, o_ref, ct_ref, m_ref, l_ref, acc_ref):
    i = pl.program_id(1)
    causal = (lax.broadcasted_iota(jnp.int32, (TQ, TK), 1)
              <= lax.broadcasted_iota(jnp.int32, (TQ, TK), 0))
    for hg in range(N_HEADS // FOX_HEAD_GROUP):
        heads = tuple(range(hg * FOX_HEAD_GROUP, (hg + 1) * FOX_HEAD_GROUP))
        _softmax_init(m_ref, l_ref, acc_ref)
        for n, h in enumerate(heads):
            ct_ref[n * TQ:(n + 1) * TQ, :] = jnp.broadcast_to(cc_ref[:, h:h + 1], (TQ, LANES))

        def step(j, masked):
            off = pl.multiple_of(j * TK, TK)
            for n, h in enumerate(heads):
                lo = h * HEAD_DIM
                rows = slice(n * TQ, (n + 1) * TQ)
                k = k_ref[pl.ds(off, TK), lo:lo + HEAD_DIM]
                v = v_ref[pl.ds(off, TK), lo:lo + HEAD_DIM]
                s = _dot_nt(q_ref[:, lo:lo + HEAD_DIM], k) - cr_ref[0, j, h:h + 1, :]
                if masked:
                    s = jnp.where(causal, s, NEG)
                _softmax_update(s, v, ct_ref[rows, :], m_ref, l_ref, acc_ref, rows)

        def body(j, carry):
            step(j, False)
            return carry

        lax.fori_loop(0, i, body, 0)
        step(i, True)
        for n, h in enumerate(heads):
            lo = h * HEAD_DIM
            o = _softmax_finish(l_ref, acc_ref, slice(n * TQ, (n + 1) * TQ))
            z = z_ref[:, lo:lo + HEAD_DIM].astype(F32)
            o_ref[:, lo:lo + HEAD_DIM] = (o * (z * _sigmoid(z))).astype(o_ref.dtype)


def _fox(h_main, c_col, c_row, batch, seq):
    nq = seq // TQ
    blk = lambda name: _MAIN_OFF[name] // WIDTH
    stat = pltpu.VMEM((FOX_HEAD_GROUP * TQ, LANES), F32)
    return pl.pallas_call(
        _fox_kernel,
        grid=(batch, nq),
        in_specs=[
            pl.BlockSpec((TQ, WIDTH), lambda b, i: (b * nq + i, blk("fox_q"))),
            pl.BlockSpec((seq, WIDTH), lambda b, i: (b, blk("fox_k"))),
            pl.BlockSpec((seq, WIDTH), lambda b, i: (b, blk("fox_v"))),
            pl.BlockSpec((TQ, WIDTH), lambda b, i: (b * nq + i, blk("fox_z"))),
            pl.BlockSpec((TQ, LANES), lambda b, i: (b * nq + i, 0)),
            pl.BlockSpec((1, seq // TK, N_HEADS, TK), lambda b, i: (b, 0, 0, 0)),
        ],
        out_specs=pl.BlockSpec((TQ, WIDTH), lambda b, i: (b * nq + i, 0)),
        out_shape=jax.ShapeDtypeStruct((batch * seq, WIDTH), BF16),
        scratch_shapes=[stat, stat, stat, stat],
        compiler_params=pltpu.CompilerParams(
            dimension_semantics=("parallel", "arbitrary"), vmem_limit_bytes=VMEM_LIMIT),
        name="fox",
    )(h_main, h_main, h_main, h_main, c_col, c_row)


N_SEL_ROWS = 32
MASK_BUCKET = REL_BUCKETS
GROUP_ROWS = HEADS_PER_GROUP * TQ
TILE_DIAG, TILE_PREV, TILE_WIN2 = 0, 1, 2


def _bias_lookup(tab_ref, h, idx):
    row = jnp.broadcast_to(tab_ref[h:h + 1, :], idx.shape)
    return jnp.take_along_axis(row, idx, axis=1, mode="promise_in_bounds")


def _nsa_kernel(q_ref, ks_ref, vs_ref, kw_ref, vw_ref, z_ref, kc_ref, vc_ref, cidx_ref, tidx_ref,
                tab_ref, g_ref, e_ref, ov_ref, o_ref,
                qg_ref, m_ref, l_ref, acc_ref, mb_ref, oc_ref, os_ref, t_ref, far_ref):
    i = pl.program_id(1)
    nk = mb_ref.shape[0]
    t0 = i * TQ

    @pl.when((pl.program_id(0) == 0) & (i == 0))
    def _():
        for h in range(N_HEADS):
            for d in range(tidx_ref.shape[0]):
                for half in range(TK // LANES):
                    cs = slice(half * LANES, (half + 1) * LANES)
                    t_ref[h, d, :, cs] = _bias_lookup(tab_ref, h, tidx_ref[d, :, cs])
            far_ref[h * TQ:(h + 1) * TQ, :] = jnp.broadcast_to(
                tab_ref[h:h + 1, REL_BUCKETS - 1:REL_BUCKETS], (TQ, LANES))

    cmaskf = (lax.broadcasted_iota(jnp.int32, (TQ, LANES), 1) * CMP_STRIDE + (CMP_LEN - 1)
              <= t0 + lax.broadcasted_iota(jnp.int32, (TQ, LANES), 0)).astype(F32)
    j_t = lax.broadcasted_iota(jnp.int32, (N_SEL_ROWS, TQ), 0)
    t_t = t0 + lax.broadcasted_iota(jnp.int32, (N_SEL_ROWS, TQ), 1)
    cur_t = t_t // SEL_LEN
    forced_t = (j_t == 0) | (j_t == cur_t) | (j_t == cur_t - 1)
    valid_t = j_t * SEL_LEN <= t_t

    for g in range(KV_GROUPS):
        glo = g * HEAD_DIM
        heads = tuple(range(g * HEADS_PER_GROUP, (g + 1) * HEADS_PER_GROUP))
        grows = slice(g * GROUP_ROWS, (g + 1) * GROUP_ROWS)
        for n, h in enumerate(heads):
            qg_ref[n * TQ:(n + 1) * TQ, :] = q_ref[:, h * HEAD_DIM:(h + 1) * HEAD_DIM]

        cidx = cidx_ref[...]
        cbias = jnp.concatenate([_bias_lookup(tab_ref, h, cidx) for h in heads], axis=0)
        sc = _dot_nt(qg_ref[...], kc_ref[0, :, glo:glo + HEAD_DIM]) + cbias
        e = jnp.exp2(sc - jnp.max(sc, axis=1, keepdims=True))
        p = e / jnp.sum(e, axis=1, keepdims=True) * jnp.concatenate([cmaskf] * HEADS_PER_GROUP, axis=0)
        oc_ref[...] = _dot(p.astype(BF16), vc_ref[0, :, glo:glo + HEAD_DIM])
        psum = p[:TQ]
        for n in range(1, HEADS_PER_GROUP):
            psum = psum + p[n * TQ:(n + 1) * TQ]

        p_hi = psum.astype(BF16)
        p_lo = (psum - p_hi.astype(F32)).astype(BF16)
        imp_t = _dot_nt(ov_ref[...], p_hi) + _dot_nt(ov_ref[...], p_lo)
        x = jnp.where(valid_t, jnp.where(forced_t, -NEG, imp_t[:N_SEL_ROWS]), NEG)
        cnt = jnp.zeros((N_SEL_ROWS, TQ), F32)
        for jp in range(N_SEL_ROWS):
            row = x[jp:jp + 1, :]
            beats = (row > x) | ((row == x) & (j_t > jp))
            cnt = cnt + jnp.where(beats, 1.0, 0.0)
        unsel_t = jnp.where(cnt < SEL_TOPK, 0.0, 1.0)
        unsel_t = jnp.concatenate([unsel_t, jnp.zeros((LANES - N_SEL_ROWS, TQ), F32)], axis=0)
        unsel = unsel_t.T.astype(BF16)
        for j in range(nk):
            @pl.when(j <= i)
            def _(j=j):
                mb_ref[j] = _dot(unsel, e_ref[:, j * TK:(j + 1) * TK])

        def tile_bias(d, extra=None):
            tiles = [t_ref[h, d] if extra is None else t_ref[h, d] + extra for h in heads]
            return jnp.concatenate(tiles, axis=0)

        _softmax_init(m_ref, l_ref, acc_ref)

        def sel_far(j, carry):
            off = pl.multiple_of(j * TK, TK)
            s = (_dot_nt(qg_ref[...], ks_ref[pl.ds(off, TK), glo:glo + HEAD_DIM])
                 + jnp.concatenate([mb_ref[j]] * HEADS_PER_GROUP, axis=0))
            _softmax_update(s, vs_ref[pl.ds(off, TK), glo:glo + HEAD_DIM], far_ref[grows, :],
                            m_ref, l_ref, acc_ref)
            return carry

        lax.fori_loop(0, i - 1, sel_far, 0)

        def sel_near(j, d):
            off = pl.multiple_of(j * TK, TK)
            s = (_dot_nt(qg_ref[...], ks_ref[pl.ds(off, TK), glo:glo + HEAD_DIM])
                 + tile_bias(d, mb_ref[j]))
            _softmax_update(s, vs_ref[pl.ds(off, TK), glo:glo + HEAD_DIM], None, m_ref, l_ref, acc_ref)

        @pl.when(i >= 1)
        def _():
            sel_near(i - 1, TILE_PREV)

        sel_near(i, TILE_DIAG)
        os_ref[...] = _softmax_finish(l_ref, acc_ref)

        _softmax_init(m_ref, l_ref, acc_ref)

        def win_step(j, d):
            off = pl.multiple_of(j * TK, TK)
            s = _dot_nt(qg_ref[...], kw_ref[pl.ds(off, TK), glo:glo + HEAD_DIM]) + tile_bias(d)
            _softmax_update(s, vw_ref[pl.ds(off, TK), glo:glo + HEAD_DIM], None, m_ref, l_ref, acc_ref)

        win_step(i, TILE_DIAG)

        @pl.when(i >= 1)
        def _():
            win_step(i - 1, TILE_PREV)

        @pl.when(i >= 2)
        def _():
            win_step(i - 2, TILE_WIN2)

        o_win = _softmax_finish(l_ref, acc_ref)

        for n, h in enumerate(heads):
            rows = slice(n * TQ, (n + 1) * TQ)
            lo = h * HEAD_DIM
            gl = GATE_LANE0 + h * N_BRANCHES
            ob = (g_ref[:, gl:gl + 1] * oc_ref[rows, :] + g_ref[:, gl + 1:gl + 2] * os_ref[rows, :]
                  + g_ref[:, gl + 2:gl + 3] * o_win[rows])
            z = z_ref[:, lo:lo + HEAD_DIM].astype(F32)
            o_ref[:, lo:lo + HEAD_DIM] = (ob * (z * _sigmoid(z))).astype(o_ref.dtype)


def _nsa(h_main, k_cmp, v_cmp, cmp_idx, tile_idx, table, gates, e_mat, ov_t, batch, seq):
    nq = seq // TQ
    nk = seq // TK
    wblk = lambda name: _MAIN_OFF[name] // WIDTH
    kvblk = lambda name: _MAIN_OFF[name] // KV_WIDTH
    full = lambda shape: pl.BlockSpec(shape, lambda b, i: (0,) * len(shape))
    kv_spec = lambda name: pl.BlockSpec((seq, KV_WIDTH), lambda b, i: (b, kvblk(name)))
    n_chunks = k_cmp.shape[1]
    stat = pltpu.VMEM((GROUP_ROWS, LANES), F32)
    return pl.pallas_call(
        _nsa_kernel,
        grid=(batch, nq),
        in_specs=[
            pl.BlockSpec((TQ, WIDTH), lambda b, i: (b * nq + i, wblk("nsa_q"))),
            kv_spec("nsa_k_sel"), kv_spec("nsa_v_sel"), kv_spec("nsa_k_win"), kv_spec("nsa_v_win"),
            pl.BlockSpec((TQ, WIDTH), lambda b, i: (b * nq + i, wblk("nsa_z"))),
            pl.BlockSpec((1, n_chunks, KV_WIDTH), lambda b, i: (b, 0, 0)),
            pl.BlockSpec((1, n_chunks, KV_WIDTH), lambda b, i: (b, 0, 0)),
            pl.BlockSpec((TQ, LANES), lambda b, i: (i, 0)),
            full(tile_idx.shape), full(table.shape),
            pl.BlockSpec((TQ, LANES), lambda b, i: (b * nq + i, 0)),
            full(e_mat.shape), full(ov_t.shape),
        ],
        out_specs=pl.BlockSpec((TQ, WIDTH), lambda b, i: (b * nq + i, 0)),
        out_shape=jax.ShapeDtypeStruct((batch * seq, WIDTH), BF16),
        scratch_shapes=[pltpu.VMEM((GROUP_ROWS, HEAD_DIM), BF16),
                        stat, stat, stat,
                        pltpu.VMEM((nk, TQ, TK), F32),
                        stat, stat,
                        pltpu.VMEM((N_HEADS,) + tile_idx.shape, F32),
                        pltpu.VMEM((N_HEADS * TQ, LANES), F32)],
        compiler_params=pltpu.CompilerParams(
            dimension_semantics=("arbitrary", "arbitrary"), vmem_limit_bytes=VMEM_LIMIT),
        name="nsa",
    )(h_main, h_main, h_main, h_main, h_main, h_main, k_cmp, v_cmp, cmp_idx, tile_idx, table, gates,
      e_mat, ov_t)


def _out_kernel(alpha, ua_ref, ub_ref, ga_ref, gb_ref, x_ref, wa_ref, wb_ref, wo_ref, lg_ref, lb_ref,
                o_ref):
    ya = _dot(ua_ref[...], wa_ref[...])
    yb = _dot(ub_ref[...], wb_ref[...])
    merged = _sigmoid(ga_ref[...].astype(F32)) * ya + _sigmoid(gb_ref[...].astype(F32)) * yb
    y = alpha * x_ref[...] + _dot(merged.astype(BF16), wo_ref[...])
    mu = jnp.mean(y, axis=-1, keepdims=True)
    d = y - mu
    var = jnp.mean(d * d, axis=-1, keepdims=True)
    o_ref[...] = d * lax.rsqrt(var + LN_EPS) * lg_ref[...] + lb_ref[...]


def _out(u_a, u_b, h_main, x2, w_a, w_b, w_o, ln_g, ln_b, alpha, tm=256):
    m = x2.shape[0]
    mblk = lambda name: _MAIN_OFF[name] // D_MODEL
    const = lambda shape: pl.BlockSpec(shape, lambda i: (0, 0), pipeline_mode=pl.Buffered(1))
    return pl.pallas_call(
        functools.partial(_out_kernel, alpha),
        grid=(m // tm,),
        in_specs=[
            pl.BlockSpec((tm, WIDTH), lambda i: (i, 0)),
            pl.BlockSpec((tm, WIDTH), lambda i: (i, 0)),
            pl.BlockSpec((tm, D_MODEL), lambda i: (i, mblk("merge_a"))),
            pl.BlockSpec((tm, D_MODEL), lambda i: (i, mblk("merge_b"))),
            pl.BlockSpec((tm, D_MODEL), lambda i: (i, 0)),
            const(w_a.shape), const(w_b.shape), const(w_o.shape),
            const(ln_g.shape), const(ln_b.shape),
        ],
        out_specs=pl.BlockSpec((tm, D_MODEL), lambda i: (i, 0)),
        out_shape=jax.ShapeDtypeStruct((m, D_MODEL), F32),
        compiler_params=pltpu.CompilerParams(
            dimension_semantics=("parallel",), vmem_limit_bytes=VMEM_LIMIT),
        name="out",
    )(u_a, u_b, h_main, h_main, x2, w_a, w_b, w_o, ln_g, ln_b)


def _bucket_np(dist):
    n = np.maximum(dist, 0)
    exact = REL_BUCKETS // 2
    large = exact + (np.log(np.maximum(n, 1).astype(np.float32) / exact)
                     / math.log(REL_MAX_DIST / exact) * (REL_BUCKETS - exact)).astype(np.int32)
    return np.where(n < exact, n, np.minimum(large, REL_BUCKETS - 1)).astype(np.int32)


@functools.lru_cache(maxsize=None)
def _static_tables(seq):
    r = np.arange(TQ)[:, None]
    c = np.arange(TK)[None, :]
    tile_idx = np.stack([_bucket_np(r - c), _bucket_np(TQ + r - c), _bucket_np(2 * TQ + r - c)])
    tile_ok = np.stack([c <= r, np.ones((TQ, TK), bool), (2 * TQ + r - c) < WINDOW])
    tile_idx = np.where(tile_ok, tile_idx, MASK_BUCKET).astype(np.int32)
    t = np.arange(seq)[:, None]
    cblk = np.arange(LANES)[None, :]
    blk_end = cblk * CMP_STRIDE + CMP_LEN - 1
    n_cmp = (seq - CMP_LEN) // CMP_STRIDE + 1
    cmp_idx = np.where((blk_end <= t) & (cblk < n_cmp), _bucket_np(t - blk_end), MASK_BUCKET).astype(np.int32)
    e_mat = np.where(np.arange(LANES)[:, None] == (np.arange(seq)[None, :] // SEL_LEN), NEG, 0.0)
    cs = (np.arange(LANES) * CMP_STRIDE)[None, :]
    ss = (np.arange(LANES) * SEL_LEN)[:, None]
    ov_t = ((cs < ss + SEL_LEN) & (cs + CMP_LEN > ss)
            & (np.arange(LANES)[None, :] < n_cmp) & (np.arange(LANES)[:, None] < seq // SEL_LEN))
    return tile_idx, cmp_idx, e_mat.astype(np.float32), ov_t.astype(np.float32)


def _layer(x, w_in, b_f, cmp_pos_k, cmp_pos_v, cmp_wk1, cmp_wk2, cmp_wv1, cmp_wv2,
           w_a, w_b, w_o, ln_g, ln_b, rel_bias, alpha):
    batch, seq, d_model = x.shape
    assert d_model == D_MODEL and seq % TQ == 0 and 2 * TQ == WINDOW and REL_MAX_DIST <= TQ
    assert seq // SEL_LEN == N_SEL_ROWS and seq // CMP_STRIDE == LANES
    x2 = x.reshape(batch * seq, d_model)

    w_t = jnp.swapaxes(w_in, 0, 1)
    trows = lambda name: w_t[_REF_OFF[name][0]:_REF_OFF[name][0] + _REF_OFF[name][1]]
    n_small = N_HEADS + N_HEADS * N_BRANCHES
    w_small_t = jnp.concatenate(
        [trows("fox_f"), trows("nsa_gate"), jnp.zeros((LANES - n_small, d_model), F32)], axis=0)
    bf_row = jnp.concatenate([b_f.astype(F32), jnp.zeros((LANES - N_HEADS,), F32)]).reshape(1, LANES)

    h_main, h_small = _proj(x2, w_t, w_small_t)
    c_col, gates = _gates(h_small, bf_row, batch, seq)
    c_row = (c_col[:, :N_HEADS].reshape(batch, seq // TK, TK, N_HEADS).transpose(0, 1, 3, 2))

    u_a = _fox(h_main, c_col, c_row, batch, seq)

    c0 = _MAIN_OFF["nsa_k_cmp"]
    r = h_main[:, c0:c0 + 2 * KV_WIDTH].reshape(batch, seq // _HALF, _CHUNK_COLS)
    k_cmp, v_cmp = _compress(r, cmp_pos_k, cmp_pos_v, cmp_wk1.astype(BF16), cmp_wk2.astype(BF16),
                             cmp_wv1.astype(BF16), cmp_wv2.astype(BF16))

    tile_idx, cmp_idx, e_mat, ov_t = _static_tables(seq)
    table = jnp.concatenate(
        [rel_bias.T.astype(F32) * LOG2E, jnp.full((N_HEADS, 1), NEG, F32),
         jnp.zeros((N_HEADS, LANES - REL_BUCKETS - 1), F32)], axis=1)
    u_b = _nsa(h_main, k_cmp, v_cmp, jnp.asarray(cmp_idx), jnp.asarray(tile_idx), table, gates,
               jnp.asarray(e_mat, BF16), jnp.asarray(ov_t, BF16), batch, seq)

    out = _out(u_a, u_b, h_main, x2, w_a.astype(BF16), w_b.astype(BF16), w_o.astype(BF16),
               ln_g.reshape(1, d_model), ln_b.reshape(1, d_model), alpha)
    return out.reshape(batch, seq, d_model)


def kernel(x, w_in, b_f, cmp_pos_k, cmp_pos_v, cmp_wk1, cmp_wk2, cmp_wv1, cmp_wv2,
           w_a, w_b, w_o, ln_g, ln_b, rel_bias):
    depth = w_in.shape[0]
    alpha = (2 * depth) ** 0.25
    for layer in range(depth):
        x = _layer(x, w_in[layer], b_f[layer], cmp_pos_k[layer], cmp_pos_v[layer], cmp_wk1[layer],
                   cmp_wk2[layer], cmp_wv1[layer], cmp_wv2[layer], w_a[layer], w_b[layer], w_o[layer],
                   ln_g[layer], ln_b[layer], rel_bias, alpha)
    return x
```

```python
import functools
import math

import jax
import jax.numpy as jnp
import numpy as np
from jax import lax
from jax.experimental import pallas as pl
from jax.experimental.pallas import tpu as pltpu

F32 = jnp.float32
BF16 = jnp.bfloat16

D_MODEL = 2048
HEAD_DIM = 128
N_HEADS = 8
WIDTH = N_HEADS * HEAD_DIM
KV_GROUPS = 2
HEADS_PER_GROUP = N_HEADS // KV_GROUPS
KV_WIDTH = KV_GROUPS * HEAD_DIM
N_BRANCHES = 3
CMP_LEN = 32
CMP_STRIDE = 16
CMP_HIDDEN = 256
SEL_LEN = 64
SEL_TOPK = 8
WINDOW = 512
REL_BUCKETS = 32
REL_MAX_DIST = 128
LN_EPS = 1e-5
NEG = -1e30
LOG2E = math.log2(math.e)
Q_PRESCALE = HEAD_DIM ** -0.5 * LOG2E

LANES = 128
SUBLANES = 8
MXU_DEPTH = 256
VMEM_LIMIT = 56 * 1024 * 1024

_REF_LAYOUT = (
    ("fox_q", WIDTH), ("fox_k", WIDTH), ("fox_v", WIDTH), ("fox_f", N_HEADS), ("fox_z", WIDTH),
    ("nsa_q", WIDTH), ("nsa_k_cmp", KV_WIDTH), ("nsa_v_cmp", KV_WIDTH), ("nsa_k_sel", KV_WIDTH),
    ("nsa_v_sel", KV_WIDTH), ("nsa_k_win", KV_WIDTH), ("nsa_v_win", KV_WIDTH),
    ("nsa_gate", N_HEADS * N_BRANCHES), ("nsa_z", WIDTH), ("merge_a", D_MODEL), ("merge_b", D_MODEL),
)
_REF_OFF = {}
_o = 0
for _n, _w in _REF_LAYOUT:
    _REF_OFF[_n] = (_o, _w)
    _o += _w

_MAIN_ORDER = ("fox_q", "fox_k", "fox_v", "fox_z", "nsa_q", "nsa_z", "merge_a", "merge_b",
               "nsa_k_cmp", "nsa_v_cmp", "nsa_k_sel", "nsa_v_sel", "nsa_k_win", "nsa_v_win")
_QUERY_COLS = ("fox_q", "nsa_q")
_MAIN_OFF = {}
_o = 0
for _n in _MAIN_ORDER:
    _MAIN_OFF[_n] = _o
    _o += _REF_OFF[_n][1]
MAIN_COLS = _o
GATE_LANE0 = N_HEADS


def _dot(a, b):
    return jnp.dot(a, b, preferred_element_type=F32)


def _dot_nt(a, b):
    return lax.dot_general(a, b, (((1,), (1,)), ((), ())), preferred_element_type=F32)


def _sigmoid(x):
    return 1.0 / (1.0 + jnp.exp(-x))


PROJ_TN = 512


def _proj_tiles():
    rows, is_query = [], []
    for name in _MAIN_ORDER:
        off, width = _REF_OFF[name]
        start = _MAIN_OFF[name]
        for c in range(start, start + width):
            if c % PROJ_TN == 0:
                assert (off + c - start) % SUBLANES == 0
                rows.append((off + c - start) // SUBLANES)
                is_query.append(int(name in _QUERY_COLS))
    return np.asarray(rows, np.int32), np.asarray(is_query, np.int32)


def _proj_kernel(rows_ref, isq_ref, x_ref, wt_ref, wst_ref, o_ref, os_ref, xb_ref):
    j = pl.program_id(1)

    @pl.when(j == 0)
    def _():
        xb_ref[...] = x_ref[...].astype(BF16)
        os_ref[...] = _dot_nt(xb_ref[...], wst_ref[...].astype(BF16))

    scale = jnp.where(isq_ref[j] == 1, Q_PRESCALE, 1.0)
    o_ref[...] = (_dot_nt(xb_ref[...], wt_ref[...].astype(BF16)) * scale).astype(o_ref.dtype)


def _proj(x2, w_t, w_small_t, tm=1024):
    m, k = x2.shape
    rows, is_query = _proj_tiles()
    assert MAIN_COLS % PROJ_TN == 0 and len(rows) == MAIN_COLS // PROJ_TN
    grid_spec = pltpu.PrefetchScalarGridSpec(
        num_scalar_prefetch=2,
        grid=(m // tm, len(rows)),
        in_specs=[
            pl.BlockSpec((tm, k), lambda i, j, rows, isq: (i, 0)),
            pl.BlockSpec((pl.Element(PROJ_TN), pl.Element(k)),
                         lambda i, j, rows, isq: (rows[j] * SUBLANES, 0)),
            pl.BlockSpec((LANES, k), lambda i, j, rows, isq: (0, 0)),
        ],
        out_specs=[
            pl.BlockSpec((tm, PROJ_TN), lambda i, j, rows, isq: (i, j)),
            pl.BlockSpec((tm, LANES), lambda i, j, rows, isq: (i, 0)),
        ],
        scratch_shapes=[pltpu.VMEM((tm, k), BF16)],
    )
    return pl.pallas_call(
        _proj_kernel,
        grid_spec=grid_spec,
        out_shape=[jax.ShapeDtypeStruct((m, MAIN_COLS), BF16), jax.ShapeDtypeStruct((m, LANES), F32)],
        compiler_params=pltpu.CompilerParams(
            dimension_semantics=("parallel", "arbitrary"), vmem_limit_bytes=VMEM_LIMIT),
        name="proj",
    )(jnp.asarray(rows), jnp.asarray(is_query), x2, w_t, w_small_t)


_CUM_CHUNK = 256


def _split3(x):
    hi = x.astype(BF16)
    r1 = x - hi.astype(F32)
    mid = r1.astype(BF16)
    lo = (r1 - mid.astype(F32)).astype(BF16)
    return hi, mid, lo


def _gate_kernel(hs_ref, bf_ref, c_ref, g_ref):
    hs = hs_ref[...]
    g_ref[...] = _sigmoid(hs)
    z = hs + bf_ref[...]
    logf = jnp.minimum(z, 0.0) - jnp.log1p(jnp.exp(-jnp.abs(z)))
    n = _CUM_CHUNK
    tri = (lax.broadcasted_iota(jnp.int32, (n, n), 1)
           <= lax.broadcasted_iota(jnp.int32, (n, n), 0)).astype(BF16)
    carry = jnp.zeros((1, LANES), F32)
    for blk in range(hs.shape[0] // n):
        hi, mid, lo = _split3(logf[blk * n:(blk + 1) * n])
        cb = _dot(tri, hi) + _dot(tri, mid) + _dot(tri, lo) + carry
        c_ref[blk * n:(blk + 1) * n, :] = cb * LOG2E
        carry = cb[n - 1:n, :]


def _gates(h_small, bf_row, batch, seq):
    return pl.pallas_call(
        _gate_kernel,
        grid=(batch,),
        in_specs=[pl.BlockSpec((seq, LANES), lambda b: (b, 0)),
                  pl.BlockSpec((1, LANES), lambda b: (0, 0))],
        out_specs=[pl.BlockSpec((seq, LANES), lambda b: (b, 0)),
                   pl.BlockSpec((seq, LANES), lambda b: (b, 0))],
        out_shape=[jax.ShapeDtypeStruct(h_small.shape, F32), jax.ShapeDtypeStruct(h_small.shape, F32)],
        compiler_params=pltpu.CompilerParams(dimension_semantics=("parallel",)),
        name="gates",
    )(h_small, bf_row)


_HALF = CMP_LEN // 2
_CHUNK_COLS = _HALF * 2 * KV_WIDTH


def _gelu_tanh(x):
    return 0.5 * x * (1.0 + jnp.tanh(math.sqrt(2.0 / math.pi) * (x + 0.044715 * (x * x * x))))


def _compress_kernel(r_ref, pk_ref, pv_ref, w1k_ref, w2k_ref, w1v_ref, w2v_ref, kc_ref, vc_ref):
    n_chunks = r_ref.shape[1]
    for kv, (pos_ref, w1_ref, w2_ref, out_ref) in enumerate(
            ((pk_ref, w1k_ref, w2k_ref, kc_ref), (pv_ref, w1v_ref, w2v_ref, vc_ref))):
        for g in range(KV_GROUPS):
            first = jnp.zeros((n_chunks, CMP_HIDDEN), F32)
            second = jnp.zeros((n_chunks, CMP_HIDDEN), F32)
            for l in range(_HALF):
                col = l * 2 * KV_WIDTH + kv * KV_WIDTH + g * HEAD_DIM
                a = r_ref[0, :, col:col + HEAD_DIM].astype(F32)
                a1 = (a + pos_ref[l:l + 1, :]).astype(BF16)
                a2 = (a + pos_ref[_HALF + l:_HALF + l + 1, :]).astype(BF16)
                first += _dot(a1, w1_ref[l * HEAD_DIM:(l + 1) * HEAD_DIM, :])
                second += _dot(a2, w1_ref[(_HALF + l) * HEAD_DIM:(_HALF + l + 1) * HEAD_DIM, :])
            hid = first + pltpu.roll(second, n_chunks - 1, 0)
            out = _dot(_gelu_tanh(hid).astype(BF16), w2_ref[...])
            out_ref[0, :, g * HEAD_DIM:(g + 1) * HEAD_DIM] = out.astype(out_ref.dtype)


def _compress(r, pos_k, pos_v, w1k, w2k, w1v, w2v):
    batch, n_chunks, cols = r.shape
    full = lambda shape: pl.BlockSpec(shape, lambda b: (0,) * len(shape))
    return pl.pallas_call(
        _compress_kernel,
        grid=(batch,),
        in_specs=[pl.BlockSpec((1, n_chunks, cols), lambda b: (b, 0, 0)),
                  full(pos_k.shape), full(pos_v.shape),
                  full(w1k.shape), full(w2k.shape), full(w1v.shape), full(w2v.shape)],
        out_specs=[pl.BlockSpec((1, n_chunks, KV_WIDTH), lambda b: (b, 0, 0)),
                   pl.BlockSpec((1, n_chunks, KV_WIDTH), lambda b: (b, 0, 0))],
        out_shape=[jax.ShapeDtypeStruct((batch, n_chunks, KV_WIDTH), BF16),
                   jax.ShapeDtypeStruct((batch, n_chunks, KV_WIDTH), BF16)],
        compiler_params=pltpu.CompilerParams(
            dimension_semantics=("parallel",), vmem_limit_bytes=VMEM_LIMIT),
        name="compress",
    )(r, pos_k, pos_v, w1k, w2k, w1v, w2v)


def _lane_tile(x, n):
    return x if n == 1 else jnp.concatenate([x] * n, axis=1)


def _lane_fold(p):
    out = p[:, :LANES]
    for t in range(1, p.shape[1] // LANES):
        out = out + p[:, t * LANES:(t + 1) * LANES]
    return out


def _softmax_first(s, v, m_ref, l_ref, acc_ref, rows=slice(None)):
    m = jnp.broadcast_to(jnp.max(s, axis=1, keepdims=True), (s.shape[0], LANES))
    p = jnp.exp2(s - _lane_tile(m, s.shape[1] // LANES))
    m_ref[rows, :] = m
    l_ref[rows, :] = _lane_fold(p)
    acc_ref[rows, :] = _dot(p.astype(BF16), v)


def _softmax_update(s, v, row_const, m_ref, l_ref, acc_ref, rows=slice(None)):
    m_prev = m_ref[rows, :]
    m_cur = jnp.max(s, axis=1, keepdims=True)
    if row_const is not None:
        m_cur = m_cur + row_const
    m_new = jnp.maximum(m_prev, m_cur)
    alpha = jnp.exp2(m_prev - m_new)
    shift = m_new if row_const is None else m_new - row_const
    p = jnp.exp2(s - _lane_tile(shift, s.shape[1] // LANES))
    l_ref[rows, :] = alpha * l_ref[rows, :] + _lane_fold(p)
    acc_ref[rows, :] = alpha * acc_ref[rows, :] + _dot(p.astype(BF16), v)
    m_ref[rows, :] = m_new


def _softmax_finish(l_ref, acc_ref, rows=slice(None)):
    return acc_ref[rows, :] / jnp.sum(l_ref[rows, :], axis=1, keepdims=True)


def _softmax_once(s, v):
    m = jnp.max(s, axis=1, keepdims=True)
    p = jnp.exp2(s - m)
    return _dot(p.astype(BF16), v) / jnp.sum(_lane_fold(p), axis=1, keepdims=True)


FOX_TQ = 512
FOX_HEAD_GROUP = 2
_N_SPLIT = 3


def _split_columns(c, sign, first_lane):
    lane = lax.broadcasted_iota(jnp.int32, (c.shape[0], LANES), 1)
    out = jnp.where(lane < 2 * _N_SPLIT, 1.0, 0.0)
    for n, term in enumerate(_split3(c)):
        out = jnp.where(lane == first_lane + n, sign * term.astype(F32), out)
    return out


def _fox_kernel(q_ref, k_ref, v_ref, z_ref, cq_ref, ck_ref, o_ref, kaug_ref, qaug_ref, m_ref, l_ref,
                acc_ref):
    i = pl.program_id(1)
    tq = FOX_TQ
    seq = k_ref.shape[0]

    @pl.when(i == 0)
    def _():
        for h in range(N_HEADS):
            for blk in range(seq // tq):
                rows = slice(blk * tq, (blk + 1) * tq)
                kaug_ref[h, rows, :HEAD_DIM] = k_ref[rows, h * HEAD_DIM:(h + 1) * HEAD_DIM]
                kaug_ref[h, rows, HEAD_DIM:] = _split_columns(
                    ck_ref[rows, h:h + 1], -1.0, _N_SPLIT).astype(BF16)

    causal = (lax.broadcasted_iota(jnp.int32, (tq, tq), 1)
              <= lax.broadcasted_iota(jnp.int32, (tq, tq), 0))
    diag = pl.multiple_of(i * tq, tq)
    for hg in range(N_HEADS // FOX_HEAD_GROUP):
        heads = tuple(range(hg * FOX_HEAD_GROUP, (hg + 1) * FOX_HEAD_GROUP))
        for n, h in enumerate(heads):
            rows = slice(n * tq, (n + 1) * tq)
            lo = h * HEAD_DIM
            qaug_ref[rows, :HEAD_DIM] = q_ref[:, lo:lo + HEAD_DIM]
            qaug_ref[rows, HEAD_DIM:] = _split_columns(cq_ref[:, h:h + 1], 1.0, 0).astype(BF16)
            s = _dot_nt(qaug_ref[rows, :], kaug_ref[h, pl.ds(diag, tq), :])
            _softmax_first(jnp.where(causal, s, NEG), v_ref[pl.ds(diag, tq), lo:lo + HEAD_DIM],
                           m_ref, l_ref, acc_ref, rows)

        def body(j, carry):
            off = pl.multiple_of(j * tq, tq)
            for n, h in enumerate(heads):
                rows = slice(n * tq, (n + 1) * tq)
                s = _dot_nt(qaug_ref[rows, :], kaug_ref[h, pl.ds(off, tq), :])
                _softmax_update(s, v_ref[pl.ds(off, tq), h * HEAD_DIM:(h + 1) * HEAD_DIM], None,
                                m_ref, l_ref, acc_ref, rows)
            return carry

        lax.fori_loop(0, i, body, 0)
        for n, h in enumerate(heads):
            lo = h * HEAD_DIM
            o = _softmax_finish(l_ref, acc_ref, slice(n * tq, (n + 1) * tq))
            z = z_ref[:, lo:lo + HEAD_DIM].astype(F32)
            o_ref[:, lo:lo + HEAD_DIM] = (o * (z * _sigmoid(z))).astype(o_ref.dtype)


def _fox(h_main, c_col, batch, seq):
    tq = FOX_TQ
    nq = seq // tq
    blk = lambda name: _MAIN_OFF[name] // WIDTH
    stat = pltpu.VMEM((FOX_HEAD_GROUP * tq, LANES), F32)
    return pl.pallas_call(
        _fox_kernel,
        grid=(batch, nq),
        in_specs=[
            pl.BlockSpec((tq, WIDTH), lambda b, i: (b * nq + i, blk("fox_q"))),
            pl.BlockSpec((seq, WIDTH), lambda b, i: (b, blk("fox_k"))),
            pl.BlockSpec((seq, WIDTH), lambda b, i: (b, blk("fox_v"))),
            pl.BlockSpec((tq, WIDTH), lambda b, i: (b * nq + i, blk("fox_z"))),
            pl.BlockSpec((tq, LANES), lambda b, i: (b * nq + i, 0)),
            pl.BlockSpec((seq, LANES), lambda b, i: (b, 0)),
        ],
        out_specs=pl.BlockSpec((tq, WIDTH), lambda b, i: (b * nq + i, 0)),
        out_shape=jax.ShapeDtypeStruct((batch * seq, WIDTH), BF16),
        scratch_shapes=[pltpu.VMEM((N_HEADS, seq, MXU_DEPTH), BF16),
                        pltpu.VMEM((FOX_HEAD_GROUP * tq, MXU_DEPTH), BF16),
                        stat, stat, stat],
        compiler_params=pltpu.CompilerParams(
            dimension_semantics=("parallel", "arbitrary"), vmem_limit_bytes=VMEM_LIMIT),
        name="fox",
    )(h_main, h_main, h_main, h_main, c_col, c_col)


TQ = 256
TK = TQ
NEAR = WINDOW + TQ
N_NEAR = NEAR // TK
FAR_TK = 2 * TK
N_SEL_ROWS = 32
MASK_BUCKET = REL_BUCKETS
GROUP_ROWS = HEADS_PER_GROUP * TQ
T_DIAG, T_PREV, T_WIN2, T_FAR, T_NONE = range(5)
N_GATHERED_TILES = 3


def _bias_lookup(tab_ref, h, idx):
    row = jnp.broadcast_to(tab_ref[h:h + 1, :], idx.shape)
    return jnp.take_along_axis(row, idx, axis=1, mode="promise_in_bounds")


def _nsa_kernel(q_ref, ks_ref, vs_ref, kw_ref, vw_ref, z_ref, kc_ref, vc_ref, cidx_ref, tidx_ref,
                tab_ref, g_ref, et_ref, ov_ref, o_ref,
                ksaug_ref, qaug_ref, m_ref, l_ref, acc_ref, oc_ref, os_ref, t_ref, far_ref):
    i = pl.program_id(1)
    t0 = i * TQ

    @pl.when((pl.program_id(0) == 0) & (i == 0))
    def _():
        for h in range(N_HEADS):
            for d in range(N_GATHERED_TILES):
                for half in range(TK // LANES):
                    cs = slice(half * LANES, (half + 1) * LANES)
                    t_ref[h, d, :, cs] = _bias_lookup(tab_ref, h, tidx_ref[d, :, cs])
            far = jnp.broadcast_to(tab_ref[h:h + 1, REL_BUCKETS - 1:REL_BUCKETS], (TQ, LANES))
            far_ref[h * TQ:(h + 1) * TQ, :] = far
            t_ref[h, T_FAR] = _lane_tile(far, TK // LANES)
            t_ref[h, T_NONE] = jnp.full((TQ, TK), NEG, F32)

    @pl.when(i == 0)
    def _():
        for g in range(KV_GROUPS):
            ksaug_ref[g, :, :HEAD_DIM] = ks_ref[:, g * HEAD_DIM:(g + 1) * HEAD_DIM]
            ksaug_ref[g, :, HEAD_DIM:] = et_ref[...]

    cmaskf = (lax.broadcasted_iota(jnp.int32, (TQ, LANES), 1) * CMP_STRIDE + (CMP_LEN - 1)
              <= t0 + lax.broadcasted_iota(jnp.int32, (TQ, LANES), 0)).astype(F32)
    j_t = lax.broadcasted_iota(jnp.int32, (N_SEL_ROWS, TQ), 0)
    t_t = t0 + lax.broadcasted_iota(jnp.int32, (N_SEL_ROWS, TQ), 1)
    cur_t = t_t // SEL_LEN
    forced_t = (j_t == 0) | (j_t == cur_t) | (j_t == cur_t - 1)
    valid_t = j_t * SEL_LEN <= t_t

    first_blk = jnp.maximum(i - (N_NEAR - 1), 0)
    near0 = pl.multiple_of(first_blk * TK, TK)

    def near_tile(kk, two_back):
        dist = i - (first_blk + kk)
        return jnp.where(dist == 0, T_DIAG, jnp.where(dist == 1, T_PREV,
                         jnp.where(dist == 2, two_back, T_NONE)))

    def near_bias(heads, two_back):
        return jnp.concatenate(
            [jnp.concatenate([t_ref[h, near_tile(kk, two_back)] for kk in range(N_NEAR)], axis=1)
             for h in heads], axis=0)

    for g in range(KV_GROUPS):
        glo = g * HEAD_DIM
        heads = tuple(range(g * HEADS_PER_GROUP, (g + 1) * HEADS_PER_GROUP))
        grows = slice(g * GROUP_ROWS, (g + 1) * GROUP_ROWS)
        for n, h in enumerate(heads):
            qaug_ref[n * TQ:(n + 1) * TQ, :HEAD_DIM] = q_ref[:, h * HEAD_DIM:(h + 1) * HEAD_DIM]

        cidx = cidx_ref[...]
        cbias = jnp.concatenate([_bias_lookup(tab_ref, h, cidx) for h in heads], axis=0)
        sc = _dot_nt(qaug_ref[:, :HEAD_DIM], kc_ref[0, :, glo:glo + HEAD_DIM]) + cbias
        e = jnp.exp2(sc - jnp.max(sc, axis=1, keepdims=True))
        p = e / jnp.sum(e, axis=1, keepdims=True) * jnp.concatenate([cmaskf] * HEADS_PER_GROUP, axis=0)
        oc_ref[...] = _dot(p.astype(BF16), vc_ref[0, :, glo:glo + HEAD_DIM])
        psum = p[:TQ]
        for n in range(1, HEADS_PER_GROUP):
            psum = psum + p[n * TQ:(n + 1) * TQ]

        p_hi = psum.astype(BF16)
        p_lo = (psum - p_hi.astype(F32)).astype(BF16)
        imp_t = _dot_nt(ov_ref[...], p_hi) + _dot_nt(ov_ref[...], p_lo)
        x = jnp.where(valid_t, jnp.where(forced_t, -NEG, imp_t[:N_SEL_ROWS]), NEG)
        cnt = jnp.zeros((N_SEL_ROWS, TQ), F32)
        for jp in range(N_SEL_ROWS):
            row = x[jp:jp + 1, :]
            beats = (row > x) | ((row == x) & (j_t > jp))
            cnt = cnt + jnp.where(beats, 1.0, 0.0)
        unsel_t = jnp.where(cnt < SEL_TOPK, 0.0, 1.0)
        unsel_t = jnp.concatenate([unsel_t, jnp.zeros((LANES - N_SEL_ROWS, TQ), F32)], axis=0)
        unsel = unsel_t.T.astype(BF16)
        for n in range(HEADS_PER_GROUP):
            qaug_ref[n * TQ:(n + 1) * TQ, HEAD_DIM:] = unsel

        s = _dot_nt(qaug_ref[...], ksaug_ref[g, pl.ds(near0, NEAR), :]) + near_bias(heads, T_FAR)
        _softmax_first(s, vs_ref[pl.ds(near0, NEAR), glo:glo + HEAD_DIM], m_ref, l_ref, acc_ref)

        def sel_far(off, width):
            s = _dot_nt(qaug_ref[...], ksaug_ref[g, pl.ds(off, width), :])
            _softmax_update(s, vs_ref[pl.ds(off, width), glo:glo + HEAD_DIM], far_ref[grows, :],
                            m_ref, l_ref, acc_ref)

        def sel_far_pair(j, carry):
            sel_far(pl.multiple_of(j * FAR_TK, FAR_TK), FAR_TK)
            return carry

        lax.fori_loop(0, first_blk // 2, sel_far_pair, 0)

        @pl.when(first_blk % 2 == 1)
        def _():
            sel_far(pl.multiple_of((first_blk - 1) * TK, TK), TK)

        os_ref[...] = _softmax_finish(l_ref, acc_ref)

        s = (_dot_nt(qaug_ref[:, :HEAD_DIM], kw_ref[pl.ds(near0, NEAR), glo:glo + HEAD_DIM])
             + near_bias(heads, T_WIN2))
        o_win = _softmax_once(s, vw_ref[pl.ds(near0, NEAR), glo:glo + HEAD_DIM])

        for n, h in enumerate(heads):
            rows = slice(n * TQ, (n + 1) * TQ)
            lo = h * HEAD_DIM
            gl = GATE_LANE0 + h * N_BRANCHES
            ob = (g_ref[:, gl:gl + 1] * oc_ref[rows, :] + g_ref[:, gl + 1:gl + 2] * os_ref[rows, :]
                  + g_ref[:, gl + 2:gl + 3] * o_win[rows])
            z = z_ref[:, lo:lo + HEAD_DIM].astype(F32)
            o_ref[:, lo:lo + HEAD_DIM] = (ob * (z * _sigmoid(z))).astype(o_ref.dtype)


def _nsa(h_main, k_cmp, v_cmp, cmp_idx, tile_idx, table, gates, e_t, ov_t, batch, seq):
    nq = seq // TQ
    wblk = lambda name: _MAIN_OFF[name] // WIDTH
    kvblk = lambda name: _MAIN_OFF[name] // KV_WIDTH
    full = lambda shape: pl.BlockSpec(shape, lambda b, i: (0,) * len(shape))
    kv_spec = lambda name: pl.BlockSpec((seq, KV_WIDTH), lambda b, i: (b, kvblk(name)))
    n_chunks = k_cmp.shape[1]
    stat = pltpu.VMEM((GROUP_ROWS, LANES), F32)
    return pl.pallas_call(
        _nsa_kernel,
        grid=(batch, nq),
        in_specs=[
            pl.BlockSpec((TQ, WIDTH), lambda b, i: (b * nq + i, wblk("nsa_q"))),
            kv_spec("nsa_k_sel"), kv_spec("nsa_v_sel"), kv_spec("nsa_k_win"), kv_spec("nsa_v_win"),
            pl.BlockSpec((TQ, WIDTH), lambda b, i: (b * nq + i, wblk("nsa_z"))),
            pl.BlockSpec((1, n_chunks, KV_WIDTH), lambda b, i: (b, 0, 0)),
            pl.BlockSpec((1, n_chunks, KV_WIDTH), lambda b, i: (b, 0, 0)),
            pl.BlockSpec((TQ, LANES), lambda b, i: (i, 0)),
            full(tile_idx.shape), full(table.shape),
            pl.BlockSpec((TQ, LANES), lambda b, i: (b * nq + i, 0)),
            full(e_t.shape), full(ov_t.shape),
        ],
        out_specs=pl.BlockSpec((TQ, WIDTH), lambda b, i: (b * nq + i, 0)),
        out_shape=jax.ShapeDtypeStruct((batch * seq, WIDTH), BF16),
        scratch_shapes=[pltpu.VMEM((KV_GROUPS, seq, MXU_DEPTH), BF16),
                        pltpu.VMEM((GROUP_ROWS, MXU_DEPTH), BF16),
                        stat, stat, stat,
                        stat, stat,
                        pltpu.VMEM((N_HEADS, T_NONE + 1, TQ, TK), F32),
                        pltpu.VMEM((N_HEADS * TQ, LANES), F32)],
        compiler_params=pltpu.CompilerParams(
            dimension_semantics=("arbitrary", "arbitrary"), vmem_limit_bytes=VMEM_LIMIT),
        name="nsa",
    )(h_main, h_main, h_main, h_main, h_main, h_main, k_cmp, v_cmp, cmp_idx, tile_idx, table, gates,
      e_t, ov_t)


def _out_kernel(alpha, ua_ref, ub_ref, ga_ref, gb_ref, x_ref, wa_ref, wb_ref, wo_ref, lg_ref, lb_ref,
                o_ref):
    ya = _dot(ua_ref[...], wa_ref[...])
    yb = _dot(ub_ref[...], wb_ref[...])
    merged = _sigmoid(ga_ref[...].astype(F32)) * ya + _sigmoid(gb_ref[...].astype(F32)) * yb
    y = alpha * x_ref[...] + _dot(merged.astype(BF16), wo_ref[...])
    mu = jnp.mean(y, axis=-1, keepdims=True)
    d = y - mu
    var = jnp.mean(d * d, axis=-1, keepdims=True)
    o_ref[...] = d * lax.rsqrt(var + LN_EPS) * lg_ref[...] + lb_ref[...]


def _out(u_a, u_b, h_main, x2, w_a, w_b, w_o, ln_g, ln_b, alpha, tm=256):
    m = x2.shape[0]
    mblk = lambda name: _MAIN_OFF[name] // D_MODEL
    const = lambda shape: pl.BlockSpec(shape, lambda i: (0, 0), pipeline_mode=pl.Buffered(1))
    return pl.pallas_call(
        functools.partial(_out_kernel, alpha),
        grid=(m // tm,),
        in_specs=[
            pl.BlockSpec((tm, WIDTH), lambda i: (i, 0)),
            pl.BlockSpec((tm, WIDTH), lambda i: (i, 0)),
            pl.BlockSpec((tm, D_MODEL), lambda i: (i, mblk("merge_a"))),
            pl.BlockSpec((tm, D_MODEL), lambda i: (i, mblk("merge_b"))),
            pl.BlockSpec((tm, D_MODEL), lambda i: (i, 0)),
            const(w_a.shape), const(w_b.shape), const(w_o.shape),
            const(ln_g.shape), const(ln_b.shape),
        ],
        out_specs=pl.BlockSpec((tm, D_MODEL), lambda i: (i, 0)),
        out_shape=jax.ShapeDtypeStruct((m, D_MODEL), F32),
        compiler_params=pltpu.CompilerParams(
            dimension_semantics=("parallel",), vmem_limit_bytes=VMEM_LIMIT),
        name="out",
    )(u_a, u_b, h_main, h_main, x2, w_a, w_b, w_o, ln_g, ln_b)


def _bucket_np(dist):
    n = np.maximum(dist, 0)
    exact = REL_BUCKETS // 2
    large = exact + (np.log(np.maximum(n, 1).astype(np.float32) / exact)
                     / math.log(REL_MAX_DIST / exact) * (REL_BUCKETS - exact)).astype(np.int32)
    return np.where(n < exact, n, np.minimum(large, REL_BUCKETS - 1)).astype(np.int32)


@functools.lru_cache(maxsize=None)
def _static_tables(seq):
    r = np.arange(TQ)[:, None]
    c = np.arange(TK)[None, :]
    tile_idx = np.stack([_bucket_np(r - c), _bucket_np(TQ + r - c), _bucket_np(2 * TQ + r - c)])
    tile_ok = np.stack([c <= r, np.ones((TQ, TK), bool), (2 * TQ + r - c) < WINDOW])
    tile_idx = np.where(tile_ok, tile_idx, MASK_BUCKET).astype(np.int32)
    t = np.arange(seq)[:, None]
    cblk = np.arange(LANES)[None, :]
    blk_end = cblk * CMP_STRIDE + CMP_LEN - 1
    n_cmp = (seq - CMP_LEN) // CMP_STRIDE + 1
    cmp_idx = np.where((blk_end <= t) & (cblk < n_cmp), _bucket_np(t - blk_end), MASK_BUCKET).astype(np.int32)
    e_t = np.where((np.arange(seq)[:, None] // SEL_LEN) == np.arange(LANES)[None, :], NEG, 0.0)
    cs = (np.arange(LANES) * CMP_STRIDE)[None, :]
    ss = (np.arange(LANES) * SEL_LEN)[:, None]
    ov_t = ((cs < ss + SEL_LEN) & (cs + CMP_LEN > ss)
            & (np.arange(LANES)[None, :] < n_cmp) & (np.arange(LANES)[:, None] < seq // SEL_LEN))
    return tile_idx, cmp_idx, e_t.astype(np.float32), ov_t.astype(np.float32)


def _layer(x, w_in, b_f, cmp_pos_k, cmp_pos_v, cmp_wk1, cmp_wk2, cmp_wv1, cmp_wv2,
           w_a, w_b, w_o, ln_g, ln_b, rel_bias, alpha):
    batch, seq, d_model = x.shape
    assert d_model == D_MODEL and seq % FOX_TQ == 0 and seq >= NEAR and REL_MAX_DIST <= TQ
    assert seq // SEL_LEN == N_SEL_ROWS and seq // CMP_STRIDE == LANES
    x2 = x.reshape(batch * seq, d_model)

    w_t = jnp.swapaxes(w_in, 0, 1)
    trows = lambda name: w_t[_REF_OFF[name][0]:_REF_OFF[name][0] + _REF_OFF[name][1]]
    n_small = N_HEADS + N_HEADS * N_BRANCHES
    w_small_t = jnp.concatenate(
        [trows("fox_f"), trows("nsa_gate"), jnp.zeros((LANES - n_small, d_model), F32)], axis=0)
    bf_row = jnp.concatenate([b_f.astype(F32), jnp.zeros((LANES - N_HEADS,), F32)]).reshape(1, LANES)

    h_main, h_small = _proj(x2, w_t, w_small_t)
    c_col, gates = _gates(h_small, bf_row, batch, seq)
    u_a = _fox(h_main, c_col, batch, seq)

    c0 = _MAIN_OFF["nsa_k_cmp"]
    r = h_main[:, c0:c0 + 2 * KV_WIDTH].reshape(batch, seq // _HALF, _CHUNK_COLS)
    k_cmp, v_cmp = _compress(r, cmp_pos_k, cmp_pos_v, cmp_wk1.astype(BF16), cmp_wk2.astype(BF16),
                             cmp_wv1.astype(BF16), cmp_wv2.astype(BF16))

    tile_idx, cmp_idx, e_t, ov_t = _static_tables(seq)
    table = jnp.concatenate(
        [rel_bias.T.astype(F32) * LOG2E, jnp.full((N_HEADS, 1), NEG, F32),
         jnp.zeros((N_HEADS, LANES - REL_BUCKETS - 1), F32)], axis=1)
    u_b = _nsa(h_main, k_cmp, v_cmp, jnp.asarray(cmp_idx), jnp.asarray(tile_idx), table, gates,
               jnp.asarray(e_t, BF16), jnp.asarray(ov_t, BF16), batch, seq)

    out = _out(u_a, u_b, h_main, x2, w_a.astype(BF16), w_b.astype(BF16), w_o.astype(BF16),
               ln_g.reshape(1, d_model), ln_b.reshape(1, d_model), alpha)
    return out.reshape(batch, seq, d_model)


def kernel(x, w_in, b_f, cmp_pos_k, cmp_pos_v, cmp_wk1, cmp_wk2, cmp_wv1, cmp_wv2,
           w_a, w_b, w_o, ln_g, ln_b, rel_bias):
    depth = w_in.shape[0]
    alpha = (2 * depth) ** 0.25
    for layer in range(depth):
        x = _layer(x, w_in[layer], b_f[layer], cmp_pos_k[layer], cmp_pos_v[layer], cmp_wk1[layer],
                   cmp_wk2[layer], cmp_wv1[layer], cmp_wv2[layer], w_a[layer], w_b[layer], w_o[layer],
                   ln_g[layer], ln_b[layer], rel_bias, alpha)
    return x
```

```python
import functools
import math

import jax
import jax.numpy as jnp
import numpy as np
from jax import lax
from jax.experimental import pallas as pl
from jax.experimental.pallas import tpu as pltpu

F32 = jnp.float32
BF16 = jnp.bfloat16

D_MODEL = 2048
HEAD_DIM = 128
N_HEADS = 8
WIDTH = N_HEADS * HEAD_DIM
KV_GROUPS = 2
HEADS_PER_GROUP = N_HEADS // KV_GROUPS
KV_WIDTH = KV_GROUPS * HEAD_DIM
N_BRANCHES = 3
CMP_LEN = 32
CMP_STRIDE = 16
CMP_HIDDEN = 256
SEL_LEN = 64
SEL_TOPK = 8
WINDOW = 512
REL_BUCKETS = 32
REL_MAX_DIST = 128
LN_EPS = 1e-5
NEG = -1e30
LOG2E = math.log2(math.e)
Q_PRESCALE = HEAD_DIM ** -0.5 * LOG2E

LANES = 128
SUBLANES = 8
MXU_DEPTH = 256
VMEM_LIMIT = 56 * 1024 * 1024

_REF_LAYOUT = (
    ("fox_q", WIDTH), ("fox_k", WIDTH), ("fox_v", WIDTH), ("fox_f", N_HEADS), ("fox_z", WIDTH),
    ("nsa_q", WIDTH), ("nsa_k_cmp", KV_WIDTH), ("nsa_v_cmp", KV_WIDTH), ("nsa_k_sel", KV_WIDTH),
    ("nsa_v_sel", KV_WIDTH), ("nsa_k_win", KV_WIDTH), ("nsa_v_win", KV_WIDTH),
    ("nsa_gate", N_HEADS * N_BRANCHES), ("nsa_z", WIDTH), ("merge_a", D_MODEL), ("merge_b", D_MODEL),
)
_REF_OFF = {}
_o = 0
for _n, _w in _REF_LAYOUT:
    _REF_OFF[_n] = (_o, _w)
    _o += _w

_MAIN_ORDER = ("fox_q", "fox_k", "fox_v", "fox_z", "nsa_q", "nsa_z", "merge_a", "merge_b",
               "nsa_k_cmp", "nsa_v_cmp", "nsa_k_sel", "nsa_v_sel", "nsa_k_win", "nsa_v_win")
_QUERY_COLS = ("fox_q", "nsa_q")
_MAIN_OFF = {}
_o = 0
for _n in _MAIN_ORDER:
    _MAIN_OFF[_n] = _o
    _o += _REF_OFF[_n][1]
MAIN_COLS = _o
GATE_LANE0 = N_HEADS


def _dot(a, b):
    return jnp.dot(a, b, preferred_element_type=F32)


def _dot_nt(a, b):
    return lax.dot_general(a, b, (((1,), (1,)), ((), ())), preferred_element_type=F32)


def _sigmoid(x):
    return 1.0 / (1.0 + jnp.exp(-x))


PROJ_TN = 512


def _proj_tiles():
    rows, is_query = [], []
    for name in _MAIN_ORDER:
        off, width = _REF_OFF[name]
        start = _MAIN_OFF[name]
        for c in range(start, start + width):
            if c % PROJ_TN == 0:
                assert (off + c - start) % SUBLANES == 0
                rows.append((off + c - start) // SUBLANES)
                is_query.append(int(name in _QUERY_COLS))
    return np.asarray(rows, np.int32), np.asarray(is_query, np.int32)


def _proj_kernel(rows_ref, isq_ref, x_ref, wt_ref, wst_ref, o_ref, os_ref, xb_ref):
    j = pl.program_id(1)

    @pl.when(j == 0)
    def _():
        xb_ref[...] = x_ref[...].astype(BF16)
        os_ref[...] = _dot_nt(xb_ref[...], wst_ref[...].astype(BF16))

    scale = jnp.where(isq_ref[j] == 1, Q_PRESCALE, 1.0)
    o_ref[...] = (_dot_nt(xb_ref[...], wt_ref[...].astype(BF16)) * scale).astype(o_ref.dtype)


def _proj(x2, w_t, w_small_t, tm=1024):
    m, k = x2.shape
    rows, is_query = _proj_tiles()
    assert MAIN_COLS % PROJ_TN == 0 and len(rows) == MAIN_COLS // PROJ_TN
    grid_spec = pltpu.PrefetchScalarGridSpec(
        num_scalar_prefetch=2,
        grid=(m // tm, len(rows)),
        in_specs=[
            pl.BlockSpec((tm, k), lambda i, j, rows, isq: (i, 0)),
            pl.BlockSpec((pl.Element(PROJ_TN), pl.Element(k)),
                         lambda i, j, rows, isq: (rows[j] * SUBLANES, 0)),
            pl.BlockSpec((LANES, k), lambda i, j, rows, isq: (0, 0)),
        ],
        out_specs=[
            pl.BlockSpec((tm, PROJ_TN), lambda i, j, rows, isq: (i, j)),
            pl.BlockSpec((tm, LANES), lambda i, j, rows, isq: (i, 0)),
        ],
        scratch_shapes=[pltpu.VMEM((tm, k), BF16)],
    )
    return pl.pallas_call(
        _proj_kernel,
        grid_spec=grid_spec,
        out_shape=[jax.ShapeDtypeStruct((m, MAIN_COLS), BF16), jax.ShapeDtypeStruct((m, LANES), F32)],
        compiler_params=pltpu.CompilerParams(
            dimension_semantics=("parallel", "arbitrary"), vmem_limit_bytes=VMEM_LIMIT),
        name="proj",
    )(jnp.asarray(rows), jnp.asarray(is_query), x2, w_t, w_small_t)


_CUM_CHUNK = 256
_N_SPLIT = 3
ONES_LANE = _N_SPLIT * N_HEADS


def _split3(x):
    hi = x.astype(BF16)
    r1 = x - hi.astype(F32)
    mid = r1.astype(BF16)
    lo = (r1 - mid.astype(F32)).astype(BF16)
    return hi, mid, lo


def _gate_kernel(hs_ref, bf_ref, c_ref, g_ref):
    hs = hs_ref[...]
    g_ref[...] = _sigmoid(hs)
    z = hs + bf_ref[...]
    logf = jnp.minimum(z, 0.0) - jnp.log1p(jnp.exp(-jnp.abs(z)))
    n = _CUM_CHUNK
    tri = (lax.broadcasted_iota(jnp.int32, (n, n), 1)
           <= lax.broadcasted_iota(jnp.int32, (n, n), 0)).astype(BF16)
    lane = lax.broadcasted_iota(jnp.int32, (n, LANES), 1)
    carry = jnp.zeros((1, LANES), F32)
    for blk in range(hs.shape[0] // n):
        hi, mid, lo = _split3(logf[blk * n:(blk + 1) * n])
        cb = _dot(tri, hi) + _dot(tri, mid) + _dot(tri, lo) + carry
        carry = cb[n - 1:n, :]
        hi, mid, lo = _split3(cb * LOG2E)
        packed = jnp.where(lane == ONES_LANE, 1.0, 0.0)
        for t, term in enumerate((hi, mid, lo)):
            shifted = term.astype(F32) if t == 0 else pltpu.roll(term.astype(F32), t * N_HEADS, 1)
            packed = jnp.where((lane >= t * N_HEADS) & (lane < (t + 1) * N_HEADS), shifted, packed)
        c_ref[blk * n:(blk + 1) * n, :] = packed.astype(c_ref.dtype)


def _gates(h_small, bf_row, batch, seq):
    return pl.pallas_call(
        _gate_kernel,
        grid=(batch,),
        in_specs=[pl.BlockSpec((seq, LANES), lambda b: (b, 0)),
                  pl.BlockSpec((1, LANES), lambda b: (0, 0))],
        out_specs=[pl.BlockSpec((seq, LANES), lambda b: (b, 0)),
                   pl.BlockSpec((seq, LANES), lambda b: (b, 0))],
        out_shape=[jax.ShapeDtypeStruct(h_small.shape, BF16), jax.ShapeDtypeStruct(h_small.shape, F32)],
        compiler_params=pltpu.CompilerParams(dimension_semantics=("parallel",)),
        name="gates",
    )(h_small, bf_row)


_HALF = CMP_LEN // 2
_CHUNK_COLS = _HALF * 2 * KV_WIDTH


def _gelu_tanh(x):
    return 0.5 * x * (1.0 + jnp.tanh(math.sqrt(2.0 / math.pi) * (x + 0.044715 * (x * x * x))))


def _compress_kernel(r_ref, pk_ref, pv_ref, w1k_ref, w2k_ref, w1v_ref, w2v_ref, kc_ref, vc_ref):
    n_chunks = r_ref.shape[1]
    for kv, (pos_ref, w1_ref, w2_ref, out_ref) in enumerate(
            ((pk_ref, w1k_ref, w2k_ref, kc_ref), (pv_ref, w1v_ref, w2v_ref, vc_ref))):
        for g in range(KV_GROUPS):
            first = jnp.zeros((n_chunks, CMP_HIDDEN), F32)
            second = jnp.zeros((n_chunks, CMP_HIDDEN), F32)
            for l in range(_HALF):
                col = l * 2 * KV_WIDTH + kv * KV_WIDTH + g * HEAD_DIM
                a = r_ref[0, :, col:col + HEAD_DIM].astype(F32)
                a1 = (a + pos_ref[l:l + 1, :]).astype(BF16)
                a2 = (a + pos_ref[_HALF + l:_HALF + l + 1, :]).astype(BF16)
                first += _dot(a1, w1_ref[l * HEAD_DIM:(l + 1) * HEAD_DIM, :])
                second += _dot(a2, w1_ref[(_HALF + l) * HEAD_DIM:(_HALF + l + 1) * HEAD_DIM, :])
            hid = first + pltpu.roll(second, n_chunks - 1, 0)
            out = _dot(_gelu_tanh(hid).astype(BF16), w2_ref[...])
            out_ref[0, :, g * HEAD_DIM:(g + 1) * HEAD_DIM] = out.astype(out_ref.dtype)


def _compress(r, pos_k, pos_v, w1k, w2k, w1v, w2v):
    batch, n_chunks, cols = r.shape
    full = lambda shape: pl.BlockSpec(shape, lambda b: (0,) * len(shape))
    return pl.pallas_call(
        _compress_kernel,
        grid=(batch,),
        in_specs=[pl.BlockSpec((1, n_chunks, cols), lambda b: (b, 0, 0)),
                  full(pos_k.shape), full(pos_v.shape),
                  full(w1k.shape), full(w2k.shape), full(w1v.shape), full(w2v.shape)],
        out_specs=[pl.BlockSpec((1, n_chunks, KV_WIDTH), lambda b: (b, 0, 0)),
                   pl.BlockSpec((1, n_chunks, KV_WIDTH), lambda b: (b, 0, 0))],
        out_shape=[jax.ShapeDtypeStruct((batch, n_chunks, KV_WIDTH), BF16),
                   jax.ShapeDtypeStruct((batch, n_chunks, KV_WIDTH), BF16)],
        compiler_params=pltpu.CompilerParams(
            dimension_semantics=("parallel",), vmem_limit_bytes=VMEM_LIMIT),
        name="compress",
    )(r, pos_k, pos_v, w1k, w2k, w1v, w2v)


def _lane_tile(x, n):
    return x if n == 1 else jnp.concatenate([x] * n, axis=1)


def _lane_fold(p):
    out = p[:, :LANES]
    for t in range(1, p.shape[1] // LANES):
        out = out + p[:, t * LANES:(t + 1) * LANES]
    return out


def _softmax_first(s, v, m_ref, l_ref, acc_ref, rows=slice(None)):
    m = jnp.broadcast_to(jnp.max(s, axis=1, keepdims=True), (s.shape[0], LANES))
    p = jnp.exp2(s - _lane_tile(m, s.shape[1] // LANES))
    m_ref[rows, :] = m
    l_ref[rows, :] = _lane_fold(p)
    acc_ref[rows, :] = _dot(p.astype(BF16), v)


def _softmax_update(s, v, row_const, m_ref, l_ref, acc_ref, rows=slice(None)):
    m_prev = m_ref[rows, :]
    m_cur = jnp.max(s, axis=1, keepdims=True)
    if row_const is not None:
        m_cur = m_cur + row_const
    m_new = jnp.maximum(m_prev, m_cur)
    alpha = jnp.exp2(m_prev - m_new)
    shift = m_new if row_const is None else m_new - row_const
    p = jnp.exp2(s - _lane_tile(shift, s.shape[1] // LANES))
    l_ref[rows, :] = alpha * l_ref[rows, :] + _lane_fold(p)
    acc_ref[rows, :] = alpha * acc_ref[rows, :] + _dot(p.astype(BF16), v)
    m_ref[rows, :] = m_new


def _softmax_finish(l_ref, acc_ref, rows=slice(None)):
    return acc_ref[rows, :] / jnp.sum(l_ref[rows, :], axis=1, keepdims=True)


def _softmax_once(s, v):
    m = jnp.max(s, axis=1, keepdims=True)
    p = jnp.exp2(s - m)
    return _dot(p.astype(BF16), v) / jnp.sum(_lane_fold(p), axis=1, keepdims=True)


FOX_TQ = 512
FOX_HALF = FOX_TQ // 2


def _fox_routing():
    pq = np.zeros((N_HEADS, LANES, LANES), np.float32)
    pk = np.zeros((N_HEADS, LANES, LANES), np.float32)
    for h in range(N_HEADS):
        for t in range(_N_SPLIT):
            pq[h, t * N_HEADS + h, t] = 1.0
            pq[h, ONES_LANE, _N_SPLIT + t] = 1.0
            pk[h, ONES_LANE, t] = 1.0
            pk[h, t * N_HEADS + h, _N_SPLIT + t] = -1.0
    return pq, pk


def _fox_kernel(q_ref, k_ref, v_ref, z_ref, cq_ref, ck_ref, pq_ref, pk_ref, o_ref,
                kaug_ref, qaug_ref, m_ref, l_ref, acc_ref):
    i = pl.program_id(1)
    tq, half = FOX_TQ, FOX_HALF
    seq = k_ref.shape[0]

    @pl.when(i == 0)
    def _():
        for h in range(N_HEADS):
            for blk in range(seq // tq):
                rows = slice(blk * tq, (blk + 1) * tq)
                kaug_ref[h, rows, :HEAD_DIM] = k_ref[rows, h * HEAD_DIM:(h + 1) * HEAD_DIM]
                kaug_ref[h, rows, HEAD_DIM:] = _dot(ck_ref[rows, :], pk_ref[h]).astype(BF16)

    causal = (lax.broadcasted_iota(jnp.int32, (half, half), 1)
              <= lax.broadcasted_iota(jnp.int32, (half, half), 0))
    diag = pl.multiple_of(i * tq, tq)
    diag2 = pl.multiple_of(i * tq + half, half)
    for h in range(N_HEADS):
        lo = h * HEAD_DIM
        rows = slice(h * tq, (h + 1) * tq)
        top = slice(h * tq, h * tq + half)
        bot = slice(h * tq + half, (h + 1) * tq)
        qaug_ref[rows, :HEAD_DIM] = q_ref[:, lo:lo + HEAD_DIM]
        qaug_ref[rows, HEAD_DIM:] = _dot(cq_ref[...], pq_ref[h]).astype(BF16)
        s_left = _dot_nt(qaug_ref[rows, :], kaug_ref[h, pl.ds(diag, half), :])
        s_right = _dot_nt(qaug_ref[bot, :], kaug_ref[h, pl.ds(diag2, half), :])
        _softmax_first(jnp.where(causal, s_left[:half], NEG), v_ref[pl.ds(diag, half), lo:lo + HEAD_DIM],
                       m_ref, l_ref, acc_ref, top)
        _softmax_first(jnp.concatenate([s_left[half:], jnp.where(causal, s_right, NEG)], axis=1),
                       v_ref[pl.ds(diag, tq), lo:lo + HEAD_DIM], m_ref, l_ref, acc_ref, bot)

    def body(j, carry):
        off = pl.multiple_of(j * tq, tq)
        for h in range(N_HEADS):
            rows = slice(h * tq, (h + 1) * tq)
            s = _dot_nt(qaug_ref[rows, :], kaug_ref[h, pl.ds(off, tq), :])
            _softmax_update(s, v_ref[pl.ds(off, tq), h * HEAD_DIM:(h + 1) * HEAD_DIM], None,
                            m_ref, l_ref, acc_ref, rows)
        return carry

    lax.fori_loop(0, i, body, 0)
    for h in range(N_HEADS):
        lo = h * HEAD_DIM
        o = _softmax_finish(l_ref, acc_ref, slice(h * tq, (h + 1) * tq))
        z = z_ref[:, lo:lo + HEAD_DIM].astype(F32)
        o_ref[:, lo:lo + HEAD_DIM] = (o * (z * _sigmoid(z))).astype(o_ref.dtype)


def _fox(h_main, c_packed, batch, seq):
    tq = FOX_TQ
    nq = seq // tq
    blk = lambda name: _MAIN_OFF[name] // WIDTH
    pq, pk = (jnp.asarray(p, BF16) for p in _fox_routing())
    full = lambda shape: pl.BlockSpec(shape, lambda b, i: (0,) * len(shape))
    stat = pltpu.VMEM((N_HEADS * tq, LANES), F32)
    return pl.pallas_call(
        _fox_kernel,
        grid=(batch, nq),
        in_specs=[
            pl.BlockSpec((tq, WIDTH), lambda b, i: (b * nq + i, blk("fox_q"))),
            pl.BlockSpec((seq, WIDTH), lambda b, i: (b, blk("fox_k"))),
            pl.BlockSpec((seq, WIDTH), lambda b, i: (b, blk("fox_v"))),
            pl.BlockSpec((tq, WIDTH), lambda b, i: (b * nq + i, blk("fox_z"))),
            pl.BlockSpec((tq, LANES), lambda b, i: (b * nq + i, 0)),
            pl.BlockSpec((seq, LANES), lambda b, i: (b, 0)),
            full(pq.shape), full(pk.shape),
        ],
        out_specs=pl.BlockSpec((tq, WIDTH), lambda b, i: (b * nq + i, 0)),
        out_shape=jax.ShapeDtypeStruct((batch * seq, WIDTH), BF16),
        scratch_shapes=[pltpu.VMEM((N_HEADS, seq, MXU_DEPTH), BF16),
                        pltpu.VMEM((N_HEADS * tq, MXU_DEPTH), BF16),
                        stat, stat, stat],
        compiler_params=pltpu.CompilerParams(
            dimension_semantics=("parallel", "arbitrary"), vmem_limit_bytes=VMEM_LIMIT),
        name="fox",
    )(h_main, h_main, h_main, h_main, c_packed, c_packed, pq, pk)


TQ = 256
TK = TQ
NEAR = WINDOW + TQ
N_NEAR = NEAR // TK
FAR_TK = 2 * TK
N_SEL_ROWS = 32
MASK_BUCKET = REL_BUCKETS
GROUP_ROWS = HEADS_PER_GROUP * TQ
T_DIAG, T_PREV, T_WIN2, T_FAR, T_NONE = range(5)
N_GATHERED_TILES = 3


def _bias_lookup(tab_ref, h, idx):
    row = jnp.broadcast_to(tab_ref[h:h + 1, :], idx.shape)
    return jnp.take_along_axis(row, idx, axis=1, mode="promise_in_bounds")


def _nsa_kernel(q_ref, ks_ref, vs_ref, kw_ref, vw_ref, z_ref, kc_ref, vc_ref, cidx_ref, tidx_ref,
                tab_ref, g_ref, et_ref, ov_ref, o_ref,
                ksaug_ref, qaug_ref, m_ref, l_ref, acc_ref, oc_ref, ow_ref, t_ref, far_ref):
    i = pl.program_id(1)
    t0 = i * TQ

    @pl.when((pl.program_id(0) == 0) & (i == 0))
    def _():
        for h in range(N_HEADS):
            for d in range(N_GATHERED_TILES):
                for half in range(TK // LANES):
                    cs = slice(half * LANES, (half + 1) * LANES)
                    t_ref[h, d, :, cs] = _bias_lookup(tab_ref, h, tidx_ref[d, :, cs])
            far = jnp.broadcast_to(tab_ref[h:h + 1, REL_BUCKETS - 1:REL_BUCKETS], (TQ, LANES))
            far_ref[h * TQ:(h + 1) * TQ, :] = far
            t_ref[h, T_FAR] = _lane_tile(far, TK // LANES)
            t_ref[h, T_NONE] = jnp.full((TQ, TK), NEG, F32)

    @pl.when(i == 0)
    def _():
        for g in range(KV_GROUPS):
            ksaug_ref[g, :, :HEAD_DIM] = ks_ref[:, g * HEAD_DIM:(g + 1) * HEAD_DIM]
            ksaug_ref[g, :, HEAD_DIM:] = et_ref[...]

    cmaskf = (lax.broadcasted_iota(jnp.int32, (TQ, LANES), 1) * CMP_STRIDE + (CMP_LEN - 1)
              <= t0 + lax.broadcasted_iota(jnp.int32, (TQ, LANES), 0)).astype(F32)
    j_t = lax.broadcasted_iota(jnp.int32, (N_SEL_ROWS, TQ), 0)
    t_t = t0 + lax.broadcasted_iota(jnp.int32, (N_SEL_ROWS, TQ), 1)
    cur_t = t_t // SEL_LEN
    forced_t = (j_t == 0) | (j_t == cur_t) | (j_t == cur_t - 1)
    valid_t = j_t * SEL_LEN <= t_t

    first_blk = jnp.maximum(i - (N_NEAR - 1), 0)
    near0 = pl.multiple_of(first_blk * TK, TK)

    def near_tile(kk, two_back):
        dist = i - (first_blk + kk)
        return jnp.where(dist == 0, T_DIAG, jnp.where(dist == 1, T_PREV,
                         jnp.where(dist == 2, two_back, T_NONE)))

    def near_bias(heads, two_back):
        return jnp.concatenate(
            [jnp.concatenate([t_ref[h, near_tile(kk, two_back)] for kk in range(N_NEAR)], axis=1)
             for h in heads], axis=0)

    for g in range(KV_GROUPS):
        glo = g * HEAD_DIM
        heads = tuple(range(g * HEADS_PER_GROUP, (g + 1) * HEADS_PER_GROUP))
        grows = slice(g * GROUP_ROWS, (g + 1) * GROUP_ROWS)
        for n, h in enumerate(heads):
            qaug_ref[g * GROUP_ROWS + n * TQ:g * GROUP_ROWS + (n + 1) * TQ, :HEAD_DIM] = (
                q_ref[:, h * HEAD_DIM:(h + 1) * HEAD_DIM])

        cidx = cidx_ref[...]
        cbias = jnp.concatenate([_bias_lookup(tab_ref, h, cidx) for h in heads], axis=0)
        sc = _dot_nt(qaug_ref[grows, :HEAD_DIM], kc_ref[0, :, glo:glo + HEAD_DIM]) + cbias
        e = jnp.exp2(sc - jnp.max(sc, axis=1, keepdims=True))
        p = e / jnp.sum(e, axis=1, keepdims=True) * jnp.concatenate([cmaskf] * HEADS_PER_GROUP, axis=0)
        oc_ref[grows, :] = _dot(p.astype(BF16), vc_ref[0, :, glo:glo + HEAD_DIM])
        psum = p[:TQ]
        for n in range(1, HEADS_PER_GROUP):
            psum = psum + p[n * TQ:(n + 1) * TQ]

        p_hi = psum.astype(BF16)
        p_lo = (psum - p_hi.astype(F32)).astype(BF16)
        imp_t = _dot_nt(ov_ref[...], p_hi) + _dot_nt(ov_ref[...], p_lo)
        x = jnp.where(valid_t, jnp.where(forced_t, -NEG, imp_t[:N_SEL_ROWS]), NEG)
        cnt = jnp.zeros((N_SEL_ROWS, TQ), F32)
        for jp in range(N_SEL_ROWS):
            row = x[jp:jp + 1, :]
            beats = (row > x) | ((row == x) & (j_t > jp))
            cnt = cnt + jnp.where(beats, 1.0, 0.0)
        unsel_t = jnp.where(cnt < SEL_TOPK, 0.0, 1.0)
        unsel_t = jnp.concatenate([unsel_t, jnp.zeros((LANES - N_SEL_ROWS, TQ), F32)], axis=0)
        unsel = unsel_t.T.astype(BF16)
        for n in range(HEADS_PER_GROUP):
            qaug_ref[g * GROUP_ROWS + n * TQ:g * GROUP_ROWS + (n + 1) * TQ, HEAD_DIM:] = unsel

        s = _dot_nt(qaug_ref[grows, :], ksaug_ref[g, pl.ds(near0, NEAR), :]) + near_bias(heads, T_FAR)
        _softmax_first(s, vs_ref[pl.ds(near0, NEAR), glo:glo + HEAD_DIM], m_ref, l_ref, acc_ref, grows)

        s = (_dot_nt(qaug_ref[grows, :HEAD_DIM], kw_ref[pl.ds(near0, NEAR), glo:glo + HEAD_DIM])
             + near_bias(heads, T_WIN2))
        ow_ref[grows, :] = _softmax_once(s, vw_ref[pl.ds(near0, NEAR), glo:glo + HEAD_DIM])

    def sel_far(off, width):
        for g in range(KV_GROUPS):
            glo = g * HEAD_DIM
            grows = slice(g * GROUP_ROWS, (g + 1) * GROUP_ROWS)
            s = _dot_nt(qaug_ref[grows, :], ksaug_ref[g, pl.ds(off, width), :])
            _softmax_update(s, vs_ref[pl.ds(off, width), glo:glo + HEAD_DIM], far_ref[grows, :],
                            m_ref, l_ref, acc_ref, grows)

    def sel_far_pair(j, carry):
        sel_far(pl.multiple_of(j * FAR_TK, FAR_TK), FAR_TK)
        return carry

    lax.fori_loop(0, first_blk // 2, sel_far_pair, 0)

    @pl.when(first_blk % 2 == 1)
    def _():
        sel_far(pl.multiple_of((first_blk - 1) * TK, TK), TK)

    for h in range(N_HEADS):
        rows = slice(h * TQ, (h + 1) * TQ)
        lo = h * HEAD_DIM
        gl = GATE_LANE0 + h * N_BRANCHES
        ob = (g_ref[:, gl:gl + 1] * oc_ref[rows, :]
              + g_ref[:, gl + 1:gl + 2] * _softmax_finish(l_ref, acc_ref, rows)
              + g_ref[:, gl + 2:gl + 3] * ow_ref[rows, :])
        z = z_ref[:, lo:lo + HEAD_DIM].astype(F32)
        o_ref[:, lo:lo + HEAD_DIM] = (ob * (z * _sigmoid(z))).astype(o_ref.dtype)


def _nsa(h_main, k_cmp, v_cmp, cmp_idx, tile_idx, table, gates, e_t, ov_t, batch, seq):
    nq = seq // TQ
    wblk = lambda name: _MAIN_OFF[name] // WIDTH
    kvblk = lambda name: _MAIN_OFF[name] // KV_WIDTH
    full = lambda shape: pl.BlockSpec(shape, lambda b, i: (0,) * len(shape))
    kv_spec = lambda name: pl.BlockSpec((seq, KV_WIDTH), lambda b, i: (b, kvblk(name)))
    n_chunks = k_cmp.shape[1]
    stat = pltpu.VMEM((N_HEADS * TQ, LANES), F32)
    return pl.pallas_call(
        _nsa_kernel,
        grid=(batch, nq),
        in_specs=[
            pl.BlockSpec((TQ, WIDTH), lambda b, i: (b * nq + i, wblk("nsa_q"))),
            kv_spec("nsa_k_sel"), kv_spec("nsa_v_sel"), kv_spec("nsa_k_win"), kv_spec("nsa_v_win"),
            pl.BlockSpec((TQ, WIDTH), lambda b, i: (b * nq + i, wblk("nsa_z"))),
            pl.BlockSpec((1, n_chunks, KV_WIDTH), lambda b, i: (b, 0, 0)),
            pl.BlockSpec((1, n_chunks, KV_WIDTH), lambda b, i: (b, 0, 0)),
            pl.BlockSpec((TQ, LANES), lambda b, i: (i, 0)),
            full(tile_idx.shape), full(table.shape),
            pl.BlockSpec((TQ, LANES), lambda b, i: (b * nq + i, 0)),
            full(e_t.shape), full(ov_t.shape),
        ],
        out_specs=pl.BlockSpec((TQ, WIDTH), lambda b, i: (b * nq + i, 0)),
        out_shape=jax.ShapeDtypeStruct((batch * seq, WIDTH), BF16),
        scratch_shapes=[pltpu.VMEM((KV_GROUPS, seq, MXU_DEPTH), BF16),
                        pltpu.VMEM((N_HEADS * TQ, MXU_DEPTH), BF16),
                        stat, stat, stat,
                        stat, stat,
                        pltpu.VMEM((N_HEADS, T_NONE + 1, TQ, TK), F32),
                        pltpu.VMEM((N_HEADS * TQ, LANES), F32)],
        compiler_params=pltpu.CompilerParams(
            dimension_semantics=("arbitrary", "arbitrary"), vmem_limit_bytes=VMEM_LIMIT),
        name="nsa",
    )(h_main, h_main, h_main, h_main, h_main, h_main, k_cmp, v_cmp, cmp_idx, tile_idx, table, gates,
      e_t, ov_t)


def _out_kernel(alpha, ua_ref, ub_ref, ga_ref, gb_ref, x_ref, wa_ref, wb_ref, wo_ref, lg_ref, lb_ref,
                o_ref):
    ya = _dot(ua_ref[...], wa_ref[...])
    yb = _dot(ub_ref[...], wb_ref[...])
    merged = _sigmoid(ga_ref[...].astype(F32)) * ya + _sigmoid(gb_ref[...].astype(F32)) * yb
    y = alpha * x_ref[...] + _dot(merged.astype(BF16), wo_ref[...])
    mu = jnp.mean(y, axis=-1, keepdims=True)
    d = y - mu
    var = jnp.mean(d * d, axis=-1, keepdims=True)
    o_ref[...] = d * lax.rsqrt(var + LN_EPS) * lg_ref[...] + lb_ref[...]


def _out(u_a, u_b, h_main, x2, w_a, w_b, w_o, ln_g, ln_b, alpha, tm=256):
    m = x2.shape[0]
    mblk = lambda name: _MAIN_OFF[name] // D_MODEL
    const = lambda shape: pl.BlockSpec(shape, lambda i: (0, 0), pipeline_mode=pl.Buffered(1))
    return pl.pallas_call(
        functools.partial(_out_kernel, alpha),
        grid=(m // tm,),
        in_specs=[
            pl.BlockSpec((tm, WIDTH), lambda i: (i, 0)),
            pl.BlockSpec((tm, WIDTH), lambda i: (i, 0)),
            pl.BlockSpec((tm, D_MODEL), lambda i: (i, mblk("merge_a"))),
            pl.BlockSpec((tm, D_MODEL), lambda i: (i, mblk("merge_b"))),
            pl.BlockSpec((tm, D_MODEL), lambda i: (i, 0)),
            const(w_a.shape), const(w_b.shape), const(w_o.shape),
            const(ln_g.shape), const(ln_b.shape),
        ],
        out_specs=pl.BlockSpec((tm, D_MODEL), lambda i: (i, 0)),
        out_shape=jax.ShapeDtypeStruct((m, D_MODEL), F32),
        compiler_params=pltpu.CompilerParams(
            dimension_semantics=("parallel",), vmem_limit_bytes=VMEM_LIMIT),
        name="out",
    )(u_a, u_b, h_main, h_main, x2, w_a, w_b, w_o, ln_g, ln_b)


def _bucket_np(dist):
    n = np.maximum(dist, 0)
    exact = REL_BUCKETS // 2
    large = exact + (np.log(np.maximum(n, 1).astype(np.float32) / exact)
                     / math.log(REL_MAX_DIST / exact) * (REL_BUCKETS - exact)).astype(np.int32)
    return np.where(n < exact, n, np.minimum(large, REL_BUCKETS - 1)).astype(np.int32)


@functools.lru_cache(maxsize=None)
def _static_tables(seq):
    r = np.arange(TQ)[:, None]
    c = np.arange(TK)[None, :]
    tile_idx = np.stack([_bucket_np(r - c), _bucket_np(TQ + r - c), _bucket_np(2 * TQ + r - c)])
    tile_ok = np.stack([c <= r, np.ones((TQ, TK), bool), (2 * TQ + r - c) < WINDOW])
    tile_idx = np.where(tile_ok, tile_idx, MASK_BUCKET).astype(np.int32)
    t = np.arange(seq)[:, None]
    cblk = np.arange(LANES)[None, :]
    blk_end = cblk * CMP_STRIDE + CMP_LEN - 1
    n_cmp = (seq - CMP_LEN) // CMP_STRIDE + 1
    cmp_idx = np.where((blk_end <= t) & (cblk < n_cmp), _bucket_np(t - blk_end), MASK_BUCKET).astype(np.int32)
    e_t = np.where((np.arange(seq)[:, None] // SEL_LEN) == np.arange(LANES)[None, :], NEG, 0.0)
    cs = (np.arange(LANES) * CMP_STRIDE)[None, :]
    ss = (np.arange(LANES) * SEL_LEN)[:, None]
    ov_t = ((cs < ss + SEL_LEN) & (cs + CMP_LEN > ss)
            & (np.arange(LANES)[None, :] < n_cmp) & (np.arange(LANES)[:, None] < seq // SEL_LEN))
    return tile_idx, cmp_idx, e_t.astype(np.float32), ov_t.astype(np.float32)


def _layer(x, w_in, b_f, cmp_pos_k, cmp_pos_v, cmp_wk1, cmp_wk2, cmp_wv1, cmp_wv2,
           w_a, w_b, w_o, ln_g, ln_b, rel_bias, alpha):
    batch, seq, d_model = x.shape
    assert d_model == D_MODEL and seq % FOX_TQ == 0 and seq >= NEAR and REL_MAX_DIST <= TQ
    assert seq // SEL_LEN == N_SEL_ROWS and seq // CMP_STRIDE == LANES
    x2 = x.reshape(batch * seq, d_model)

    w_t = jnp.swapaxes(w_in, 0, 1)
    trows = lambda name: w_t[_REF_OFF[name][0]:_REF_OFF[name][0] + _REF_OFF[name][1]]
    n_small = N_HEADS + N_HEADS * N_BRANCHES
    w_small_t = jnp.concatenate(
        [trows("fox_f"), trows("nsa_gate"), jnp.zeros((LANES - n_small, d_model), F32)], axis=0)
    bf_row = jnp.concatenate([b_f.astype(F32), jnp.zeros((LANES - N_HEADS,), F32)]).reshape(1, LANES)

    h_main, h_small = _proj(x2, w_t, w_small_t)
    c_col, gates = _gates(h_small, bf_row, batch, seq)
    u_a = _fox(h_main, c_col, batch, seq)

    c0 = _MAIN_OFF["nsa_k_cmp"]
    r = h_main[:, c0:c0 + 2 * KV_WIDTH].reshape(batch, seq // _HALF, _CHUNK_COLS)
    k_cmp, v_cmp = _compress(r, cmp_pos_k, cmp_pos_v, cmp_wk1.astype(BF16), cmp_wk2.astype(BF16),
                             cmp_wv1.astype(BF16), cmp_wv2.astype(BF16))

    tile_idx, cmp_idx, e_t, ov_t = _static_tables(seq)
    table = jnp.concatenate(
        [rel_bias.T.astype(F32) * LOG2E, jnp.full((N_HEADS, 1), NEG, F32),
         jnp.zeros((N_HEADS, LANES - REL_BUCKETS - 1), F32)], axis=1)
    u_b = _nsa(h_main, k_cmp, v_cmp, jnp.asarray(cmp_idx), jnp.asarray(tile_idx), table, gates,
               jnp.asarray(e_t, BF16), jnp.asarray(ov_t, BF16), batch, seq)

    out = _out(u_a, u_b, h_main, x2, w_a.astype(BF16), w_b.astype(BF16), w_o.astype(BF16),
               ln_g.reshape(1, d_model), ln_b.reshape(1, d_model), alpha)
    return out.reshape(batch, seq, d_model)


def kernel(x, w_in, b_f, cmp_pos_k, cmp_pos_v, cmp_wk1, cmp_wk2, cmp_wv1, cmp_wv2,
           w_a, w_b, w_o, ln_g, ln_b, rel_bias):
    depth = w_in.shape[0]
    alpha = (2 * depth) ** 0.25
    for layer in range(depth):
        x = _layer(x, w_in[layer], b_f[layer], cmp_pos_k[layer], cmp_pos_v[layer], cmp_wk1[layer],
                   cmp_wk2[layer], cmp_wv1[layer], cmp_wv2[layer], w_a[layer], w_b[layer], w_o[layer],
                   ln_g[layer], ln_b[layer], rel_bias, alpha)
    return x
```

```python
import functools
import math

import jax
import jax.numpy as jnp
import numpy as np
from jax import lax
from jax.experimental import pallas as pl
from jax.experimental.pallas import tpu as pltpu

F32 = jnp.float32
BF16 = jnp.bfloat16

D_MODEL = 2048
HEAD_DIM = 128
N_HEADS = 8
WIDTH = N_HEADS * HEAD_DIM
KV_GROUPS = 2
HEADS_PER_GROUP = N_HEADS // KV_GROUPS
KV_WIDTH = KV_GROUPS * HEAD_DIM
N_BRANCHES = 3
CMP_LEN = 32
CMP_STRIDE = 16
CMP_HIDDEN = 256
SEL_LEN = 64
SEL_TOPK = 8
WINDOW = 512
REL_BUCKETS = 32
REL_MAX_DIST = 128
LN_EPS = 1e-5
NEG = -1e30
LOG2E = math.log2(math.e)
Q_PRESCALE = HEAD_DIM ** -0.5 * LOG2E

LANES = 128
SUBLANES = 8
MXU_DEPTH = 256
VMEM_LIMIT = 56 * 1024 * 1024

_REF_LAYOUT = (
    ("fox_q", WIDTH), ("fox_k", WIDTH), ("fox_v", WIDTH), ("fox_f", N_HEADS), ("fox_z", WIDTH),
    ("nsa_q", WIDTH), ("nsa_k_cmp", KV_WIDTH), ("nsa_v_cmp", KV_WIDTH), ("nsa_k_sel", KV_WIDTH),
    ("nsa_v_sel", KV_WIDTH), ("nsa_k_win", KV_WIDTH), ("nsa_v_win", KV_WIDTH),
    ("nsa_gate", N_HEADS * N_BRANCHES), ("nsa_z", WIDTH), ("merge_a", D_MODEL), ("merge_b", D_MODEL),
)
_REF_OFF = {}
_o = 0
for _n, _w in _REF_LAYOUT:
    _REF_OFF[_n] = (_o, _w)
    _o += _w

_MAIN_ORDER = ("fox_q", "fox_k", "fox_v", "fox_z", "nsa_q", "nsa_z", "merge_a", "merge_b",
               "nsa_k_cmp", "nsa_v_cmp", "nsa_k_sel", "nsa_v_sel", "nsa_k_win", "nsa_v_win")
_QUERY_COLS = ("fox_q", "nsa_q")
_MAIN_OFF = {}
_o = 0
for _n in _MAIN_ORDER:
    _MAIN_OFF[_n] = _o
    _o += _REF_OFF[_n][1]
MAIN_COLS = _o
GATE_LANE0 = N_HEADS


def _dot(a, b):
    return jnp.dot(a, b, preferred_element_type=F32)


def _dot_nt(a, b):
    return lax.dot_general(a, b, (((1,), (1,)), ((), ())), preferred_element_type=F32)


def _sigmoid(x):
    return 1.0 / (1.0 + jnp.exp(-x))


PROJ_TN = 512


def _proj_tiles():
    rows, is_query = [], []
    for name in _MAIN_ORDER:
        off, width = _REF_OFF[name]
        start = _MAIN_OFF[name]
        for c in range(start, start + width):
            if c % PROJ_TN == 0:
                assert (off + c - start) % SUBLANES == 0
                rows.append((off + c - start) // SUBLANES)
                is_query.append(int(name in _QUERY_COLS))
    return np.asarray(rows, np.int32), np.asarray(is_query, np.int32)


PROJ_ROWS = 4096
PROJ_RB = 512


def _proj_kernel(rows_ref, isq_ref, x_hbm, wt_ref, wst_ref, o_ref, os_ref,
                 xb_ref, stage_ref, wb_ref, sem):
    i = pl.program_id(0)
    j = pl.program_id(1)
    n_blocks = PROJ_ROWS // PROJ_RB
    wb_ref[...] = wt_ref[...].astype(BF16)
    scale = jnp.where(isq_ref[j] == 1, Q_PRESCALE, 1.0)

    def x_copy(r, slot):
        row0 = pl.multiple_of(i * PROJ_ROWS + r * PROJ_RB, PROJ_RB)
        return pltpu.make_async_copy(x_hbm.at[pl.ds(row0, PROJ_RB), :], stage_ref.at[slot], sem.at[slot])

    def block(r):
        rows = slice(r * PROJ_RB, (r + 1) * PROJ_RB)
        o_ref[rows, :] = (_dot_nt(xb_ref[rows, :], wb_ref[...]) * scale).astype(o_ref.dtype)

    @pl.when(j == 0)
    def _():
        wsb = wst_ref[...].astype(BF16)
        x_copy(0, 0).start()
        for r in range(n_blocks):
            if r + 1 < n_blocks:
                x_copy(r + 1, (r + 1) % 2).start()
            x_copy(r, r % 2).wait()
            rows = slice(r * PROJ_RB, (r + 1) * PROJ_RB)
            xb_ref[rows, :] = stage_ref[r % 2].astype(BF16)
            os_ref[rows, :] = _dot_nt(xb_ref[rows, :], wsb)
            block(r)

    @pl.when(j > 0)
    def _():
        for r in range(n_blocks):
            block(r)


def _proj(x2, w_t, w_small_t):
    m, k = x2.shape
    rows, is_query = _proj_tiles()
    assert MAIN_COLS % PROJ_TN == 0 and len(rows) == MAIN_COLS // PROJ_TN and m % PROJ_ROWS == 0
    grid_spec = pltpu.PrefetchScalarGridSpec(
        num_scalar_prefetch=2,
        grid=(m // PROJ_ROWS, len(rows)),
        in_specs=[
            pl.BlockSpec(memory_space=pl.ANY),
            pl.BlockSpec((pl.Element(PROJ_TN), pl.Element(k)),
                         lambda i, j, rows, isq: (rows[j] * SUBLANES, 0)),
            pl.BlockSpec((LANES, k), lambda i, j, rows, isq: (0, 0)),
        ],
        out_specs=[
            pl.BlockSpec((PROJ_ROWS, PROJ_TN), lambda i, j, rows, isq: (i, j)),
            pl.BlockSpec((PROJ_ROWS, LANES), lambda i, j, rows, isq: (i, 0)),
        ],
        scratch_shapes=[pltpu.VMEM((PROJ_ROWS, k), BF16),
                        pltpu.VMEM((2, PROJ_RB, k), F32),
                        pltpu.VMEM((PROJ_TN, k), BF16),
                        pltpu.SemaphoreType.DMA((2,))],
    )
    return pl.pallas_call(
        _proj_kernel,
        grid_spec=grid_spec,
        out_shape=[jax.ShapeDtypeStruct((m, MAIN_COLS), BF16), jax.ShapeDtypeStruct((m, LANES), F32)],
        compiler_params=pltpu.CompilerParams(
            dimension_semantics=("parallel", "arbitrary"), vmem_limit_bytes=VMEM_LIMIT),
        name="proj",
    )(jnp.asarray(rows), jnp.asarray(is_query), x2, w_t, w_small_t)


_CUM_CHUNK = 256
_N_SPLIT = 3
ONES_LANE = _N_SPLIT * N_HEADS


def _split3(x):
    hi = x.astype(BF16)
    r1 = x - hi.astype(F32)
    mid = r1.astype(BF16)
    lo = (r1 - mid.astype(F32)).astype(BF16)
    return hi, mid, lo


def _gate_kernel(hs_ref, bf_ref, c_ref, g_ref):
    hs = hs_ref[...]
    g_ref[...] = _sigmoid(hs)
    z = hs + bf_ref[...]
    logf = jnp.minimum(z, 0.0) - jnp.log1p(jnp.exp(-jnp.abs(z)))
    n = _CUM_CHUNK
    tri = (lax.broadcasted_iota(jnp.int32, (n, n), 1)
           <= lax.broadcasted_iota(jnp.int32, (n, n), 0)).astype(BF16)
    lane = lax.broadcasted_iota(jnp.int32, (n, LANES), 1)
    carry = jnp.zeros((1, LANES), F32)
    for blk in range(hs.shape[0] // n):
        hi, mid, lo = _split3(logf[blk * n:(blk + 1) * n])
        cb = _dot(tri, hi) + _dot(tri, mid) + _dot(tri, lo) + carry
        carry = cb[n - 1:n, :]
        hi, mid, lo = _split3(cb * LOG2E)
        packed = jnp.where(lane == ONES_LANE, 1.0, 0.0)
        for t, term in enumerate((hi, mid, lo)):
            shifted = term.astype(F32) if t == 0 else pltpu.roll(term.astype(F32), t * N_HEADS, 1)
            packed = jnp.where((lane >= t * N_HEADS) & (lane < (t + 1) * N_HEADS), shifted, packed)
        c_ref[blk * n:(blk + 1) * n, :] = packed.astype(c_ref.dtype)


def _gates(h_small, bf_row, batch, seq):
    return pl.pallas_call(
        _gate_kernel,
        grid=(batch,),
        in_specs=[pl.BlockSpec((seq, LANES), lambda b: (b, 0)),
                  pl.BlockSpec((1, LANES), lambda b: (0, 0))],
        out_specs=[pl.BlockSpec((seq, LANES), lambda b: (b, 0)),
                   pl.BlockSpec((seq, LANES), lambda b: (b, 0))],
        out_shape=[jax.ShapeDtypeStruct(h_small.shape, BF16), jax.ShapeDtypeStruct(h_small.shape, F32)],
        compiler_params=pltpu.CompilerParams(dimension_semantics=("parallel",)),
        name="gates",
    )(h_small, bf_row)


_HALF = CMP_LEN // 2
_CHUNK_COLS = _HALF * 2 * KV_WIDTH


def _gelu_tanh(x):
    return 0.5 * x * (1.0 + jnp.tanh(math.sqrt(2.0 / math.pi) * (x + 0.044715 * (x * x * x))))


def _compress_kernel(r_ref, pk_ref, pv_ref, w1k_ref, w2k_ref, w1v_ref, w2v_ref, kc_ref, vc_ref):
    n_chunks = r_ref.shape[1]
    for kv, (pos_ref, w1_ref, w2_ref, out_ref) in enumerate(
            ((pk_ref, w1k_ref, w2k_ref, kc_ref), (pv_ref, w1v_ref, w2v_ref, vc_ref))):
        for g in range(KV_GROUPS):
            first = jnp.zeros((n_chunks, CMP_HIDDEN), F32)
            second = jnp.zeros((n_chunks, CMP_HIDDEN), F32)
            for l in range(_HALF):
                col = l * 2 * KV_WIDTH + kv * KV_WIDTH + g * HEAD_DIM
                a = r_ref[0, :, col:col + HEAD_DIM].astype(F32)
                a1 = (a + pos_ref[l:l + 1, :]).astype(BF16)
                a2 = (a + pos_ref[_HALF + l:_HALF + l + 1, :]).astype(BF16)
                first += _dot(a1, w1_ref[l * HEAD_DIM:(l + 1) * HEAD_DIM, :])
                second += _dot(a2, w1_ref[(_HALF + l) * HEAD_DIM:(_HALF + l + 1) * HEAD_DIM, :])
            hid = first + pltpu.roll(second, n_chunks - 1, 0)
            out = _dot(_gelu_tanh(hid).astype(BF16), w2_ref[...])
            out_ref[0, :, g * HEAD_DIM:(g + 1) * HEAD_DIM] = out.astype(out_ref.dtype)


def _compress(r, pos_k, pos_v, w1k, w2k, w1v, w2v):
    batch, n_chunks, cols = r.shape
    full = lambda shape: pl.BlockSpec(shape, lambda b: (0,) * len(shape))
    return pl.pallas_call(
        _compress_kernel,
        grid=(batch,),
        in_specs=[pl.BlockSpec((1, n_chunks, cols), lambda b: (b, 0, 0)),
                  full(pos_k.shape), full(pos_v.shape),
                  full(w1k.shape), full(w2k.shape), full(w1v.shape), full(w2v.shape)],
        out_specs=[pl.BlockSpec((1, n_chunks, KV_WIDTH), lambda b: (b, 0, 0)),
                   pl.BlockSpec((1, n_chunks, KV_WIDTH), lambda b: (b, 0, 0))],
        out_shape=[jax.ShapeDtypeStruct((batch, n_chunks, KV_WIDTH), BF16),
                   jax.ShapeDtypeStruct((batch, n_chunks, KV_WIDTH), BF16)],
        compiler_params=pltpu.CompilerParams(
            dimension_semantics=("parallel",), vmem_limit_bytes=VMEM_LIMIT),
        name="compress",
    )(r, pos_k, pos_v, w1k, w2k, w1v, w2v)


def _lane_tile(x, n):
    return x if n == 1 else jnp.concatenate([x] * n, axis=1)


def _lane_fold(p):
    out = p[:, :LANES]
    for t in range(1, p.shape[1] // LANES):
        out = out + p[:, t * LANES:(t + 1) * LANES]
    return out


def _softmax_first(s, v, m_ref, l_ref, acc_ref, rows=slice(None)):
    m = jnp.broadcast_to(jnp.max(s, axis=1, keepdims=True), (s.shape[0], LANES))
    p = jnp.exp2(s - _lane_tile(m, s.shape[1] // LANES))
    m_ref[rows, :] = m
    l_ref[rows, :] = _lane_fold(p)
    acc_ref[rows, :] = _dot(p.astype(BF16), v)


def _softmax_update(s, v, row_const, m_ref, l_ref, acc_ref, rows=slice(None)):
    m_prev = m_ref[rows, :]
    m_cur = jnp.max(s, axis=1, keepdims=True)
    if row_const is not None:
        m_cur = m_cur + row_const
    m_new = jnp.maximum(m_prev, m_cur)
    alpha = jnp.exp2(m_prev - m_new)
    shift = m_new if row_const is None else m_new - row_const
    p = jnp.exp2(s - _lane_tile(shift, s.shape[1] // LANES))
    l_ref[rows, :] = alpha * l_ref[rows, :] + _lane_fold(p)
    acc_ref[rows, :] = alpha * acc_ref[rows, :] + _dot(p.astype(BF16), v)
    m_ref[rows, :] = m_new


def _softmax_finish(l_ref, acc_ref, rows=slice(None)):
    return acc_ref[rows, :] / jnp.sum(l_ref[rows, :], axis=1, keepdims=True)


def _softmax_once(s, v):
    m = jnp.max(s, axis=1, keepdims=True)
    p = jnp.exp2(s - m)
    return _dot(p.astype(BF16), v) / jnp.sum(_lane_fold(p), axis=1, keepdims=True)


FOX_TQ = 512
FOX_HALF = FOX_TQ // 2


def _fox_routing():
    pq = np.zeros((N_HEADS, LANES, LANES), np.float32)
    pk = np.zeros((N_HEADS, LANES, LANES), np.float32)
    for h in range(N_HEADS):
        for t in range(_N_SPLIT):
            pq[h, t * N_HEADS + h, t] = 1.0
            pq[h, ONES_LANE, _N_SPLIT + t] = 1.0
            pk[h, ONES_LANE, t] = 1.0
            pk[h, t * N_HEADS + h, _N_SPLIT + t] = -1.0
    return pq, pk


def _fox_kernel(q_ref, k_ref, v_ref, z_ref, cq_ref, ck_ref, pq_ref, pk_ref, o_ref,
                kaug_ref, qaug_ref, m_ref, l_ref, acc_ref):
    i = pl.program_id(1)
    tq, half = FOX_TQ, FOX_HALF
    seq = k_ref.shape[0]

    @pl.when(i == 0)
    def _():
        for h in range(N_HEADS):
            for blk in range(seq // tq):
                rows = slice(blk * tq, (blk + 1) * tq)
                kaug_ref[h, rows, :HEAD_DIM] = k_ref[rows, h * HEAD_DIM:(h + 1) * HEAD_DIM]
                kaug_ref[h, rows, HEAD_DIM:] = _dot(ck_ref[rows, :], pk_ref[h]).astype(BF16)

    causal = (lax.broadcasted_iota(jnp.int32, (half, half), 1)
              <= lax.broadcasted_iota(jnp.int32, (half, half), 0))
    diag = pl.multiple_of(i * tq, tq)
    diag2 = pl.multiple_of(i * tq + half, half)
    for h in range(N_HEADS):
        lo = h * HEAD_DIM
        rows = slice(h * tq, (h + 1) * tq)
        top = slice(h * tq, h * tq + half)
        bot = slice(h * tq + half, (h + 1) * tq)
        qaug_ref[rows, :HEAD_DIM] = q_ref[:, lo:lo + HEAD_DIM]
        qaug_ref[rows, HEAD_DIM:] = _dot(cq_ref[...], pq_ref[h]).astype(BF16)
        s_left = _dot_nt(qaug_ref[rows, :], kaug_ref[h, pl.ds(diag, half), :])
        s_right = _dot_nt(qaug_ref[bot, :], kaug_ref[h, pl.ds(diag2, half), :])
        _softmax_first(jnp.where(causal, s_left[:half], NEG), v_ref[pl.ds(diag, half), lo:lo + HEAD_DIM],
                       m_ref, l_ref, acc_ref, top)
        _softmax_first(jnp.concatenate([s_left[half:], jnp.where(causal, s_right, NEG)], axis=1),
                       v_ref[pl.ds(diag, tq), lo:lo + HEAD_DIM], m_ref, l_ref, acc_ref, bot)

    def body(j, carry):
        off = pl.multiple_of(j * tq, tq)
        for h in range(N_HEADS):
            rows = slice(h * tq, (h + 1) * tq)
            s = _dot_nt(qaug_ref[rows, :], kaug_ref[h, pl.ds(off, tq), :])
            _softmax_update(s, v_ref[pl.ds(off, tq), h * HEAD_DIM:(h + 1) * HEAD_DIM], None,
                            m_ref, l_ref, acc_ref, rows)
        return carry

    lax.fori_loop(0, i, body, 0)
    for h in range(N_HEADS):
        lo = h * HEAD_DIM
        o = _softmax_finish(l_ref, acc_ref, slice(h * tq, (h + 1) * tq))
        z = z_ref[:, lo:lo + HEAD_DIM].astype(F32)
        o_ref[:, lo:lo + HEAD_DIM] = (o * (z * _sigmoid(z))).astype(o_ref.dtype)


def _fox(h_main, c_packed, batch, seq):
    tq = FOX_TQ
    nq = seq // tq
    blk = lambda name: _MAIN_OFF[name] // WIDTH
    pq, pk = (jnp.asarray(p, BF16) for p in _fox_routing())
    full = lambda shape: pl.BlockSpec(shape, lambda b, i: (0,) * len(shape))
    stat = pltpu.VMEM((N_HEADS * tq, LANES), F32)
    return pl.pallas_call(
        _fox_kernel,
        grid=(batch, nq),
        in_specs=[
            pl.BlockSpec((tq, WIDTH), lambda b, i: (b * nq + i, blk("fox_q"))),
            pl.BlockSpec((seq, WIDTH), lambda b, i: (b, blk("fox_k"))),
            pl.BlockSpec((seq, WIDTH), lambda b, i: (b, blk("fox_v"))),
            pl.BlockSpec((tq, WIDTH), lambda b, i: (b * nq + i, blk("fox_z"))),
            pl.BlockSpec((tq, LANES), lambda b, i: (b * nq + i, 0)),
            pl.BlockSpec((seq, LANES), lambda b, i: (b, 0)),
            full(pq.shape), full(pk.shape),
        ],
        out_specs=pl.BlockSpec((tq, WIDTH), lambda b, i: (b * nq + i, 0)),
        out_shape=jax.ShapeDtypeStruct((batch * seq, WIDTH), BF16),
        scratch_shapes=[pltpu.VMEM((N_HEADS, seq, MXU_DEPTH), BF16),
                        pltpu.VMEM((N_HEADS * tq, MXU_DEPTH), BF16),
                        stat, stat, stat],
        compiler_params=pltpu.CompilerParams(
            dimension_semantics=("parallel", "arbitrary"), vmem_limit_bytes=VMEM_LIMIT),
        name="fox",
    )(h_main, h_main, h_main, h_main, c_packed, c_packed, pq, pk)


TQ = 256
TK = TQ
NEAR = WINDOW + TQ
N_NEAR = NEAR // TK
FAR_TK = 2 * TK
N_SEL_ROWS = 32
MASK_BUCKET = REL_BUCKETS
GROUP_ROWS = HEADS_PER_GROUP * TQ
T_DIAG, T_PREV, T_WIN2, T_FAR, T_NONE = range(5)
N_GATHERED_TILES = 3


def _bias_lookup(tab_ref, h, idx):
    row = jnp.broadcast_to(tab_ref[h:h + 1, :], idx.shape)
    return jnp.take_along_axis(row, idx, axis=1, mode="promise_in_bounds")


def _nsa_kernel(q_ref, ks_ref, vs_ref, kw_ref, vw_ref, z_ref, kc_ref, vc_ref, cidx_ref, tidx_ref,
                tab_ref, g_ref, et_ref, ov_ref, o_ref,
                ksaug_ref, qaug_ref, m_ref, l_ref, acc_ref, oc_ref, ow_ref, t_ref, far_ref):
    i = pl.program_id(1)
    t0 = i * TQ

    @pl.when((pl.program_id(0) == 0) & (i == 0))
    def _():
        for h in range(N_HEADS):
            for d in range(N_GATHERED_TILES):
                for half in range(TK // LANES):
                    cs = slice(half * LANES, (half + 1) * LANES)
                    t_ref[h, d, :, cs] = _bias_lookup(tab_ref, h, tidx_ref[d, :, cs])
            far = jnp.broadcast_to(tab_ref[h:h + 1, REL_BUCKETS - 1:REL_BUCKETS], (TQ, LANES))
            far_ref[h * TQ:(h + 1) * TQ, :] = far
            t_ref[h, T_FAR] = _lane_tile(far, TK // LANES)
            t_ref[h, T_NONE] = jnp.full((TQ, TK), NEG, F32)

    @pl.when(i == 0)
    def _():
        for g in range(KV_GROUPS):
            ksaug_ref[g, :, :HEAD_DIM] = ks_ref[:, g * HEAD_DIM:(g + 1) * HEAD_DIM]
            ksaug_ref[g, :, HEAD_DIM:] = et_ref[...]

    cmaskf = (lax.broadcasted_iota(jnp.int32, (TQ, LANES), 1) * CMP_STRIDE + (CMP_LEN - 1)
              <= t0 + lax.broadcasted_iota(jnp.int32, (TQ, LANES), 0)).astype(F32)
    j_t = lax.broadcasted_iota(jnp.int32, (N_SEL_ROWS, TQ), 0)
    t_t = t0 + lax.broadcasted_iota(jnp.int32, (N_SEL_ROWS, TQ), 1)
    cur_t = t_t // SEL_LEN
    forced_t = (j_t == 0) | (j_t == cur_t) | (j_t == cur_t - 1)
    valid_t = j_t * SEL_LEN <= t_t

    first_blk = jnp.maximum(i - (N_NEAR - 1), 0)
    near0 = pl.multiple_of(first_blk * TK, TK)

    def near_tile(kk, two_back):
        dist = i - (first_blk + kk)
        return jnp.where(dist == 0, T_DIAG, jnp.where(dist == 1, T_PREV,
                         jnp.where(dist == 2, two_back, T_NONE)))

    def near_bias(heads, two_back):
        return jnp.concatenate(
            [jnp.concatenate([t_ref[h, near_tile(kk, two_back)] for kk in range(N_NEAR)], axis=1)
             for h in heads], axis=0)

    for g in range(KV_GROUPS):
        glo = g * HEAD_DIM
        heads = tuple(range(g * HEADS_PER_GROUP, (g + 1) * HEADS_PER_GROUP))
        grows = slice(g * GROUP_ROWS, (g + 1) * GROUP_ROWS)
        for n, h in enumerate(heads):
            qaug_ref[g * GROUP_ROWS + n * TQ:g * GROUP_ROWS + (n + 1) * TQ, :HEAD_DIM] = (
                q_ref[:, h * HEAD_DIM:(h + 1) * HEAD_DIM])

        cidx = cidx_ref[...]
        cbias = jnp.concatenate([_bias_lookup(tab_ref, h, cidx) for h in heads], axis=0)
        sc = _dot_nt(qaug_ref[grows, :HEAD_DIM], kc_ref[0, :, glo:glo + HEAD_DIM]) + cbias
        e = jnp.exp2(sc - jnp.max(sc, axis=1, keepdims=True))
        p = e / jnp.sum(e, axis=1, keepdims=True) * jnp.concatenate([cmaskf] * HEADS_PER_GROUP, axis=0)
        oc_ref[grows, :] = _dot(p.astype(BF16), vc_ref[0, :, glo:glo + HEAD_DIM])
        psum = p[:TQ]
        for n in range(1, HEADS_PER_GROUP):
            psum = psum + p[n * TQ:(n + 1) * TQ]

        p_hi = psum.astype(BF16)
        p_lo = (psum - p_hi.astype(F32)).astype(BF16)
        imp_t = _dot_nt(ov_ref[...], p_hi) + _dot_nt(ov_ref[...], p_lo)
        x = jnp.where(valid_t, jnp.where(forced_t, -NEG, imp_t[:N_SEL_ROWS]), NEG)
        cnt = jnp.zeros((N_SEL_ROWS, TQ), F32)
        for jp in range(N_SEL_ROWS):
            row = x[jp:jp + 1, :]
            beats = (row > x) | ((row == x) & (j_t > jp))
            cnt = cnt + jnp.where(beats, 1.0, 0.0)
        unsel_t = jnp.where(cnt < SEL_TOPK, 0.0, 1.0)
        unsel_t = jnp.concatenate([unsel_t, jnp.zeros((LANES - N_SEL_ROWS, TQ), F32)], axis=0)
        unsel = unsel_t.T.astype(BF16)
        for n in range(HEADS_PER_GROUP):
            qaug_ref[g * GROUP_ROWS + n * TQ:g * GROUP_ROWS + (n + 1) * TQ, HEAD_DIM:] = unsel

        s = _dot_nt(qaug_ref[grows, :], ksaug_ref[g, pl.ds(near0, NEAR), :]) + near_bias(heads, T_FAR)
        _softmax_first(s, vs_ref[pl.ds(near0, NEAR), glo:glo + HEAD_DIM], m_ref, l_ref, acc_ref, grows)

        s = (_dot_nt(qaug_ref[grows, :HEAD_DIM], kw_ref[pl.ds(near0, NEAR), glo:glo + HEAD_DIM])
             + near_bias(heads, T_WIN2))
        ow_ref[grows, :] = _softmax_once(s, vw_ref[pl.ds(near0, NEAR), glo:glo + HEAD_DIM])

    def sel_far(off, width):
        for g in range(KV_GROUPS):
            glo = g * HEAD_DIM
            grows = slice(g * GROUP_ROWS, (g + 1) * GROUP_ROWS)
            s = _dot_nt(qaug_ref[grows, :], ksaug_ref[g, pl.ds(off, width), :])
            _softmax_update(s, vs_ref[pl.ds(off, width), glo:glo + HEAD_DIM], far_ref[grows, :],
                            m_ref, l_ref, acc_ref, grows)

    def sel_far_pair(j, carry):
        sel_far(pl.multiple_of(j * FAR_TK, FAR_TK), FAR_TK)
        return carry

    lax.fori_loop(0, first_blk // 2, sel_far_pair, 0)

    @pl.when(first_blk % 2 == 1)
    def _():
        sel_far(pl.multiple_of((first_blk - 1) * TK, TK), TK)

    for h in range(N_HEADS):
        rows = slice(h * TQ, (h + 1) * TQ)
        lo = h * HEAD_DIM
        gl = GATE_LANE0 + h * N_BRANCHES
        ob = (g_ref[:, gl:gl + 1] * oc_ref[rows, :]
              + g_ref[:, gl + 1:gl + 2] * _softmax_finish(l_ref, acc_ref, rows)
              + g_ref[:, gl + 2:gl + 3] * ow_ref[rows, :])
        z = z_ref[:, lo:lo + HEAD_DIM].astype(F32)
        o_ref[:, lo:lo + HEAD_DIM] = (ob * (z * _sigmoid(z))).astype(o_ref.dtype)


def _nsa(h_main, k_cmp, v_cmp, cmp_idx, tile_idx, table, gates, e_t, ov_t, batch, seq):
    nq = seq // TQ
    wblk = lambda name: _MAIN_OFF[name] // WIDTH
    kvblk = lambda name: _MAIN_OFF[name] // KV_WIDTH
    full = lambda shape: pl.BlockSpec(shape, lambda b, i: (0,) * len(shape))
    kv_spec = lambda name: pl.BlockSpec((seq, KV_WIDTH), lambda b, i: (b, kvblk(name)))
    n_chunks = k_cmp.shape[1]
    stat = pltpu.VMEM((N_HEADS * TQ, LANES), F32)
    return pl.pallas_call(
        _nsa_kernel,
        grid=(batch, nq),
        in_specs=[
            pl.BlockSpec((TQ, WIDTH), lambda b, i: (b * nq + i, wblk("nsa_q"))),
            kv_spec("nsa_k_sel"), kv_spec("nsa_v_sel"), kv_spec("nsa_k_win"), kv_spec("nsa_v_win"),
            pl.BlockSpec((TQ, WIDTH), lambda b, i: (b * nq + i, wblk("nsa_z"))),
            pl.BlockSpec((1, n_chunks, KV_WIDTH), lambda b, i: (b, 0, 0)),
            pl.BlockSpec((1, n_chunks, KV_WIDTH), lambda b, i: (b, 0, 0)),
            pl.BlockSpec((TQ, LANES), lambda b, i: (i, 0)),
            full(tile_idx.shape), full(table.shape),
            pl.BlockSpec((TQ, LANES), lambda b, i: (b * nq + i, 0)),
            full(e_t.shape), full(ov_t.shape),
        ],
        out_specs=pl.BlockSpec((TQ, WIDTH), lambda b, i: (b * nq + i, 0)),
        out_shape=jax.ShapeDtypeStruct((batch * seq, WIDTH), BF16),
        scratch_shapes=[pltpu.VMEM((KV_GROUPS, seq, MXU_DEPTH), BF16),
                        pltpu.VMEM((N_HEADS * TQ, MXU_DEPTH), BF16),
                        stat, stat, stat,
                        stat, stat,
                        pltpu.VMEM((N_HEADS, T_NONE + 1, TQ, TK), F32),
                        pltpu.VMEM((N_HEADS * TQ, LANES), F32)],
        compiler_params=pltpu.CompilerParams(
            dimension_semantics=("arbitrary", "arbitrary"), vmem_limit_bytes=VMEM_LIMIT),
        name="nsa",
    )(h_main, h_main, h_main, h_main, h_main, h_main, k_cmp, v_cmp, cmp_idx, tile_idx, table, gates,
      e_t, ov_t)


def _out_kernel(alpha, ua_ref, ub_ref, ga_ref, gb_ref, x_ref, wa_ref, wb_ref, wo_ref, lg_ref, lb_ref,
                o_ref):
    ya = _dot(ua_ref[...], wa_ref[...])
    yb = _dot(ub_ref[...], wb_ref[...])
    merged = _sigmoid(ga_ref[...].astype(F32)) * ya + _sigmoid(gb_ref[...].astype(F32)) * yb
    y = alpha * x_ref[...] + _dot(merged.astype(BF16), wo_ref[...])
    mu = jnp.mean(y, axis=-1, keepdims=True)
    d = y - mu
    var = jnp.mean(d * d, axis=-1, keepdims=True)
    o_ref[...] = d * lax.rsqrt(var + LN_EPS) * lg_ref[...] + lb_ref[...]


def _out(u_a, u_b, h_main, x2, w_a, w_b, w_o, ln_g, ln_b, alpha, tm=256):
    m = x2.shape[0]
    mblk = lambda name: _MAIN_OFF[name] // D_MODEL
    const = lambda shape: pl.BlockSpec(shape, lambda i: (0, 0), pipeline_mode=pl.Buffered(1))
    return pl.pallas_call(
        functools.partial(_out_kernel, alpha),
        grid=(m // tm,),
        in_specs=[
            pl.BlockSpec((tm, WIDTH), lambda i: (i, 0)),
            pl.BlockSpec((tm, WIDTH), lambda i: (i, 0)),
            pl.BlockSpec((tm, D_MODEL), lambda i: (i, mblk("merge_a"))),
            pl.BlockSpec((tm, D_MODEL), lambda i: (i, mblk("merge_b"))),
            pl.BlockSpec((tm, D_MODEL), lambda i: (i, 0)),
            const(w_a.shape), const(w_b.shape), const(w_o.shape),
            const(ln_g.shape), const(ln_b.shape),
        ],
        out_specs=pl.BlockSpec((tm, D_MODEL), lambda i: (i, 0)),
        out_shape=jax.ShapeDtypeStruct((m, D_MODEL), F32),
        compiler_params=pltpu.CompilerParams(
            dimension_semantics=("parallel",), vmem_limit_bytes=VMEM_LIMIT),
        name="out",
    )(u_a, u_b, h_main, h_main, x2, w_a, w_b, w_o, ln_g, ln_b)


def _bucket_np(dist):
    n = np.maximum(dist, 0)
    exact = REL_BUCKETS // 2
    large = exact + (np.log(np.maximum(n, 1).astype(np.float32) / exact)
                     / math.log(REL_MAX_DIST / exact) * (REL_BUCKETS - exact)).astype(np.int32)
    return np.where(n < exact, n, np.minimum(large, REL_BUCKETS - 1)).astype(np.int32)


@functools.lru_cache(maxsize=None)
def _static_tables(seq):
    r = np.arange(TQ)[:, None]
    c = np.arange(TK)[None, :]
    tile_idx = np.stack([_bucket_np(r - c), _bucket_np(TQ + r - c), _bucket_np(2 * TQ + r - c)])
    tile_ok = np.stack([c <= r, np.ones((TQ, TK), bool), (2 * TQ + r - c) < WINDOW])
    tile_idx = np.where(tile_ok, tile_idx, MASK_BUCKET).astype(np.int32)
    t = np.arange(seq)[:, None]
    cblk = np.arange(LANES)[None, :]
    blk_end = cblk * CMP_STRIDE + CMP_LEN - 1
    n_cmp = (seq - CMP_LEN) // CMP_STRIDE + 1
    cmp_idx = np.where((blk_end <= t) & (cblk < n_cmp), _bucket_np(t - blk_end), MASK_BUCKET).astype(np.int32)
    e_t = np.where((np.arange(seq)[:, None] // SEL_LEN) == np.arange(LANES)[None, :], NEG, 0.0)
    cs = (np.arange(LANES) * CMP_STRIDE)[None, :]
    ss = (np.arange(LANES) * SEL_LEN)[:, None]
    ov_t = ((cs < ss + SEL_LEN) & (cs + CMP_LEN > ss)
            & (np.arange(LANES)[None, :] < n_cmp) & (np.arange(LANES)[:, None] < seq // SEL_LEN))
    return tile_idx, cmp_idx, e_t.astype(np.float32), ov_t.astype(np.float32)


def _layer(x, w_in, b_f, cmp_pos_k, cmp_pos_v, cmp_wk1, cmp_wk2, cmp_wv1, cmp_wv2,
           w_a, w_b, w_o, ln_g, ln_b, rel_bias, alpha):
    batch, seq, d_model = x.shape
    assert d_model == D_MODEL and seq % FOX_TQ == 0 and seq >= NEAR and REL_MAX_DIST <= TQ
    assert seq // SEL_LEN == N_SEL_ROWS and seq // CMP_STRIDE == LANES
    x2 = x.reshape(batch * seq, d_model)

    w_t = jnp.swapaxes(w_in, 0, 1)
    trows = lambda name: w_t[_REF_OFF[name][0]:_REF_OFF[name][0] + _REF_OFF[name][1]]
    n_small = N_HEADS + N_HEADS * N_BRANCHES
    w_small_t = jnp.concatenate(
        [trows("fox_f"), trows("nsa_gate"), jnp.zeros((LANES - n_small, d_model), F32)], axis=0)
    bf_row = jnp.concatenate([b_f.astype(F32), jnp.zeros((LANES - N_HEADS,), F32)]).reshape(1, LANES)

    h_main, h_small = _proj(x2, w_t, w_small_t)
    c_col, gates = _gates(h_small, bf_row, batch, seq)
    u_a = _fox(h_main, c_col, batch, seq)

    c0 = _MAIN_OFF["nsa_k_cmp"]
    r = h_main[:, c0:c0 + 2 * KV_WIDTH].reshape(batch, seq // _HALF, _CHUNK_COLS)
    k_cmp, v_cmp = _compress(r, cmp_pos_k, cmp_pos_v, cmp_wk1.astype(BF16), cmp_wk2.astype(BF16),
                             cmp_wv1.astype(BF16), cmp_wv2.astype(BF16))

    tile_idx, cmp_idx, e_t, ov_t = _static_tables(seq)
    table = jnp.concatenate(
        [rel_bias.T.astype(F32) * LOG2E, jnp.full((N_HEADS, 1), NEG, F32),
         jnp.zeros((N_HEADS, LANES - REL_BUCKETS - 1), F32)], axis=1)
    u_b = _nsa(h_main, k_cmp, v_cmp, jnp.asarray(cmp_idx), jnp.asarray(tile_idx), table, gates,
               jnp.asarray(e_t, BF16), jnp.asarray(ov_t, BF16), batch, seq)

    out = _out(u_a, u_b, h_main, x2, w_a.astype(BF16), w_b.astype(BF16), w_o.astype(BF16),
               ln_g.reshape(1, d_model), ln_b.reshape(1, d_model), alpha)
    return out.reshape(batch, seq, d_model)


def kernel(x, w_in, b_f, cmp_pos_k, cmp_pos_v, cmp_wk1, cmp_wk2, cmp_wv1, cmp_wv2,
           w_a, w_b, w_o, ln_g, ln_b, rel_bias):
    depth = w_in.shape[0]
    alpha = (2 * depth) ** 0.25
    for layer in range(depth):
        x = _layer(x, w_in[layer], b_f[layer], cmp_pos_k[layer], cmp_pos_v[layer], cmp_wk1[layer],
                   cmp_wk2[layer], cmp_wv1[layer], cmp_wv2[layer], w_a[layer], w_b[layer], w_o[layer],
                   ln_g[layer], ln_b[layer], rel_bias, alpha)
    return x
```

```python
import functools
import math

import jax
import jax.numpy as jnp
import numpy as np
from jax import lax
from jax.experimental import pallas as pl
from jax.experimental.pallas import tpu as pltpu

F32 = jnp.float32
BF16 = jnp.bfloat16

D_MODEL = 2048
HEAD_DIM = 128
N_HEADS = 8
WIDTH = N_HEADS * HEAD_DIM
KV_GROUPS = 2
HEADS_PER_GROUP = N_HEADS // KV_GROUPS
KV_WIDTH = KV_GROUPS * HEAD_DIM
N_BRANCHES = 3
CMP_LEN = 32
CMP_STRIDE = 16
CMP_HIDDEN = 256
SEL_LEN = 64
SEL_TOPK = 8
WINDOW = 512
REL_BUCKETS = 32
REL_MAX_DIST = 128
LN_EPS = 1e-5
NEG = -1e30
LOG2E = math.log2(math.e)
Q_PRESCALE = HEAD_DIM ** -0.5 * LOG2E

LANES = 128
SUBLANES = 8
MXU_DEPTH = 256
VMEM_LIMIT = 56 * 1024 * 1024

_REF_LAYOUT = (
    ("fox_q", WIDTH), ("fox_k", WIDTH), ("fox_v", WIDTH), ("fox_f", N_HEADS), ("fox_z", WIDTH),
    ("nsa_q", WIDTH), ("nsa_k_cmp", KV_WIDTH), ("nsa_v_cmp", KV_WIDTH), ("nsa_k_sel", KV_WIDTH),
    ("nsa_v_sel", KV_WIDTH), ("nsa_k_win", KV_WIDTH), ("nsa_v_win", KV_WIDTH),
    ("nsa_gate", N_HEADS * N_BRANCHES), ("nsa_z", WIDTH), ("merge_a", D_MODEL), ("merge_b", D_MODEL),
)
_REF_OFF = {}
_o = 0
for _n, _w in _REF_LAYOUT:
    _REF_OFF[_n] = (_o, _w)
    _o += _w

_MAIN_ORDER = ("fox_q", "fox_k", "fox_v", "fox_z", "nsa_q", "nsa_z", "merge_a", "merge_b",
               "nsa_k_cmp", "nsa_v_cmp", "nsa_k_sel", "nsa_v_sel", "nsa_k_win", "nsa_v_win")
_QUERY_COLS = ("fox_q", "nsa_q")
_MAIN_OFF = {}
_o = 0
for _n in _MAIN_ORDER:
    _MAIN_OFF[_n] = _o
    _o += _REF_OFF[_n][1]
MAIN_COLS = _o
GATE_LANE0 = N_HEADS


def _dot(a, b):
    return jnp.dot(a, b, preferred_element_type=F32)


def _dot_nt(a, b):
    return lax.dot_general(a, b, (((1,), (1,)), ((), ())), preferred_element_type=F32)


def _sigmoid(x):
    return 1.0 / (1.0 + jnp.exp(-x))


PROJ_TN = 512


def _proj_tiles():
    rows, is_query = [], []
    for name in _MAIN_ORDER:
        off, width = _REF_OFF[name]
        start = _MAIN_OFF[name]
        for c in range(start, start + width):
            if c % PROJ_TN == 0:
                assert (off + c - start) % SUBLANES == 0
                rows.append((off + c - start) // SUBLANES)
                is_query.append(int(name in _QUERY_COLS))
    return np.asarray(rows, np.int32), np.asarray(is_query, np.int32)


PROJ_ROWS = 4096
PROJ_RB = 512


def _proj_kernel(rows_ref, isq_ref, x_hbm, wt_ref, wst_ref, o_ref, os_ref,
                 xb_ref, stage_ref, wb_ref, sem):
    i = pl.program_id(0)
    j = pl.program_id(1)
    n_blocks = PROJ_ROWS // PROJ_RB
    wb_ref[...] = wt_ref[...].astype(BF16)
    scale = jnp.where(isq_ref[j] == 1, Q_PRESCALE, 1.0)

    def x_copy(r, slot):
        row0 = pl.multiple_of(i * PROJ_ROWS + r * PROJ_RB, PROJ_RB)
        return pltpu.make_async_copy(x_hbm.at[pl.ds(row0, PROJ_RB), :], stage_ref.at[slot], sem.at[slot])

    def block(r):
        rows = slice(r * PROJ_RB, (r + 1) * PROJ_RB)
        o_ref[rows, :] = (_dot_nt(xb_ref[rows, :], wb_ref[...]) * scale).astype(o_ref.dtype)

    @pl.when(j == 0)
    def _():
        wsb = wst_ref[...].astype(BF16)
        x_copy(0, 0).start()
        for r in range(n_blocks):
            if r + 1 < n_blocks:
                x_copy(r + 1, (r + 1) % 2).start()
            x_copy(r, r % 2).wait()
            rows = slice(r * PROJ_RB, (r + 1) * PROJ_RB)
            xb_ref[rows, :] = stage_ref[r % 2].astype(BF16)
            os_ref[rows, :] = _dot_nt(xb_ref[rows, :], wsb)
            block(r)

    @pl.when(j > 0)
    def _():
        for r in range(n_blocks):
            block(r)


def _proj(x2, w_t, w_small_t):
    m, k = x2.shape
    rows, is_query = _proj_tiles()
    assert MAIN_COLS % PROJ_TN == 0 and len(rows) == MAIN_COLS // PROJ_TN and m % PROJ_ROWS == 0
    grid_spec = pltpu.PrefetchScalarGridSpec(
        num_scalar_prefetch=2,
        grid=(m // PROJ_ROWS, len(rows)),
        in_specs=[
            pl.BlockSpec(memory_space=pl.ANY),
            pl.BlockSpec((pl.Element(PROJ_TN), pl.Element(k)),
                         lambda i, j, rows, isq: (rows[j] * SUBLANES, 0)),
            pl.BlockSpec((LANES, k), lambda i, j, rows, isq: (0, 0)),
        ],
        out_specs=[
            pl.BlockSpec((PROJ_ROWS, PROJ_TN), lambda i, j, rows, isq: (i, j)),
            pl.BlockSpec((PROJ_ROWS, LANES), lambda i, j, rows, isq: (i, 0)),
        ],
        scratch_shapes=[pltpu.VMEM((PROJ_ROWS, k), BF16),
                        pltpu.VMEM((2, PROJ_RB, k), F32),
                        pltpu.VMEM((PROJ_TN, k), BF16),
                        pltpu.SemaphoreType.DMA((2,))],
    )
    return pl.pallas_call(
        _proj_kernel,
        grid_spec=grid_spec,
        out_shape=[jax.ShapeDtypeStruct((m, MAIN_COLS), BF16), jax.ShapeDtypeStruct((m, LANES), F32)],
        compiler_params=pltpu.CompilerParams(
            dimension_semantics=("parallel", "arbitrary"), vmem_limit_bytes=VMEM_LIMIT),
        name="proj",
    )(jnp.asarray(rows), jnp.asarray(is_query), x2, w_t, w_small_t)


_CUM_CHUNK = 256
_N_SPLIT = 3
ONES_LANE = _N_SPLIT * N_HEADS


def _split3(x):
    hi = x.astype(BF16)
    r1 = x - hi.astype(F32)
    mid = r1.astype(BF16)
    lo = (r1 - mid.astype(F32)).astype(BF16)
    return hi, mid, lo


def _gate_kernel(hs_ref, bf_ref, c_ref, g_ref):
    hs = hs_ref[...]
    g_ref[...] = _sigmoid(hs)
    z = hs + bf_ref[...]
    logf = jnp.minimum(z, 0.0) - jnp.log1p(jnp.exp(-jnp.abs(z)))
    n = _CUM_CHUNK
    tri = (lax.broadcasted_iota(jnp.int32, (n, n), 1)
           <= lax.broadcasted_iota(jnp.int32, (n, n), 0)).astype(BF16)
    lane = lax.broadcasted_iota(jnp.int32, (n, LANES), 1)
    carry = jnp.zeros((1, LANES), F32)
    for blk in range(hs.shape[0] // n):
        hi, mid, lo = _split3(logf[blk * n:(blk + 1) * n])
        cb = _dot(tri, hi) + _dot(tri, mid) + _dot(tri, lo) + carry
        carry = cb[n - 1:n, :]
        hi, mid, lo = _split3(cb * LOG2E)
        packed = jnp.where(lane == ONES_LANE, 1.0, 0.0)
        for t, term in enumerate((hi, mid, lo)):
            shifted = term.astype(F32) if t == 0 else pltpu.roll(term.astype(F32), t * N_HEADS, 1)
            packed = jnp.where((lane >= t * N_HEADS) & (lane < (t + 1) * N_HEADS), shifted, packed)
        c_ref[blk * n:(blk + 1) * n, :] = packed.astype(c_ref.dtype)


def _gates(h_small, bf_row, batch, seq):
    return pl.pallas_call(
        _gate_kernel,
        grid=(batch,),
        in_specs=[pl.BlockSpec((seq, LANES), lambda b: (b, 0)),
                  pl.BlockSpec((1, LANES), lambda b: (0, 0))],
        out_specs=[pl.BlockSpec((seq, LANES), lambda b: (b, 0)),
                   pl.BlockSpec((seq, LANES), lambda b: (b, 0))],
        out_shape=[jax.ShapeDtypeStruct(h_small.shape, BF16), jax.ShapeDtypeStruct(h_small.shape, F32)],
        compiler_params=pltpu.CompilerParams(dimension_semantics=("parallel",)),
        name="gates",
    )(h_small, bf_row)


_HALF = CMP_LEN // 2
_CHUNK_COLS = _HALF * 2 * KV_WIDTH


def _gelu_tanh(x):
    return 0.5 * x * (1.0 + jnp.tanh(math.sqrt(2.0 / math.pi) * (x + 0.044715 * (x * x * x))))


def _compress_kernel(r_ref, pk_ref, pv_ref, w1k_ref, w2k_ref, w1v_ref, w2v_ref, kc_ref, vc_ref):
    n_chunks = r_ref.shape[1]
    for kv, (pos_ref, w1_ref, w2_ref, out_ref) in enumerate(
            ((pk_ref, w1k_ref, w2k_ref, kc_ref), (pv_ref, w1v_ref, w2v_ref, vc_ref))):
        for g in range(KV_GROUPS):
            first = jnp.zeros((n_chunks, CMP_HIDDEN), F32)
            second = jnp.zeros((n_chunks, CMP_HIDDEN), F32)
            for l in range(_HALF):
                col = l * 2 * KV_WIDTH + kv * KV_WIDTH + g * HEAD_DIM
                a = r_ref[0, :, col:col + HEAD_DIM].astype(F32)
                a1 = (a + pos_ref[l:l + 1, :]).astype(BF16)
                a2 = (a + pos_ref[_HALF + l:_HALF + l + 1, :]).astype(BF16)
                first += _dot(a1, w1_ref[l * HEAD_DIM:(l + 1) * HEAD_DIM, :])
                second += _dot(a2, w1_ref[(_HALF + l) * HEAD_DIM:(_HALF + l + 1) * HEAD_DIM, :])
            hid = first + pltpu.roll(second, n_chunks - 1, 0)
            out = _dot(_gelu_tanh(hid).astype(BF16), w2_ref[...])
            out_ref[0, :, g * HEAD_DIM:(g + 1) * HEAD_DIM] = out.astype(out_ref.dtype)


def _compress(r, pos_k, pos_v, w1k, w2k, w1v, w2v):
    batch, n_chunks, cols = r.shape
    full = lambda shape: pl.BlockSpec(shape, lambda b: (0,) * len(shape))
    return pl.pallas_call(
        _compress_kernel,
        grid=(batch,),
        in_specs=[pl.BlockSpec((1, n_chunks, cols), lambda b: (b, 0, 0)),
                  full(pos_k.shape), full(pos_v.shape),
                  full(w1k.shape), full(w2k.shape), full(w1v.shape), full(w2v.shape)],
        out_specs=[pl.BlockSpec((1, n_chunks, KV_WIDTH), lambda b: (b, 0, 0)),
                   pl.BlockSpec((1, n_chunks, KV_WIDTH), lambda b: (b, 0, 0))],
        out_shape=[jax.ShapeDtypeStruct((batch, n_chunks, KV_WIDTH), BF16),
                   jax.ShapeDtypeStruct((batch, n_chunks, KV_WIDTH), BF16)],
        compiler_params=pltpu.CompilerParams(
            dimension_semantics=("parallel",), vmem_limit_bytes=VMEM_LIMIT),
        name="compress",
    )(r, pos_k, pos_v, w1k, w2k, w1v, w2v)


def _lane_tile(x, n):
    return x if n == 1 else jnp.concatenate([x] * n, axis=1)


def _lane_fold(p):
    out = p[:, :LANES]
    for t in range(1, p.shape[1] // LANES):
        out = out + p[:, t * LANES:(t + 1) * LANES]
    return out


def _softmax_first(s, v, m_ref, l_ref, acc_ref, rows=slice(None)):
    m = jnp.broadcast_to(jnp.max(s, axis=1, keepdims=True), (s.shape[0], LANES))
    p = jnp.exp2(s - _lane_tile(m, s.shape[1] // LANES))
    m_ref[rows, :] = m
    l_ref[rows, :] = _lane_fold(p)
    acc_ref[rows, :] = _dot(p.astype(BF16), v)


def _softmax_update(s, v, row_const, m_ref, l_ref, acc_ref, rows=slice(None)):
    m_prev = m_ref[rows, :]
    m_cur = jnp.max(s, axis=1, keepdims=True)
    if row_const is not None:
        m_cur = m_cur + row_const
    m_new = jnp.maximum(m_prev, m_cur)
    alpha = jnp.exp2(m_prev - m_new)
    shift = m_new if row_const is None else m_new - row_const
    p = jnp.exp2(s - _lane_tile(shift, s.shape[1] // LANES))
    l_ref[rows, :] = alpha * l_ref[rows, :] + _lane_fold(p)
    acc_ref[rows, :] = alpha * acc_ref[rows, :] + _dot(p.astype(BF16), v)
    m_ref[rows, :] = m_new


def _softmax_finish(l_ref, acc_ref, rows=slice(None)):
    return acc_ref[rows, :] / jnp.sum(l_ref[rows, :], axis=1, keepdims=True)


def _softmax_once(s, v):
    m = jnp.max(s, axis=1, keepdims=True)
    p = jnp.exp2(s - m)
    return _dot(p.astype(BF16), v) / jnp.sum(_lane_fold(p), axis=1, keepdims=True)


FOX_TQ = 512
FOX_HALF = FOX_TQ // 2


def _fox_routing():
    pq = np.zeros((N_HEADS, LANES, LANES), np.float32)
    pk = np.zeros((N_HEADS, LANES, LANES), np.float32)
    for h in range(N_HEADS):
        for t in range(_N_SPLIT):
            pq[h, t * N_HEADS + h, t] = 1.0
            pq[h, ONES_LANE, _N_SPLIT + t] = 1.0
            pk[h, ONES_LANE, t] = 1.0
            pk[h, t * N_HEADS + h, _N_SPLIT + t] = -1.0
    return pq, pk


def _fox_kernel(q_ref, k_ref, v_ref, z_ref, cq_ref, ck_ref, pq_ref, pk_ref, o_ref,
                kaug_ref, qaug_ref, m_ref, l_ref, acc_ref):
    i = pl.program_id(1)
    tq, half = FOX_TQ, FOX_HALF
    seq = k_ref.shape[0]

    @pl.when(i == 0)
    def _():
        for h in range(N_HEADS):
            for blk in range(seq // tq):
                rows = slice(blk * tq, (blk + 1) * tq)
                kaug_ref[h, rows, :HEAD_DIM] = k_ref[rows, h * HEAD_DIM:(h + 1) * HEAD_DIM]
                kaug_ref[h, rows, HEAD_DIM:] = _dot(ck_ref[rows, :], pk_ref[h]).astype(BF16)

    causal = (lax.broadcasted_iota(jnp.int32, (half, half), 1)
              <= lax.broadcasted_iota(jnp.int32, (half, half), 0))
    diag = pl.multiple_of(i * tq, tq)
    diag2 = pl.multiple_of(i * tq + half, half)
    for h in range(N_HEADS):
        lo = h * HEAD_DIM
        rows = slice(h * tq, (h + 1) * tq)
        top = slice(h * tq, h * tq + half)
        bot = slice(h * tq + half, (h + 1) * tq)
        qaug_ref[rows, :HEAD_DIM] = q_ref[:, lo:lo + HEAD_DIM]
        qaug_ref[rows, HEAD_DIM:] = _dot(cq_ref[...], pq_ref[h]).astype(BF16)
        s_left = _dot_nt(qaug_ref[rows, :], kaug_ref[h, pl.ds(diag, half), :])
        s_right = _dot_nt(qaug_ref[bot, :], kaug_ref[h, pl.ds(diag2, half), :])
        _softmax_first(jnp.where(causal, s_left[:half], NEG), v_ref[pl.ds(diag, half), lo:lo + HEAD_DIM],
                       m_ref, l_ref, acc_ref, top)
        _softmax_first(jnp.concatenate([s_left[half:], jnp.where(causal, s_right, NEG)], axis=1),
                       v_ref[pl.ds(diag, tq), lo:lo + HEAD_DIM], m_ref, l_ref, acc_ref, bot)

    def body(j, carry):
        off = pl.multiple_of(j * tq, tq)
        for h in range(N_HEADS):
            rows = slice(h * tq, (h + 1) * tq)
            s = _dot_nt(qaug_ref[rows, :], kaug_ref[h, pl.ds(off, tq), :])
            _softmax_update(s, v_ref[pl.ds(off, tq), h * HEAD_DIM:(h + 1) * HEAD_DIM], None,
                            m_ref, l_ref, acc_ref, rows)
        return carry

    lax.fori_loop(0, i, body, 0)
    for h in range(N_HEADS):
        lo = h * HEAD_DIM
        o = _softmax_finish(l_ref, acc_ref, slice(h * tq, (h + 1) * tq))
        z = z_ref[:, lo:lo + HEAD_DIM].astype(F32)
        o_ref[:, lo:lo + HEAD_DIM] = (o * (z * _sigmoid(z))).astype(o_ref.dtype)


def _fox(h_main, c_packed, batch, seq):
    tq = FOX_TQ
    nq = seq // tq
    blk = lambda name: _MAIN_OFF[name] // WIDTH
    pq, pk = (jnp.asarray(p, BF16) for p in _fox_routing())
    full = lambda shape: pl.BlockSpec(shape, lambda b, i: (0,) * len(shape))
    stat = pltpu.VMEM((N_HEADS * tq, LANES), F32)
    return pl.pallas_call(
        _fox_kernel,
        grid=(batch, nq),
        in_specs=[
            pl.BlockSpec((tq, WIDTH), lambda b, i: (b * nq + i, blk("fox_q"))),
            pl.BlockSpec((seq, WIDTH), lambda b, i: (b, blk("fox_k"))),
            pl.BlockSpec((seq, WIDTH), lambda b, i: (b, blk("fox_v"))),
            pl.BlockSpec((tq, WIDTH), lambda b, i: (b * nq + i, blk("fox_z"))),
            pl.BlockSpec((tq, LANES), lambda b, i: (b * nq + i, 0)),
            pl.BlockSpec((seq, LANES), lambda b, i: (b, 0)),
            full(pq.shape), full(pk.shape),
        ],
        out_specs=pl.BlockSpec((tq, WIDTH), lambda b, i: (b * nq + i, 0)),
        out_shape=jax.ShapeDtypeStruct((batch * seq, WIDTH), BF16),
        scratch_shapes=[pltpu.VMEM((N_HEADS, seq, MXU_DEPTH), BF16),
                        pltpu.VMEM((N_HEADS * tq, MXU_DEPTH), BF16),
                        stat, stat, stat],
        compiler_params=pltpu.CompilerParams(
            dimension_semantics=("parallel", "arbitrary"), vmem_limit_bytes=VMEM_LIMIT),
        name="fox",
    )(h_main, h_main, h_main, h_main, c_packed, c_packed, pq, pk)


TQ = 256
TK = TQ
NEAR = WINDOW + TQ
N_NEAR = NEAR // TK
FAR_TK = 2 * TK
N_SEL_ROWS = 32
MASK_BUCKET = REL_BUCKETS
GROUP_ROWS = HEADS_PER_GROUP * TQ
T_DIAG, T_PREV, T_WIN2, T_FAR, T_NONE = range(5)
N_GATHERED_TILES = 3


def _bias_lookup(tab_ref, h, idx):
    row = jnp.broadcast_to(tab_ref[h:h + 1, :], idx.shape)
    return jnp.take_along_axis(row, idx, axis=1, mode="promise_in_bounds")


def _nsa_kernel(q_ref, ks_ref, vs_ref, kw_ref, vw_ref, z_ref, kc_ref, vc_ref, cidx_ref, tidx_ref,
                tab_ref, g_ref, et_ref, ov_ref, o_ref,
                ksaug_ref, qs_ref, qaug_ref, m_ref, l_ref, acc_ref, oc_ref, ow_ref, t_ref, far_ref, cb_ref):
    i = pl.program_id(1)
    t0 = i * TQ

    @pl.when((pl.program_id(0) == 0) & (i == 0))
    def _():
        for h in range(N_HEADS):
            for d in range(N_GATHERED_TILES):
                for half in range(TK // LANES):
                    cs = slice(half * LANES, (half + 1) * LANES)
                    t_ref[h, d, :, cs] = _bias_lookup(tab_ref, h, tidx_ref[d, :, cs])
            for blk in range(cidx_ref.shape[0] // TQ):
                rows = slice(blk * TQ, (blk + 1) * TQ)
                cb_ref[h, rows, :] = _bias_lookup(tab_ref, h, cidx_ref[rows, :])
            far = jnp.broadcast_to(tab_ref[h:h + 1, REL_BUCKETS - 1:REL_BUCKETS], (TQ, LANES))
            far_ref[h * TQ:(h + 1) * TQ, :] = far
            t_ref[h, T_FAR] = _lane_tile(far, TK // LANES)
            t_ref[h, T_NONE] = jnp.full((TQ, TK), NEG, F32)

    @pl.when(i == 0)
    def _():
        for g in range(KV_GROUPS):
            ksaug_ref[g, :, :HEAD_DIM] = ks_ref[:, g * HEAD_DIM:(g + 1) * HEAD_DIM]
            ksaug_ref[g, :, HEAD_DIM:] = et_ref[...]

    cmaskf = (lax.broadcasted_iota(jnp.int32, (TQ, LANES), 1) * CMP_STRIDE + (CMP_LEN - 1)
              <= t0 + lax.broadcasted_iota(jnp.int32, (TQ, LANES), 0)).astype(F32)
    j_t = lax.broadcasted_iota(jnp.int32, (N_SEL_ROWS, TQ), 0)
    t_t = t0 + lax.broadcasted_iota(jnp.int32, (N_SEL_ROWS, TQ), 1)
    cur_t = t_t // SEL_LEN
    forced_t = (j_t == 0) | (j_t == cur_t) | (j_t == cur_t - 1)
    valid_t = j_t * SEL_LEN <= t_t

    first_blk = jnp.maximum(i - (N_NEAR - 1), 0)
    near0 = pl.multiple_of(first_blk * TK, TK)

    def near_tile(kk, two_back):
        dist = i - (first_blk + kk)
        return jnp.where(dist == 0, T_DIAG, jnp.where(dist == 1, T_PREV,
                         jnp.where(dist == 2, two_back, T_NONE)))

    def near_bias(heads, two_back):
        return jnp.concatenate(
            [jnp.concatenate([t_ref[h, near_tile(kk, two_back)] for kk in range(N_NEAR)], axis=1)
             for h in heads], axis=0)

    groups = tuple((g, g * HEAD_DIM, tuple(range(g * HEADS_PER_GROUP, (g + 1) * HEADS_PER_GROUP)),
                    slice(g * GROUP_ROWS, (g + 1) * GROUP_ROWS)) for g in range(KV_GROUPS))
    for h in range(N_HEADS):
        q = q_ref[:, h * HEAD_DIM:(h + 1) * HEAD_DIM]
        qs_ref[h * TQ:(h + 1) * TQ, :] = q
        qaug_ref[h * TQ:(h + 1) * TQ, :HEAD_DIM] = q

    def window(group):
        g, glo, heads, grows = group
        s = (_dot_nt(qs_ref[grows, :], kw_ref[pl.ds(near0, NEAR), glo:glo + HEAD_DIM])
             + near_bias(heads, T_WIN2))
        ow_ref[grows, :] = _softmax_once(s, vw_ref[pl.ds(near0, NEAR), glo:glo + HEAD_DIM])

    cbias = jnp.concatenate([cb_ref[h, pl.ds(pl.multiple_of(t0, TQ), TQ), :] for h in range(N_HEADS)],
                            axis=0)
    sc = jnp.concatenate([_dot_nt(qs_ref[grows, :], kc_ref[0, :, glo:glo + HEAD_DIM])
                          for g, glo, heads, grows in groups], axis=0) + cbias
    e = jnp.exp2(sc - jnp.max(sc, axis=1, keepdims=True))
    p = e / jnp.sum(e, axis=1, keepdims=True) * jnp.concatenate([cmaskf] * N_HEADS, axis=0)
    imp_parts = []
    for g, glo, heads, grows in groups:
        oc_ref[grows, :] = _dot(p[grows].astype(BF16), vc_ref[0, :, glo:glo + HEAD_DIM])
        psum = p[heads[0] * TQ:(heads[0] + 1) * TQ]
        for h in heads[1:]:
            psum = psum + p[h * TQ:(h + 1) * TQ]
        p_hi = psum.astype(BF16)
        p_lo = (psum - p_hi.astype(F32)).astype(BF16)
        imp_parts.append((_dot_nt(ov_ref[...], p_hi) + _dot_nt(ov_ref[...], p_lo))[:N_SEL_ROWS])
    window(groups[0])
    both = lambda a: jnp.concatenate([a] * KV_GROUPS, axis=1)
    j_b = both(j_t)
    x = jnp.where(both(valid_t), jnp.where(both(forced_t), -NEG, jnp.concatenate(imp_parts, axis=1)), NEG)
    cnt = jnp.zeros(x.shape, F32)
    for jp in range(N_SEL_ROWS):
        row = x[jp:jp + 1, :]
        beats = (row > x) | ((row == x) & (j_b > jp))
        cnt = cnt + jnp.where(beats, 1.0, 0.0)
    unsel_t = jnp.where(cnt < SEL_TOPK, 0.0, 1.0)
    unsel_t = jnp.concatenate([unsel_t, jnp.zeros((LANES - N_SEL_ROWS, x.shape[1]), F32)], axis=0)
    unsel = unsel_t.T.astype(BF16)
    for h in range(N_HEADS):
        g = h // HEADS_PER_GROUP
        qaug_ref[h * TQ:(h + 1) * TQ, HEAD_DIM:] = unsel[g * TQ:(g + 1) * TQ]
    window(groups[1])

    for g, glo, heads, grows in groups:
        s = _dot_nt(qaug_ref[grows, :], ksaug_ref[g, pl.ds(near0, NEAR), :]) + near_bias(heads, T_FAR)
        _softmax_first(s, vs_ref[pl.ds(near0, NEAR), glo:glo + HEAD_DIM], m_ref, l_ref, acc_ref, grows)

    def sel_far(off, width):
        for g in range(KV_GROUPS):
            glo = g * HEAD_DIM
            grows = slice(g * GROUP_ROWS, (g + 1) * GROUP_ROWS)
            s = _dot_nt(qaug_ref[grows, :], ksaug_ref[g, pl.ds(off, width), :])
            _softmax_update(s, vs_ref[pl.ds(off, width), glo:glo + HEAD_DIM], far_ref[grows, :],
                            m_ref, l_ref, acc_ref, grows)

    def sel_far_pair(j, carry):
        sel_far(pl.multiple_of(j * FAR_TK, FAR_TK), FAR_TK)
        return carry

    lax.fori_loop(0, first_blk // 2, sel_far_pair, 0)

    @pl.when(first_blk % 2 == 1)
    def _():
        sel_far(pl.multiple_of((first_blk - 1) * TK, TK), TK)

    for h in range(N_HEADS):
        rows = slice(h * TQ, (h + 1) * TQ)
        lo = h * HEAD_DIM
        gl = GATE_LANE0 + h * N_BRANCHES
        ob = (g_ref[:, gl:gl + 1] * oc_ref[rows, :]
              + g_ref[:, gl + 1:gl + 2] * _softmax_finish(l_ref, acc_ref, rows)
              + g_ref[:, gl + 2:gl + 3] * ow_ref[rows, :])
        z = z_ref[:, lo:lo + HEAD_DIM].astype(F32)
        o_ref[:, lo:lo + HEAD_DIM] = (ob * (z * _sigmoid(z))).astype(o_ref.dtype)


def _nsa(h_main, k_cmp, v_cmp, cmp_idx, tile_idx, table, gates, e_t, ov_t, batch, seq):
    nq = seq // TQ
    wblk = lambda name: _MAIN_OFF[name] // WIDTH
    kvblk = lambda name: _MAIN_OFF[name] // KV_WIDTH
    full = lambda shape: pl.BlockSpec(shape, lambda b, i: (0,) * len(shape))
    kv_spec = lambda name: pl.BlockSpec((seq, KV_WIDTH), lambda b, i: (b, kvblk(name)))
    n_chunks = k_cmp.shape[1]
    stat = pltpu.VMEM((N_HEADS * TQ, LANES), F32)
    return pl.pallas_call(
        _nsa_kernel,
        grid=(batch, nq),
        in_specs=[
            pl.BlockSpec((TQ, WIDTH), lambda b, i: (b * nq + i, wblk("nsa_q"))),
            kv_spec("nsa_k_sel"), kv_spec("nsa_v_sel"), kv_spec("nsa_k_win"), kv_spec("nsa_v_win"),
            pl.BlockSpec((TQ, WIDTH), lambda b, i: (b * nq + i, wblk("nsa_z"))),
            pl.BlockSpec((1, n_chunks, KV_WIDTH), lambda b, i: (b, 0, 0)),
            pl.BlockSpec((1, n_chunks, KV_WIDTH), lambda b, i: (b, 0, 0)),
            full(cmp_idx.shape), full(tile_idx.shape), full(table.shape),
            pl.BlockSpec((TQ, LANES), lambda b, i: (b * nq + i, 0)),
            full(e_t.shape), full(ov_t.shape),
        ],
        out_specs=pl.BlockSpec((TQ, WIDTH), lambda b, i: (b * nq + i, 0)),
        out_shape=jax.ShapeDtypeStruct((batch * seq, WIDTH), BF16),
        scratch_shapes=[pltpu.VMEM((KV_GROUPS, seq, MXU_DEPTH), BF16),
                        pltpu.VMEM((N_HEADS * TQ, HEAD_DIM), BF16),
                        pltpu.VMEM((N_HEADS * TQ, MXU_DEPTH), BF16),
                        stat, stat, stat,
                        stat, stat,
                        pltpu.VMEM((N_HEADS, T_NONE + 1, TQ, TK), F32),
                        pltpu.VMEM((N_HEADS * TQ, LANES), F32),
                        pltpu.VMEM((N_HEADS, seq, LANES), F32)],
        compiler_params=pltpu.CompilerParams(
            dimension_semantics=("arbitrary", "arbitrary"), vmem_limit_bytes=VMEM_LIMIT),
        name="nsa",
    )(h_main, h_main, h_main, h_main, h_main, h_main, k_cmp, v_cmp, cmp_idx, tile_idx, table, gates,
      e_t, ov_t)


def _out_kernel(alpha, ua_ref, ub_ref, ga_ref, gb_ref, x_ref, wa_ref, wb_ref, wo_ref, lg_ref, lb_ref,
                o_ref):
    ya = _dot(ua_ref[...], wa_ref[...])
    yb = _dot(ub_ref[...], wb_ref[...])
    merged = _sigmoid(ga_ref[...].astype(F32)) * ya + _sigmoid(gb_ref[...].astype(F32)) * yb
    y = alpha * x_ref[...] + _dot(merged.astype(BF16), wo_ref[...])
    mu = jnp.mean(y, axis=-1, keepdims=True)
    d = y - mu
    var = jnp.mean(d * d, axis=-1, keepdims=True)
    o_ref[...] = d * lax.rsqrt(var + LN_EPS) * lg_ref[...] + lb_ref[...]


def _out(u_a, u_b, h_main, x2, w_a, w_b, w_o, ln_g, ln_b, alpha, tm=256):
    m = x2.shape[0]
    mblk = lambda name: _MAIN_OFF[name] // D_MODEL
    const = lambda shape: pl.BlockSpec(shape, lambda i: (0, 0), pipeline_mode=pl.Buffered(1))
    return pl.pallas_call(
        functools.partial(_out_kernel, alpha),
        grid=(m // tm,),
        in_specs=[
            pl.BlockSpec((tm, WIDTH), lambda i: (i, 0)),
            pl.BlockSpec((tm, WIDTH), lambda i: (i, 0)),
            pl.BlockSpec((tm, D_MODEL), lambda i: (i, mblk("merge_a"))),
            pl.BlockSpec((tm, D_MODEL), lambda i: (i, mblk("merge_b"))),
            pl.BlockSpec((tm, D_MODEL), lambda i: (i, 0)),
            const(w_a.shape), const(w_b.shape), const(w_o.shape),
            const(ln_g.shape), const(ln_b.shape),
        ],
        out_specs=pl.BlockSpec((tm, D_MODEL), lambda i: (i, 0)),
        out_shape=jax.ShapeDtypeStruct((m, D_MODEL), F32),
        compiler_params=pltpu.CompilerParams(
            dimension_semantics=("parallel",), vmem_limit_bytes=VMEM_LIMIT),
        name="out",
    )(u_a, u_b, h_main, h_main, x2, w_a, w_b, w_o, ln_g, ln_b)


def _bucket_np(dist):
    n = np.maximum(dist, 0)
    exact = REL_BUCKETS // 2
    large = exact + (np.log(np.maximum(n, 1).astype(np.float32) / exact)
                     / math.log(REL_MAX_DIST / exact) * (REL_BUCKETS - exact)).astype(np.int32)
    return np.where(n < exact, n, np.minimum(large, REL_BUCKETS - 1)).astype(np.int32)


@functools.lru_cache(maxsize=None)
def _static_tables(seq):
    r = np.arange(TQ)[:, None]
    c = np.arange(TK)[None, :]
    tile_idx = np.stack([_bucket_np(r - c), _bucket_np(TQ + r - c), _bucket_np(2 * TQ + r - c)])
    tile_ok = np.stack([c <= r, np.ones((TQ, TK), bool), (2 * TQ + r - c) < WINDOW])
    tile_idx = np.where(tile_ok, tile_idx, MASK_BUCKET).astype(np.int32)
    t = np.arange(seq)[:, None]
    cblk = np.arange(LANES)[None, :]
    blk_end = cblk * CMP_STRIDE + CMP_LEN - 1
    n_cmp = (seq - CMP_LEN) // CMP_STRIDE + 1
    cmp_idx = np.where((blk_end <= t) & (cblk < n_cmp), _bucket_np(t - blk_end), MASK_BUCKET).astype(np.int32)
    e_t = np.where((np.arange(seq)[:, None] // SEL_LEN) == np.arange(LANES)[None, :], NEG, 0.0)
    cs = (np.arange(LANES) * CMP_STRIDE)[None, :]
    ss = (np.arange(LANES) * SEL_LEN)[:, None]
    ov_t = ((cs < ss + SEL_LEN) & (cs + CMP_LEN > ss)
            & (np.arange(LANES)[None, :] < n_cmp) & (np.arange(LANES)[:, None] < seq // SEL_LEN))
    return tile_idx, cmp_idx, e_t.astype(np.float32), ov_t.astype(np.float32)


def _layer(x, w_in, b_f, cmp_pos_k, cmp_pos_v, cmp_wk1, cmp_wk2, cmp_wv1, cmp_wv2,
           w_a, w_b, w_o, ln_g, ln_b, rel_bias, alpha):
    batch, seq, d_model = x.shape
    assert d_model == D_MODEL and seq % FOX_TQ == 0 and seq >= NEAR and REL_MAX_DIST <= TQ
    assert seq // SEL_LEN == N_SEL_ROWS and seq // CMP_STRIDE == LANES
    x2 = x.reshape(batch * seq, d_model)

    w_t = jnp.swapaxes(w_in, 0, 1)
    trows = lambda name: w_t[_REF_OFF[name][0]:_REF_OFF[name][0] + _REF_OFF[name][1]]
    n_small = N_HEADS + N_HEADS * N_BRANCHES
    w_small_t = jnp.concatenate(
        [trows("fox_f"), trows("nsa_gate"), jnp.zeros((LANES - n_small, d_model), F32)], axis=0)
    bf_row = jnp.concatenate([b_f.astype(F32), jnp.zeros((LANES - N_HEADS,), F32)]).reshape(1, LANES)

    h_main, h_small = _proj(x2, w_t, w_small_t)
    c_col, gates = _gates(h_small, bf_row, batch, seq)
    u_a = _fox(h_main, c_col, batch, seq)

    c0 = _MAIN_OFF["nsa_k_cmp"]
    r = h_main[:, c0:c0 + 2 * KV_WIDTH].reshape(batch, seq // _HALF, _CHUNK_COLS)
    k_cmp, v_cmp = _compress(r, cmp_pos_k, cmp_pos_v, cmp_wk1.astype(BF16), cmp_wk2.astype(BF16),
                             cmp_wv1.astype(BF16), cmp_wv2.astype(BF16))

    tile_idx, cmp_idx, e_t, ov_t = _static_tables(seq)
    table = jnp.concatenate(
        [rel_bias.T.astype(F32) * LOG2E, jnp.full((N_HEADS, 1), NEG, F32),
         jnp.zeros((N_HEADS, LANES - REL_BUCKETS - 1), F32)], axis=1)
    u_b = _nsa(h_main, k_cmp, v_cmp, jnp.asarray(cmp_idx), jnp.asarray(tile_idx), table, gates,
               jnp.asarray(e_t, BF16), jnp.asarray(ov_t, BF16), batch, seq)

    out = _out(u_a, u_b, h_main, x2, w_a.astype(BF16), w_b.astype(BF16), w_o.astype(BF16),
               ln_g.reshape(1, d_model), ln_b.reshape(1, d_model), alpha)
    return out.reshape(batch, seq, d_model)


def kernel(x, w_in, b_f, cmp_pos_k, cmp_pos_v, cmp_wk1, cmp_wk2, cmp_wv1, cmp_wv2,
           w_a, w_b, w_o, ln_g, ln_b, rel_bias):
    depth = w_in.shape[0]
    alpha = (2 * depth) ** 0.25
    for layer in range(depth):
        x = _layer(x, w_in[layer], b_f[layer], cmp_pos_k[layer], cmp_pos_v[layer], cmp_wk1[layer],
                   cmp_wk2[layer], cmp_wv1[layer], cmp_wv2[layer], w_a[layer], w_b[layer], w_o[layer],
                   ln_g[layer], ln_b[layer], rel_bias, alpha)
    return x
```

```python
import functools
import math

import jax
import jax.numpy as jnp
import numpy as np
from jax import lax
from jax.experimental import pallas as pl
from jax.experimental.pallas import tpu as pltpu

F32 = jnp.float32
BF16 = jnp.bfloat16

D_MODEL = 2048
HEAD_DIM = 128
N_HEADS = 8
WIDTH = N_HEADS * HEAD_DIM
KV_GROUPS = 2
HEADS_PER_GROUP = N_HEADS // KV_GROUPS
KV_WIDTH = KV_GROUPS * HEAD_DIM
N_BRANCHES = 3
CMP_LEN = 32
CMP_STRIDE = 16
CMP_HIDDEN = 256
SEL_LEN = 64
SEL_TOPK = 8
WINDOW = 512
REL_BUCKETS = 32
REL_MAX_DIST = 128
LN_EPS = 1e-5
NEG = -1e30
LOG2E = math.log2(math.e)
Q_PRESCALE = HEAD_DIM ** -0.5 * LOG2E

LANES = 128
SUBLANES = 8
MXU_DEPTH = 256
VMEM_LIMIT = 60 * 1024 * 1024

_REF_LAYOUT = (
    ("fox_q", WIDTH), ("fox_k", WIDTH), ("fox_v", WIDTH), ("fox_f", N_HEADS), ("fox_z", WIDTH),
    ("nsa_q", WIDTH), ("nsa_k_cmp", KV_WIDTH), ("nsa_v_cmp", KV_WIDTH), ("nsa_k_sel", KV_WIDTH),
    ("nsa_v_sel", KV_WIDTH), ("nsa_k_win", KV_WIDTH), ("nsa_v_win", KV_WIDTH),
    ("nsa_gate", N_HEADS * N_BRANCHES), ("nsa_z", WIDTH), ("merge_a", D_MODEL), ("merge_b", D_MODEL),
)
_REF_OFF = {}
_o = 0
for _n, _w in _REF_LAYOUT:
    _REF_OFF[_n] = (_o, _w)
    _o += _w

_MAIN_ORDER = ("fox_q", "fox_k", "fox_v", "fox_z", "nsa_q", "nsa_z", "merge_a", "merge_b",
               "nsa_k_cmp", "nsa_v_cmp", "nsa_k_sel", "nsa_v_sel", "nsa_k_win", "nsa_v_win")
_QUERY_COLS = ("fox_q", "nsa_q")
_MAIN_OFF = {}
_o = 0
for _n in _MAIN_ORDER:
    _MAIN_OFF[_n] = _o
    _o += _REF_OFF[_n][1]
MAIN_COLS = _o
GATE_LANE0 = N_HEADS


def _dot(a, b):
    return jnp.dot(a, b, preferred_element_type=F32)


def _dot_nt(a, b):
    return lax.dot_general(a, b, (((1,), (1,)), ((), ())), preferred_element_type=F32)


def _sigmoid(x):
    return 0.5 + 0.5 * jnp.tanh(0.5 * x)


def _silu(x):
    return x * _sigmoid(x)


PROJ_TN = 512


def _proj_tiles():
    rows, is_query = [], []
    for name in _MAIN_ORDER:
        off, width = _REF_OFF[name]
        start = _MAIN_OFF[name]
        for c in range(start, start + width):
            if c % PROJ_TN == 0:
                assert (off + c - start) % SUBLANES == 0
                rows.append((off + c - start) // SUBLANES)
                is_query.append(int(name in _QUERY_COLS))
    return np.asarray(rows, np.int32), np.asarray(is_query, np.int32)


PROJ_ROWS = 4096
PROJ_RB = 512


def _proj_kernel(rows_ref, isq_ref, x_hbm, wt_ref, wst_ref, o_ref, os_ref,
                 xb_ref, stage_ref, wb_ref, sem):
    i = pl.program_id(0)
    j = pl.program_id(1)
    n_blocks = PROJ_ROWS // PROJ_RB
    wb_ref[...] = wt_ref[...].astype(BF16)
    scale = jnp.where(isq_ref[j] == 1, Q_PRESCALE, 1.0)

    def x_copy(r, slot):
        row0 = pl.multiple_of(i * PROJ_ROWS + r * PROJ_RB, PROJ_RB)
        return pltpu.make_async_copy(x_hbm.at[pl.ds(row0, PROJ_RB), :], stage_ref.at[slot], sem.at[slot])

    def block(r):
        rows = slice(r * PROJ_RB, (r + 1) * PROJ_RB)
        o_ref[rows, :] = (_dot_nt(xb_ref[rows, :], wb_ref[...]) * scale).astype(o_ref.dtype)

    @pl.when(j == 0)
    def _():
        wsb = wst_ref[...].astype(BF16)
        x_copy(0, 0).start()
        for r in range(n_blocks):
            if r + 1 < n_blocks:
                x_copy(r + 1, (r + 1) % 2).start()
            x_copy(r, r % 2).wait()
            rows = slice(r * PROJ_RB, (r + 1) * PROJ_RB)
            xb_ref[rows, :] = stage_ref[r % 2].astype(BF16)
            os_ref[rows, :] = _dot_nt(xb_ref[rows, :], wsb)
            block(r)

    @pl.when(j > 0)
    def _():
        for r in range(n_blocks):
            block(r)


def _proj(x2, w_t, w_small_t):
    m, k = x2.shape
    rows, is_query = _proj_tiles()
    assert MAIN_COLS % PROJ_TN == 0 and len(rows) == MAIN_COLS // PROJ_TN and m % PROJ_ROWS == 0
    grid_spec = pltpu.PrefetchScalarGridSpec(
        num_scalar_prefetch=2,
        grid=(m // PROJ_ROWS, len(rows)),
        in_specs=[
            pl.BlockSpec(memory_space=pl.ANY),
            pl.BlockSpec((pl.Element(PROJ_TN), pl.Element(k)),
                         lambda i, j, rows, isq: (rows[j] * SUBLANES, 0)),
            pl.BlockSpec((LANES, k), lambda i, j, rows, isq: (0, 0)),
        ],
        out_specs=[
            pl.BlockSpec((PROJ_ROWS, PROJ_TN), lambda i, j, rows, isq: (i, j)),
            pl.BlockSpec((PROJ_ROWS, LANES), lambda i, j, rows, isq: (i, 0)),
        ],
        scratch_shapes=[pltpu.VMEM((PROJ_ROWS, k), BF16),
                        pltpu.VMEM((2, PROJ_RB, k), F32),
                        pltpu.VMEM((PROJ_TN, k), BF16),
                        pltpu.SemaphoreType.DMA((2,))],
    )
    return pl.pallas_call(
        _proj_kernel,
        grid_spec=grid_spec,
        out_shape=[jax.ShapeDtypeStruct((m, MAIN_COLS), BF16), jax.ShapeDtypeStruct((m, LANES), F32)],
        compiler_params=pltpu.CompilerParams(
            dimension_semantics=("parallel", "arbitrary"), vmem_limit_bytes=VMEM_LIMIT),
        name="proj",
    )(jnp.asarray(rows), jnp.asarray(is_query), x2, w_t, w_small_t)


_CUM_CHUNK = 256
_N_SPLIT = 3
ONES_LANE = _N_SPLIT * N_HEADS


def _split3(x):
    hi = x.astype(BF16)
    r1 = x - hi.astype(F32)
    mid = r1.astype(BF16)
    lo = (r1 - mid.astype(F32)).astype(BF16)
    return hi, mid, lo


def _gate_kernel(hs_ref, bf_ref, c_ref, g_ref):
    hs = hs_ref[...]
    g_ref[...] = _sigmoid(hs)
    z = hs + bf_ref[...]
    logf = jnp.minimum(z, 0.0) - jnp.log1p(jnp.exp(-jnp.abs(z)))
    n = _CUM_CHUNK
    tri = (lax.broadcasted_iota(jnp.int32, (n, n), 1)
           <= lax.broadcasted_iota(jnp.int32, (n, n), 0)).astype(BF16)
    lane = lax.broadcasted_iota(jnp.int32, (n, LANES), 1)
    carry = jnp.zeros((1, LANES), F32)
    for blk in range(hs.shape[0] // n):
        hi, mid, lo = _split3(logf[blk * n:(blk + 1) * n])
        cb = _dot(tri, hi) + _dot(tri, mid) + _dot(tri, lo) + carry
        carry = cb[n - 1:n, :]
        hi, mid, lo = _split3(cb * LOG2E)
        packed = jnp.where(lane == ONES_LANE, 1.0, 0.0)
        for t, term in enumerate((hi, mid, lo)):
            shifted = term.astype(F32) if t == 0 else pltpu.roll(term.astype(F32), t * N_HEADS, 1)
            packed = jnp.where((lane >= t * N_HEADS) & (lane < (t + 1) * N_HEADS), shifted, packed)
        c_ref[blk * n:(blk + 1) * n, :] = packed.astype(c_ref.dtype)


def _gates(h_small, bf_row, batch, seq):
    return pl.pallas_call(
        _gate_kernel,
        grid=(batch,),
        in_specs=[pl.BlockSpec((seq, LANES), lambda b: (b, 0)),
                  pl.BlockSpec((1, LANES), lambda b: (0, 0))],
        out_specs=[pl.BlockSpec((seq, LANES), lambda b: (b, 0)),
                   pl.BlockSpec((seq, LANES), lambda b: (b, 0))],
        out_shape=[jax.ShapeDtypeStruct(h_small.shape, BF16), jax.ShapeDtypeStruct(h_small.shape, F32)],
        compiler_params=pltpu.CompilerParams(dimension_semantics=("parallel",)),
        name="gates",
    )(h_small, bf_row)


_HALF = CMP_LEN // 2
_CHUNK_COLS = _HALF * 2 * KV_WIDTH


def _gelu_tanh(x):
    return 0.5 * x * (1.0 + jnp.tanh(math.sqrt(2.0 / math.pi) * (x + 0.044715 * (x * x * x))))


def _compress_kernel(r_ref, pk_ref, pv_ref, w1k_ref, w2k_ref, w1v_ref, w2v_ref, kc_ref, vc_ref):
    n_chunks = r_ref.shape[1]
    for kv, (pos_ref, w1_ref, w2_ref, out_ref) in enumerate(
            ((pk_ref, w1k_ref, w2k_ref, kc_ref), (pv_ref, w1v_ref, w2v_ref, vc_ref))):
        for g in range(KV_GROUPS):
            first = jnp.zeros((n_chunks, CMP_HIDDEN), F32)
            second = jnp.zeros((n_chunks, CMP_HIDDEN), F32)
            for l in range(_HALF):
                col = l * 2 * KV_WIDTH + kv * KV_WIDTH + g * HEAD_DIM
                a = r_ref[0, :, col:col + HEAD_DIM].astype(F32)
                a1 = (a + pos_ref[l:l + 1, :]).astype(BF16)
                a2 = (a + pos_ref[_HALF + l:_HALF + l + 1, :]).astype(BF16)
                first += _dot(a1, w1_ref[l * HEAD_DIM:(l + 1) * HEAD_DIM, :])
                second += _dot(a2, w1_ref[(_HALF + l) * HEAD_DIM:(_HALF + l + 1) * HEAD_DIM, :])
            hid = first + pltpu.roll(second, n_chunks - 1, 0)
            out = _dot(_gelu_tanh(hid).astype(BF16), w2_ref[...])
            out_ref[0, :, g * HEAD_DIM:(g + 1) * HEAD_DIM] = out.astype(out_ref.dtype)


def _compress(r, pos_k, pos_v, w1k, w2k, w1v, w2v):
    batch, n_chunks, cols = r.shape
    full = lambda shape: pl.BlockSpec(shape, lambda b: (0,) * len(shape))
    return pl.pallas_call(
        _compress_kernel,
        grid=(batch,),
        in_specs=[pl.BlockSpec((1, n_chunks, cols), lambda b: (b, 0, 0)),
                  full(pos_k.shape), full(pos_v.shape),
                  full(w1k.shape), full(w2k.shape), full(w1v.shape), full(w2v.shape)],
        out_specs=[pl.BlockSpec((1, n_chunks, KV_WIDTH), lambda b: (b, 0, 0)),
                   pl.BlockSpec((1, n_chunks, KV_WIDTH), lambda b: (b, 0, 0))],
        out_shape=[jax.ShapeDtypeStruct((batch, n_chunks, KV_WIDTH), BF16),
                   jax.ShapeDtypeStruct((batch, n_chunks, KV_WIDTH), BF16)],
        compiler_params=pltpu.CompilerParams(
            dimension_semantics=("parallel",), vmem_limit_bytes=VMEM_LIMIT),
        name="compress",
    )(r, pos_k, pos_v, w1k, w2k, w1v, w2v)


def _lane_tile(x, n):
    return x if n == 1 else jnp.concatenate([x] * n, axis=1)


def _lane_fold(p):
    out = p[:, :LANES]
    for t in range(1, p.shape[1] // LANES):
        out = out + p[:, t * LANES:(t + 1) * LANES]
    return out


def _first_weights(s, m_ref, l_ref, rows=slice(None)):
    m = jnp.broadcast_to(jnp.max(s, axis=1, keepdims=True), (s.shape[0], LANES))
    p = jnp.exp2(s - _lane_tile(m, s.shape[1] // LANES))
    m_ref[rows, :] = m
    l_ref[rows, :] = _lane_fold(p)
    return p.astype(BF16)


def _next_weights(s, row_const, m_ref, l_ref, rows=slice(None)):
    m_prev = m_ref[rows, :]
    m_cur = jnp.max(s, axis=1, keepdims=True)
    if row_const is not None:
        m_cur = m_cur + row_const
    m_new = jnp.maximum(m_prev, m_cur)
    alpha = jnp.exp2(m_prev - m_new)
    shift = m_new if row_const is None else m_new - row_const
    p = jnp.exp2(s - _lane_tile(shift, s.shape[1] // LANES))
    l_ref[rows, :] = alpha * l_ref[rows, :] + _lane_fold(p)
    m_ref[rows, :] = m_new
    return alpha, p.astype(BF16)


def _softmax_first(s, v, m_ref, l_ref, acc_ref, rows=slice(None)):
    acc_ref[rows, :] = _dot(_first_weights(s, m_ref, l_ref, rows), v)


def _softmax_update(s, v, row_const, m_ref, l_ref, acc_ref, rows=slice(None)):
    alpha, p = _next_weights(s, row_const, m_ref, l_ref, rows)
    acc_ref[rows, :] = alpha * acc_ref[rows, :] + _dot(p, v)


def _skewed(stages, jobs, before_tick=None):
    state = [None] * len(jobs)
    for tick in range(len(jobs) + len(stages) - 1):
        if before_tick and tick in before_tick:
            before_tick[tick]()
        for k, stage in enumerate(stages):
            j = tick - k
            if 0 <= j < len(jobs):
                state[j] = stage(jobs[j], state[j])


def _softmax_finish(l_ref, acc_ref, rows=slice(None)):
    return acc_ref[rows, :] / jnp.sum(l_ref[rows, :], axis=1, keepdims=True)


def _softmax_once(s, v):
    m = jnp.max(s, axis=1, keepdims=True)
    p = jnp.exp2(s - m)
    return _dot(p.astype(BF16), v) / jnp.sum(_lane_fold(p), axis=1, keepdims=True)


FOX_TQ = 512
FOX_HALF = FOX_TQ // 2


def _fox_routing():
    pq = np.zeros((N_HEADS, LANES, LANES), np.float32)
    pk = np.zeros((N_HEADS, LANES, LANES), np.float32)
    for h in range(N_HEADS):
        for t in range(_N_SPLIT):
            pq[h, t * N_HEADS + h, t] = 1.0
            pq[h, ONES_LANE, _N_SPLIT + t] = 1.0
            pk[h, ONES_LANE, t] = 1.0
            pk[h, t * N_HEADS + h, _N_SPLIT + t] = -1.0
    return pq, pk


def _fox_kernel(q_ref, k_ref, v_ref, z_ref, cq_ref, ck_ref, pq_ref, pk_ref, o_ref,
                kaug_ref, qaug_ref, sz_ref, m_ref, l_ref, acc_ref):
    i = pl.program_id(1)
    tq, half = FOX_TQ, FOX_HALF
    seq = k_ref.shape[0]

    @pl.when(i == 0)
    def _():
        for h in range(N_HEADS):
            for blk in range(seq // tq):
                rows = slice(blk * tq, (blk + 1) * tq)
                kaug_ref[h, rows, :HEAD_DIM] = k_ref[rows, h * HEAD_DIM:(h + 1) * HEAD_DIM]
                kaug_ref[h, rows, HEAD_DIM:] = _dot(ck_ref[rows, :], pk_ref[h]).astype(BF16)

    causal = (lax.broadcasted_iota(jnp.int32, (half, half), 1)
              <= lax.broadcasted_iota(jnp.int32, (half, half), 0))
    diag = pl.multiple_of(i * tq, tq)
    diag2 = pl.multiple_of(i * tq + half, half)
    heads = tuple(range(N_HEADS))
    rows = lambda h: slice(h * tq, (h + 1) * tq)
    top = lambda h: slice(h * tq, h * tq + half)
    bot = lambda h: slice(h * tq + half, (h + 1) * tq)
    cols = lambda h: slice(h * HEAD_DIM, (h + 1) * HEAD_DIM)

    def diag_scores(h, _):
        qaug_ref[rows(h), :HEAD_DIM] = q_ref[:, cols(h)]
        qaug_ref[rows(h), HEAD_DIM:] = _dot(cq_ref[...], pq_ref[h]).astype(BF16)
        s_left = _dot_nt(qaug_ref[rows(h), :], kaug_ref[h, pl.ds(diag, half), :])
        s_right = _dot_nt(qaug_ref[bot(h), :], kaug_ref[h, pl.ds(diag2, half), :])
        return s_left, s_right

    def diag_weights(h, scores):
        s_left, s_right = scores
        p_top = _first_weights(jnp.where(causal, s_left[:half], NEG), m_ref, l_ref, top(h))
        p_bot = _first_weights(jnp.concatenate([s_left[half:], jnp.where(causal, s_right, NEG)], axis=1),
                               m_ref, l_ref, bot(h))
        return p_top, p_bot

    def diag_values(h, weights):
        p_top, p_bot = weights
        acc_ref[top(h), :] = _dot(p_top, v_ref[pl.ds(diag, half), cols(h)])
        acc_ref[bot(h), :] = _dot(p_bot, v_ref[pl.ds(diag, tq), cols(h)])
        sz_ref[:, cols(h)] = _silu(z_ref[:, cols(h)].astype(F32))

    _skewed((diag_scores, diag_weights, diag_values), heads)

    def body(j, carry):
        off = pl.multiple_of(j * tq, tq)

        def scores(h, _):
            return _dot_nt(qaug_ref[rows(h), :], kaug_ref[h, pl.ds(off, tq), :])

        def weights(h, s):
            return _next_weights(s, None, m_ref, l_ref, rows(h))

        def values(h, rescale_and_weights):
            alpha, p = rescale_and_weights
            acc_ref[rows(h), :] = alpha * acc_ref[rows(h), :] + _dot(p, v_ref[pl.ds(off, tq), cols(h)])

        _skewed((scores, weights, values), heads)
        return carry

    lax.fori_loop(0, i, body, 0)
    for h in heads:
        o_ref[:, cols(h)] = (_softmax_finish(l_ref, acc_ref, rows(h)) * sz_ref[:, cols(h)]).astype(o_ref.dtype)


def _fox(h_main, c_packed, batch, seq):
    tq = FOX_TQ
    nq = seq // tq
    blk = lambda name: _MAIN_OFF[name] // WIDTH
    pq, pk = (jnp.asarray(p, BF16) for p in _fox_routing())
    full = lambda shape: pl.BlockSpec(shape, lambda b, i: (0,) * len(shape))
    stat = pltpu.VMEM((N_HEADS * tq, LANES), F32)
    return pl.pallas_call(
        _fox_kernel,
        grid=(batch, nq),
        in_specs=[
            pl.BlockSpec((tq, WIDTH), lambda b, i: (b * nq + i, blk("fox_q"))),
            pl.BlockSpec((seq, WIDTH), lambda b, i: (b, blk("fox_k"))),
            pl.BlockSpec((seq, WIDTH), lambda b, i: (b, blk("fox_v"))),
            pl.BlockSpec((tq, WIDTH), lambda b, i: (b * nq + i, blk("fox_z"))),
            pl.BlockSpec((tq, LANES), lambda b, i: (b * nq + i, 0)),
            pl.BlockSpec((seq, LANES), lambda b, i: (b, 0)),
            full(pq.shape), full(pk.shape),
        ],
        out_specs=pl.BlockSpec((tq, WIDTH), lambda b, i: (b * nq + i, 0)),
        out_shape=jax.ShapeDtypeStruct((batch * seq, WIDTH), BF16),
        scratch_shapes=[pltpu.VMEM((N_HEADS, seq, MXU_DEPTH), BF16),
                        pltpu.VMEM((N_HEADS * tq, MXU_DEPTH), BF16),
                        pltpu.VMEM((tq, WIDTH), F32),
                        stat, stat, stat],
        compiler_params=pltpu.CompilerParams(
            dimension_semantics=("parallel", "arbitrary"), vmem_limit_bytes=VMEM_LIMIT),
        name="fox",
    )(h_main, h_main, h_main, h_main, c_packed, c_packed, pq, pk)


TQ = 256
TK = TQ
NEAR = WINDOW + TQ
N_NEAR = NEAR // TK
FAR_TK = 2 * TK
N_SEL_ROWS = 32
MASK_BUCKET = REL_BUCKETS
GROUP_ROWS = HEADS_PER_GROUP * TQ
T_DIAG, T_PREV, T_WIN2, T_FAR, T_NONE = range(5)
N_GATHERED_TILES = 3


def _bias_lookup(tab_ref, h, idx):
    row = jnp.broadcast_to(tab_ref[h:h + 1, :], idx.shape)
    return jnp.take_along_axis(row, idx, axis=1, mode="promise_in_bounds")


def _nsa_kernel(q_ref, ks_ref, vs_ref, kw_ref, vw_ref, z_ref, kc_ref, vc_ref, cidx_ref, tidx_ref,
                tab_ref, g_ref, et_ref, ov_ref, o_ref,
                ksaug_ref, qs_ref, qaug_ref, m_ref, l_ref, acc_ref, oc_ref, ow_ref, gs_ref, t_ref, far_ref,
                cb_ref):
    i = pl.program_id(1)
    t0 = i * TQ

    @pl.when((pl.program_id(0) == 0) & (i == 0))
    def _():
        for h in range(N_HEADS):
            for d in range(N_GATHERED_TILES):
                for half in range(TK // LANES):
                    cs = slice(half * LANES, (half + 1) * LANES)
                    t_ref[h, d, :, cs] = _bias_lookup(tab_ref, h, tidx_ref[d, :, cs])
            for blk in range(cidx_ref.shape[0] // TQ):
                rows = slice(blk * TQ, (blk + 1) * TQ)
                cb_ref[h, rows, :] = _bias_lookup(tab_ref, h, cidx_ref[rows, :])
            far = jnp.broadcast_to(tab_ref[h:h + 1, REL_BUCKETS - 1:REL_BUCKETS], (TQ, LANES))
            far_ref[h * TQ:(h + 1) * TQ, :] = far
            t_ref[h, T_FAR] = _lane_tile(far, TK // LANES)
            t_ref[h, T_NONE] = jnp.full((TQ, TK), NEG, F32)

    @pl.when(i == 0)
    def _():
        for g in range(KV_GROUPS):
            ksaug_ref[g, :, :HEAD_DIM] = ks_ref[:, g * HEAD_DIM:(g + 1) * HEAD_DIM]
            ksaug_ref[g, :, HEAD_DIM:] = et_ref[...]

    cmaskf = (lax.broadcasted_iota(jnp.int32, (TQ, LANES), 1) * CMP_STRIDE + (CMP_LEN - 1)
              <= t0 + lax.broadcasted_iota(jnp.int32, (TQ, LANES), 0)).astype(F32)
    j_t = lax.broadcasted_iota(jnp.int32, (N_SEL_ROWS, TQ), 0)
    t_t = t0 + lax.broadcasted_iota(jnp.int32, (N_SEL_ROWS, TQ), 1)
    cur_t = t_t // SEL_LEN
    forced_t = (j_t == 0) | (j_t == cur_t) | (j_t == cur_t - 1)
    valid_t = j_t * SEL_LEN <= t_t

    first_blk = jnp.maximum(i - (N_NEAR - 1), 0)
    near0 = pl.multiple_of(first_blk * TK, TK)

    def near_tile(kk, two_back):
        dist = i - (first_blk + kk)
        return jnp.where(dist == 0, T_DIAG, jnp.where(dist == 1, T_PREV,
                         jnp.where(dist == 2, two_back, T_NONE)))

    def near_bias(heads, two_back):
        return jnp.concatenate(
            [jnp.concatenate([t_ref[h, near_tile(kk, two_back)] for kk in range(N_NEAR)], axis=1)
             for h in heads], axis=0)

    groups = tuple((g, g * HEAD_DIM, tuple(range(g * HEADS_PER_GROUP, (g + 1) * HEADS_PER_GROUP)),
                    slice(g * GROUP_ROWS, (g + 1) * GROUP_ROWS)) for g in range(KV_GROUPS))
    hrows = lambda h: slice(h * TQ, (h + 1) * TQ)
    for h in range(N_HEADS):
        q = q_ref[:, h * HEAD_DIM:(h + 1) * HEAD_DIM]
        qs_ref[hrows(h), :] = q
        qaug_ref[hrows(h), :HEAD_DIM] = q

    chain = {}

    def compressed_branch():
        cbias = jnp.concatenate([cb_ref[h, pl.ds(pl.multiple_of(t0, TQ), TQ), :] for h in range(N_HEADS)],
                                axis=0)
        sc = jnp.concatenate([_dot_nt(qs_ref[grows, :], kc_ref[0, :, glo:glo + HEAD_DIM])
                              for g, glo, heads, grows in groups], axis=0) + cbias
        e = jnp.exp2(sc - jnp.max(sc, axis=1, keepdims=True))
        p = e / jnp.sum(e, axis=1, keepdims=True) * jnp.concatenate([cmaskf] * N_HEADS, axis=0)
        imp_parts = []
        for g, glo, heads, grows in groups:
            oc_ref[grows, :] = _dot(p[grows].astype(BF16), vc_ref[0, :, glo:glo + HEAD_DIM])
            psum = p[hrows(heads[0])]
            for h in heads[1:]:
                psum = psum + p[hrows(h)]
            p_hi = psum.astype(BF16)
            p_lo = (psum - p_hi.astype(F32)).astype(BF16)
            imp_parts.append((_dot_nt(ov_ref[...], p_hi) + _dot_nt(ov_ref[...], p_lo))[:N_SEL_ROWS])
        chain["importance"] = jnp.concatenate(imp_parts, axis=1)

    def block_selection():
        both = lambda a: jnp.concatenate([a] * KV_GROUPS, axis=1)
        j_b = both(j_t)
        x = jnp.where(both(valid_t), jnp.where(both(forced_t), -NEG, chain["importance"]), NEG)
        cnt = jnp.zeros(x.shape, F32)
        for jp in range(N_SEL_ROWS):
            row = x[jp:jp + 1, :]
            beats = (row > x) | ((row == x) & (j_b > jp))
            cnt = cnt + jnp.where(beats, 1.0, 0.0)
        unsel_t = jnp.where(cnt < SEL_TOPK, 0.0, 1.0)
        unsel_t = jnp.concatenate([unsel_t, jnp.zeros((LANES - N_SEL_ROWS, x.shape[1]), F32)], axis=0)
        unsel = unsel_t.T.astype(BF16)
        for h in range(N_HEADS):
            g = h // HEADS_PER_GROUP
            qaug_ref[hrows(h), HEAD_DIM:] = unsel[g * TQ:(g + 1) * TQ]

    def near_scores(job, _):
        branch, (g, glo, heads, grows) = job
        if branch == "window":
            return (_dot_nt(qs_ref[grows, :], kw_ref[pl.ds(near0, NEAR), glo:glo + HEAD_DIM])
                    + near_bias(heads, T_WIN2))
        return _dot_nt(qaug_ref[grows, :], ksaug_ref[g, pl.ds(near0, NEAR), :]) + near_bias(heads, T_FAR)

    def near_weights(job, s):
        branch, (g, glo, heads, grows) = job
        if branch == "window":
            p = jnp.exp2(s - jnp.max(s, axis=1, keepdims=True))
            return p.astype(BF16), _lane_fold(p)
        return _first_weights(s, m_ref, l_ref, grows)

    def near_values(job, weights):
        branch, (g, glo, heads, grows) = job
        if branch == "selected":
            acc_ref[grows, :] = _dot(weights, vs_ref[pl.ds(near0, NEAR), glo:glo + HEAD_DIM])
            return
        p, l = weights
        o_win = _dot(p, vw_ref[pl.ds(near0, NEAR), glo:glo + HEAD_DIM]) / jnp.sum(l, axis=1, keepdims=True)
        for n, h in enumerate(heads):
            gl = GATE_LANE0 + h * N_BRANCHES
            sz = _silu(z_ref[:, h * HEAD_DIM:(h + 1) * HEAD_DIM].astype(F32))
            gs_ref[hrows(h), :] = g_ref[:, gl + 1:gl + 2] * sz
            ow_ref[hrows(h), :] = (g_ref[:, gl:gl + 1] * oc_ref[hrows(h), :]
                                   + g_ref[:, gl + 2:gl + 3] * o_win[n * TQ:(n + 1) * TQ]) * sz

    _skewed((near_scores, near_weights, near_values),
            tuple((branch, group) for branch in ("window", "selected") for group in groups),
            before_tick={1: compressed_branch, 2: block_selection})

    def sel_far(off, width):
        def scores(group, _):
            g, glo, heads, grows = group
            return _dot_nt(qaug_ref[grows, :], ksaug_ref[g, pl.ds(off, width), :])

        def weights(group, s):
            g, glo, heads, grows = group
            return _next_weights(s, far_ref[grows, :], m_ref, l_ref, grows)

        def values(group, rescale_and_weights):
            g, glo, heads, grows = group
            alpha, p = rescale_and_weights
            acc_ref[grows, :] = alpha * acc_ref[grows, :] + _dot(p, vs_ref[pl.ds(off, width), glo:glo + HEAD_DIM])

        _skewed((scores, weights, values), groups)

    def sel_far_pair(j, carry):
        sel_far(pl.multiple_of(j * FAR_TK, FAR_TK), FAR_TK)
        return carry

    lax.fori_loop(0, first_blk // 2, sel_far_pair, 0)

    @pl.when(first_blk % 2 == 1)
    def _():
        sel_far(pl.multiple_of((first_blk - 1) * TK, TK), TK)

    for h in range(N_HEADS):
        o_ref[:, h * HEAD_DIM:(h + 1) * HEAD_DIM] = (
            ow_ref[hrows(h), :] + gs_ref[hrows(h), :] * _softmax_finish(l_ref, acc_ref, hrows(h))
        ).astype(o_ref.dtype)


def _nsa(h_main, k_cmp, v_cmp, cmp_idx, tile_idx, table, gates, e_t, ov_t, batch, seq):
    nq = seq // TQ
    wblk = lambda name: _MAIN_OFF[name] // WIDTH
    kvblk = lambda name: _MAIN_OFF[name] // KV_WIDTH
    full = lambda shape: pl.BlockSpec(shape, lambda b, i: (0,) * len(shape),
                                      pipeline_mode=pl.Buffered(1))
    kv_spec = lambda name: pl.BlockSpec((seq, KV_WIDTH), lambda b, i: (b, kvblk(name)))
    n_chunks = k_cmp.shape[1]
    stat = pltpu.VMEM((N_HEADS * TQ, LANES), F32)
    return pl.pallas_call(
        _nsa_kernel,
        grid=(batch, nq),
        in_specs=[
            pl.BlockSpec((TQ, WIDTH), lambda b, i: (b * nq + i, wblk("nsa_q"))),
            kv_spec("nsa_k_sel"), kv_spec("nsa_v_sel"), kv_spec("nsa_k_win"), kv_spec("nsa_v_win"),
            pl.BlockSpec((TQ, WIDTH), lambda b, i: (b * nq + i, wblk("nsa_z"))),
            pl.BlockSpec((1, n_chunks, KV_WIDTH), lambda b, i: (b, 0, 0)),
            pl.BlockSpec((1, n_chunks, KV_WIDTH), lambda b, i: (b, 0, 0)),
            full(cmp_idx.shape), full(tile_idx.shape), full(table.shape),
            pl.BlockSpec((TQ, LANES), lambda b, i: (b * nq + i, 0)),
            full(e_t.shape), full(ov_t.shape),
        ],
        out_specs=pl.BlockSpec((TQ, WIDTH), lambda b, i: (b * nq + i, 0)),
        out_shape=jax.ShapeDtypeStruct((batch * seq, WIDTH), BF16),
        scratch_shapes=[pltpu.VMEM((KV_GROUPS, seq, MXU_DEPTH), BF16),
                        pltpu.VMEM((N_HEADS * TQ, HEAD_DIM), BF16),
                        pltpu.VMEM((N_HEADS * TQ, MXU_DEPTH), BF16),
                        stat, stat, stat,
                        stat, stat, stat,
                        pltpu.VMEM((N_HEADS, T_NONE + 1, TQ, TK), F32),
                        pltpu.VMEM((N_HEADS * TQ, LANES), F32),
                        pltpu.VMEM((N_HEADS, seq, LANES), F32)],
        compiler_params=pltpu.CompilerParams(
            dimension_semantics=("arbitrary", "arbitrary"), vmem_limit_bytes=VMEM_LIMIT),
        name="nsa",
    )(h_main, h_main, h_main, h_main, h_main, h_main, k_cmp, v_cmp, cmp_idx, tile_idx, table, gates,
      e_t, ov_t)


def _out_kernel(alpha, ua_ref, ub_ref, ga_ref, gb_ref, x_ref, wa_ref, wb_ref, wo_ref, lg_ref, lb_ref,
                o_ref):
    ya = _dot(ua_ref[...], wa_ref[...])
    yb = _dot(ub_ref[...], wb_ref[...])
    merged = _sigmoid(ga_ref[...].astype(F32)) * ya + _sigmoid(gb_ref[...].astype(F32)) * yb
    y = alpha * x_ref[...] + _dot(merged.astype(BF16), wo_ref[...])
    mu = jnp.mean(y, axis=-1, keepdims=True)
    d = y - mu
    var = jnp.mean(d * d, axis=-1, keepdims=True)
    o_ref[...] = d * lax.rsqrt(var + LN_EPS) * lg_ref[...] + lb_ref[...]


def _out(u_a, u_b, h_main, x2, w_a, w_b, w_o, ln_g, ln_b, alpha, tm=256):
    m = x2.shape[0]
    mblk = lambda name: _MAIN_OFF[name] // D_MODEL
    const = lambda shape: pl.BlockSpec(shape, lambda i: (0, 0), pipeline_mode=pl.Buffered(1))
    return pl.pallas_call(
        functools.partial(_out_kernel, alpha),
        grid=(m // tm,),
        in_specs=[
            pl.BlockSpec((tm, WIDTH), lambda i: (i, 0)),
            pl.BlockSpec((tm, WIDTH), lambda i: (i, 0)),
            pl.BlockSpec((tm, D_MODEL), lambda i: (i, mblk("merge_a"))),
            pl.BlockSpec((tm, D_MODEL), lambda i: (i, mblk("merge_b"))),
            pl.BlockSpec((tm, D_MODEL), lambda i: (i, 0)),
            const(w_a.shape), const(w_b.shape), const(w_o.shape),
            const(ln_g.shape), const(ln_b.shape),
        ],
        out_specs=pl.BlockSpec((tm, D_MODEL), lambda i: (i, 0)),
        out_shape=jax.ShapeDtypeStruct((m, D_MODEL), F32),
        compiler_params=pltpu.CompilerParams(
            dimension_semantics=("parallel",), vmem_limit_bytes=VMEM_LIMIT),
        name="out",
    )(u_a, u_b, h_main, h_main, x2, w_a, w_b, w_o, ln_g, ln_b)


def _bucket_np(dist):
    n = np.maximum(dist, 0)
    exact = REL_BUCKETS // 2
    large = exact + (np.log(np.maximum(n, 1).astype(np.float32) / exact)
                     / math.log(REL_MAX_DIST / exact) * (REL_BUCKETS - exact)).astype(np.int32)
    return np.where(n < exact, n, np.minimum(large, REL_BUCKETS - 1)).astype(np.int32)


@functools.lru_cache(maxsize=None)
def _static_tables(seq):
    r = np.arange(TQ)[:, None]
    c = np.arange(TK)[None, :]
    tile_idx = np.stack([_bucket_np(r - c), _bucket_np(TQ + r - c), _bucket_np(2 * TQ + r - c)])
    tile_ok = np.stack([c <= r, np.ones((TQ, TK), bool), (2 * TQ + r - c) < WINDOW])
    tile_idx = np.where(tile_ok, tile_idx, MASK_BUCKET).astype(np.int32)
    t = np.arange(seq)[:, None]
    cblk = np.arange(LANES)[None, :]
    blk_end = cblk * CMP_STRIDE + CMP_LEN - 1
    n_cmp = (seq - CMP_LEN) // CMP_STRIDE + 1
    cmp_idx = np.where((blk_end <= t) & (cblk < n_cmp), _bucket_np(t - blk_end), MASK_BUCKET).astype(np.int32)
    e_t = np.where((np.arange(seq)[:, None] // SEL_LEN) == np.arange(LANES)[None, :], NEG, 0.0)
    cs = (np.arange(LANES) * CMP_STRIDE)[None, :]
    ss = (np.arange(LANES) * SEL_LEN)[:, None]
    ov_t = ((cs < ss + SEL_LEN) & (cs + CMP_LEN > ss)
            & (np.arange(LANES)[None, :] < n_cmp) & (np.arange(LANES)[:, None] < seq // SEL_LEN))
    return tile_idx, cmp_idx, e_t.astype(np.float32), ov_t.astype(np.float32)


def _layer(x, w_in, b_f, cmp_pos_k, cmp_pos_v, cmp_wk1, cmp_wk2, cmp_wv1, cmp_wv2,
           w_a, w_b, w_o, ln_g, ln_b, rel_bias, alpha):
    batch, seq, d_model = x.shape
    assert d_model == D_MODEL and seq % FOX_TQ == 0 and seq >= NEAR and REL_MAX_DIST <= TQ
    assert seq // SEL_LEN == N_SEL_ROWS and seq // CMP_STRIDE == LANES
    x2 = x.reshape(batch * seq, d_model)

    w_t = jnp.swapaxes(w_in, 0, 1)
    trows = lambda name: w_t[_REF_OFF[name][0]:_REF_OFF[name][0] + _REF_OFF[name][1]]
    n_small = N_HEADS + N_HEADS * N_BRANCHES
    w_small_t = jnp.concatenate(
        [trows("fox_f"), trows("nsa_gate"), jnp.zeros((LANES - n_small, d_model), F32)], axis=0)
    bf_row = jnp.concatenate([b_f.astype(F32), jnp.zeros((LANES - N_HEADS,), F32)]).reshape(1, LANES)

    h_main, h_small = _proj(x2, w_t, w_small_t)
    c_col, gates = _gates(h_small, bf_row, batch, seq)
    u_a = _fox(h_main, c_col, batch, seq)

    c0 = _MAIN_OFF["nsa_k_cmp"]
    r = h_main[:, c0:c0 + 2 * KV_WIDTH].reshape(batch, seq // _HALF, _CHUNK_COLS)
    k_cmp, v_cmp = _compress(r, cmp_pos_k, cmp_pos_v, cmp_wk1.astype(BF16), cmp_wk2.astype(BF16),
                             cmp_wv1.astype(BF16), cmp_wv2.astype(BF16))

    tile_idx, cmp_idx, e_t, ov_t = _static_tables(seq)
    table = jnp.concatenate(
        [rel_bias.T.astype(F32) * LOG2E, jnp.full((N_HEADS, 1), NEG, F32),
         jnp.zeros((N_HEADS, LANES - REL_BUCKETS - 1), F32)], axis=1)
    u_b = _nsa(h_main, k_cmp, v_cmp, jnp.asarray(cmp_idx), jnp.asarray(tile_idx), table, gates,
               jnp.asarray(e_t, BF16), jnp.asarray(ov_t, BF16), batch, seq)

    out = _out(u_a, u_b, h_main, x2, w_a.astype(BF16), w_b.astype(BF16), w_o.astype(BF16),
               ln_g.reshape(1, d_model), ln_b.reshape(1, d_model), alpha)
    return out.reshape(batch, seq, d_model)


def kernel(x, w_in, b_f, cmp_pos_k, cmp_pos_v, cmp_wk1, cmp_wk2, cmp_wv1, cmp_wv2,
           w_a, w_b, w_o, ln_g, ln_b, rel_bias):
    depth = w_in.shape[0]
    alpha = (2 * depth) ** 0.25
    for layer in range(depth):
        x = _layer(x, w_in[layer], b_f[layer], cmp_pos_k[layer], cmp_pos_v[layer], cmp_wk1[layer],
                   cmp_wk2[layer], cmp_wv1[layer], cmp_wv2[layer], w_a[layer], w_b[layer], w_o[layer],
                   ln_g[layer], ln_b[layer], rel_bias, alpha)
    return x
```

```python
import functools
import math

import jax
import jax.numpy as jnp
import numpy as np
from jax import lax
from jax.experimental import pallas as pl
from jax.experimental.pallas import tpu as pltpu

F32 = jnp.float32
BF16 = jnp.bfloat16

D_MODEL = 2048
HEAD_DIM = 128
N_HEADS = 8
WIDTH = N_HEADS * HEAD_DIM
KV_GROUPS = 2
HEADS_PER_GROUP = N_HEADS // KV_GROUPS
KV_WIDTH = KV_GROUPS * HEAD_DIM
N_BRANCHES = 3
CMP_LEN = 32
CMP_STRIDE = 16
CMP_HIDDEN = 256
SEL_LEN = 64
SEL_TOPK = 8
WINDOW = 512
REL_BUCKETS = 32
REL_MAX_DIST = 128
LN_EPS = 1e-5
NEG = -1e30
LOG2E = math.log2(math.e)
Q_PRESCALE = HEAD_DIM ** -0.5 * LOG2E

LANES = 128
SUBLANES = 8
MXU_DEPTH = 256
VMEM_LIMIT = 60 * 1024 * 1024

_REF_LAYOUT = (
    ("fox_q", WIDTH), ("fox_k", WIDTH), ("fox_v", WIDTH), ("fox_f", N_HEADS), ("fox_z", WIDTH),
    ("nsa_q", WIDTH), ("nsa_k_cmp", KV_WIDTH), ("nsa_v_cmp", KV_WIDTH), ("nsa_k_sel", KV_WIDTH),
    ("nsa_v_sel", KV_WIDTH), ("nsa_k_win", KV_WIDTH), ("nsa_v_win", KV_WIDTH),
    ("nsa_gate", N_HEADS * N_BRANCHES), ("nsa_z", WIDTH), ("merge_a", D_MODEL), ("merge_b", D_MODEL),
)
_REF_OFF = {}
_o = 0
for _n, _w in _REF_LAYOUT:
    _REF_OFF[_n] = (_o, _w)
    _o += _w

_MAIN_ORDER = ("fox_q", "fox_k", "fox_v", "fox_z", "nsa_q", "nsa_z", "merge_a", "merge_b",
               "nsa_k_cmp", "nsa_v_cmp", "nsa_k_sel", "nsa_v_sel", "nsa_k_win", "nsa_v_win")
_QUERY_COLS = ("fox_q", "nsa_q")
_MAIN_OFF = {}
_o = 0
for _n in _MAIN_ORDER:
    _MAIN_OFF[_n] = _o
    _o += _REF_OFF[_n][1]
MAIN_COLS = _o
GATE_LANE0 = N_HEADS


def _dot(a, b):
    return jnp.dot(a, b, preferred_element_type=F32)


def _dot_nt(a, b):
    return lax.dot_general(a, b, (((1,), (1,)), ((), ())), preferred_element_type=F32)


def _sigmoid(x):
    return 0.5 + 0.5 * jnp.tanh(0.5 * x)


def _silu(x):
    return x * _sigmoid(x)


PROJ_TN = 512


def _proj_tiles():
    rows, is_query = [], []
    for name in _MAIN_ORDER:
        off, width = _REF_OFF[name]
        start = _MAIN_OFF[name]
        for c in range(start, start + width):
            if c % PROJ_TN == 0:
                assert (off + c - start) % SUBLANES == 0
                rows.append((off + c - start) // SUBLANES)
                is_query.append(int(name in _QUERY_COLS))
    return np.asarray(rows, np.int32), np.asarray(is_query, np.int32)


PROJ_ROWS = 4096
PROJ_RB = 512


def _proj_kernel(rows_ref, isq_ref, x_hbm, wt_ref, wst_ref, o_ref, os_ref,
                 xb_ref, stage_ref, wb_ref, sem):
    i = pl.program_id(0)
    j = pl.program_id(1)
    n_blocks = PROJ_ROWS // PROJ_RB
    wb_ref[...] = wt_ref[...].astype(BF16)
    scale = jnp.where(isq_ref[j] == 1, Q_PRESCALE, 1.0)

    def x_copy(r, slot):
        row0 = pl.multiple_of(i * PROJ_ROWS + r * PROJ_RB, PROJ_RB)
        return pltpu.make_async_copy(x_hbm.at[pl.ds(row0, PROJ_RB), :], stage_ref.at[slot], sem.at[slot])

    def block(r):
        rows = slice(r * PROJ_RB, (r + 1) * PROJ_RB)
        o_ref[rows, :] = (_dot_nt(xb_ref[rows, :], wb_ref[...]) * scale).astype(o_ref.dtype)

    @pl.when(j == 0)
    def _():
        wsb = wst_ref[...].astype(BF16)
        x_copy(0, 0).start()
        for r in range(n_blocks):
            if r + 1 < n_blocks:
                x_copy(r + 1, (r + 1) % 2).start()
            x_copy(r, r % 2).wait()
            rows = slice(r * PROJ_RB, (r + 1) * PROJ_RB)
            xb_ref[rows, :] = stage_ref[r % 2].astype(BF16)
            os_ref[rows, :] = _dot_nt(xb_ref[rows, :], wsb)
            block(r)

    @pl.when(j > 0)
    def _():
        for r in range(n_blocks):
            block(r)


def _proj(x2, w_t, w_small_t):
    m, k = x2.shape
    rows, is_query = _proj_tiles()
    assert MAIN_COLS % PROJ_TN == 0 and len(rows) == MAIN_COLS // PROJ_TN and m % PROJ_ROWS == 0
    grid_spec = pltpu.PrefetchScalarGridSpec(
        num_scalar_prefetch=2,
        grid=(m // PROJ_ROWS, len(rows)),
        in_specs=[
            pl.BlockSpec(memory_space=pl.ANY),
            pl.BlockSpec((pl.Element(PROJ_TN), pl.Element(k)),
                         lambda i, j, rows, isq: (rows[j] * SUBLANES, 0)),
            pl.BlockSpec((LANES, k), lambda i, j, rows, isq: (0, 0)),
        ],
        out_specs=[
            pl.BlockSpec((PROJ_ROWS, PROJ_TN), lambda i, j, rows, isq: (i, j)),
            pl.BlockSpec((PROJ_ROWS, LANES), lambda i, j, rows, isq: (i, 0)),
        ],
        scratch_shapes=[pltpu.VMEM((PROJ_ROWS, k), BF16),
                        pltpu.VMEM((2, PROJ_RB, k), F32),
                        pltpu.VMEM((PROJ_TN, k), BF16),
                        pltpu.SemaphoreType.DMA((2,))],
    )
    return pl.pallas_call(
        _proj_kernel,
        grid_spec=grid_spec,
        out_shape=[jax.ShapeDtypeStruct((m, MAIN_COLS), BF16), jax.ShapeDtypeStruct((m, LANES), F32)],
        compiler_params=pltpu.CompilerParams(
            dimension_semantics=("parallel", "arbitrary"), vmem_limit_bytes=VMEM_LIMIT),
        name="proj",
    )(jnp.asarray(rows), jnp.asarray(is_query), x2, w_t, w_small_t)


_CUM_CHUNK = 256
_N_SPLIT = 3
ONES_LANE = _N_SPLIT * N_HEADS


def _split3(x):
    hi = x.astype(BF16)
    r1 = x - hi.astype(F32)
    mid = r1.astype(BF16)
    lo = (r1 - mid.astype(F32)).astype(BF16)
    return hi, mid, lo


def _gate_kernel(hs_ref, bf_ref, c_ref, g_ref):
    hs = hs_ref[...]
    g_ref[...] = _sigmoid(hs)
    z = hs + bf_ref[...]
    logf = jnp.minimum(z, 0.0) - jnp.log1p(jnp.exp(-jnp.abs(z)))
    n = _CUM_CHUNK
    tri = (lax.broadcasted_iota(jnp.int32, (n, n), 1)
           <= lax.broadcasted_iota(jnp.int32, (n, n), 0)).astype(BF16)
    lane = lax.broadcasted_iota(jnp.int32, (n, LANES), 1)
    carry = jnp.zeros((1, LANES), F32)
    for blk in range(hs.shape[0] // n):
        hi, mid, lo = _split3(logf[blk * n:(blk + 1) * n])
        cb = _dot(tri, hi) + _dot(tri, mid) + _dot(tri, lo) + carry
        carry = cb[n - 1:n, :]
        hi, mid, lo = _split3(cb * LOG2E)
        packed = jnp.where(lane == ONES_LANE, 1.0, 0.0)
        for t, term in enumerate((hi, mid, lo)):
            shifted = term.astype(F32) if t == 0 else pltpu.roll(term.astype(F32), t * N_HEADS, 1)
            packed = jnp.where((lane >= t * N_HEADS) & (lane < (t + 1) * N_HEADS), shifted, packed)
        c_ref[blk * n:(blk + 1) * n, :] = packed.astype(c_ref.dtype)


def _gates(h_small, bf_row, batch, seq):
    return pl.pallas_call(
        _gate_kernel,
        grid=(batch,),
        in_specs=[pl.BlockSpec((seq, LANES), lambda b: (b, 0)),
                  pl.BlockSpec((1, LANES), lambda b: (0, 0))],
        out_specs=[pl.BlockSpec((seq, LANES), lambda b: (b, 0)),
                   pl.BlockSpec((seq, LANES), lambda b: (b, 0))],
        out_shape=[jax.ShapeDtypeStruct(h_small.shape, BF16), jax.ShapeDtypeStruct(h_small.shape, F32)],
        compiler_params=pltpu.CompilerParams(dimension_semantics=("parallel",)),
        name="gates",
    )(h_small, bf_row)


_HALF = CMP_LEN // 2


def _gelu_tanh(x):
    return 0.5 * x * (1.0 + jnp.tanh(math.sqrt(2.0 / math.pi) * (x + 0.044715 * (x * x * x))))


def _compress_kernel(raw_ref, pk_ref, pv_ref, w1k_ref, w2k_ref, w1v_ref, w2v_ref, kc_ref, vc_ref, raw32_ref):
    seq = raw_ref.shape[0]
    n_chunks = seq // _HALF
    step = 512
    for slab in range(raw_ref.shape[1] // HEAD_DIM):
        for blk in range(seq // step):
            raw32_ref[slab, blk * step:(blk + 1) * step, :] = (
                raw_ref[blk * step:(blk + 1) * step, slab * HEAD_DIM:(slab + 1) * HEAD_DIM].astype(F32))
    for kv, (pos_ref, w1_ref, w2_ref, out_ref) in enumerate(
            ((pk_ref, w1k_ref, w2k_ref, kc_ref), (pv_ref, w1v_ref, w2v_ref, vc_ref))):
        for g in range(KV_GROUPS):
            slab = kv * KV_GROUPS + g
            first = jnp.zeros((n_chunks, CMP_HIDDEN), F32)
            second = jnp.zeros((n_chunks, CMP_HIDDEN), F32)
            for l in range(_HALF):
                a = raw32_ref[slab, pl.ds(l, n_chunks, stride=_HALF), :]
                a1 = (a + pos_ref[l:l + 1, :]).astype(BF16)
                a2 = (a + pos_ref[_HALF + l:_HALF + l + 1, :]).astype(BF16)
                first += _dot(a1, w1_ref[l * HEAD_DIM:(l + 1) * HEAD_DIM, :].astype(BF16))
                second += _dot(a2, w1_ref[(_HALF + l) * HEAD_DIM:(_HALF + l + 1) * HEAD_DIM, :].astype(BF16))
            hid = first + pltpu.roll(second, n_chunks - 1, 0)
            out = _dot(_gelu_tanh(hid).astype(BF16), w2_ref[...].astype(BF16))
            out_ref[0, :, g * HEAD_DIM:(g + 1) * HEAD_DIM] = out.astype(out_ref.dtype)


def _compress(h_main, pos_k, pos_v, w1k, w2k, w1v, w2v, batch, seq):
    n_chunks = seq // _HALF
    raw_cols = 2 * KV_WIDTH
    assert _MAIN_OFF["nsa_v_cmp"] == _MAIN_OFF["nsa_k_cmp"] + KV_WIDTH
    raw_blk = _MAIN_OFF["nsa_k_cmp"] // raw_cols
    full = lambda shape: pl.BlockSpec(shape, lambda b: (0,) * len(shape), pipeline_mode=pl.Buffered(1))
    return pl.pallas_call(
        _compress_kernel,
        grid=(batch,),
        in_specs=[pl.BlockSpec((seq, raw_cols), lambda b: (b, raw_blk)),
                  full(pos_k.shape), full(pos_v.shape),
                  full(w1k.shape), full(w2k.shape), full(w1v.shape), full(w2v.shape)],
        out_specs=[pl.BlockSpec((1, n_chunks, KV_WIDTH), lambda b: (b, 0, 0)),
                   pl.BlockSpec((1, n_chunks, KV_WIDTH), lambda b: (b, 0, 0))],
        out_shape=[jax.ShapeDtypeStruct((batch, n_chunks, KV_WIDTH), BF16),
                   jax.ShapeDtypeStruct((batch, n_chunks, KV_WIDTH), BF16)],
        scratch_shapes=[pltpu.VMEM((raw_cols // HEAD_DIM, seq, HEAD_DIM), F32)],
        compiler_params=pltpu.CompilerParams(
            dimension_semantics=("parallel",), vmem_limit_bytes=VMEM_LIMIT),
        name="compress",
    )(h_main, pos_k, pos_v, w1k, w2k, w1v, w2v)


def _lane_tile(x, n):
    return x if n == 1 else jnp.concatenate([x] * n, axis=1)


def _lane_fold(p):
    out = p[:, :LANES]
    for t in range(1, p.shape[1] // LANES):
        out = out + p[:, t * LANES:(t + 1) * LANES]
    return out


def _first_weights(s, m_ref, l_ref, rows=slice(None)):
    m = jnp.broadcast_to(jnp.max(s, axis=1, keepdims=True), (s.shape[0], LANES))
    p = jnp.exp2(s - _lane_tile(m, s.shape[1] // LANES))
    m_ref[rows, :] = m
    l_ref[rows, :] = _lane_fold(p)
    return p.astype(BF16)


def _next_weights(s, row_const, m_ref, l_ref, rows=slice(None)):
    m_prev = m_ref[rows, :]
    m_cur = jnp.max(s, axis=1, keepdims=True)
    if row_const is not None:
        m_cur = m_cur + row_const
    m_new = jnp.maximum(m_prev, m_cur)
    alpha = jnp.exp2(m_prev - m_new)
    shift = m_new if row_const is None else m_new - row_const
    p = jnp.exp2(s - _lane_tile(shift, s.shape[1] // LANES))
    l_ref[rows, :] = alpha * l_ref[rows, :] + _lane_fold(p)
    m_ref[rows, :] = m_new
    return alpha, p.astype(BF16)


def _softmax_first(s, v, m_ref, l_ref, acc_ref, rows=slice(None)):
    acc_ref[rows, :] = _dot(_first_weights(s, m_ref, l_ref, rows), v)


def _softmax_update(s, v, row_const, m_ref, l_ref, acc_ref, rows=slice(None)):
    alpha, p = _next_weights(s, row_const, m_ref, l_ref, rows)
    acc_ref[rows, :] = alpha * acc_ref[rows, :] + _dot(p, v)


def _skewed(stages, jobs, before_tick=None):
    state = [None] * len(jobs)
    for tick in range(len(jobs) + len(stages) - 1):
        if before_tick and tick in before_tick:
            before_tick[tick]()
        for k, stage in enumerate(stages):
            j = tick - k
            if 0 <= j < len(jobs):
                state[j] = stage(jobs[j], state[j])


def _softmax_finish(l_ref, acc_ref, rows=slice(None)):
    return acc_ref[rows, :] / jnp.sum(l_ref[rows, :], axis=1, keepdims=True)


def _softmax_once(s, v):
    m = jnp.max(s, axis=1, keepdims=True)
    p = jnp.exp2(s - m)
    return _dot(p.astype(BF16), v) / jnp.sum(_lane_fold(p), axis=1, keepdims=True)


FOX_TQ = 512
FOX_HALF = FOX_TQ // 2


def _fox_routing():
    pq = np.zeros((N_HEADS, LANES, LANES), np.float32)
    pk = np.zeros((N_HEADS, LANES, LANES), np.float32)
    for h in range(N_HEADS):
        for t in range(_N_SPLIT):
            pq[h, t * N_HEADS + h, t] = 1.0
            pq[h, ONES_LANE, _N_SPLIT + t] = 1.0
            pk[h, ONES_LANE, t] = 1.0
            pk[h, t * N_HEADS + h, _N_SPLIT + t] = -1.0
    return pq, pk


def _fox_kernel(q_ref, k_ref, v_ref, z_ref, cq_ref, ck_ref, pq_ref, pk_ref, o_ref,
                kaug_ref, qaug_ref, sz_ref, m_ref, l_ref, acc_ref):
    i = pl.program_id(1)
    tq, half = FOX_TQ, FOX_HALF
    seq = k_ref.shape[0]

    @pl.when(i == 0)
    def _():
        for h in range(N_HEADS):
            for blk in range(seq // tq):
                rows = slice(blk * tq, (blk + 1) * tq)
                kaug_ref[h, rows, :HEAD_DIM] = k_ref[rows, h * HEAD_DIM:(h + 1) * HEAD_DIM]
                kaug_ref[h, rows, HEAD_DIM:] = _dot(ck_ref[rows, :], pk_ref[h]).astype(BF16)

    causal = (lax.broadcasted_iota(jnp.int32, (half, half), 1)
              <= lax.broadcasted_iota(jnp.int32, (half, half), 0))
    diag = pl.multiple_of(i * tq, tq)
    diag2 = pl.multiple_of(i * tq + half, half)
    heads = tuple(range(N_HEADS))
    rows = lambda h: slice(h * tq, (h + 1) * tq)
    top = lambda h: slice(h * tq, h * tq + half)
    bot = lambda h: slice(h * tq + half, (h + 1) * tq)
    cols = lambda h: slice(h * HEAD_DIM, (h + 1) * HEAD_DIM)

    def diag_scores(h, _):
        qaug_ref[rows(h), :HEAD_DIM] = q_ref[:, cols(h)]
        qaug_ref[rows(h), HEAD_DIM:] = _dot(cq_ref[...], pq_ref[h]).astype(BF16)
        s_left = _dot_nt(qaug_ref[rows(h), :], kaug_ref[h, pl.ds(diag, half), :])
        s_right = _dot_nt(qaug_ref[bot(h), :], kaug_ref[h, pl.ds(diag2, half), :])
        return s_left, s_right

    def diag_weights(h, scores):
        s_left, s_right = scores
        p_top = _first_weights(jnp.where(causal, s_left[:half], NEG), m_ref, l_ref, top(h))
        p_bot = _first_weights(jnp.concatenate([s_left[half:], jnp.where(causal, s_right, NEG)], axis=1),
                               m_ref, l_ref, bot(h))
        return p_top, p_bot

    def diag_values(h, weights):
        p_top, p_bot = weights
        acc_ref[top(h), :] = _dot(p_top, v_ref[pl.ds(diag, half), cols(h)])
        acc_ref[bot(h), :] = _dot(p_bot, v_ref[pl.ds(diag, tq), cols(h)])
        sz_ref[:, cols(h)] = _silu(z_ref[:, cols(h)].astype(F32))

    _skewed((diag_scores, diag_weights, diag_values), heads)

    def body(j, carry):
        off = pl.multiple_of(j * tq, tq)

        def scores(h, _):
            return _dot_nt(qaug_ref[rows(h), :], kaug_ref[h, pl.ds(off, tq), :])

        def weights(h, s):
            return _next_weights(s, None, m_ref, l_ref, rows(h))

        def values(h, rescale_and_weights):
            alpha, p = rescale_and_weights
            acc_ref[rows(h), :] = alpha * acc_ref[rows(h), :] + _dot(p, v_ref[pl.ds(off, tq), cols(h)])

        _skewed((scores, weights, values), heads)
        return carry

    lax.fori_loop(0, i, body, 0)
    for h in heads:
        o_ref[:, cols(h)] = (_softmax_finish(l_ref, acc_ref, rows(h)) * sz_ref[:, cols(h)]).astype(o_ref.dtype)


def _fox(h_main, c_packed, batch, seq):
    tq = FOX_TQ
    nq = seq // tq
    blk = lambda name: _MAIN_OFF[name] // WIDTH
    pq, pk = (jnp.asarray(p, BF16) for p in _fox_routing())
    full = lambda shape: pl.BlockSpec(shape, lambda b, i: (0,) * len(shape))
    stat = pltpu.VMEM((N_HEADS * tq, LANES), F32)
    return pl.pallas_call(
        _fox_kernel,
        grid=(batch, nq),
        in_specs=[
            pl.BlockSpec((tq, WIDTH), lambda b, i: (b * nq + i, blk("fox_q"))),
            pl.BlockSpec((seq, WIDTH), lambda b, i: (b, blk("fox_k"))),
            pl.BlockSpec((seq, WIDTH), lambda b, i: (b, blk("fox_v"))),
            pl.BlockSpec((tq, WIDTH), lambda b, i: (b * nq + i, blk("fox_z"))),
            pl.BlockSpec((tq, LANES), lambda b, i: (b * nq + i, 0)),
            pl.BlockSpec((seq, LANES), lambda b, i: (b, 0)),
            full(pq.shape), full(pk.shape),
        ],
        out_specs=pl.BlockSpec((tq, WIDTH), lambda b, i: (b * nq + i, 0)),
        out_shape=jax.ShapeDtypeStruct((batch * seq, WIDTH), BF16),
        scratch_shapes=[pltpu.VMEM((N_HEADS, seq, MXU_DEPTH), BF16),
                        pltpu.VMEM((N_HEADS * tq, MXU_DEPTH), BF16),
                        pltpu.VMEM((tq, WIDTH), F32),
                        stat, stat, stat],
        compiler_params=pltpu.CompilerParams(
            dimension_semantics=("parallel", "arbitrary"), vmem_limit_bytes=VMEM_LIMIT),
        name="fox",
    )(h_main, h_main, h_main, h_main, c_packed, c_packed, pq, pk)


TQ = 256
TK = TQ
NEAR = WINDOW + TQ
N_NEAR = NEAR // TK
FAR_TK = 2 * TK
N_SEL_ROWS = 32
MASK_BUCKET = REL_BUCKETS
GROUP_ROWS = HEADS_PER_GROUP * TQ
T_DIAG, T_PREV, T_WIN2, T_FAR, T_NONE = range(5)
N_GATHERED_TILES = 3


def _bias_lookup(tab_ref, h, idx):
    row = jnp.broadcast_to(tab_ref[h:h + 1, :], idx.shape)
    return jnp.take_along_axis(row, idx, axis=1, mode="promise_in_bounds")


def _nsa_kernel(q_ref, ks_ref, vs_ref, kw_ref, vw_ref, z_ref, kc_ref, vc_ref, cidx_ref, tidx_ref,
                tab_ref, g_ref, et_ref, ov_ref, o_ref,
                ksaug_ref, qs_ref, qaug_ref, m_ref, l_ref, acc_ref, oc_ref, ow_ref, gs_ref, t_ref, far_ref,
                cb_ref):
    i = pl.program_id(1)
    t0 = i * TQ

    @pl.when((pl.program_id(0) == 0) & (i == 0))
    def _():
        for h in range(N_HEADS):
            for d in range(N_GATHERED_TILES):
                for half in range(TK // LANES):
                    cs = slice(half * LANES, (half + 1) * LANES)
                    t_ref[h, d, :, cs] = _bias_lookup(tab_ref, h, tidx_ref[d, :, cs])
            for blk in range(cidx_ref.shape[0] // TQ):
                rows = slice(blk * TQ, (blk + 1) * TQ)
                cb_ref[h, rows, :] = _bias_lookup(tab_ref, h, cidx_ref[rows, :])
            far = jnp.broadcast_to(tab_ref[h:h + 1, REL_BUCKETS - 1:REL_BUCKETS], (TQ, LANES))
            far_ref[h * TQ:(h + 1) * TQ, :] = far
            t_ref[h, T_FAR] = _lane_tile(far, TK // LANES)
            t_ref[h, T_NONE] = jnp.full((TQ, TK), NEG, F32)

    @pl.when(i == 0)
    def _():
        for g in range(KV_GROUPS):
            ksaug_ref[g, :, :HEAD_DIM] = ks_ref[:, g * HEAD_DIM:(g + 1) * HEAD_DIM]
            ksaug_ref[g, :, HEAD_DIM:] = et_ref[...]

    cmaskf = (lax.broadcasted_iota(jnp.int32, (TQ, LANES), 1) * CMP_STRIDE + (CMP_LEN - 1)
              <= t0 + lax.broadcasted_iota(jnp.int32, (TQ, LANES), 0)).astype(F32)
    j_t = lax.broadcasted_iota(jnp.int32, (N_SEL_ROWS, TQ), 0)
    t_t = t0 + lax.broadcasted_iota(jnp.int32, (N_SEL_ROWS, TQ), 1)
    cur_t = t_t // SEL_LEN
    forced_t = (j_t == 0) | (j_t == cur_t) | (j_t == cur_t - 1)
    valid_t = j_t * SEL_LEN <= t_t

    first_blk = jnp.maximum(i - (N_NEAR - 1), 0)
    near0 = pl.multiple_of(first_blk * TK, TK)

    def near_tile(kk, two_back):
        dist = i - (first_blk + kk)
        return jnp.where(dist == 0, T_DIAG, jnp.where(dist == 1, T_PREV,
                         jnp.where(dist == 2, two_back, T_NONE)))

    def near_bias(heads, two_back):
        return jnp.concatenate(
            [jnp.concatenate([t_ref[h, near_tile(kk, two_back)] for kk in range(N_NEAR)], axis=1)
             for h in heads], axis=0)

    groups = tuple((g, g * HEAD_DIM, tuple(range(g * HEADS_PER_GROUP, (g + 1) * HEADS_PER_GROUP)),
                    slice(g * GROUP_ROWS, (g + 1) * GROUP_ROWS)) for g in range(KV_GROUPS))
    hrows = lambda h: slice(h * TQ, (h + 1) * TQ)
    for h in range(N_HEADS):
        q = q_ref[:, h * HEAD_DIM:(h + 1) * HEAD_DIM]
        qs_ref[hrows(h), :] = q
        qaug_ref[hrows(h), :HEAD_DIM] = q

    chain = {}

    def compressed_branch():
        cbias = jnp.concatenate([cb_ref[h, pl.ds(pl.multiple_of(t0, TQ), TQ), :] for h in range(N_HEADS)],
                                axis=0)
        sc = jnp.concatenate([_dot_nt(qs_ref[grows, :], kc_ref[0, :, glo:glo + HEAD_DIM])
                              for g, glo, heads, grows in groups], axis=0) + cbias
        e = jnp.exp2(sc - jnp.max(sc, axis=1, keepdims=True))
        p = e / jnp.sum(e, axis=1, keepdims=True) * jnp.concatenate([cmaskf] * N_HEADS, axis=0)
        imp_parts = []
        for g, glo, heads, grows in groups:
            oc_ref[grows, :] = _dot(p[grows].astype(BF16), vc_ref[0, :, glo:glo + HEAD_DIM])
            psum = p[hrows(heads[0])]
            for h in heads[1:]:
                psum = psum + p[hrows(h)]
            p_hi = psum.astype(BF16)
            p_lo = (psum - p_hi.astype(F32)).astype(BF16)
            imp_parts.append((_dot_nt(ov_ref[...], p_hi) + _dot_nt(ov_ref[...], p_lo))[:N_SEL_ROWS])
        chain["importance"] = jnp.concatenate(imp_parts, axis=1)

    def block_selection():
        both = lambda a: jnp.concatenate([a] * KV_GROUPS, axis=1)
        j_b = both(j_t)
        x = jnp.where(both(valid_t), jnp.where(both(forced_t), -NEG, chain["importance"]), NEG)
        cnt = jnp.zeros(x.shape, F32)
        for jp in range(N_SEL_ROWS):
            row = x[jp:jp + 1, :]
            beats = (row > x) | ((row == x) & (j_b > jp))
            cnt = cnt + jnp.where(beats, 1.0, 0.0)
        unsel_t = jnp.where(cnt < SEL_TOPK, 0.0, 1.0)
        unsel_t = jnp.concatenate([unsel_t, jnp.zeros((LANES - N_SEL_ROWS, x.shape[1]), F32)], axis=0)
        unsel = unsel_t.T.astype(BF16)
        for h in range(N_HEADS):
            g = h // HEADS_PER_GROUP
            qaug_ref[hrows(h), HEAD_DIM:] = unsel[g * TQ:(g + 1) * TQ]

    def near_scores(job, _):
        branch, (g, glo, heads, grows) = job
        if branch == "window":
            return (_dot_nt(qs_ref[grows, :], kw_ref[pl.ds(near0, NEAR), glo:glo + HEAD_DIM])
                    + near_bias(heads, T_WIN2))
        return _dot_nt(qaug_ref[grows, :], ksaug_ref[g, pl.ds(near0, NEAR), :]) + near_bias(heads, T_FAR)

    def near_weights(job, s):
        branch, (g, glo, heads, grows) = job
        if branch == "window":
            p = jnp.exp2(s - jnp.max(s, axis=1, keepdims=True))
            return p.astype(BF16), _lane_fold(p)
        return _first_weights(s, m_ref, l_ref, grows)

    def near_values(job, weights):
        branch, (g, glo, heads, grows) = job
        if branch == "selected":
            acc_ref[grows, :] = _dot(weights, vs_ref[pl.ds(near0, NEAR), glo:glo + HEAD_DIM])
            return
        p, l = weights
        o_win = _dot(p, vw_ref[pl.ds(near0, NEAR), glo:glo + HEAD_DIM]) / jnp.sum(l, axis=1, keepdims=True)
        for n, h in enumerate(heads):
            gl = GATE_LANE0 + h * N_BRANCHES
            sz = _silu(z_ref[:, h * HEAD_DIM:(h + 1) * HEAD_DIM].astype(F32))
            gs_ref[hrows(h), :] = g_ref[:, gl + 1:gl + 2] * sz
            ow_ref[hrows(h), :] = (g_ref[:, gl:gl + 1] * oc_ref[hrows(h), :]
                                   + g_ref[:, gl + 2:gl + 3] * o_win[n * TQ:(n + 1) * TQ]) * sz

    _skewed((near_scores, near_weights, near_values),
            tuple((branch, group) for branch in ("window", "selected") for group in groups),
            before_tick={1: compressed_branch, 2: block_selection})

    def sel_far(off, width):
        def scores(group, _):
            g, glo, heads, grows = group
            return _dot_nt(qaug_ref[grows, :], ksaug_ref[g, pl.ds(off, width), :])

        def weights(group, s):
            g, glo, heads, grows = group
            return _next_weights(s, far_ref[grows, :], m_ref, l_ref, grows)

        def values(group, rescale_and_weights):
            g, glo, heads, grows = group
            alpha, p = rescale_and_weights
            acc_ref[grows, :] = alpha * acc_ref[grows, :] + _dot(p, vs_ref[pl.ds(off, width), glo:glo + HEAD_DIM])

        _skewed((scores, weights, values), groups)

    def sel_far_pair(j, carry):
        sel_far(pl.multiple_of(j * FAR_TK, FAR_TK), FAR_TK)
        return carry

    lax.fori_loop(0, first_blk // 2, sel_far_pair, 0)

    @pl.when(first_blk % 2 == 1)
    def _():
        sel_far(pl.multiple_of((first_blk - 1) * TK, TK), TK)

    for h in range(N_HEADS):
        o_ref[:, h * HEAD_DIM:(h + 1) * HEAD_DIM] = (
            ow_ref[hrows(h), :] + gs_ref[hrows(h), :] * _softmax_finish(l_ref, acc_ref, hrows(h))
        ).astype(o_ref.dtype)


def _nsa(h_main, k_cmp, v_cmp, cmp_idx, tile_idx, table, gates, e_t, ov_t, batch, seq):
    nq = seq // TQ
    wblk = lambda name: _MAIN_OFF[name] // WIDTH
    kvblk = lambda name: _MAIN_OFF[name] // KV_WIDTH
    full = lambda shape: pl.BlockSpec(shape, lambda b, i: (0,) * len(shape),
                                      pipeline_mode=pl.Buffered(1))
    kv_spec = lambda name: pl.BlockSpec((seq, KV_WIDTH), lambda b, i: (b, kvblk(name)))
    n_chunks = k_cmp.shape[1]
    stat = pltpu.VMEM((N_HEADS * TQ, LANES), F32)
    return pl.pallas_call(
        _nsa_kernel,
        grid=(batch, nq),
        in_specs=[
            pl.BlockSpec((TQ, WIDTH), lambda b, i: (b * nq + i, wblk("nsa_q"))),
            kv_spec("nsa_k_sel"), kv_spec("nsa_v_sel"), kv_spec("nsa_k_win"), kv_spec("nsa_v_win"),
            pl.BlockSpec((TQ, WIDTH), lambda b, i: (b * nq + i, wblk("nsa_z"))),
            pl.BlockSpec((1, n_chunks, KV_WIDTH), lambda b, i: (b, 0, 0)),
            pl.BlockSpec((1, n_chunks, KV_WIDTH), lambda b, i: (b, 0, 0)),
            full(cmp_idx.shape), full(tile_idx.shape), full(table.shape),
            pl.BlockSpec((TQ, LANES), lambda b, i: (b * nq + i, 0)),
            full(e_t.shape), full(ov_t.shape),
        ],
        out_specs=pl.BlockSpec((TQ, WIDTH), lambda b, i: (b * nq + i, 0)),
        out_shape=jax.ShapeDtypeStruct((batch * seq, WIDTH), BF16),
        scratch_shapes=[pltpu.VMEM((KV_GROUPS, seq, MXU_DEPTH), BF16),
                        pltpu.VMEM((N_HEADS * TQ, HEAD_DIM), BF16),
                        pltpu.VMEM((N_HEADS * TQ, MXU_DEPTH), BF16),
                        stat, stat, stat,
                        stat, stat, stat,
                        pltpu.VMEM((N_HEADS, T_NONE + 1, TQ, TK), F32),
                        pltpu.VMEM((N_HEADS * TQ, LANES), F32),
                        pltpu.VMEM((N_HEADS, seq, LANES), F32)],
        compiler_params=pltpu.CompilerParams(
            dimension_semantics=("arbitrary", "arbitrary"), vmem_limit_bytes=VMEM_LIMIT),
        name="nsa",
    )(h_main, h_main, h_main, h_main, h_main, h_main, k_cmp, v_cmp, cmp_idx, tile_idx, table, gates,
      e_t, ov_t)


OUT_TM = 512
OUT_TN = 512


def _out_kernel(alpha, ua_ref, ub_ref, ga_ref, gb_ref, x_ref, wa_ref, wb_ref, wo_ref, lg_ref, lb_ref,
                o_ref, merged_ref):
    d_model = o_ref.shape[1]
    chunks = tuple(slice(c * OUT_TN, (c + 1) * OUT_TN) for c in range(d_model // OUT_TN))
    for cs in chunks:
        ya = _dot(ua_ref[...], wa_ref[:, cs])
        yb = _dot(ub_ref[...], wb_ref[:, cs])
        merged_ref[:, cs] = (_sigmoid(ga_ref[:, cs].astype(F32)) * ya
                             + _sigmoid(gb_ref[:, cs].astype(F32)) * yb).astype(merged_ref.dtype)
    total = jnp.zeros((o_ref.shape[0], LANES), F32)
    for cs in chunks:
        y = alpha * x_ref[:, cs] + _dot(merged_ref[...], wo_ref[:, cs])
        o_ref[:, cs] = y
        total = total + _lane_fold(y)
    mu = jnp.sum(total, axis=1, keepdims=True) * (1.0 / d_model)
    sq = jnp.zeros((o_ref.shape[0], LANES), F32)
    for cs in chunks:
        d = o_ref[:, cs] - mu
        sq = sq + _lane_fold(d * d)
    rstd = lax.rsqrt(jnp.sum(sq, axis=1, keepdims=True) * (1.0 / d_model) + LN_EPS)
    for cs in chunks:
        o_ref[:, cs] = (o_ref[:, cs] - mu) * rstd * lg_ref[:, cs] + lb_ref[:, cs]


def _out(u_a, u_b, h_main, x2, w_a, w_b, w_o, ln_g, ln_b, alpha, tm=OUT_TM):
    m = x2.shape[0]
    mblk = lambda name: _MAIN_OFF[name] // D_MODEL
    const = lambda shape: pl.BlockSpec(shape, lambda i: (0, 0), pipeline_mode=pl.Buffered(1))
    return pl.pallas_call(
        functools.partial(_out_kernel, alpha),
        grid=(m // tm,),
        in_specs=[
            pl.BlockSpec((tm, WIDTH), lambda i: (i, 0)),
            pl.BlockSpec((tm, WIDTH), lambda i: (i, 0)),
            pl.BlockSpec((tm, D_MODEL), lambda i: (i, mblk("merge_a"))),
            pl.BlockSpec((tm, D_MODEL), lambda i: (i, mblk("merge_b"))),
            pl.BlockSpec((tm, D_MODEL), lambda i: (i, 0)),
            const(w_a.shape), const(w_b.shape), const(w_o.shape),
            const(ln_g.shape), const(ln_b.shape),
        ],
        out_specs=pl.BlockSpec((tm, D_MODEL), lambda i: (i, 0)),
        out_shape=jax.ShapeDtypeStruct((m, D_MODEL), F32),
        scratch_shapes=[pltpu.VMEM((tm, D_MODEL), BF16)],
        compiler_params=pltpu.CompilerParams(
            dimension_semantics=("parallel",), vmem_limit_bytes=VMEM_LIMIT),
        name="out",
    )(u_a, u_b, h_main, h_main, x2, w_a, w_b, w_o, ln_g, ln_b)


def _bucket_np(dist):
    n = np.maximum(dist, 0)
    exact = REL_BUCKETS // 2
    large = exact + (np.log(np.maximum(n, 1).astype(np.float32) / exact)
                     / math.log(REL_MAX_DIST / exact) * (REL_BUCKETS - exact)).astype(np.int32)
    return np.where(n < exact, n, np.minimum(large, REL_BUCKETS - 1)).astype(np.int32)


@functools.lru_cache(maxsize=None)
def _static_tables(seq):
    r = np.arange(TQ)[:, None]
    c = np.arange(TK)[None, :]
    tile_idx = np.stack([_bucket_np(r - c), _bucket_np(TQ + r - c), _bucket_np(2 * TQ + r - c)])
    tile_ok = np.stack([c <= r, np.ones((TQ, TK), bool), (2 * TQ + r - c) < WINDOW])
    tile_idx = np.where(tile_ok, tile_idx, MASK_BUCKET).astype(np.int32)
    t = np.arange(seq)[:, None]
    cblk = np.arange(LANES)[None, :]
    blk_end = cblk * CMP_STRIDE + CMP_LEN - 1
    n_cmp = (seq - CMP_LEN) // CMP_STRIDE + 1
    cmp_idx = np.where((blk_end <= t) & (cblk < n_cmp), _bucket_np(t - blk_end), MASK_BUCKET).astype(np.int32)
    e_t = np.where((np.arange(seq)[:, None] // SEL_LEN) == np.arange(LANES)[None, :], NEG, 0.0)
    cs = (np.arange(LANES) * CMP_STRIDE)[None, :]
    ss = (np.arange(LANES) * SEL_LEN)[:, None]
    ov_t = ((cs < ss + SEL_LEN) & (cs + CMP_LEN > ss)
            & (np.arange(LANES)[None, :] < n_cmp) & (np.arange(LANES)[:, None] < seq // SEL_LEN))
    return tile_idx, cmp_idx, e_t.astype(np.float32), ov_t.astype(np.float32)


def _layer(x, w_in, b_f, cmp_pos_k, cmp_pos_v, cmp_wk1, cmp_wk2, cmp_wv1, cmp_wv2,
           w_a, w_b, w_o, ln_g, ln_b, rel_bias, alpha):
    batch, seq, d_model = x.shape
    assert d_model == D_MODEL and seq % FOX_TQ == 0 and seq >= NEAR and REL_MAX_DIST <= TQ
    assert seq // SEL_LEN == N_SEL_ROWS and seq // CMP_STRIDE == LANES
    x2 = x.reshape(batch * seq, d_model)

    w_t = jnp.swapaxes(w_in, 0, 1)
    trows = lambda name: w_t[_REF_OFF[name][0]:_REF_OFF[name][0] + _REF_OFF[name][1]]
    n_small = N_HEADS + N_HEADS * N_BRANCHES
    w_small_t = jnp.concatenate(
        [trows("fox_f"), trows("nsa_gate"), jnp.zeros((LANES - n_small, d_model), F32)], axis=0)
    bf_row = jnp.concatenate([b_f.astype(F32), jnp.zeros((LANES - N_HEADS,), F32)]).reshape(1, LANES)

    h_main, h_small = _proj(x2, w_t, w_small_t)
    c_col, gates = _gates(h_small, bf_row, batch, seq)
    u_a = _fox(h_main, c_col, batch, seq)

    k_cmp, v_cmp = _compress(h_main, cmp_pos_k, cmp_pos_v, cmp_wk1, cmp_wk2, cmp_wv1, cmp_wv2, batch, seq)

    tile_idx, cmp_idx, e_t, ov_t = _static_tables(seq)
    table = jnp.concatenate(
        [rel_bias.T.astype(F32) * LOG2E, jnp.full((N_HEADS, 1), NEG, F32),
         jnp.zeros((N_HEADS, LANES - REL_BUCKETS - 1), F32)], axis=1)
    u_b = _nsa(h_main, k_cmp, v_cmp, jnp.asarray(cmp_idx), jnp.asarray(tile_idx), table, gates,
               jnp.asarray(e_t, BF16), jnp.asarray(ov_t, BF16), batch, seq)

    out = _out(u_a, u_b, h_main, x2, w_a.astype(BF16), w_b.astype(BF16), w_o.astype(BF16),
               ln_g.reshape(1, d_model), ln_b.reshape(1, d_model), alpha)
    return out.reshape(batch, seq, d_model)


def kernel(x, w_in, b_f, cmp_pos_k, cmp_pos_v, cmp_wk1, cmp_wk2, cmp_wv1, cmp_wv2,
           w_a, w_b, w_o, ln_g, ln_b, rel_bias):
    depth = w_in.shape[0]
    alpha = (2 * depth) ** 0.25
    for layer in range(depth):
        x = _layer(x, w_in[layer], b_f[layer], cmp_pos_k[layer], cmp_pos_v[layer], cmp_wk1[layer],
                   cmp_wk2[layer], cmp_wv1[layer], cmp_wv2[layer], w_a[layer], w_b[layer], w_o[layer],
                   ln_g[layer], ln_b[layer], rel_bias, alpha)
    return x
```

```python
import functools
import math

import jax
import jax.numpy as jnp
import numpy as np
from jax import lax
from jax.experimental import pallas as pl
from jax.experimental.pallas import tpu as pltpu

F32 = jnp.float32
BF16 = jnp.bfloat16

D_MODEL = 2048
HEAD_DIM = 128
N_HEADS = 8
WIDTH = N_HEADS * HEAD_DIM
KV_GROUPS = 2
HEADS_PER_GROUP = N_HEADS // KV_GROUPS
KV_WIDTH = KV_GROUPS * HEAD_DIM
N_BRANCHES = 3
CMP_LEN = 32
CMP_STRIDE = 16
CMP_HIDDEN = 256
SEL_LEN = 64
SEL_TOPK = 8
WINDOW = 512
REL_BUCKETS = 32
REL_MAX_DIST = 128
LN_EPS = 1e-5
NEG = -1e30
LOG2E = math.log2(math.e)
Q_PRESCALE = HEAD_DIM ** -0.5 * LOG2E

LANES = 128
SUBLANES = 8
MXU_DEPTH = 256
VMEM_LIMIT = 60 * 1024 * 1024

_REF_LAYOUT = (
    ("fox_q", WIDTH), ("fox_k", WIDTH), ("fox_v", WIDTH), ("fox_f", N_HEADS), ("fox_z", WIDTH),
    ("nsa_q", WIDTH), ("nsa_k_cmp", KV_WIDTH), ("nsa_v_cmp", KV_WIDTH), ("nsa_k_sel", KV_WIDTH),
    ("nsa_v_sel", KV_WIDTH), ("nsa_k_win", KV_WIDTH), ("nsa_v_win", KV_WIDTH),
    ("nsa_gate", N_HEADS * N_BRANCHES), ("nsa_z", WIDTH), ("merge_a", D_MODEL), ("merge_b", D_MODEL),
)
_REF_OFF = {}
_o = 0
for _n, _w in _REF_LAYOUT:
    _REF_OFF[_n] = (_o, _w)
    _o += _w

_MAIN_ORDER = ("fox_q", "fox_k", "fox_v", "fox_z", "nsa_q", "nsa_z", "merge_a", "merge_b",
               "nsa_k_cmp", "nsa_v_cmp", "nsa_k_sel", "nsa_v_sel", "nsa_k_win", "nsa_v_win")
_QUERY_COLS = ("fox_q", "nsa_q")
_MAIN_OFF = {}
_o = 0
for _n in _MAIN_ORDER:
    _MAIN_OFF[_n] = _o
    _o += _REF_OFF[_n][1]
MAIN_COLS = _o
GATE_LANE0 = N_HEADS


def _dot(a, b):
    return jnp.dot(a, b, preferred_element_type=F32)


def _dot_nt(a, b):
    return lax.dot_general(a, b, (((1,), (1,)), ((), ())), preferred_element_type=F32)


def _sigmoid(x):
    return 0.5 + 0.5 * jnp.tanh(0.5 * x)


def _silu(x):
    return x * _sigmoid(x)


PROJ_TN = 512


def _proj_tiles():
    rows, is_query = [], []
    for name in _MAIN_ORDER:
        off, width = _REF_OFF[name]
        start = _MAIN_OFF[name]
        for c in range(start, start + width):
            if c % PROJ_TN == 0:
                assert (off + c - start) % SUBLANES == 0
                rows.append((off + c - start) // SUBLANES)
                is_query.append(int(name in _QUERY_COLS))
    return np.asarray(rows, np.int32), np.asarray(is_query, np.int32)


PROJ_ROWS = 4096
PROJ_RB = 512


def _proj_kernel(rows_ref, isq_ref, x_hbm, wt_ref, wst_ref, o_ref, os_ref,
                 xb_ref, stage_ref, wb_ref, sem):
    i = pl.program_id(0)
    j = pl.program_id(1)
    n_blocks = PROJ_ROWS // PROJ_RB
    wb_ref[...] = wt_ref[...].astype(BF16)
    scale = jnp.where(isq_ref[j] == 1, Q_PRESCALE, 1.0)

    def x_copy(r, slot):
        row0 = pl.multiple_of(i * PROJ_ROWS + r * PROJ_RB, PROJ_RB)
        return pltpu.make_async_copy(x_hbm.at[pl.ds(row0, PROJ_RB), :], stage_ref.at[slot], sem.at[slot])

    def block(r):
        rows = slice(r * PROJ_RB, (r + 1) * PROJ_RB)
        o_ref[rows, :] = (_dot_nt(xb_ref[rows, :], wb_ref[...]) * scale).astype(o_ref.dtype)

    @pl.when(j == 0)
    def _():
        wsb = wst_ref[...].astype(BF16)
        x_copy(0, 0).start()
        for r in range(n_blocks):
            if r + 1 < n_blocks:
                x_copy(r + 1, (r + 1) % 2).start()
            x_copy(r, r % 2).wait()
            rows = slice(r * PROJ_RB, (r + 1) * PROJ_RB)
            xb_ref[rows, :] = stage_ref[r % 2].astype(BF16)
            os_ref[rows, :] = _dot_nt(xb_ref[rows, :], wsb)
            block(r)

    @pl.when(j > 0)
    def _():
        for r in range(n_blocks):
            block(r)


def _proj(x2, w_t, w_small_t):
    m, k = x2.shape
    rows, is_query = _proj_tiles()
    assert MAIN_COLS % PROJ_TN == 0 and len(rows) == MAIN_COLS // PROJ_TN and m % PROJ_ROWS == 0
    grid_spec = pltpu.PrefetchScalarGridSpec(
        num_scalar_prefetch=2,
        grid=(m // PROJ_ROWS, len(rows)),
        in_specs=[
            pl.BlockSpec(memory_space=pl.ANY),
            pl.BlockSpec((pl.Element(PROJ_TN), pl.Element(k)),
                         lambda i, j, rows, isq: (rows[j] * SUBLANES, 0)),
            pl.BlockSpec((LANES, k), lambda i, j, rows, isq: (0, 0)),
        ],
        out_specs=[
            pl.BlockSpec((PROJ_ROWS, PROJ_TN), lambda i, j, rows, isq: (i, j)),
            pl.BlockSpec((PROJ_ROWS, LANES), lambda i, j, rows, isq: (i, 0)),
        ],
        scratch_shapes=[pltpu.VMEM((PROJ_ROWS, k), BF16),
                        pltpu.VMEM((2, PROJ_RB, k), F32),
                        pltpu.VMEM((PROJ_TN, k), BF16),
                        pltpu.SemaphoreType.DMA((2,))],
    )
    return pl.pallas_call(
        _proj_kernel,
        grid_spec=grid_spec,
        out_shape=[jax.ShapeDtypeStruct((m, MAIN_COLS), BF16), jax.ShapeDtypeStruct((m, LANES), F32)],
        compiler_params=pltpu.CompilerParams(
            dimension_semantics=("parallel", "arbitrary"), vmem_limit_bytes=VMEM_LIMIT),
        name="proj",
    )(jnp.asarray(rows), jnp.asarray(is_query), x2, w_t, w_small_t)


_CUM_CHUNK = 256
_N_SPLIT = 3
ONES_LANE = _N_SPLIT * N_HEADS


def _split3(x):
    hi = x.astype(BF16)
    r1 = x - hi.astype(F32)
    mid = r1.astype(BF16)
    lo = (r1 - mid.astype(F32)).astype(BF16)
    return hi, mid, lo


def _gate_kernel(hs_ref, bf_ref, c_ref, g_ref):
    hs = hs_ref[...]
    g_ref[...] = _sigmoid(hs)
    z = hs + bf_ref[...]
    logf = jnp.minimum(z, 0.0) - jnp.log1p(jnp.exp(-jnp.abs(z)))
    n = _CUM_CHUNK
    tri = (lax.broadcasted_iota(jnp.int32, (n, n), 1)
           <= lax.broadcasted_iota(jnp.int32, (n, n), 0)).astype(BF16)
    lane = lax.broadcasted_iota(jnp.int32, (n, LANES), 1)
    carry = jnp.zeros((1, LANES), F32)
    for blk in range(hs.shape[0] // n):
        hi, mid, lo = _split3(logf[blk * n:(blk + 1) * n])
        cb = _dot(tri, hi) + _dot(tri, mid) + _dot(tri, lo) + carry
        carry = cb[n - 1:n, :]
        hi, mid, lo = _split3(cb * LOG2E)
        packed = jnp.where((lane >= ONES_LANE) & (lane < ONES_LANE + _N_SPLIT), 1.0, 0.0)
        for t, term in enumerate((hi, mid, lo)):
            shifted = term.astype(F32) if t == 0 else pltpu.roll(term.astype(F32), t * N_HEADS, 1)
            packed = jnp.where((lane >= t * N_HEADS) & (lane < (t + 1) * N_HEADS), shifted, packed)
        c_ref[blk * n:(blk + 1) * n, :] = packed.astype(c_ref.dtype)


def _gates(h_small, bf_row, batch, seq):
    return pl.pallas_call(
        _gate_kernel,
        grid=(batch,),
        in_specs=[pl.BlockSpec((seq, LANES), lambda b: (b, 0)),
                  pl.BlockSpec((1, LANES), lambda b: (0, 0))],
        out_specs=[pl.BlockSpec((seq, LANES), lambda b: (b, 0)),
                   pl.BlockSpec((seq, LANES), lambda b: (b, 0))],
        out_shape=[jax.ShapeDtypeStruct(h_small.shape, BF16), jax.ShapeDtypeStruct(h_small.shape, F32)],
        compiler_params=pltpu.CompilerParams(dimension_semantics=("parallel",)),
        name="gates",
    )(h_small, bf_row)


_HALF = CMP_LEN // 2


def _gelu_tanh(x):
    return 0.5 * x * (1.0 + jnp.tanh(math.sqrt(2.0 / math.pi) * (x + 0.044715 * (x * x * x))))


def _compress_kernel(raw_ref, pk_ref, pv_ref, w1k_ref, w2k_ref, w1v_ref, w2v_ref, kc_ref, vc_ref, raw32_ref):
    seq = raw_ref.shape[0]
    n_chunks = seq // _HALF
    step = 512
    for slab in range(raw_ref.shape[1] // HEAD_DIM):
        for blk in range(seq // step):
            raw32_ref[slab, blk * step:(blk + 1) * step, :] = (
                raw_ref[blk * step:(blk + 1) * step, slab * HEAD_DIM:(slab + 1) * HEAD_DIM].astype(F32))
    for kv, (pos_ref, w1_ref, w2_ref, out_ref) in enumerate(
            ((pk_ref, w1k_ref, w2k_ref, kc_ref), (pv_ref, w1v_ref, w2v_ref, vc_ref))):
        for g in range(KV_GROUPS):
            slab = kv * KV_GROUPS + g
            first = jnp.zeros((n_chunks, CMP_HIDDEN), F32)
            second = jnp.zeros((n_chunks, CMP_HIDDEN), F32)
            for l in range(_HALF):
                a = raw32_ref[slab, pl.ds(l, n_chunks, stride=_HALF), :]
                a1 = (a + pos_ref[l:l + 1, :]).astype(BF16)
                a2 = (a + pos_ref[_HALF + l:_HALF + l + 1, :]).astype(BF16)
                first += _dot(a1, w1_ref[l * HEAD_DIM:(l + 1) * HEAD_DIM, :].astype(BF16))
                second += _dot(a2, w1_ref[(_HALF + l) * HEAD_DIM:(_HALF + l + 1) * HEAD_DIM, :].astype(BF16))
            hid = first + pltpu.roll(second, n_chunks - 1, 0)
            out = _dot(_gelu_tanh(hid).astype(BF16), w2_ref[...].astype(BF16))
            out_ref[0, :, g * HEAD_DIM:(g + 1) * HEAD_DIM] = out.astype(out_ref.dtype)


def _compress(h_main, pos_k, pos_v, w1k, w2k, w1v, w2v, batch, seq):
    n_chunks = seq // _HALF
    raw_cols = 2 * KV_WIDTH
    assert _MAIN_OFF["nsa_v_cmp"] == _MAIN_OFF["nsa_k_cmp"] + KV_WIDTH
    raw_blk = _MAIN_OFF["nsa_k_cmp"] // raw_cols
    full = lambda shape: pl.BlockSpec(shape, lambda b: (0,) * len(shape), pipeline_mode=pl.Buffered(1))
    return pl.pallas_call(
        _compress_kernel,
        grid=(batch,),
        in_specs=[pl.BlockSpec((seq, raw_cols), lambda b: (b, raw_blk)),
                  full(pos_k.shape), full(pos_v.shape),
                  full(w1k.shape), full(w2k.shape), full(w1v.shape), full(w2v.shape)],
        out_specs=[pl.BlockSpec((1, n_chunks, KV_WIDTH), lambda b: (b, 0, 0)),
                   pl.BlockSpec((1, n_chunks, KV_WIDTH), lambda b: (b, 0, 0))],
        out_shape=[jax.ShapeDtypeStruct((batch, n_chunks, KV_WIDTH), BF16),
                   jax.ShapeDtypeStruct((batch, n_chunks, KV_WIDTH), BF16)],
        scratch_shapes=[pltpu.VMEM((raw_cols // HEAD_DIM, seq, HEAD_DIM), F32)],
        compiler_params=pltpu.CompilerParams(
            dimension_semantics=("parallel",), vmem_limit_bytes=VMEM_LIMIT),
        name="compress",
    )(h_main, pos_k, pos_v, w1k, w2k, w1v, w2v)


def _lane_tile(x, n):
    return x if n == 1 else jnp.concatenate([x] * n, axis=1)


def _lane_fold(p):
    out = p[:, :LANES]
    for t in range(1, p.shape[1] // LANES):
        out = out + p[:, t * LANES:(t + 1) * LANES]
    return out


def _first_weights(s, m_ref, l_ref, rows=slice(None)):
    m = jnp.broadcast_to(jnp.max(s, axis=1, keepdims=True), (s.shape[0], LANES))
    p = jnp.exp2(s - _lane_tile(m, s.shape[1] // LANES))
    m_ref[rows, :] = m
    l_ref[rows, :] = _lane_fold(p)
    return p.astype(BF16)


def _next_weights(s, row_const, m_ref, l_ref, rows=slice(None)):
    m_prev = m_ref[rows, :]
    m_cur = jnp.max(s, axis=1, keepdims=True)
    if row_const is not None:
        m_cur = m_cur + row_const
    m_new = jnp.maximum(m_prev, m_cur)
    alpha = jnp.exp2(m_prev - m_new)
    shift = m_new if row_const is None else m_new - row_const
    p = jnp.exp2(s - _lane_tile(shift, s.shape[1] // LANES))
    l_ref[rows, :] = alpha * l_ref[rows, :] + _lane_fold(p)
    m_ref[rows, :] = m_new
    return alpha, p.astype(BF16)


def _softmax_first(s, v, m_ref, l_ref, acc_ref, rows=slice(None)):
    acc_ref[rows, :] = _dot(_first_weights(s, m_ref, l_ref, rows), v)


def _softmax_update(s, v, row_const, m_ref, l_ref, acc_ref, rows=slice(None)):
    alpha, p = _next_weights(s, row_const, m_ref, l_ref, rows)
    acc_ref[rows, :] = alpha * acc_ref[rows, :] + _dot(p, v)


def _skewed(stages, jobs, before_tick=None):
    state = [None] * len(jobs)
    for tick in range(len(jobs) + len(stages) - 1):
        if before_tick and tick in before_tick:
            before_tick[tick]()
        for k, stage in enumerate(stages):
            j = tick - k
            if 0 <= j < len(jobs):
                state[j] = stage(jobs[j], state[j])


def _softmax_finish(l_ref, acc_ref, rows=slice(None)):
    return acc_ref[rows, :] / jnp.sum(l_ref[rows, :], axis=1, keepdims=True)


def _softmax_once(s, v):
    m = jnp.max(s, axis=1, keepdims=True)
    p = jnp.exp2(s - m)
    return _dot(p.astype(BF16), v) / jnp.sum(_lane_fold(p), axis=1, keepdims=True)


FOX_TQ = 512
FOX_HALF = FOX_TQ // 2


def _fox_routing():
    pk = np.zeros((N_HEADS, LANES, LANES), np.float32)
    for h in range(N_HEADS):
        for t in range(_N_SPLIT):
            pk[h, ONES_LANE, t * N_HEADS + h] = 1.0
            pk[h, t * N_HEADS + h, ONES_LANE + t] = -1.0
    return pk


def _fox_kernel(q_ref, k_ref, v_ref, z_ref, cq_ref, ck_ref, pk_ref, o_ref,
                kaug_ref, qaug_ref, sz_ref, m_ref, l_ref, acc_ref):
    i = pl.program_id(1)
    tq, half = FOX_TQ, FOX_HALF
    seq = k_ref.shape[0]

    @pl.when(i == 0)
    def _():
        for h in range(N_HEADS):
            for blk in range(seq // tq):
                rows = slice(blk * tq, (blk + 1) * tq)
                kaug_ref[h, rows, :HEAD_DIM] = k_ref[rows, h * HEAD_DIM:(h + 1) * HEAD_DIM]
                kaug_ref[h, rows, HEAD_DIM:] = _dot(ck_ref[rows, :], pk_ref[h]).astype(BF16)

    causal = (lax.broadcasted_iota(jnp.int32, (half, half), 1)
              <= lax.broadcasted_iota(jnp.int32, (half, half), 0))
    diag = pl.multiple_of(i * tq, tq)
    diag2 = pl.multiple_of(i * tq + half, half)
    heads = tuple(range(N_HEADS))
    rows = lambda h: slice(h * tq, (h + 1) * tq)
    top = lambda h: slice(h * tq, h * tq + half)
    bot = lambda h: slice(h * tq + half, (h + 1) * tq)
    cols = lambda h: slice(h * HEAD_DIM, (h + 1) * HEAD_DIM)

    def diag_scores(h, _):
        qaug_ref[rows(h), :HEAD_DIM] = q_ref[:, cols(h)]
        qaug_ref[rows(h), HEAD_DIM:] = cq_ref[...]
        s_left = _dot_nt(qaug_ref[rows(h), :], kaug_ref[h, pl.ds(diag, half), :])
        s_right = _dot_nt(qaug_ref[bot(h), :], kaug_ref[h, pl.ds(diag2, half), :])
        return s_left, s_right

    def diag_weights(h, scores):
        s_left, s_right = scores
        p_top = _first_weights(jnp.where(causal, s_left[:half], NEG), m_ref, l_ref, top(h))
        p_bot = _first_weights(jnp.concatenate([s_left[half:], jnp.where(causal, s_right, NEG)], axis=1),
                               m_ref, l_ref, bot(h))
        return p_top, p_bot

    def diag_values(h, weights):
        p_top, p_bot = weights
        acc_ref[top(h), :] = _dot(p_top, v_ref[pl.ds(diag, half), cols(h)])
        acc_ref[bot(h), :] = _dot(p_bot, v_ref[pl.ds(diag, tq), cols(h)])
        sz_ref[:, cols(h)] = _silu(z_ref[:, cols(h)].astype(F32))

    _skewed((diag_scores, diag_weights, diag_values), heads)

    def body(j, carry):
        off = pl.multiple_of(j * tq, tq)

        def scores(h, _):
            return _dot_nt(qaug_ref[rows(h), :], kaug_ref[h, pl.ds(off, tq), :])

        def weights(h, s):
            return _next_weights(s, None, m_ref, l_ref, rows(h))

        def values(h, rescale_and_weights):
            alpha, p = rescale_and_weights
            acc_ref[rows(h), :] = alpha * acc_ref[rows(h), :] + _dot(p, v_ref[pl.ds(off, tq), cols(h)])

        _skewed((scores, weights, values), heads)
        return carry

    lax.fori_loop(0, i, body, 0)
    for h in heads:
        o_ref[:, cols(h)] = (_softmax_finish(l_ref, acc_ref, rows(h)) * sz_ref[:, cols(h)]).astype(o_ref.dtype)


def _fox(h_main, c_packed, batch, seq):
    tq = FOX_TQ
    nq = seq // tq
    blk = lambda name: _MAIN_OFF[name] // WIDTH
    pk = jnp.asarray(_fox_routing(), BF16)
    full = lambda shape: pl.BlockSpec(shape, lambda b, i: (0,) * len(shape))
    stat = pltpu.VMEM((N_HEADS * tq, LANES), F32)
    return pl.pallas_call(
        _fox_kernel,
        grid=(batch, nq),
        in_specs=[
            pl.BlockSpec((tq, WIDTH), lambda b, i: (b * nq + i, blk("fox_q"))),
            pl.BlockSpec((seq, WIDTH), lambda b, i: (b, blk("fox_k"))),
            pl.BlockSpec((seq, WIDTH), lambda b, i: (b, blk("fox_v"))),
            pl.BlockSpec((tq, WIDTH), lambda b, i: (b * nq + i, blk("fox_z"))),
            pl.BlockSpec((tq, LANES), lambda b, i: (b * nq + i, 0)),
            pl.BlockSpec((seq, LANES), lambda b, i: (b, 0)),
            full(pk.shape),
        ],
        out_specs=pl.BlockSpec((tq, WIDTH), lambda b, i: (b * nq + i, 0)),
        out_shape=jax.ShapeDtypeStruct((batch * seq, WIDTH), BF16),
        scratch_shapes=[pltpu.VMEM((N_HEADS, seq, MXU_DEPTH), BF16),
                        pltpu.VMEM((N_HEADS * tq, MXU_DEPTH), BF16),
                        pltpu.VMEM((tq, WIDTH), F32),
                        stat, stat, stat],
        compiler_params=pltpu.CompilerParams(
            dimension_semantics=("parallel", "arbitrary"), vmem_limit_bytes=VMEM_LIMIT),
        name="fox",
    )(h_main, h_main, h_main, h_main, c_packed, c_packed, pk)


TQ = 256
TK = TQ
WIN_BLOCKS = (WINDOW + TQ) // TK
SEL_NEAR_BLOCKS = 2
FAR_TK = 2 * TK
N_SEL_ROWS = 32
MASK_BUCKET = REL_BUCKETS
GROUP_ROWS = HEADS_PER_GROUP * TQ
T_DIAG, T_PREV, T_WIN2, T_NONE = range(4)
N_GATHERED_TILES = 3


def _bias_lookup(tab_ref, h, idx):
    row = jnp.broadcast_to(tab_ref[h:h + 1, :], idx.shape)
    return jnp.take_along_axis(row, idx, axis=1, mode="promise_in_bounds")


def _nsa_kernel(q_ref, ks_ref, vs_ref, kw_ref, vw_ref, z_ref, kc_ref, vc_ref, cidx_ref, tidx_ref,
                tab_ref, g_ref, et_ref, ov_ref, o_ref,
                ksaug_ref, qs_ref, qaug_ref, m_ref, l_ref, acc_ref, oc_ref, ow_ref, gs_ref, t_ref, cb_ref):
    i = pl.program_id(1)
    t0 = i * TQ

    @pl.when((pl.program_id(0) == 0) & (i == 0))
    def _():
        for h in range(N_HEADS):
            for d in range(N_GATHERED_TILES):
                for half in range(TK // LANES):
                    cs = slice(half * LANES, (half + 1) * LANES)
                    t_ref[h, d, :, cs] = _bias_lookup(tab_ref, h, tidx_ref[d, :, cs])
            for blk in range(cidx_ref.shape[0] // TQ):
                rows = slice(blk * TQ, (blk + 1) * TQ)
                cb_ref[h, rows, :] = _bias_lookup(tab_ref, h, cidx_ref[rows, :])
            t_ref[h, T_NONE] = jnp.full((TQ, TK), NEG, F32)

    @pl.when(i == 0)
    def _():
        for g in range(KV_GROUPS):
            ksaug_ref[g, :, :HEAD_DIM] = ks_ref[:, g * HEAD_DIM:(g + 1) * HEAD_DIM]
            ksaug_ref[g, :, HEAD_DIM:] = et_ref[...]

    cmaskf = (lax.broadcasted_iota(jnp.int32, (TQ, LANES), 1) * CMP_STRIDE + (CMP_LEN - 1)
              <= t0 + lax.broadcasted_iota(jnp.int32, (TQ, LANES), 0)).astype(F32)
    j_t = lax.broadcasted_iota(jnp.int32, (N_SEL_ROWS, TQ), 0)
    t_t = t0 + lax.broadcasted_iota(jnp.int32, (N_SEL_ROWS, TQ), 1)
    cur_t = t_t // SEL_LEN
    forced_t = (j_t == 0) | (j_t == cur_t) | (j_t == cur_t - 1)
    valid_t = j_t * SEL_LEN <= t_t

    def near_span(n_blocks):
        first = jnp.maximum(i - (n_blocks - 1), 0)
        return first, pl.ds(pl.multiple_of(first * TK, TK), n_blocks * TK)

    win_first, win_keys = near_span(WIN_BLOCKS)
    sel_first, sel_keys = near_span(SEL_NEAR_BLOCKS)

    def near_bias(heads, first, n_blocks):
        def tile(kk):
            dist = i - (first + kk)
            return jnp.where(dist == 0, T_DIAG, jnp.where(dist == 1, T_PREV,
                             jnp.where(dist == 2, T_WIN2, T_NONE)))
        return jnp.concatenate(
            [jnp.concatenate([t_ref[h, tile(kk)] for kk in range(n_blocks)], axis=1) for h in heads],
            axis=0)

    groups = tuple((g, g * HEAD_DIM, tuple(range(g * HEADS_PER_GROUP, (g + 1) * HEADS_PER_GROUP)),
                    slice(g * GROUP_ROWS, (g + 1) * GROUP_ROWS)) for g in range(KV_GROUPS))
    hrows = lambda h: slice(h * TQ, (h + 1) * TQ)
    for h in range(N_HEADS):
        q = q_ref[:, h * HEAD_DIM:(h + 1) * HEAD_DIM]
        qs_ref[hrows(h), :] = q
        qaug_ref[hrows(h), :HEAD_DIM] = q

    chain = {}

    def compressed_branch():
        cbias = jnp.concatenate([cb_ref[h, pl.ds(pl.multiple_of(t0, TQ), TQ), :] for h in range(N_HEADS)],
                                axis=0)
        sc = jnp.concatenate([_dot_nt(qs_ref[grows, :], kc_ref[0, :, glo:glo + HEAD_DIM])
                              for g, glo, heads, grows in groups], axis=0) + cbias
        e = jnp.exp2(sc - jnp.max(sc, axis=1, keepdims=True))
        p = e / jnp.sum(e, axis=1, keepdims=True) * jnp.concatenate([cmaskf] * N_HEADS, axis=0)
        imp_parts = []
        for g, glo, heads, grows in groups:
            oc_ref[grows, :] = _dot(p[grows].astype(BF16), vc_ref[0, :, glo:glo + HEAD_DIM])
            psum = p[hrows(heads[0])]
            for h in heads[1:]:
                psum = psum + p[hrows(h)]
            p_hi = psum.astype(BF16)
            p_lo = (psum - p_hi.astype(F32)).astype(BF16)
            imp_parts.append((_dot_nt(ov_ref[...], p_hi) + _dot_nt(ov_ref[...], p_lo))[:N_SEL_ROWS])
        chain["importance"] = jnp.concatenate(imp_parts, axis=1)

    def block_selection():
        both = lambda a: jnp.concatenate([a] * KV_GROUPS, axis=1)
        j_b = both(j_t)
        x = jnp.where(both(valid_t), jnp.where(both(forced_t), -NEG, chain["importance"]), NEG)
        cnt = jnp.zeros(x.shape, F32)
        for jp in range(N_SEL_ROWS):
            row = x[jp:jp + 1, :]
            beats = (row > x) | ((row == x) & (j_b > jp))
            cnt = cnt + jnp.where(beats, 1.0, 0.0)
        unsel_t = jnp.where(cnt < SEL_TOPK, 0.0, 1.0)
        unsel_t = jnp.concatenate([unsel_t, jnp.zeros((LANES - N_SEL_ROWS, x.shape[1]), F32)], axis=0)
        unsel = unsel_t.T.astype(BF16)
        for h in range(N_HEADS):
            g = h // HEADS_PER_GROUP
            qaug_ref[hrows(h), HEAD_DIM:] = unsel[g * TQ:(g + 1) * TQ]

    def near_scores(job, _):
        branch, (g, glo, heads, grows) = job
        if branch == "window":
            return (_dot_nt(qs_ref[grows, :], kw_ref[win_keys, glo:glo + HEAD_DIM])
                    + near_bias(heads, win_first, WIN_BLOCKS))
        return (_dot_nt(qaug_ref[grows, :], ksaug_ref[g, sel_keys, :])
                + near_bias(heads, sel_first, SEL_NEAR_BLOCKS))

    def near_weights(job, s):
        branch, (g, glo, heads, grows) = job
        if branch == "window":
            p = jnp.exp2(s - jnp.max(s, axis=1, keepdims=True))
            return p.astype(BF16), _lane_fold(p)
        return _first_weights(s, m_ref, l_ref, grows)

    def near_values(job, weights):
        branch, (g, glo, heads, grows) = job
        if branch == "selected":
            acc_ref[grows, :] = _dot(weights, vs_ref[sel_keys, glo:glo + HEAD_DIM])
            return
        p, l = weights
        o_win = _dot(p, vw_ref[win_keys, glo:glo + HEAD_DIM]) / jnp.sum(l, axis=1, keepdims=True)
        for n, h in enumerate(heads):
            gl = GATE_LANE0 + h * N_BRANCHES
            sz = _silu(z_ref[:, h * HEAD_DIM:(h + 1) * HEAD_DIM].astype(F32))
            gs_ref[hrows(h), :] = g_ref[:, gl + 1:gl + 2] * sz
            ow_ref[hrows(h), :] = (g_ref[:, gl:gl + 1] * oc_ref[hrows(h), :]
                                   + g_ref[:, gl + 2:gl + 3] * o_win[n * TQ:(n + 1) * TQ]) * sz

    _skewed((near_scores, near_weights, near_values),
            tuple((branch, group) for branch in ("window", "selected") for group in groups),
            before_tick={1: compressed_branch, 2: block_selection})

    def sel_far(off, width):
        def scores(group, _):
            g, glo, heads, grows = group
            return _dot_nt(qaug_ref[grows, :], ksaug_ref[g, pl.ds(off, width), :])

        def weights(group, s):
            g, glo, heads, grows = group
            return _next_weights(s, None, m_ref, l_ref, grows)

        def values(group, rescale_and_weights):
            g, glo, heads, grows = group
            alpha, p = rescale_and_weights
            acc_ref[grows, :] = alpha * acc_ref[grows, :] + _dot(p, vs_ref[pl.ds(off, width), glo:glo + HEAD_DIM])

        _skewed((scores, weights, values), groups)

    def sel_far_pair(j, carry):
        sel_far(pl.multiple_of(j * FAR_TK, FAR_TK), FAR_TK)
        return carry

    lax.fori_loop(0, sel_first // 2, sel_far_pair, 0)

    @pl.when(sel_first % 2 == 1)
    def _():
        sel_far(pl.multiple_of((sel_first - 1) * TK, TK), TK)

    for h in range(N_HEADS):
        o_ref[:, h * HEAD_DIM:(h + 1) * HEAD_DIM] = (
            ow_ref[hrows(h), :] + gs_ref[hrows(h), :] * _softmax_finish(l_ref, acc_ref, hrows(h))
        ).astype(o_ref.dtype)


def _nsa(h_main, k_cmp, v_cmp, cmp_idx, tile_idx, table, gates, e_t, ov_t, batch, seq):
    nq = seq // TQ
    wblk = lambda name: _MAIN_OFF[name] // WIDTH
    kvblk = lambda name: _MAIN_OFF[name] // KV_WIDTH
    full = lambda shape: pl.BlockSpec(shape, lambda b, i: (0,) * len(shape),
                                      pipeline_mode=pl.Buffered(1))
    kv_spec = lambda name: pl.BlockSpec((seq, KV_WIDTH), lambda b, i: (b, kvblk(name)))
    n_chunks = k_cmp.shape[1]
    stat = pltpu.VMEM((N_HEADS * TQ, LANES), F32)
    return pl.pallas_call(
        _nsa_kernel,
        grid=(batch, nq),
        in_specs=[
            pl.BlockSpec((TQ, WIDTH), lambda b, i: (b * nq + i, wblk("nsa_q"))),
            kv_spec("nsa_k_sel"), kv_spec("nsa_v_sel"), kv_spec("nsa_k_win"), kv_spec("nsa_v_win"),
            pl.BlockSpec((TQ, WIDTH), lambda b, i: (b * nq + i, wblk("nsa_z"))),
            pl.BlockSpec((1, n_chunks, KV_WIDTH), lambda b, i: (b, 0, 0)),
            pl.BlockSpec((1, n_chunks, KV_WIDTH), lambda b, i: (b, 0, 0)),
            full(cmp_idx.shape), full(tile_idx.shape), full(table.shape),
            pl.BlockSpec((TQ, LANES), lambda b, i: (b * nq + i, 0)),
            full(e_t.shape), full(ov_t.shape),
        ],
        out_specs=pl.BlockSpec((TQ, WIDTH), lambda b, i: (b * nq + i, 0)),
        out_shape=jax.ShapeDtypeStruct((batch * seq, WIDTH), BF16),
        scratch_shapes=[pltpu.VMEM((KV_GROUPS, seq, MXU_DEPTH), BF16),
                        pltpu.VMEM((N_HEADS * TQ, HEAD_DIM), BF16),
                        pltpu.VMEM((N_HEADS * TQ, MXU_DEPTH), BF16),
                        stat, stat, stat,
                        stat, stat, stat,
                        pltpu.VMEM((N_HEADS, T_NONE + 1, TQ, TK), F32),
                        pltpu.VMEM((N_HEADS, seq, LANES), F32)],
        compiler_params=pltpu.CompilerParams(
            dimension_semantics=("arbitrary", "arbitrary"), vmem_limit_bytes=VMEM_LIMIT),
        name="nsa",
    )(h_main, h_main, h_main, h_main, h_main, h_main, k_cmp, v_cmp, cmp_idx, tile_idx, table, gates,
      e_t, ov_t)


OUT_TM = 512
OUT_TN = 512


def _out_kernel(alpha, ua_ref, ub_ref, ga_ref, gb_ref, x_ref, wa_ref, wb_ref, wo_ref, lg_ref, lb_ref,
                o_ref, merged_ref):
    d_model = o_ref.shape[1]
    chunks = tuple(slice(c * OUT_TN, (c + 1) * OUT_TN) for c in range(d_model // OUT_TN))
    for cs in chunks:
        ya = _dot(ua_ref[...], wa_ref[:, cs])
        yb = _dot(ub_ref[...], wb_ref[:, cs])
        merged_ref[:, cs] = (_sigmoid(ga_ref[:, cs].astype(F32)) * ya
                             + _sigmoid(gb_ref[:, cs].astype(F32)) * yb).astype(merged_ref.dtype)
    total = jnp.zeros((o_ref.shape[0], LANES), F32)
    for cs in chunks:
        y = alpha * x_ref[:, cs] + _dot(merged_ref[...], wo_ref[:, cs])
        o_ref[:, cs] = y
        total = total + _lane_fold(y)
    mu = jnp.sum(total, axis=1, keepdims=True) * (1.0 / d_model)
    sq = jnp.zeros((o_ref.shape[0], LANES), F32)
    for cs in chunks:
        d = o_ref[:, cs] - mu
        sq = sq + _lane_fold(d * d)
    rstd = lax.rsqrt(jnp.sum(sq, axis=1, keepdims=True) * (1.0 / d_model) + LN_EPS)
    for cs in chunks:
        o_ref[:, cs] = (o_ref[:, cs] - mu) * rstd * lg_ref[:, cs] + lb_ref[:, cs]


def _out(u_a, u_b, h_main, x2, w_a, w_b, w_o, ln_g, ln_b, alpha, tm=OUT_TM):
    m = x2.shape[0]
    mblk = lambda name: _MAIN_OFF[name] // D_MODEL
    const = lambda shape: pl.BlockSpec(shape, lambda i: (0, 0), pipeline_mode=pl.Buffered(1))
    return pl.pallas_call(
        functools.partial(_out_kernel, alpha),
        grid=(m // tm,),
        in_specs=[
            pl.BlockSpec((tm, WIDTH), lambda i: (i, 0)),
            pl.BlockSpec((tm, WIDTH), lambda i: (i, 0)),
            pl.BlockSpec((tm, D_MODEL), lambda i: (i, mblk("merge_a"))),
            pl.BlockSpec((tm, D_MODEL), lambda i: (i, mblk("merge_b"))),
            pl.BlockSpec((tm, D_MODEL), lambda i: (i, 0)),
            const(w_a.shape), const(w_b.shape), const(w_o.shape),
            const(ln_g.shape), const(ln_b.shape),
        ],
        out_specs=pl.BlockSpec((tm, D_MODEL), lambda i: (i, 0)),
        out_shape=jax.ShapeDtypeStruct((m, D_MODEL), F32),
        scratch_shapes=[pltpu.VMEM((tm, D_MODEL), BF16)],
        compiler_params=pltpu.CompilerParams(
            dimension_semantics=("parallel",), vmem_limit_bytes=VMEM_LIMIT),
        name="out",
    )(u_a, u_b, h_main, h_main, x2, w_a, w_b, w_o, ln_g, ln_b)


def _bucket_np(dist):
    n = np.maximum(dist, 0)
    exact = REL_BUCKETS // 2
    large = exact + (np.log(np.maximum(n, 1).astype(np.float32) / exact)
                     / math.log(REL_MAX_DIST / exact) * (REL_BUCKETS - exact)).astype(np.int32)
    return np.where(n < exact, n, np.minimum(large, REL_BUCKETS - 1)).astype(np.int32)


@functools.lru_cache(maxsize=None)
def _static_tables(seq):
    r = np.arange(TQ)[:, None]
    c = np.arange(TK)[None, :]
    tile_idx = np.stack([_bucket_np(r - c), _bucket_np(TQ + r - c), _bucket_np(2 * TQ + r - c)])
    tile_ok = np.stack([c <= r, np.ones((TQ, TK), bool), (2 * TQ + r - c) < WINDOW])
    tile_idx = np.where(tile_ok, tile_idx, MASK_BUCKET).astype(np.int32)
    t = np.arange(seq)[:, None]
    cblk = np.arange(LANES)[None, :]
    blk_end = cblk * CMP_STRIDE + CMP_LEN - 1
    n_cmp = (seq - CMP_LEN) // CMP_STRIDE + 1
    cmp_idx = np.where((blk_end <= t) & (cblk < n_cmp), _bucket_np(t - blk_end), MASK_BUCKET).astype(np.int32)
    e_t = np.where((np.arange(seq)[:, None] // SEL_LEN) == np.arange(LANES)[None, :], NEG, 0.0)
    cs = (np.arange(LANES) * CMP_STRIDE)[None, :]
    ss = (np.arange(LANES) * SEL_LEN)[:, None]
    ov_t = ((cs < ss + SEL_LEN) & (cs + CMP_LEN > ss)
            & (np.arange(LANES)[None, :] < n_cmp) & (np.arange(LANES)[:, None] < seq // SEL_LEN))
    return tile_idx, cmp_idx, e_t.astype(np.float32), ov_t.astype(np.float32)


def _layer(x, w_in, b_f, cmp_pos_k, cmp_pos_v, cmp_wk1, cmp_wk2, cmp_wv1, cmp_wv2,
           w_a, w_b, w_o, ln_g, ln_b, rel_bias, alpha):
    batch, seq, d_model = x.shape
    assert d_model == D_MODEL and seq % FOX_TQ == 0 and seq >= WIN_BLOCKS * TK and REL_MAX_DIST <= TQ
    assert seq // SEL_LEN == N_SEL_ROWS and seq // CMP_STRIDE == LANES
    x2 = x.reshape(batch * seq, d_model)

    w_t = jnp.swapaxes(w_in, 0, 1)
    trows = lambda name: w_t[_REF_OFF[name][0]:_REF_OFF[name][0] + _REF_OFF[name][1]]
    n_small = N_HEADS + N_HEADS * N_BRANCHES
    w_small_t = jnp.concatenate(
        [trows("fox_f"), trows("nsa_gate"), jnp.zeros((LANES - n_small, d_model), F32)], axis=0)
    bf_row = jnp.concatenate([b_f.astype(F32), jnp.zeros((LANES - N_HEADS,), F32)]).reshape(1, LANES)

    h_main, h_small = _proj(x2, w_t, w_small_t)
    c_col, gates = _gates(h_small, bf_row, batch, seq)
    u_a = _fox(h_main, c_col, batch, seq)

    k_cmp, v_cmp = _compress(h_main, cmp_pos_k, cmp_pos_v, cmp_wk1, cmp_wk2, cmp_wv1, cmp_wv2, batch, seq)

    tile_idx, cmp_idx, e_t, ov_t = _static_tables(seq)
    bias = rel_bias.T.astype(F32)
    table = jnp.concatenate(
        [(bias - bias[:, REL_BUCKETS - 1:]) * LOG2E, jnp.full((N_HEADS, 1), NEG, F32),
         jnp.zeros((N_HEADS, LANES - REL_BUCKETS - 1), F32)], axis=1)
    u_b = _nsa(h_main, k_cmp, v_cmp, jnp.asarray(cmp_idx), jnp.asarray(tile_idx), table, gates,
               jnp.asarray(e_t, BF16), jnp.asarray(ov_t, BF16), batch, seq)

    out = _out(u_a, u_b, h_main, x2, w_a.astype(BF16), w_b.astype(BF16), w_o.astype(BF16),
               ln_g.reshape(1, d_model), ln_b.reshape(1, d_model), alpha)
    return out.reshape(batch, seq, d_model)


def kernel(x, w_in, b_f, cmp_pos_k, cmp_pos_v, cmp_wk1, cmp_wk2, cmp_wv1, cmp_wv2,
           w_a, w_b, w_o, ln_g, ln_b, rel_bias):
    depth = w_in.shape[0]
    alpha = (2 * depth) ** 0.25
    for layer in range(depth):
        x = _layer(x, w_in[layer], b_f[layer], cmp_pos_k[layer], cmp_pos_v[layer], cmp_wk1[layer],
                   cmp_wk2[layer], cmp_wv1[layer], cmp_wv2[layer], w_a[layer], w_b[layer], w_o[layer],
                   ln_g[layer], ln_b[layer], rel_bias, alpha)
    return x
```

```python
import functools
import math

import jax
import jax.numpy as jnp
import numpy as np
from jax import lax
from jax.experimental import pallas as pl
from jax.experimental.pallas import tpu as pltpu

F32 = jnp.float32
BF16 = jnp.bfloat16

D_MODEL = 2048
HEAD_DIM = 128
N_HEADS = 8
WIDTH = N_HEADS * HEAD_DIM
KV_GROUPS = 2
HEADS_PER_GROUP = N_HEADS // KV_GROUPS
KV_WIDTH = KV_GROUPS * HEAD_DIM
N_BRANCHES = 3
CMP_LEN = 32
CMP_STRIDE = 16
CMP_HIDDEN = 256
SEL_LEN = 64
SEL_TOPK = 8
WINDOW = 512
REL_BUCKETS = 32
REL_MAX_DIST = 128
LN_EPS = 1e-5
NEG = -1e30
LOG2E = math.log2(math.e)
Q_PRESCALE = HEAD_DIM ** -0.5 * LOG2E

LANES = 128
SUBLANES = 8
MXU_DEPTH = 256
VMEM_LIMIT = 60 * 1024 * 1024

_REF_LAYOUT = (
    ("fox_q", WIDTH), ("fox_k", WIDTH), ("fox_v", WIDTH), ("fox_f", N_HEADS), ("fox_z", WIDTH),
    ("nsa_q", WIDTH), ("nsa_k_cmp", KV_WIDTH), ("nsa_v_cmp", KV_WIDTH), ("nsa_k_sel", KV_WIDTH),
    ("nsa_v_sel", KV_WIDTH), ("nsa_k_win", KV_WIDTH), ("nsa_v_win", KV_WIDTH),
    ("nsa_gate", N_HEADS * N_BRANCHES), ("nsa_z", WIDTH), ("merge_a", D_MODEL), ("merge_b", D_MODEL),
)
_REF_OFF = {}
_o = 0
for _n, _w in _REF_LAYOUT:
    _REF_OFF[_n] = (_o, _w)
    _o += _w

_MAIN_ORDER = ("fox_q", "fox_k", "fox_v", "fox_z", "nsa_q", "nsa_z", "merge_a", "merge_b",
               "nsa_k_cmp", "nsa_v_cmp", "nsa_k_sel", "nsa_v_sel", "nsa_k_win", "nsa_v_win")
_QUERY_COLS = ("fox_q", "nsa_q")
_MAIN_OFF = {}
_o = 0
for _n in _MAIN_ORDER:
    _MAIN_OFF[_n] = _o
    _o += _REF_OFF[_n][1]
MAIN_COLS = _o
GATE_LANE0 = N_HEADS


def _dot(a, b):
    return jnp.dot(a, b, preferred_element_type=F32)


def _dot_nt(a, b):
    return lax.dot_general(a, b, (((1,), (1,)), ((), ())), preferred_element_type=F32)


def _sigmoid(x):
    return 0.5 + 0.5 * jnp.tanh(0.5 * x)


def _silu(x):
    return x * _sigmoid(x)


PROJ_TN = 512


def _proj_tiles():
    rows, is_query = [], []
    for name in _MAIN_ORDER:
        off, width = _REF_OFF[name]
        start = _MAIN_OFF[name]
        for c in range(start, start + width):
            if c % PROJ_TN == 0:
                assert (off + c - start) % SUBLANES == 0
                rows.append((off + c - start) // SUBLANES)
                is_query.append(int(name in _QUERY_COLS))
    return np.asarray(rows, np.int32), np.asarray(is_query, np.int32)


PROJ_ROWS = 4096
PROJ_RB = 512


def _proj_kernel(rows_ref, isq_ref, x_hbm, wt_ref, wst_ref, o_ref, os_ref,
                 xb_ref, stage_ref, wb_ref, sem):
    i = pl.program_id(0)
    j = pl.program_id(1)
    n_blocks = PROJ_ROWS // PROJ_RB
    wb_ref[...] = wt_ref[...].astype(BF16)
    scale = jnp.where(isq_ref[j] == 1, Q_PRESCALE, 1.0)

    def x_copy(r, slot):
        row0 = pl.multiple_of(i * PROJ_ROWS + r * PROJ_RB, PROJ_RB)
        return pltpu.make_async_copy(x_hbm.at[pl.ds(row0, PROJ_RB), :], stage_ref.at[slot], sem.at[slot])

    def block(r):
        rows = slice(r * PROJ_RB, (r + 1) * PROJ_RB)
        o_ref[rows, :] = (_dot_nt(xb_ref[rows, :], wb_ref[...]) * scale).astype(o_ref.dtype)

    @pl.when(j == 0)
    def _():
        wsb = wst_ref[...].astype(BF16)
        x_copy(0, 0).start()
        for r in range(n_blocks):
            if r + 1 < n_blocks:
                x_copy(r + 1, (r + 1) % 2).start()
            x_copy(r, r % 2).wait()
            rows = slice(r * PROJ_RB, (r + 1) * PROJ_RB)
            xb_ref[rows, :] = stage_ref[r % 2].astype(BF16)
            os_ref[rows, :] = _dot_nt(xb_ref[rows, :], wsb)
            block(r)

    @pl.when(j > 0)
    def _():
        for r in range(n_blocks):
            block(r)


def _proj(x2, w_t, w_small_t):
    m, k = x2.shape
    rows, is_query = _proj_tiles()
    assert MAIN_COLS % PROJ_TN == 0 and len(rows) == MAIN_COLS // PROJ_TN and m % PROJ_ROWS == 0
    grid_spec = pltpu.PrefetchScalarGridSpec(
        num_scalar_prefetch=2,
        grid=(m // PROJ_ROWS, len(rows)),
        in_specs=[
            pl.BlockSpec(memory_space=pl.ANY),
            pl.BlockSpec((pl.Element(PROJ_TN), pl.Element(k)),
                         lambda i, j, rows, isq: (rows[j] * SUBLANES, 0)),
            pl.BlockSpec((LANES, k), lambda i, j, rows, isq: (0, 0)),
        ],
        out_specs=[
            pl.BlockSpec((PROJ_ROWS, PROJ_TN), lambda i, j, rows, isq: (i, j)),
            pl.BlockSpec((PROJ_ROWS, LANES), lambda i, j, rows, isq: (i, 0)),
        ],
        scratch_shapes=[pltpu.VMEM((PROJ_ROWS, k), BF16),
                        pltpu.VMEM((2, PROJ_RB, k), F32),
                        pltpu.VMEM((PROJ_TN, k), BF16),
                        pltpu.SemaphoreType.DMA((2,))],
    )
    return pl.pallas_call(
        _proj_kernel,
        grid_spec=grid_spec,
        out_shape=[jax.ShapeDtypeStruct((m, MAIN_COLS), BF16), jax.ShapeDtypeStruct((m, LANES), F32)],
        compiler_params=pltpu.CompilerParams(
            dimension_semantics=("parallel", "arbitrary"), vmem_limit_bytes=VMEM_LIMIT),
        name="proj",
    )(jnp.asarray(rows), jnp.asarray(is_query), x2, w_t, w_small_t)


_CUM_CHUNK = 256
_N_SPLIT = 3
ONES_LANE = _N_SPLIT * N_HEADS


def _split3(x):
    hi = x.astype(BF16)
    r1 = x - hi.astype(F32)
    mid = r1.astype(BF16)
    lo = (r1 - mid.astype(F32)).astype(BF16)
    return hi, mid, lo


def _gate_kernel(hs_ref, bf_ref, c_ref, g_ref):
    hs = hs_ref[...]
    g_ref[...] = _sigmoid(hs)
    z = hs + bf_ref[...]
    logf = jnp.minimum(z, 0.0) - jnp.log1p(jnp.exp(-jnp.abs(z)))
    n = _CUM_CHUNK
    tri = (lax.broadcasted_iota(jnp.int32, (n, n), 1)
           <= lax.broadcasted_iota(jnp.int32, (n, n), 0)).astype(BF16)
    lane = lax.broadcasted_iota(jnp.int32, (n, LANES), 1)
    carry = jnp.zeros((1, LANES), F32)
    for blk in range(hs.shape[0] // n):
        hi, mid, lo = _split3(logf[blk * n:(blk + 1) * n])
        cb = _dot(tri, hi) + _dot(tri, mid) + _dot(tri, lo) + carry
        carry = cb[n - 1:n, :]
        hi, mid, lo = _split3(cb * LOG2E)
        packed = jnp.where((lane >= ONES_LANE) & (lane < ONES_LANE + _N_SPLIT), 1.0, 0.0)
        for t, term in enumerate((hi, mid, lo)):
            shifted = term.astype(F32) if t == 0 else pltpu.roll(term.astype(F32), t * N_HEADS, 1)
            packed = jnp.where((lane >= t * N_HEADS) & (lane < (t + 1) * N_HEADS), shifted, packed)
        c_ref[blk * n:(blk + 1) * n, :] = packed.astype(c_ref.dtype)


def _gates(h_small, bf_row, batch, seq):
    return pl.pallas_call(
        _gate_kernel,
        grid=(batch,),
        in_specs=[pl.BlockSpec((seq, LANES), lambda b: (b, 0)),
                  pl.BlockSpec((1, LANES), lambda b: (0, 0))],
        out_specs=[pl.BlockSpec((seq, LANES), lambda b: (b, 0)),
                   pl.BlockSpec((seq, LANES), lambda b: (b, 0))],
        out_shape=[jax.ShapeDtypeStruct(h_small.shape, BF16), jax.ShapeDtypeStruct(h_small.shape, F32)],
        compiler_params=pltpu.CompilerParams(dimension_semantics=("parallel",)),
        name="gates",
    )(h_small, bf_row)


_HALF = CMP_LEN // 2


def _gelu_tanh(x):
    return 0.5 * x * (1.0 + jnp.tanh(math.sqrt(2.0 / math.pi) * (x + 0.044715 * (x * x * x))))


def _compress_kernel(raw_ref, pk_ref, pv_ref, w1k_ref, w2k_ref, w1v_ref, w2v_ref, kc_ref, vc_ref, raw32_ref):
    seq = raw_ref.shape[0]
    n_chunks = seq // _HALF
    step = 512
    for slab in range(raw_ref.shape[1] // HEAD_DIM):
        for blk in range(seq // step):
            raw32_ref[slab, blk * step:(blk + 1) * step, :] = (
                raw_ref[blk * step:(blk + 1) * step, slab * HEAD_DIM:(slab + 1) * HEAD_DIM].astype(F32))
    for kv, (pos_ref, w1_ref, w2_ref, out_ref) in enumerate(
            ((pk_ref, w1k_ref, w2k_ref, kc_ref), (pv_ref, w1v_ref, w2v_ref, vc_ref))):
        first = jnp.zeros((KV_GROUPS * n_chunks, CMP_HIDDEN), F32)
        second = jnp.zeros((KV_GROUPS * n_chunks, CMP_HIDDEN), F32)
        for l in range(_HALF):
            a = jnp.concatenate([raw32_ref[kv * KV_GROUPS + g, pl.ds(l, n_chunks, stride=_HALF), :]
                                 for g in range(KV_GROUPS)], axis=0)
            a1 = (a + pos_ref[l:l + 1, :]).astype(BF16)
            a2 = (a + pos_ref[_HALF + l:_HALF + l + 1, :]).astype(BF16)
            first += _dot(a1, w1_ref[l * HEAD_DIM:(l + 1) * HEAD_DIM, :].astype(BF16))
            second += _dot(a2, w1_ref[(_HALF + l) * HEAD_DIM:(_HALF + l + 1) * HEAD_DIM, :].astype(BF16))
        hid = first + jnp.concatenate(
            [pltpu.roll(second[g * n_chunks:(g + 1) * n_chunks], n_chunks - 1, 0) for g in range(KV_GROUPS)],
            axis=0)
        out = _dot(_gelu_tanh(hid).astype(BF16), w2_ref[...].astype(BF16))
        for g in range(KV_GROUPS):
            out_ref[0, :, g * HEAD_DIM:(g + 1) * HEAD_DIM] = (
                out[g * n_chunks:(g + 1) * n_chunks].astype(out_ref.dtype))


def _compress(h_main, pos_k, pos_v, w1k, w2k, w1v, w2v, batch, seq):
    n_chunks = seq // _HALF
    raw_cols = 2 * KV_WIDTH
    assert _MAIN_OFF["nsa_v_cmp"] == _MAIN_OFF["nsa_k_cmp"] + KV_WIDTH
    raw_blk = _MAIN_OFF["nsa_k_cmp"] // raw_cols
    full = lambda shape: pl.BlockSpec(shape, lambda b: (0,) * len(shape), pipeline_mode=pl.Buffered(1))
    return pl.pallas_call(
        _compress_kernel,
        grid=(batch,),
        in_specs=[pl.BlockSpec((seq, raw_cols), lambda b: (b, raw_blk)),
                  full(pos_k.shape), full(pos_v.shape),
                  full(w1k.shape), full(w2k.shape), full(w1v.shape), full(w2v.shape)],
        out_specs=[pl.BlockSpec((1, n_chunks, KV_WIDTH), lambda b: (b, 0, 0)),
                   pl.BlockSpec((1, n_chunks, KV_WIDTH), lambda b: (b, 0, 0))],
        out_shape=[jax.ShapeDtypeStruct((batch, n_chunks, KV_WIDTH), BF16),
                   jax.ShapeDtypeStruct((batch, n_chunks, KV_WIDTH), BF16)],
        scratch_shapes=[pltpu.VMEM((raw_cols // HEAD_DIM, seq, HEAD_DIM), F32)],
        compiler_params=pltpu.CompilerParams(
            dimension_semantics=("parallel",), vmem_limit_bytes=VMEM_LIMIT),
        name="compress",
    )(h_main, pos_k, pos_v, w1k, w2k, w1v, w2v)


def _lane_tile(x, n):
    return x if n == 1 else jnp.concatenate([x] * n, axis=1)


def _lane_fold(p):
    out = p[:, :LANES]
    for t in range(1, p.shape[1] // LANES):
        out = out + p[:, t * LANES:(t + 1) * LANES]
    return out


def _first_weights(s, m_ref, l_ref, rows=slice(None)):
    m = jnp.broadcast_to(jnp.max(s, axis=1, keepdims=True), (s.shape[0], LANES))
    p = jnp.exp2(s - _lane_tile(m, s.shape[1] // LANES))
    m_ref[rows, :] = m
    l_ref[rows, :] = _lane_fold(p)
    return p.astype(BF16)


def _next_weights(s, row_const, m_ref, l_ref, rows=slice(None)):
    m_prev = m_ref[rows, :]
    m_cur = jnp.max(s, axis=1, keepdims=True)
    if row_const is not None:
        m_cur = m_cur + row_const
    m_new = jnp.maximum(m_prev, m_cur)
    alpha = jnp.exp2(m_prev - m_new)
    shift = m_new if row_const is None else m_new - row_const
    p = jnp.exp2(s - _lane_tile(shift, s.shape[1] // LANES))
    l_ref[rows, :] = alpha * l_ref[rows, :] + _lane_fold(p)
    m_ref[rows, :] = m_new
    return alpha, p.astype(BF16)


def _softmax_first(s, v, m_ref, l_ref, acc_ref, rows=slice(None)):
    acc_ref[rows, :] = _dot(_first_weights(s, m_ref, l_ref, rows), v)


def _softmax_update(s, v, row_const, m_ref, l_ref, acc_ref, rows=slice(None)):
    alpha, p = _next_weights(s, row_const, m_ref, l_ref, rows)
    acc_ref[rows, :] = alpha * acc_ref[rows, :] + _dot(p, v)


def _skewed(stages, jobs, before_tick=None):
    state = [None] * len(jobs)
    for tick in range(len(jobs) + len(stages) - 1):
        if before_tick and tick in before_tick:
            before_tick[tick]()
        for k, stage in enumerate(stages):
            j = tick - k
            if 0 <= j < len(jobs):
                state[j] = stage(jobs[j], state[j])


def _softmax_finish(l_ref, acc_ref, rows=slice(None)):
    return acc_ref[rows, :] / jnp.sum(l_ref[rows, :], axis=1, keepdims=True)


def _softmax_once(s, v):
    m = jnp.max(s, axis=1, keepdims=True)
    p = jnp.exp2(s - m)
    return _dot(p.astype(BF16), v) / jnp.sum(_lane_fold(p), axis=1, keepdims=True)


FOX_TQ = 512
FOX_HALF = FOX_TQ // 2


def _fox_routing():
    pk = np.zeros((N_HEADS, LANES, LANES), np.float32)
    for h in range(N_HEADS):
        for t in range(_N_SPLIT):
            pk[h, ONES_LANE, t * N_HEADS + h] = 1.0
            pk[h, t * N_HEADS + h, ONES_LANE + t] = -1.0
    return pk


def _fox_kernel(q_ref, k_ref, v_ref, z_ref, cq_ref, ck_ref, pk_ref, o_ref,
                kaug_ref, qaug_ref, sz_ref, m_ref, l_ref, acc_ref):
    i = pl.program_id(1)
    tq, half = FOX_TQ, FOX_HALF
    seq = k_ref.shape[0]

    @pl.when(i == 0)
    def _():
        for h in range(N_HEADS):
            for blk in range(seq // tq):
                rows = slice(blk * tq, (blk + 1) * tq)
                kaug_ref[h, rows, :HEAD_DIM] = k_ref[rows, h * HEAD_DIM:(h + 1) * HEAD_DIM]
                kaug_ref[h, rows, HEAD_DIM:] = _dot(ck_ref[rows, :], pk_ref[h]).astype(BF16)

    causal = (lax.broadcasted_iota(jnp.int32, (half, half), 1)
              <= lax.broadcasted_iota(jnp.int32, (half, half), 0))
    diag = pl.multiple_of(i * tq, tq)
    diag2 = pl.multiple_of(i * tq + half, half)
    heads = tuple(range(N_HEADS))
    rows = lambda h: slice(h * tq, (h + 1) * tq)
    top = lambda h: slice(h * tq, h * tq + half)
    bot = lambda h: slice(h * tq + half, (h + 1) * tq)
    cols = lambda h: slice(h * HEAD_DIM, (h + 1) * HEAD_DIM)

    def diag_scores(h, _):
        qaug_ref[rows(h), :HEAD_DIM] = q_ref[:, cols(h)]
        qaug_ref[rows(h), HEAD_DIM:] = cq_ref[...]
        s_left = _dot_nt(qaug_ref[rows(h), :], kaug_ref[h, pl.ds(diag, half), :])
        s_right = _dot_nt(qaug_ref[bot(h), :], kaug_ref[h, pl.ds(diag2, half), :])
        return s_left, s_right

    def diag_weights(h, scores):
        s_left, s_right = scores
        p_top = _first_weights(jnp.where(causal, s_left[:half], NEG), m_ref, l_ref, top(h))
        p_bot = _first_weights(jnp.concatenate([s_left[half:], jnp.where(causal, s_right, NEG)], axis=1),
                               m_ref, l_ref, bot(h))
        return p_top, p_bot

    def diag_values(h, weights):
        p_top, p_bot = weights
        acc_ref[top(h), :] = _dot(p_top, v_ref[pl.ds(diag, half), cols(h)])
        acc_ref[bot(h), :] = _dot(p_bot, v_ref[pl.ds(diag, tq), cols(h)])
        sz_ref[:, cols(h)] = _silu(z_ref[:, cols(h)].astype(F32))

    _skewed((diag_scores, diag_weights, diag_values), heads)

    def body(j, carry):
        off = pl.multiple_of(j * tq, tq)

        def scores(h, _):
            return _dot_nt(qaug_ref[rows(h), :], kaug_ref[h, pl.ds(off, tq), :])

        def weights(h, s):
            return _next_weights(s, None, m_ref, l_ref, rows(h))

        def values(h, rescale_and_weights):
            alpha, p = rescale_and_weights
            acc_ref[rows(h), :] = alpha * acc_ref[rows(h), :] + _dot(p, v_ref[pl.ds(off, tq), cols(h)])

        _skewed((scores, weights, values), heads)
        return carry

    lax.fori_loop(0, i, body, 0)
    for h in heads:
        o_ref[:, cols(h)] = (_softmax_finish(l_ref, acc_ref, rows(h)) * sz_ref[:, cols(h)]).astype(o_ref.dtype)


def _fox(h_main, c_packed, batch, seq):
    tq = FOX_TQ
    nq = seq // tq
    blk = lambda name: _MAIN_OFF[name] // WIDTH
    pk = jnp.asarray(_fox_routing(), BF16)
    full = lambda shape: pl.BlockSpec(shape, lambda b, i: (0,) * len(shape))
    stat = pltpu.VMEM((N_HEADS * tq, LANES), F32)
    return pl.pallas_call(
        _fox_kernel,
        grid=(batch, nq),
        in_specs=[
            pl.BlockSpec((tq, WIDTH), lambda b, i: (b * nq + i, blk("fox_q"))),
            pl.BlockSpec((seq, WIDTH), lambda b, i: (b, blk("fox_k"))),
            pl.BlockSpec((seq, WIDTH), lambda b, i: (b, blk("fox_v"))),
            pl.BlockSpec((tq, WIDTH), lambda b, i: (b * nq + i, blk("fox_z"))),
            pl.BlockSpec((tq, LANES), lambda b, i: (b * nq + i, 0)),
            pl.BlockSpec((seq, LANES), lambda b, i: (b, 0)),
            full(pk.shape),
        ],
        out_specs=pl.BlockSpec((tq, WIDTH), lambda b, i: (b * nq + i, 0)),
        out_shape=jax.ShapeDtypeStruct((batch * seq, WIDTH), BF16),
        scratch_shapes=[pltpu.VMEM((N_HEADS, seq, MXU_DEPTH), BF16),
                        pltpu.VMEM((N_HEADS * tq, MXU_DEPTH), BF16),
                        pltpu.VMEM((tq, WIDTH), F32),
                        stat, stat, stat],
        compiler_params=pltpu.CompilerParams(
            dimension_semantics=("parallel", "arbitrary"), vmem_limit_bytes=VMEM_LIMIT),
        name="fox",
    )(h_main, h_main, h_main, h_main, c_packed, c_packed, pk)


TQ = 256
TK = TQ
WIN_BLOCKS = (WINDOW + TQ) // TK
SEL_NEAR_BLOCKS = 2
FAR_TK = 2 * TK
N_SEL_ROWS = 32
MASK_BUCKET = REL_BUCKETS
GROUP_ROWS = HEADS_PER_GROUP * TQ
T_DIAG, T_PREV, T_WIN2, T_NONE = range(4)
N_GATHERED_TILES = 3


def _bias_lookup(tab_ref, h, idx):
    row = jnp.broadcast_to(tab_ref[h:h + 1, :], idx.shape)
    return jnp.take_along_axis(row, idx, axis=1, mode="promise_in_bounds")


def _nsa_kernel(q_ref, ks_ref, vs_ref, kw_ref, vw_ref, z_ref, kc_ref, vc_ref, cidx_ref, tidx_ref,
                tab_ref, g_ref, et_ref, ov_ref, o_ref,
                ksaug_ref, qs_ref, qaug_ref, m_ref, l_ref, acc_ref, oc_ref, ow_ref, gs_ref, t_ref, cb_ref):
    i = pl.program_id(1)
    t0 = i * TQ

    @pl.when((pl.program_id(0) == 0) & (i == 0))
    def _():
        for h in range(N_HEADS):
            for d in range(N_GATHERED_TILES):
                for half in range(TK // LANES):
                    cs = slice(half * LANES, (half + 1) * LANES)
                    t_ref[h, d, :, cs] = _bias_lookup(tab_ref, h, tidx_ref[d, :, cs])
            for blk in range(cidx_ref.shape[0] // TQ):
                rows = slice(blk * TQ, (blk + 1) * TQ)
                cb_ref[h, rows, :] = _bias_lookup(tab_ref, h, cidx_ref[rows, :])
            t_ref[h, T_NONE] = jnp.full((TQ, TK), NEG, F32)

    @pl.when(i == 0)
    def _():
        for g in range(KV_GROUPS):
            ksaug_ref[g, :, :HEAD_DIM] = ks_ref[:, g * HEAD_DIM:(g + 1) * HEAD_DIM]
            ksaug_ref[g, :, HEAD_DIM:] = et_ref[...]

    cmaskf = (lax.broadcasted_iota(jnp.int32, (TQ, LANES), 1) * CMP_STRIDE + (CMP_LEN - 1)
              <= t0 + lax.broadcasted_iota(jnp.int32, (TQ, LANES), 0)).astype(F32)
    j_t = lax.broadcasted_iota(jnp.int32, (N_SEL_ROWS, TQ), 0)
    t_t = t0 + lax.broadcasted_iota(jnp.int32, (N_SEL_ROWS, TQ), 1)
    cur_t = t_t // SEL_LEN
    forced_t = (j_t == 0) | (j_t == cur_t) | (j_t == cur_t - 1)
    valid_t = j_t * SEL_LEN <= t_t

    def near_span(n_blocks):
        first = jnp.maximum(i - (n_blocks - 1), 0)
        return first, pl.ds(pl.multiple_of(first * TK, TK), n_blocks * TK)

    win_first, win_keys = near_span(WIN_BLOCKS)
    sel_first, sel_keys = near_span(SEL_NEAR_BLOCKS)

    def near_bias(heads, first, n_blocks):
        def tile(kk):
            dist = i - (first + kk)
            return jnp.where(dist == 0, T_DIAG, jnp.where(dist == 1, T_PREV,
                             jnp.where(dist == 2, T_WIN2, T_NONE)))
        return jnp.concatenate(
            [jnp.concatenate([t_ref[h, tile(kk)] for kk in range(n_blocks)], axis=1) for h in heads],
            axis=0)

    groups = tuple((g, g * HEAD_DIM, tuple(range(g * HEADS_PER_GROUP, (g + 1) * HEADS_PER_GROUP)),
                    slice(g * GROUP_ROWS, (g + 1) * GROUP_ROWS)) for g in range(KV_GROUPS))
    hrows = lambda h: slice(h * TQ, (h + 1) * TQ)
    for h in range(N_HEADS):
        q = q_ref[:, h * HEAD_DIM:(h + 1) * HEAD_DIM]
        qs_ref[hrows(h), :] = q
        qaug_ref[hrows(h), :HEAD_DIM] = q

    chain = {}

    def compressed_branch():
        cbias = jnp.concatenate([cb_ref[h, pl.ds(pl.multiple_of(t0, TQ), TQ), :] for h in range(N_HEADS)],
                                axis=0)
        sc = jnp.concatenate([_dot_nt(qs_ref[grows, :], kc_ref[0, :, glo:glo + HEAD_DIM])
                              for g, glo, heads, grows in groups], axis=0) + cbias
        e = jnp.exp2(sc - jnp.max(sc, axis=1, keepdims=True))
        p = e / jnp.sum(e, axis=1, keepdims=True) * jnp.concatenate([cmaskf] * N_HEADS, axis=0)
        imp_parts = []
        for g, glo, heads, grows in groups:
            oc_ref[grows, :] = _dot(p[grows].astype(BF16), vc_ref[0, :, glo:glo + HEAD_DIM])
            psum = p[hrows(heads[0])]
            for h in heads[1:]:
                psum = psum + p[hrows(h)]
            p_hi = psum.astype(BF16)
            p_lo = (psum - p_hi.astype(F32)).astype(BF16)
            imp_parts.append((_dot_nt(ov_ref[...], p_hi) + _dot_nt(ov_ref[...], p_lo))[:N_SEL_ROWS])
        chain["importance"] = jnp.concatenate(imp_parts, axis=1)

    def block_selection():
        both = lambda a: jnp.concatenate([a] * KV_GROUPS, axis=1)
        j_b = both(j_t)
        x = jnp.where(both(valid_t), jnp.where(both(forced_t), -NEG, chain["importance"]), NEG)
        cnt = jnp.zeros(x.shape, F32)
        for jp in range(N_SEL_ROWS):
            row = x[jp:jp + 1, :]
            beats = (row > x) | ((row == x) & (j_b > jp))
            cnt = cnt + jnp.where(beats, 1.0, 0.0)
        unsel_t = jnp.where(cnt < SEL_TOPK, 0.0, 1.0)
        unsel_t = jnp.concatenate([unsel_t, jnp.zeros((LANES - N_SEL_ROWS, x.shape[1]), F32)], axis=0)
        unsel = unsel_t.T.astype(BF16)
        for h in range(N_HEADS):
            g = h // HEADS_PER_GROUP
            qaug_ref[hrows(h), HEAD_DIM:] = unsel[g * TQ:(g + 1) * TQ]

    def near_scores(job, _):
        branch, (g, glo, heads, grows) = job
        if branch == "window":
            return (_dot_nt(qs_ref[grows, :], kw_ref[win_keys, glo:glo + HEAD_DIM])
                    + near_bias(heads, win_first, WIN_BLOCKS))
        return (_dot_nt(qaug_ref[grows, :], ksaug_ref[g, sel_keys, :])
                + near_bias(heads, sel_first, SEL_NEAR_BLOCKS))

    def near_weights(job, s):
        branch, (g, glo, heads, grows) = job
        if branch == "window":
            p = jnp.exp2(s - jnp.max(s, axis=1, keepdims=True))
            return p.astype(BF16), _lane_fold(p)
        return _first_weights(s, m_ref, l_ref, grows)

    def near_values(job, weights):
        branch, (g, glo, heads, grows) = job
        if branch == "selected":
            acc_ref[grows, :] = _dot(weights, vs_ref[sel_keys, glo:glo + HEAD_DIM])
            return
        p, l = weights
        o_win = _dot(p, vw_ref[win_keys, glo:glo + HEAD_DIM]) / jnp.sum(l, axis=1, keepdims=True)
        for n, h in enumerate(heads):
            gl = GATE_LANE0 + h * N_BRANCHES
            sz = _silu(z_ref[:, h * HEAD_DIM:(h + 1) * HEAD_DIM].astype(F32))
            gs_ref[hrows(h), :] = g_ref[:, gl + 1:gl + 2] * sz
            ow_ref[hrows(h), :] = (g_ref[:, gl:gl + 1] * oc_ref[hrows(h), :]
                                   + g_ref[:, gl + 2:gl + 3] * o_win[n * TQ:(n + 1) * TQ]) * sz

    _skewed((near_scores, near_weights, near_values),
            tuple((branch, group) for branch in ("window", "selected") for group in groups),
            before_tick={1: compressed_branch, 2: block_selection})

    def sel_far(off, width):
        def scores(group, _):
            g, glo, heads, grows = group
            return _dot_nt(qaug_ref[grows, :], ksaug_ref[g, pl.ds(off, width), :])

        def weights(group, s):
            g, glo, heads, grows = group
            return _next_weights(s, None, m_ref, l_ref, grows)

        def values(group, rescale_and_weights):
            g, glo, heads, grows = group
            alpha, p = rescale_and_weights
            acc_ref[grows, :] = alpha * acc_ref[grows, :] + _dot(p, vs_ref[pl.ds(off, width), glo:glo + HEAD_DIM])

        _skewed((scores, weights, values), groups)

    def sel_far_pair(j, carry):
        sel_far(pl.multiple_of(j * FAR_TK, FAR_TK), FAR_TK)
        return carry

    lax.fori_loop(0, sel_first // 2, sel_far_pair, 0)

    @pl.when(sel_first % 2 == 1)
    def _():
        sel_far(pl.multiple_of((sel_first - 1) * TK, TK), TK)

    for h in range(N_HEADS):
        o_ref[:, h * HEAD_DIM:(h + 1) * HEAD_DIM] = (
            ow_ref[hrows(h), :] + gs_ref[hrows(h), :] * _softmax_finish(l_ref, acc_ref, hrows(h))
        ).astype(o_ref.dtype)


def _nsa(h_main, k_cmp, v_cmp, cmp_idx, tile_idx, table, gates, e_t, ov_t, batch, seq):
    nq = seq // TQ
    wblk = lambda name: _MAIN_OFF[name] // WIDTH
    kvblk = lambda name: _MAIN_OFF[name] // KV_WIDTH
    full = lambda shape: pl.BlockSpec(shape, lambda b, i: (0,) * len(shape),
                                      pipeline_mode=pl.Buffered(1))
    kv_spec = lambda name: pl.BlockSpec((seq, KV_WIDTH), lambda b, i: (b, kvblk(name)))
    n_chunks = k_cmp.shape[1]
    stat = pltpu.VMEM((N_HEADS * TQ, LANES), F32)
    return pl.pallas_call(
        _nsa_kernel,
        grid=(batch, nq),
        in_specs=[
            pl.BlockSpec((TQ, WIDTH), lambda b, i: (b * nq + i, wblk("nsa_q"))),
            kv_spec("nsa_k_sel"), kv_spec("nsa_v_sel"), kv_spec("nsa_k_win"), kv_spec("nsa_v_win"),
            pl.BlockSpec((TQ, WIDTH), lambda b, i: (b * nq + i, wblk("nsa_z"))),
            pl.BlockSpec((1, n_chunks, KV_WIDTH), lambda b, i: (b, 0, 0)),
            pl.BlockSpec((1, n_chunks, KV_WIDTH), lambda b, i: (b, 0, 0)),
            full(cmp_idx.shape), full(tile_idx.shape), full(table.shape),
            pl.BlockSpec((TQ, LANES), lambda b, i: (b * nq + i, 0)),
            full(e_t.shape), full(ov_t.shape),
        ],
        out_specs=pl.BlockSpec((TQ, WIDTH), lambda b, i: (b * nq + i, 0)),
        out_shape=jax.ShapeDtypeStruct((batch * seq, WIDTH), BF16),
        scratch_shapes=[pltpu.VMEM((KV_GROUPS, seq, MXU_DEPTH), BF16),
                        pltpu.VMEM((N_HEADS * TQ, HEAD_DIM), BF16),
                        pltpu.VMEM((N_HEADS * TQ, MXU_DEPTH), BF16),
                        stat, stat, stat,
                        stat, stat, stat,
                        pltpu.VMEM((N_HEADS, T_NONE + 1, TQ, TK), F32),
                        pltpu.VMEM((N_HEADS, seq, LANES), F32)],
        compiler_params=pltpu.CompilerParams(
            dimension_semantics=("arbitrary", "arbitrary"), vmem_limit_bytes=VMEM_LIMIT),
        name="nsa",
    )(h_main, h_main, h_main, h_main, h_main, h_main, k_cmp, v_cmp, cmp_idx, tile_idx, table, gates,
      e_t, ov_t)


OUT_TM = 512
OUT_TN = 512
OUT_ROW_GROUPS = 2


def _out_kernel(alpha, ua_ref, ub_ref, ga_ref, gb_ref, x_ref, wa_ref, wb_ref, wo_ref, lg_ref, lb_ref,
                o_ref, merged_ref):
    d_model = o_ref.shape[1]
    chunks = tuple(slice(c * OUT_TN, (c + 1) * OUT_TN) for c in range(d_model // OUT_TN))
    half = o_ref.shape[0] // OUT_ROW_GROUPS

    def merge(rows, _):
        for cs in chunks:
            ya = _dot(ua_ref[rows, :], wa_ref[:, cs])
            yb = _dot(ub_ref[rows, :], wb_ref[:, cs])
            merged_ref[rows, cs] = (_sigmoid(ga_ref[rows, cs].astype(F32)) * ya
                                    + _sigmoid(gb_ref[rows, cs].astype(F32)) * yb).astype(merged_ref.dtype)

    def residual(rows, _):
        total = jnp.zeros((half, LANES), F32)
        for cs in chunks:
            y = alpha * x_ref[rows, cs] + _dot(merged_ref[rows, :], wo_ref[:, cs])
            o_ref[rows, cs] = y
            total = total + _lane_fold(y)
        return total

    def layer_norm(rows, total):
        mu = jnp.sum(total, axis=1, keepdims=True) * (1.0 / d_model)
        sq = jnp.zeros((half, LANES), F32)
        for cs in chunks:
            d = o_ref[rows, cs] - mu
            sq = sq + _lane_fold(d * d)
        rstd = lax.rsqrt(jnp.sum(sq, axis=1, keepdims=True) * (1.0 / d_model) + LN_EPS)
        for cs in chunks:
            o_ref[rows, cs] = (o_ref[rows, cs] - mu) * rstd * lg_ref[:, cs] + lb_ref[:, cs]

    _skewed((merge, residual, layer_norm),
            tuple(slice(r * half, (r + 1) * half) for r in range(OUT_ROW_GROUPS)))


def _out(u_a, u_b, h_main, x2, w_a, w_b, w_o, ln_g, ln_b, alpha, tm=OUT_TM):
    m = x2.shape[0]
    mblk = lambda name: _MAIN_OFF[name] // D_MODEL
    const = lambda shape: pl.BlockSpec(shape, lambda i: (0, 0), pipeline_mode=pl.Buffered(1))
    return pl.pallas_call(
        functools.partial(_out_kernel, alpha),
        grid=(m // tm,),
        in_specs=[
            pl.BlockSpec((tm, WIDTH), lambda i: (i, 0)),
            pl.BlockSpec((tm, WIDTH), lambda i: (i, 0)),
            pl.BlockSpec((tm, D_MODEL), lambda i: (i, mblk("merge_a"))),
            pl.BlockSpec((tm, D_MODEL), lambda i: (i, mblk("merge_b"))),
            pl.BlockSpec((tm, D_MODEL), lambda i: (i, 0)),
            const(w_a.shape), const(w_b.shape), const(w_o.shape),
            const(ln_g.shape), const(ln_b.shape),
        ],
        out_specs=pl.BlockSpec((tm, D_MODEL), lambda i: (i, 0)),
        out_shape=jax.ShapeDtypeStruct((m, D_MODEL), F32),
        scratch_shapes=[pltpu.VMEM((tm, D_MODEL), BF16)],
        compiler_params=pltpu.CompilerParams(
            dimension_semantics=("parallel",), vmem_limit_bytes=VMEM_LIMIT),
        name="out",
    )(u_a, u_b, h_main, h_main, x2, w_a, w_b, w_o, ln_g, ln_b)


def _bucket_np(dist):
    n = np.maximum(dist, 0)
    exact = REL_BUCKETS // 2
    large = exact + (np.log(np.maximum(n, 1).astype(np.float32) / exact)
                     / math.log(REL_MAX_DIST / exact) * (REL_BUCKETS - exact)).astype(np.int32)
    return np.where(n < exact, n, np.minimum(large, REL_BUCKETS - 1)).astype(np.int32)


@functools.lru_cache(maxsize=None)
def _static_tables(seq):
    r = np.arange(TQ)[:, None]
    c = np.arange(TK)[None, :]
    tile_idx = np.stack([_bucket_np(r - c), _bucket_np(TQ + r - c), _bucket_np(2 * TQ + r - c)])
    tile_ok = np.stack([c <= r, np.ones((TQ, TK), bool), (2 * TQ + r - c) < WINDOW])
    tile_idx = np.where(tile_ok, tile_idx, MASK_BUCKET).astype(np.int32)
    t = np.arange(seq)[:, None]
    cblk = np.arange(LANES)[None, :]
    blk_end = cblk * CMP_STRIDE + CMP_LEN - 1
    n_cmp = (seq - CMP_LEN) // CMP_STRIDE + 1
    cmp_idx = np.where((blk_end <= t) & (cblk < n_cmp), _bucket_np(t - blk_end), MASK_BUCKET).astype(np.int32)
    e_t = np.where((np.arange(seq)[:, None] // SEL_LEN) == np.arange(LANES)[None, :], NEG, 0.0)
    cs = (np.arange(LANES) * CMP_STRIDE)[None, :]
    ss = (np.arange(LANES) * SEL_LEN)[:, None]
    ov_t = ((cs < ss + SEL_LEN) & (cs + CMP_LEN > ss)
            & (np.arange(LANES)[None, :] < n_cmp) & (np.arange(LANES)[:, None] < seq // SEL_LEN))
    return tile_idx, cmp_idx, e_t.astype(np.float32), ov_t.astype(np.float32)


def _layer(x, w_in, b_f, cmp_pos_k, cmp_pos_v, cmp_wk1, cmp_wk2, cmp_wv1, cmp_wv2,
           w_a, w_b, w_o, ln_g, ln_b, rel_bias, alpha):
    batch, seq, d_model = x.shape
    assert d_model == D_MODEL and seq % FOX_TQ == 0 and seq >= WIN_BLOCKS * TK and REL_MAX_DIST <= TQ
    assert seq // SEL_LEN == N_SEL_ROWS and seq // CMP_STRIDE == LANES
    x2 = x.reshape(batch * seq, d_model)

    w_t = jnp.swapaxes(w_in, 0, 1)
    trows = lambda name: w_t[_REF_OFF[name][0]:_REF_OFF[name][0] + _REF_OFF[name][1]]
    n_small = N_HEADS + N_HEADS * N_BRANCHES
    w_small_t = jnp.concatenate(
        [trows("fox_f"), trows("nsa_gate"), jnp.zeros((LANES - n_small, d_model), F32)], axis=0)
    bf_row = jnp.concatenate([b_f.astype(F32), jnp.zeros((LANES - N_HEADS,), F32)]).reshape(1, LANES)

    h_main, h_small = _proj(x2, w_t, w_small_t)
    c_col, gates = _gates(h_small, bf_row, batch, seq)
    u_a = _fox(h_main, c_col, batch, seq)

    k_cmp, v_cmp = _compress(h_main, cmp_pos_k, cmp_pos_v, cmp_wk1, cmp_wk2, cmp_wv1, cmp_wv2, batch, seq)

    tile_idx, cmp_idx, e_t, ov_t = _static_tables(seq)
    bias = rel_bias.T.astype(F32)
    table = jnp.concatenate(
        [(bias - bias[:, REL_BUCKETS - 1:]) * LOG2E, jnp.full((N_HEADS, 1), NEG, F32),
         jnp.zeros((N_HEADS, LANES - REL_BUCKETS - 1), F32)], axis=1)
    u_b = _nsa(h_main, k_cmp, v_cmp, jnp.asarray(cmp_idx), jnp.asarray(tile_idx), table, gates,
               jnp.asarray(e_t, BF16), jnp.asarray(ov_t, BF16), batch, seq)

    out = _out(u_a, u_b, h_main, x2, w_a.astype(BF16), w_b.astype(BF16), w_o.astype(BF16),
               ln_g.reshape(1, d_model), ln_b.reshape(1, d_model), alpha)
    return out.reshape(batch, seq, d_model)


def kernel(x, w_in, b_f, cmp_pos_k, cmp_pos_v, cmp_wk1, cmp_wk2, cmp_wv1, cmp_wv2,
           w_a, w_b, w_o, ln_g, ln_b, rel_bias):
    depth = w_in.shape[0]
    alpha = (2 * depth) ** 0.25
    for layer in range(depth):
        x = _layer(x, w_in[layer], b_f[layer], cmp_pos_k[layer], cmp_pos_v[layer], cmp_wk1[layer],
                   cmp_wk2[layer], cmp_wv1[layer], cmp_wv2[layer], w_a[layer], w_b[layer], w_o[layer],
                   ln_g[layer], ln_b[layer], rel_bias, alpha)
    return x
```

```python
import functools
import math

import jax
import jax.numpy as jnp
import numpy as np
from jax import lax
from jax.experimental import pallas as pl
from jax.experimental.pallas import tpu as pltpu

F32 = jnp.float32
BF16 = jnp.bfloat16

D_MODEL = 2048
HEAD_DIM = 128
N_HEADS = 8
WIDTH = N_HEADS * HEAD_DIM
KV_GROUPS = 2
HEADS_PER_GROUP = N_HEADS // KV_GROUPS
KV_WIDTH = KV_GROUPS * HEAD_DIM
N_BRANCHES = 3
CMP_LEN = 32
CMP_STRIDE = 16
CMP_HIDDEN = 256
SEL_LEN = 64
SEL_TOPK = 8
WINDOW = 512
REL_BUCKETS = 32
REL_MAX_DIST = 128
LN_EPS = 1e-5
NEG = -1e30
LOG2E = math.log2(math.e)
Q_PRESCALE = HEAD_DIM ** -0.5 * LOG2E

LANES = 128
SUBLANES = 8
MXU_DEPTH = 256
VMEM_LIMIT = 60 * 1024 * 1024

_REF_LAYOUT = (
    ("fox_q", WIDTH), ("fox_k", WIDTH), ("fox_v", WIDTH), ("fox_f", N_HEADS), ("fox_z", WIDTH),
    ("nsa_q", WIDTH), ("nsa_k_cmp", KV_WIDTH), ("nsa_v_cmp", KV_WIDTH), ("nsa_k_sel", KV_WIDTH),
    ("nsa_v_sel", KV_WIDTH), ("nsa_k_win", KV_WIDTH), ("nsa_v_win", KV_WIDTH),
    ("nsa_gate", N_HEADS * N_BRANCHES), ("nsa_z", WIDTH), ("merge_a", D_MODEL), ("merge_b", D_MODEL),
)
_REF_OFF = {}
_o = 0
for _n, _w in _REF_LAYOUT:
    _REF_OFF[_n] = (_o, _w)
    _o += _w

_MAIN_ORDER = ("fox_q", "fox_k", "fox_v", "fox_z", "nsa_q", "nsa_z", "merge_a", "merge_b",
               "nsa_k_cmp", "nsa_v_cmp", "nsa_k_sel", "nsa_v_sel", "nsa_k_win", "nsa_v_win")
_QUERY_COLS = ("fox_q", "nsa_q")
_MAIN_OFF = {}
_o = 0
for _n in _MAIN_ORDER:
    _MAIN_OFF[_n] = _o
    _o += _REF_OFF[_n][1]
MAIN_COLS = _o
GATE_LANE0 = N_HEADS


def _dot(a, b):
    return jnp.dot(a, b, preferred_element_type=F32)


def _dot_nt(a, b):
    return lax.dot_general(a, b, (((1,), (1,)), ((), ())), preferred_element_type=F32)


def _sigmoid(x):
    return 0.5 + 0.5 * jnp.tanh(0.5 * x)


def _silu(x):
    return x * _sigmoid(x)


PROJ_TN = 512


def _proj_tiles():
    rows, is_query = [], []
    for name in _MAIN_ORDER:
        off, width = _REF_OFF[name]
        start = _MAIN_OFF[name]
        for c in range(start, start + width):
            if c % PROJ_TN == 0:
                assert (off + c - start) % SUBLANES == 0
                rows.append((off + c - start) // SUBLANES)
                is_query.append(int(name in _QUERY_COLS))
    return np.asarray(rows, np.int32), np.asarray(is_query, np.int32)


PROJ_ROWS = 4096
PROJ_RB = 512


def _proj_kernel(rows_ref, isq_ref, x_hbm, wt_ref, wst_ref, o_ref, os_ref,
                 xb_ref, stage_ref, wb_ref, sem):
    i = pl.program_id(0)
    j = pl.program_id(1)
    n_blocks = PROJ_ROWS // PROJ_RB
    wb_ref[...] = wt_ref[...].astype(BF16)
    scale = jnp.where(isq_ref[j] == 1, Q_PRESCALE, 1.0)

    def x_copy(r, slot):
        row0 = pl.multiple_of(i * PROJ_ROWS + r * PROJ_RB, PROJ_RB)
        return pltpu.make_async_copy(x_hbm.at[pl.ds(row0, PROJ_RB), :], stage_ref.at[slot], sem.at[slot])

    def block(r):
        rows = slice(r * PROJ_RB, (r + 1) * PROJ_RB)
        o_ref[rows, :] = (_dot_nt(xb_ref[rows, :], wb_ref[...]) * scale).astype(o_ref.dtype)

    @pl.when(j == 0)
    def _():
        wsb = wst_ref[...].astype(BF16)
        x_copy(0, 0).start()
        for r in range(n_blocks):
            if r + 1 < n_blocks:
                x_copy(r + 1, (r + 1) % 2).start()
            x_copy(r, r % 2).wait()
            rows = slice(r * PROJ_RB, (r + 1) * PROJ_RB)
            xb_ref[rows, :] = stage_ref[r % 2].astype(BF16)
            os_ref[rows, :] = _dot_nt(xb_ref[rows, :], wsb)
            block(r)

    @pl.when(j > 0)
    def _():
        for r in range(n_blocks):
            block(r)


def _proj(x2, w_t, w_small_t):
    m, k = x2.shape
    rows, is_query = _proj_tiles()
    assert MAIN_COLS % PROJ_TN == 0 and len(rows) == MAIN_COLS // PROJ_TN and m % PROJ_ROWS == 0
    grid_spec = pltpu.PrefetchScalarGridSpec(
        num_scalar_prefetch=2,
        grid=(m // PROJ_ROWS, len(rows)),
        in_specs=[
            pl.BlockSpec(memory_space=pl.ANY),
            pl.BlockSpec((pl.Element(PROJ_TN), pl.Element(k)),
                         lambda i, j, rows, isq: (rows[j] * SUBLANES, 0)),
            pl.BlockSpec((LANES, k), lambda i, j, rows, isq: (0, 0)),
        ],
        out_specs=[
            pl.BlockSpec((PROJ_ROWS, PROJ_TN), lambda i, j, rows, isq: (i, j)),
            pl.BlockSpec((PROJ_ROWS, LANES), lambda i, j, rows, isq: (i, 0)),
        ],
        scratch_shapes=[pltpu.VMEM((PROJ_ROWS, k), BF16),
                        pltpu.VMEM((2, PROJ_RB, k), F32),
                        pltpu.VMEM((PROJ_TN, k), BF16),
                        pltpu.SemaphoreType.DMA((2,))],
    )
    return pl.pallas_call(
        _proj_kernel,
        grid_spec=grid_spec,
        out_shape=[jax.ShapeDtypeStruct((m, MAIN_COLS), BF16), jax.ShapeDtypeStruct((m, LANES), F32)],
        compiler_params=pltpu.CompilerParams(
            dimension_semantics=("parallel", "arbitrary"), vmem_limit_bytes=VMEM_LIMIT),
        name="proj",
    )(jnp.asarray(rows), jnp.asarray(is_query), x2, w_t, w_small_t)


_CUM_CHUNK = 256
_N_SPLIT = 3
ONES_LANE = _N_SPLIT * N_HEADS


def _split3(x):
    hi = x.astype(BF16)
    r1 = x - hi.astype(F32)
    mid = r1.astype(BF16)
    lo = (r1 - mid.astype(F32)).astype(BF16)
    return hi, mid, lo


def _gate_kernel(hs_ref, bf_ref, c_ref, g_ref):
    hs = hs_ref[...]
    g_ref[...] = _sigmoid(hs)
    z = hs + bf_ref[...]
    logf = jnp.minimum(z, 0.0) - jnp.log1p(jnp.exp(-jnp.abs(z)))
    n = _CUM_CHUNK
    tri = (lax.broadcasted_iota(jnp.int32, (n, n), 1)
           <= lax.broadcasted_iota(jnp.int32, (n, n), 0)).astype(BF16)
    lane = lax.broadcasted_iota(jnp.int32, (n, LANES), 1)
    carry = jnp.zeros((1, LANES), F32)
    for blk in range(hs.shape[0] // n):
        hi, mid, lo = _split3(logf[blk * n:(blk + 1) * n])
        cb = _dot(tri, hi) + _dot(tri, mid) + _dot(tri, lo) + carry
        carry = cb[n - 1:n, :]
        hi, mid, lo = _split3(cb * LOG2E)
        packed = jnp.where((lane >= ONES_LANE) & (lane < ONES_LANE + _N_SPLIT), 1.0, 0.0)
        for t, term in enumerate((hi, mid, lo)):
            shifted = term.astype(F32) if t == 0 else pltpu.roll(term.astype(F32), t * N_HEADS, 1)
            packed = jnp.where((lane >= t * N_HEADS) & (lane < (t + 1) * N_HEADS), shifted, packed)
        c_ref[blk * n:(blk + 1) * n, :] = packed.astype(c_ref.dtype)


def _gates(h_small, bf_row, batch, seq):
    return pl.pallas_call(
        _gate_kernel,
        grid=(batch,),
        in_specs=[pl.BlockSpec((seq, LANES), lambda b: (b, 0)),
                  pl.BlockSpec((1, LANES), lambda b: (0, 0))],
        out_specs=[pl.BlockSpec((seq, LANES), lambda b: (b, 0)),
                   pl.BlockSpec((seq, LANES), lambda b: (b, 0))],
        out_shape=[jax.ShapeDtypeStruct(h_small.shape, BF16), jax.ShapeDtypeStruct(h_small.shape, F32)],
        compiler_params=pltpu.CompilerParams(dimension_semantics=("parallel",)),
        name="gates",
    )(h_small, bf_row)


_HALF = CMP_LEN // 2


def _gelu_tanh(x):
    return 0.5 * x * (1.0 + jnp.tanh(math.sqrt(2.0 / math.pi) * (x + 0.044715 * (x * x * x))))


def _compress_kernel(raw_ref, pk_ref, pv_ref, w1k_ref, w2k_ref, w1v_ref, w2v_ref, kc_ref, vc_ref, raw32_ref):
    seq = raw_ref.shape[0]
    n_chunks = seq // _HALF
    step = 512
    for slab in range(raw_ref.shape[1] // HEAD_DIM):
        for blk in range(seq // step):
            raw32_ref[slab, blk * step:(blk + 1) * step, :] = (
                raw_ref[blk * step:(blk + 1) * step, slab * HEAD_DIM:(slab + 1) * HEAD_DIM].astype(F32))
    for kv, (pos_ref, w1_ref, w2_ref, out_ref) in enumerate(
            ((pk_ref, w1k_ref, w2k_ref, kc_ref), (pv_ref, w1v_ref, w2v_ref, vc_ref))):
        first = jnp.zeros((KV_GROUPS * n_chunks, CMP_HIDDEN), F32)
        second = jnp.zeros((KV_GROUPS * n_chunks, CMP_HIDDEN), F32)
        for l in range(_HALF):
            a = jnp.concatenate([raw32_ref[kv * KV_GROUPS + g, pl.ds(l, n_chunks, stride=_HALF), :]
                                 for g in range(KV_GROUPS)], axis=0)
            a1 = (a + pos_ref[l:l + 1, :]).astype(BF16)
            a2 = (a + pos_ref[_HALF + l:_HALF + l + 1, :]).astype(BF16)
            first += _dot(a1, w1_ref[l * HEAD_DIM:(l + 1) * HEAD_DIM, :].astype(BF16))
            second += _dot(a2, w1_ref[(_HALF + l) * HEAD_DIM:(_HALF + l + 1) * HEAD_DIM, :].astype(BF16))
        hid = first + jnp.concatenate(
            [pltpu.roll(second[g * n_chunks:(g + 1) * n_chunks], n_chunks - 1, 0) for g in range(KV_GROUPS)],
            axis=0)
        out = _dot(_gelu_tanh(hid).astype(BF16), w2_ref[...].astype(BF16))
        for g in range(KV_GROUPS):
            out_ref[0, :, g * HEAD_DIM:(g + 1) * HEAD_DIM] = (
                out[g * n_chunks:(g + 1) * n_chunks].astype(out_ref.dtype))


def _compress(h_main, pos_k, pos_v, w1k, w2k, w1v, w2v, batch, seq):
    n_chunks = seq // _HALF
    raw_cols = 2 * KV_WIDTH
    assert _MAIN_OFF["nsa_v_cmp"] == _MAIN_OFF["nsa_k_cmp"] + KV_WIDTH
    raw_blk = _MAIN_OFF["nsa_k_cmp"] // raw_cols
    full = lambda shape: pl.BlockSpec(shape, lambda b: (0,) * len(shape), pipeline_mode=pl.Buffered(1))
    return pl.pallas_call(
        _compress_kernel,
        grid=(batch,),
        in_specs=[pl.BlockSpec((seq, raw_cols), lambda b: (b, raw_blk)),
                  full(pos_k.shape), full(pos_v.shape),
                  full(w1k.shape), full(w2k.shape), full(w1v.shape), full(w2v.shape)],
        out_specs=[pl.BlockSpec((1, n_chunks, KV_WIDTH), lambda b: (b, 0, 0)),
                   pl.BlockSpec((1, n_chunks, KV_WIDTH), lambda b: (b, 0, 0))],
        out_shape=[jax.ShapeDtypeStruct((batch, n_chunks, KV_WIDTH), BF16),
                   jax.ShapeDtypeStruct((batch, n_chunks, KV_WIDTH), BF16)],
        scratch_shapes=[pltpu.VMEM((raw_cols // HEAD_DIM, seq, HEAD_DIM), F32)],
        compiler_params=pltpu.CompilerParams(
            dimension_semantics=("parallel",), vmem_limit_bytes=VMEM_LIMIT),
        name="compress",
    )(h_main, pos_k, pos_v, w1k, w2k, w1v, w2v)


def _lane_tile(x, n):
    return x if n == 1 else jnp.concatenate([x] * n, axis=1)


def _lane_fold(p):
    out = p[:, :LANES]
    for t in range(1, p.shape[1] // LANES):
        out = out + p[:, t * LANES:(t + 1) * LANES]
    return out


def _first_weights(s, m_ref, l_ref, rows=slice(None)):
    m = jnp.broadcast_to(jnp.max(s, axis=1, keepdims=True), (s.shape[0], LANES))
    p = jnp.exp2(s - _lane_tile(m, s.shape[1] // LANES))
    m_ref[rows, :] = m
    l_ref[rows, :] = _lane_fold(p)
    return p.astype(BF16)


def _next_weights(s, row_const, m_ref, l_ref, rows=slice(None)):
    m_prev = m_ref[rows, :]
    m_cur = jnp.max(s, axis=1, keepdims=True)
    if row_const is not None:
        m_cur = m_cur + row_const
    m_new = jnp.maximum(m_prev, m_cur)
    alpha = jnp.exp2(m_prev - m_new)
    shift = m_new if row_const is None else m_new - row_const
    p = jnp.exp2(s - _lane_tile(shift, s.shape[1] // LANES))
    l_ref[rows, :] = alpha * l_ref[rows, :] + _lane_fold(p)
    m_ref[rows, :] = m_new
    return alpha, p.astype(BF16)


def _softmax_first(s, v, m_ref, l_ref, acc_ref, rows=slice(None)):
    acc_ref[rows, :] = _dot(_first_weights(s, m_ref, l_ref, rows), v)


def _softmax_update(s, v, row_const, m_ref, l_ref, acc_ref, rows=slice(None)):
    alpha, p = _next_weights(s, row_const, m_ref, l_ref, rows)
    acc_ref[rows, :] = alpha * acc_ref[rows, :] + _dot(p, v)


def _skewed(stages, jobs, before_tick=None):
    state = [None] * len(jobs)
    for tick in range(len(jobs) + len(stages) - 1):
        if before_tick and tick in before_tick:
            before_tick[tick]()
        for k, stage in enumerate(stages):
            j = tick - k
            if 0 <= j < len(jobs):
                state[j] = stage(jobs[j], state[j])


def _softmax_finish(l_ref, acc_ref, rows=slice(None)):
    return acc_ref[rows, :] / jnp.sum(l_ref[rows, :], axis=1, keepdims=True)


def _softmax_once(s, v):
    m = jnp.max(s, axis=1, keepdims=True)
    p = jnp.exp2(s - m)
    return _dot(p.astype(BF16), v) / jnp.sum(_lane_fold(p), axis=1, keepdims=True)


FOX_TQ = 512
FOX_HALF = FOX_TQ // 2


def _fox_routing():
    pk = np.zeros((N_HEADS, LANES, LANES), np.float32)
    for h in range(N_HEADS):
        for t in range(_N_SPLIT):
            pk[h, ONES_LANE, t * N_HEADS + h] = 1.0
            pk[h, t * N_HEADS + h, ONES_LANE + t] = -1.0
    return pk


def _fox_kernel(w_blocks, q_ref, k_ref, v_ref, z_ref, cq_ref, ck_ref, pk_ref, wa_ref, wb_ref, wo_ref,
                o_ref, wcast_ref, kaug_ref, qaug_ref, sz_ref, m_ref, l_ref, acc_ref):
    i = pl.program_id(1)
    tq, half = FOX_TQ, FOX_HALF
    seq = k_ref.shape[0]

    step = pl.program_id(0) * pl.num_programs(1) + i
    first = 0
    for w_ref, n_blocks in zip((wa_ref, wb_ref, wo_ref), w_blocks):
        @pl.when((step >= first) & (step < first + n_blocks))
        def _(w_ref=w_ref):
            wcast_ref[...] = w_ref[...].astype(wcast_ref.dtype)
        first += n_blocks

    @pl.when(i == 0)
    def _():
        for h in range(N_HEADS):
            for blk in range(seq // tq):
                rows = slice(blk * tq, (blk + 1) * tq)
                kaug_ref[h, rows, :HEAD_DIM] = k_ref[rows, h * HEAD_DIM:(h + 1) * HEAD_DIM]
                kaug_ref[h, rows, HEAD_DIM:] = _dot(ck_ref[rows, :], pk_ref[h]).astype(BF16)

    causal = (lax.broadcasted_iota(jnp.int32, (half, half), 1)
              <= lax.broadcasted_iota(jnp.int32, (half, half), 0))
    diag = pl.multiple_of(i * tq, tq)
    diag2 = pl.multiple_of(i * tq + half, half)
    heads = tuple(range(N_HEADS))
    rows = lambda h: slice(h * tq, (h + 1) * tq)
    top = lambda h: slice(h * tq, h * tq + half)
    bot = lambda h: slice(h * tq + half, (h + 1) * tq)
    cols = lambda h: slice(h * HEAD_DIM, (h + 1) * HEAD_DIM)

    def diag_scores(h, _):
        qaug_ref[rows(h), :HEAD_DIM] = q_ref[:, cols(h)]
        qaug_ref[rows(h), HEAD_DIM:] = cq_ref[...]
        s_left = _dot_nt(qaug_ref[rows(h), :], kaug_ref[h, pl.ds(diag, half), :])
        s_right = _dot_nt(qaug_ref[bot(h), :], kaug_ref[h, pl.ds(diag2, half), :])
        return s_left, s_right

    def diag_weights(h, scores):
        s_left, s_right = scores
        p_top = _first_weights(jnp.where(causal, s_left[:half], NEG), m_ref, l_ref, top(h))
        p_bot = _first_weights(jnp.concatenate([s_left[half:], jnp.where(causal, s_right, NEG)], axis=1),
                               m_ref, l_ref, bot(h))
        return p_top, p_bot

    def diag_values(h, weights):
        p_top, p_bot = weights
        acc_ref[top(h), :] = _dot(p_top, v_ref[pl.ds(diag, half), cols(h)])
        acc_ref[bot(h), :] = _dot(p_bot, v_ref[pl.ds(diag, tq), cols(h)])
        sz_ref[:, cols(h)] = _silu(z_ref[:, cols(h)].astype(F32))

    _skewed((diag_scores, diag_weights, diag_values), heads)

    def body(j, carry):
        off = pl.multiple_of(j * tq, tq)

        def scores(h, _):
            return _dot_nt(qaug_ref[rows(h), :], kaug_ref[h, pl.ds(off, tq), :])

        def weights(h, s):
            return _next_weights(s, None, m_ref, l_ref, rows(h))

        def values(h, rescale_and_weights):
            alpha, p = rescale_and_weights
            acc_ref[rows(h), :] = alpha * acc_ref[rows(h), :] + _dot(p, v_ref[pl.ds(off, tq), cols(h)])

        _skewed((scores, weights, values), heads)
        return carry

    lax.fori_loop(0, i, body, 0)
    for h in heads:
        o_ref[:, cols(h)] = (_softmax_finish(l_ref, acc_ref, rows(h)) * sz_ref[:, cols(h)]).astype(o_ref.dtype)


def _fox(h_main, c_packed, w_a, w_b, w_o, batch, seq):
    tq = FOX_TQ
    nq = seq // tq
    blk = lambda name: _MAIN_OFF[name] // WIDTH
    pk = jnp.asarray(_fox_routing(), BF16)
    full = lambda shape: pl.BlockSpec(shape, lambda b, i: (0,) * len(shape))
    stat = pltpu.VMEM((N_HEADS * tq, LANES), F32)
    n_steps = batch * nq
    w_rows = sum(w.shape[0] for w in (w_a, w_b, w_o))
    rows_per_step = w_rows // n_steps
    assert w_rows % n_steps == 0 and all(w.shape[0] % rows_per_step == 0 for w in (w_a, w_b, w_o))
    w_blocks = tuple(w.shape[0] // rows_per_step for w in (w_a, w_b, w_o))

    def w_spec(first, n_blocks):
        return pl.BlockSpec((rows_per_step, D_MODEL),
                            lambda b, i: (jnp.clip(b * nq + i - first, 0, n_blocks - 1), 0),
                            pipeline_mode=pl.Buffered(1))

    return pl.pallas_call(
        functools.partial(_fox_kernel, w_blocks),
        grid=(batch, nq),
        in_specs=[
            pl.BlockSpec((tq, WIDTH), lambda b, i: (b * nq + i, blk("fox_q"))),
            pl.BlockSpec((seq, WIDTH), lambda b, i: (b, blk("fox_k"))),
            pl.BlockSpec((seq, WIDTH), lambda b, i: (b, blk("fox_v"))),
            pl.BlockSpec((tq, WIDTH), lambda b, i: (b * nq + i, blk("fox_z"))),
            pl.BlockSpec((tq, LANES), lambda b, i: (b * nq + i, 0)),
            pl.BlockSpec((seq, LANES), lambda b, i: (b, 0)),
            full(pk.shape),
            w_spec(0, w_blocks[0]), w_spec(w_blocks[0], w_blocks[1]),
            w_spec(w_blocks[0] + w_blocks[1], w_blocks[2]),
        ],
        out_specs=[pl.BlockSpec((tq, WIDTH), lambda b, i: (b * nq + i, 0)),
                   pl.BlockSpec((rows_per_step, D_MODEL), lambda b, i: (b * nq + i, 0))],
        out_shape=[jax.ShapeDtypeStruct((batch * seq, WIDTH), BF16),
                   jax.ShapeDtypeStruct((w_rows, D_MODEL), BF16)],
        scratch_shapes=[pltpu.VMEM((N_HEADS, seq, MXU_DEPTH), BF16),
                        pltpu.VMEM((N_HEADS * tq, MXU_DEPTH), BF16),
                        pltpu.VMEM((tq, WIDTH), F32),
                        stat, stat, stat],
        compiler_params=pltpu.CompilerParams(
            dimension_semantics=("parallel", "arbitrary"), vmem_limit_bytes=VMEM_LIMIT),
        name="fox",
    )(h_main, h_main, h_main, h_main, c_packed, c_packed, pk, w_a, w_b, w_o)


TQ = 256
TK = TQ
WIN_BLOCKS = (WINDOW + TQ) // TK
SEL_NEAR_BLOCKS = 2
FAR_TK = 2 * TK
N_SEL_ROWS = 32
MASK_BUCKET = REL_BUCKETS
GROUP_ROWS = HEADS_PER_GROUP * TQ
T_DIAG, T_PREV, T_WIN2, T_NONE = range(4)
N_GATHERED_TILES = 3


def _bias_lookup(tab_ref, h, idx):
    row = jnp.broadcast_to(tab_ref[h:h + 1, :], idx.shape)
    return jnp.take_along_axis(row, idx, axis=1, mode="promise_in_bounds")


def _nsa_kernel(q_ref, ks_ref, vs_ref, kw_ref, vw_ref, z_ref, kc_ref, vc_ref, cidx_ref, tidx_ref,
                tab_ref, g_ref, et_ref, ov_ref, o_ref,
                ksaug_ref, qs_ref, qaug_ref, m_ref, l_ref, acc_ref, oc_ref, ow_ref, gs_ref, t_ref, cb_ref):
    i = pl.program_id(1)
    t0 = i * TQ

    @pl.when((pl.program_id(0) == 0) & (i == 0))
    def _():
        for h in range(N_HEADS):
            for d in range(N_GATHERED_TILES):
                for half in range(TK // LANES):
                    cs = slice(half * LANES, (half + 1) * LANES)
                    t_ref[h, d, :, cs] = _bias_lookup(tab_ref, h, tidx_ref[d, :, cs])
            for blk in range(cidx_ref.shape[0] // TQ):
                rows = slice(blk * TQ, (blk + 1) * TQ)
                cb_ref[h, rows, :] = _bias_lookup(tab_ref, h, cidx_ref[rows, :])
            t_ref[h, T_NONE] = jnp.full((TQ, TK), NEG, F32)

    @pl.when(i == 0)
    def _():
        for g in range(KV_GROUPS):
            ksaug_ref[g, :, :HEAD_DIM] = ks_ref[:, g * HEAD_DIM:(g + 1) * HEAD_DIM]
            ksaug_ref[g, :, HEAD_DIM:] = et_ref[...]

    cmaskf = (lax.broadcasted_iota(jnp.int32, (TQ, LANES), 1) * CMP_STRIDE + (CMP_LEN - 1)
              <= t0 + lax.broadcasted_iota(jnp.int32, (TQ, LANES), 0)).astype(F32)
    j_t = lax.broadcasted_iota(jnp.int32, (N_SEL_ROWS, TQ), 0)
    t_t = t0 + lax.broadcasted_iota(jnp.int32, (N_SEL_ROWS, TQ), 1)
    cur_t = t_t // SEL_LEN
    forced_t = (j_t == 0) | (j_t == cur_t) | (j_t == cur_t - 1)
    valid_t = j_t * SEL_LEN <= t_t

    def near_span(n_blocks):
        first = jnp.maximum(i - (n_blocks - 1), 0)
        return first, pl.ds(pl.multiple_of(first * TK, TK), n_blocks * TK)

    win_first, win_keys = near_span(WIN_BLOCKS)
    sel_first, sel_keys = near_span(SEL_NEAR_BLOCKS)

    def near_bias(heads, first, n_blocks):
        def tile(kk):
            dist = i - (first + kk)
            return jnp.where(dist == 0, T_DIAG, jnp.where(dist == 1, T_PREV,
                             jnp.where(dist == 2, T_WIN2, T_NONE)))
        return jnp.concatenate(
            [jnp.concatenate([t_ref[h, tile(kk)] for kk in range(n_blocks)], axis=1) for h in heads],
            axis=0)

    groups = tuple((g, g * HEAD_DIM, tuple(range(g * HEADS_PER_GROUP, (g + 1) * HEADS_PER_GROUP)),
                    slice(g * GROUP_ROWS, (g + 1) * GROUP_ROWS)) for g in range(KV_GROUPS))
    hrows = lambda h: slice(h * TQ, (h + 1) * TQ)
    for h in range(N_HEADS):
        q = q_ref[:, h * HEAD_DIM:(h + 1) * HEAD_DIM]
        qs_ref[hrows(h), :] = q
        qaug_ref[hrows(h), :HEAD_DIM] = q

    chain = {}

    def compressed_branch():
        cbias = jnp.concatenate([cb_ref[h, pl.ds(pl.multiple_of(t0, TQ), TQ), :] for h in range(N_HEADS)],
                                axis=0)
        sc = jnp.concatenate([_dot_nt(qs_ref[grows, :], kc_ref[0, :, glo:glo + HEAD_DIM])
                              for g, glo, heads, grows in groups], axis=0) + cbias
        e = jnp.exp2(sc - jnp.max(sc, axis=1, keepdims=True))
        p = e / jnp.sum(e, axis=1, keepdims=True) * jnp.concatenate([cmaskf] * N_HEADS, axis=0)
        imp_parts = []
        for g, glo, heads, grows in groups:
            oc_ref[grows, :] = _dot(p[grows].astype(BF16), vc_ref[0, :, glo:glo + HEAD_DIM])
            psum = p[hrows(heads[0])]
            for h in heads[1:]:
                psum = psum + p[hrows(h)]
            p_hi = psum.astype(BF16)
            p_lo = (psum - p_hi.astype(F32)).astype(BF16)
            imp_parts.append((_dot_nt(ov_ref[...], p_hi) + _dot_nt(ov_ref[...], p_lo))[:N_SEL_ROWS])
        chain["importance"] = jnp.concatenate(imp_parts, axis=1)

    def block_selection():
        both = lambda a: jnp.concatenate([a] * KV_GROUPS, axis=1)
        j_b = both(j_t)
        x = jnp.where(both(valid_t), jnp.where(both(forced_t), -NEG, chain["importance"]), NEG)
        cnt = jnp.zeros(x.shape, F32)
        for jp in range(N_SEL_ROWS):
            row = x[jp:jp + 1, :]
            beats = (row > x) | ((row == x) & (j_b > jp))
            cnt = cnt + jnp.where(beats, 1.0, 0.0)
        unsel_t = jnp.where(cnt < SEL_TOPK, 0.0, 1.0)
        unsel_t = jnp.concatenate([unsel_t, jnp.zeros((LANES - N_SEL_ROWS, x.shape[1]), F32)], axis=0)
        unsel = unsel_t.T.astype(BF16)
        for h in range(N_HEADS):
            g = h // HEADS_PER_GROUP
            qaug_ref[hrows(h), HEAD_DIM:] = unsel[g * TQ:(g + 1) * TQ]

    def near_scores(job, _):
        branch, (g, glo, heads, grows) = job
        if branch == "window":
            return (_dot_nt(qs_ref[grows, :], kw_ref[win_keys, glo:glo + HEAD_DIM])
                    + near_bias(heads, win_first, WIN_BLOCKS))
        return (_dot_nt(qaug_ref[grows, :], ksaug_ref[g, sel_keys, :])
                + near_bias(heads, sel_first, SEL_NEAR_BLOCKS))

    def near_weights(job, s):
        branch, (g, glo, heads, grows) = job
        if branch == "window":
            p = jnp.exp2(s - jnp.max(s, axis=1, keepdims=True))
            return p.astype(BF16), _lane_fold(p)
        return _first_weights(s, m_ref, l_ref, grows)

    def near_values(job, weights):
        branch, (g, glo, heads, grows) = job
        if branch == "selected":
            acc_ref[grows, :] = _dot(weights, vs_ref[sel_keys, glo:glo + HEAD_DIM])
            return
        p, l = weights
        o_win = _dot(p, vw_ref[win_keys, glo:glo + HEAD_DIM]) / jnp.sum(l, axis=1, keepdims=True)
        for n, h in enumerate(heads):
            gl = GATE_LANE0 + h * N_BRANCHES
            sz = _silu(z_ref[:, h * HEAD_DIM:(h + 1) * HEAD_DIM].astype(F32))
            gs_ref[hrows(h), :] = g_ref[:, gl + 1:gl + 2] * sz
            ow_ref[hrows(h), :] = (g_ref[:, gl:gl + 1] * oc_ref[hrows(h), :]
                                   + g_ref[:, gl + 2:gl + 3] * o_win[n * TQ:(n + 1) * TQ]) * sz

    _skewed((near_scores, near_weights, near_values),
            tuple((branch, group) for branch in ("window", "selected") for group in groups),
            before_tick={1: compressed_branch, 2: block_selection})

    def sel_far(off, width):
        def scores(group, _):
            g, glo, heads, grows = group
            return _dot_nt(qaug_ref[grows, :], ksaug_ref[g, pl.ds(off, width), :])

        def weights(group, s):
            g, glo, heads, grows = group
            return _next_weights(s, None, m_ref, l_ref, grows)

        def values(group, rescale_and_weights):
            g, glo, heads, grows = group
            alpha, p = rescale_and_weights
            acc_ref[grows, :] = alpha * acc_ref[grows, :] + _dot(p, vs_ref[pl.ds(off, width), glo:glo + HEAD_DIM])

        _skewed((scores, weights, values), groups)

    def sel_far_pair(j, carry):
        sel_far(pl.multiple_of(j * FAR_TK, FAR_TK), FAR_TK)
        return carry

    lax.fori_loop(0, sel_first // 2, sel_far_pair, 0)

    @pl.when(sel_first % 2 == 1)
    def _():
        sel_far(pl.multiple_of((sel_first - 1) * TK, TK), TK)

    for h in range(N_HEADS):
        o_ref[:, h * HEAD_DIM:(h + 1) * HEAD_DIM] = (
            ow_ref[hrows(h), :] + gs_ref[hrows(h), :] * _softmax_finish(l_ref, acc_ref, hrows(h))
        ).astype(o_ref.dtype)


def _nsa(h_main, k_cmp, v_cmp, cmp_idx, tile_idx, table, gates, e_t, ov_t, batch, seq):
    nq = seq // TQ
    wblk = lambda name: _MAIN_OFF[name] // WIDTH
    kvblk = lambda name: _MAIN_OFF[name] // KV_WIDTH
    full = lambda shape: pl.BlockSpec(shape, lambda b, i: (0,) * len(shape),
                                      pipeline_mode=pl.Buffered(1))
    kv_spec = lambda name: pl.BlockSpec((seq, KV_WIDTH), lambda b, i: (b, kvblk(name)))
    n_chunks = k_cmp.shape[1]
    stat = pltpu.VMEM((N_HEADS * TQ, LANES), F32)
    return pl.pallas_call(
        _nsa_kernel,
        grid=(batch, nq),
        in_specs=[
            pl.BlockSpec((TQ, WIDTH), lambda b, i: (b * nq + i, wblk("nsa_q"))),
            kv_spec("nsa_k_sel"), kv_spec("nsa_v_sel"), kv_spec("nsa_k_win"), kv_spec("nsa_v_win"),
            pl.BlockSpec((TQ, WIDTH), lambda b, i: (b * nq + i, wblk("nsa_z"))),
            pl.BlockSpec((1, n_chunks, KV_WIDTH), lambda b, i: (b, 0, 0)),
            pl.BlockSpec((1, n_chunks, KV_WIDTH), lambda b, i: (b, 0, 0)),
            full(cmp_idx.shape), full(tile_idx.shape), full(table.shape),
            pl.BlockSpec((TQ, LANES), lambda b, i: (b * nq + i, 0)),
            full(e_t.shape), full(ov_t.shape),
        ],
        out_specs=pl.BlockSpec((TQ, WIDTH), lambda b, i: (b * nq + i, 0)),
        out_shape=jax.ShapeDtypeStruct((batch * seq, WIDTH), BF16),
        scratch_shapes=[pltpu.VMEM((KV_GROUPS, seq, MXU_DEPTH), BF16),
                        pltpu.VMEM((N_HEADS * TQ, HEAD_DIM), BF16),
                        pltpu.VMEM((N_HEADS * TQ, MXU_DEPTH), BF16),
                        stat, stat, stat,
                        stat, stat, stat,
                        pltpu.VMEM((N_HEADS, T_NONE + 1, TQ, TK), F32),
                        pltpu.VMEM((N_HEADS, seq, LANES), F32)],
        compiler_params=pltpu.CompilerParams(
            dimension_semantics=("arbitrary", "arbitrary"), vmem_limit_bytes=VMEM_LIMIT),
        name="nsa",
    )(h_main, h_main, h_main, h_main, h_main, h_main, k_cmp, v_cmp, cmp_idx, tile_idx, table, gates,
      e_t, ov_t)


OUT_TM = 512
OUT_TN = 512
OUT_ROW_GROUPS = 2


def _out_kernel(alpha, ua_ref, ub_ref, ga_ref, gb_ref, x_ref, wa_ref, wb_ref, wo_ref, lg_ref, lb_ref,
                o_ref, merged_ref):
    d_model = o_ref.shape[1]
    chunks = tuple(slice(c * OUT_TN, (c + 1) * OUT_TN) for c in range(d_model // OUT_TN))
    half = o_ref.shape[0] // OUT_ROW_GROUPS

    def merge(rows, _):
        for cs in chunks:
            ya = _dot(ua_ref[rows, :], wa_ref[:, cs])
            yb = _dot(ub_ref[rows, :], wb_ref[:, cs])
            merged_ref[rows, cs] = (_sigmoid(ga_ref[rows, cs].astype(F32)) * ya
                                    + _sigmoid(gb_ref[rows, cs].astype(F32)) * yb).astype(merged_ref.dtype)

    def residual(rows, _):
        total = jnp.zeros((half, LANES), F32)
        for cs in chunks:
            y = alpha * x_ref[rows, cs] + _dot(merged_ref[rows, :], wo_ref[:, cs])
            o_ref[rows, cs] = y
            total = total + _lane_fold(y)
        return total

    def layer_norm(rows, total):
        mu = jnp.sum(total, axis=1, keepdims=True) * (1.0 / d_model)
        sq = jnp.zeros((half, LANES), F32)
        for cs in chunks:
            d = o_ref[rows, cs] - mu
            sq = sq + _lane_fold(d * d)
        rstd = lax.rsqrt(jnp.sum(sq, axis=1, keepdims=True) * (1.0 / d_model) + LN_EPS)
        for cs in chunks:
            o_ref[rows, cs] = (o_ref[rows, cs] - mu) * rstd * lg_ref[:, cs] + lb_ref[:, cs]

    _skewed((merge, residual, layer_norm),
            tuple(slice(r * half, (r + 1) * half) for r in range(OUT_ROW_GROUPS)))


def _out(u_a, u_b, h_main, x2, w_cat, ln_g, ln_b, alpha, tm=OUT_TM):
    m = x2.shape[0]
    mblk = lambda name: _MAIN_OFF[name] // D_MODEL
    const = lambda shape: pl.BlockSpec(shape, lambda i: (0, 0), pipeline_mode=pl.Buffered(1))
    assert w_cat.shape == (2 * WIDTH + D_MODEL, D_MODEL) and D_MODEL == 2 * WIDTH
    w_part = lambda rows, index: pl.BlockSpec((rows, D_MODEL), lambda i: (index, 0),
                                              pipeline_mode=pl.Buffered(1))
    return pl.pallas_call(
        functools.partial(_out_kernel, alpha),
        grid=(m // tm,),
        in_specs=[
            pl.BlockSpec((tm, WIDTH), lambda i: (i, 0)),
            pl.BlockSpec((tm, WIDTH), lambda i: (i, 0)),
            pl.BlockSpec((tm, D_MODEL), lambda i: (i, mblk("merge_a"))),
            pl.BlockSpec((tm, D_MODEL), lambda i: (i, mblk("merge_b"))),
            pl.BlockSpec((tm, D_MODEL), lambda i: (i, 0)),
            w_part(WIDTH, 0), w_part(WIDTH, 1), w_part(D_MODEL, 1),
            const(ln_g.shape), const(ln_b.shape),
        ],
        out_specs=pl.BlockSpec((tm, D_MODEL), lambda i: (i, 0)),
        out_shape=jax.ShapeDtypeStruct((m, D_MODEL), F32),
        scratch_shapes=[pltpu.VMEM((tm, D_MODEL), BF16)],
        compiler_params=pltpu.CompilerParams(
            dimension_semantics=("parallel",), vmem_limit_bytes=VMEM_LIMIT),
        name="out",
    )(u_a, u_b, h_main, h_main, x2, w_cat, w_cat, w_cat, ln_g, ln_b)


def _bucket_np(dist):
    n = np.maximum(dist, 0)
    exact = REL_BUCKETS // 2
    large = exact + (np.log(np.maximum(n, 1).astype(np.float32) / exact)
                     / math.log(REL_MAX_DIST / exact) * (REL_BUCKETS - exact)).astype(np.int32)
    return np.where(n < exact, n, np.minimum(large, REL_BUCKETS - 1)).astype(np.int32)


@functools.lru_cache(maxsize=None)
def _static_tables(seq):
    r = np.arange(TQ)[:, None]
    c = np.arange(TK)[None, :]
    tile_idx = np.stack([_bucket_np(r - c), _bucket_np(TQ + r - c), _bucket_np(2 * TQ + r - c)])
    tile_ok = np.stack([c <= r, np.ones((TQ, TK), bool), (2 * TQ + r - c) < WINDOW])
    tile_idx = np.where(tile_ok, tile_idx, MASK_BUCKET).astype(np.int32)
    t = np.arange(seq)[:, None]
    cblk = np.arange(LANES)[None, :]
    blk_end = cblk * CMP_STRIDE + CMP_LEN - 1
    n_cmp = (seq - CMP_LEN) // CMP_STRIDE + 1
    cmp_idx = np.where((blk_end <= t) & (cblk < n_cmp), _bucket_np(t - blk_end), MASK_BUCKET).astype(np.int32)
    e_t = np.where((np.arange(seq)[:, None] // SEL_LEN) == np.arange(LANES)[None, :], NEG, 0.0)
    cs = (np.arange(LANES) * CMP_STRIDE)[None, :]
    ss = (np.arange(LANES) * SEL_LEN)[:, None]
    ov_t = ((cs < ss + SEL_LEN) & (cs + CMP_LEN > ss)
            & (np.arange(LANES)[None, :] < n_cmp) & (np.arange(LANES)[:, None] < seq // SEL_LEN))
    return tile_idx, cmp_idx, e_t.astype(np.float32), ov_t.astype(np.float32)


def _layer(x, w_in, b_f, cmp_pos_k, cmp_pos_v, cmp_wk1, cmp_wk2, cmp_wv1, cmp_wv2,
           w_a, w_b, w_o, ln_g, ln_b, rel_bias, alpha):
    batch, seq, d_model = x.shape
    assert d_model == D_MODEL and seq % FOX_TQ == 0 and seq >= WIN_BLOCKS * TK and REL_MAX_DIST <= TQ
    assert seq // SEL_LEN == N_SEL_ROWS and seq // CMP_STRIDE == LANES
    x2 = x.reshape(batch * seq, d_model)

    w_t = jnp.swapaxes(w_in, 0, 1)
    trows = lambda name: w_t[_REF_OFF[name][0]:_REF_OFF[name][0] + _REF_OFF[name][1]]
    n_small = N_HEADS + N_HEADS * N_BRANCHES
    w_small_t = jnp.concatenate(
        [trows("fox_f"), trows("nsa_gate"), jnp.zeros((LANES - n_small, d_model), F32)], axis=0)
    bf_row = jnp.concatenate([b_f.astype(F32), jnp.zeros((LANES - N_HEADS,), F32)]).reshape(1, LANES)

    h_main, h_small = _proj(x2, w_t, w_small_t)
    c_col, gates = _gates(h_small, bf_row, batch, seq)
    u_a, w_cat = _fox(h_main, c_col, w_a, w_b, w_o, batch, seq)

    k_cmp, v_cmp = _compress(h_main, cmp_pos_k, cmp_pos_v, cmp_wk1, cmp_wk2, cmp_wv1, cmp_wv2, batch, seq)

    tile_idx, cmp_idx, e_t, ov_t = _static_tables(seq)
    bias = rel_bias.T.astype(F32)
    table = jnp.concatenate(
        [(bias - bias[:, REL_BUCKETS - 1:]) * LOG2E, jnp.full((N_HEADS, 1), NEG, F32),
         jnp.zeros((N_HEADS, LANES - REL_BUCKETS - 1), F32)], axis=1)
    u_b = _nsa(h_main, k_cmp, v_cmp, jnp.asarray(cmp_idx), jnp.asarray(tile_idx), table, gates,
               jnp.asarray(e_t, BF16), jnp.asarray(ov_t, BF16), batch, seq)

    out = _out(u_a, u_b, h_main, x2, w_cat, ln_g.reshape(1, d_model), ln_b.reshape(1, d_model), alpha)
    return out.reshape(batch, seq, d_model)


def kernel(x, w_in, b_f, cmp_pos_k, cmp_pos_v, cmp_wk1, cmp_wk2, cmp_wv1, cmp_wv2,
           w_a, w_b, w_o, ln_g, ln_b, rel_bias):
    depth = w_in.shape[0]
    alpha = (2 * depth) ** 0.25
    for layer in range(depth):
        x = _layer(x, w_in[layer], b_f[layer], cmp_pos_k[layer], cmp_pos_v[layer], cmp_wk1[layer],
                   cmp_wk2[layer], cmp_wv1[layer], cmp_wv2[layer], w_a[layer], w_b[layer], w_o[layer],
                   ln_g[layer], ln_b[layer], rel_bias, alpha)
    return x
```

```python
import functools
import math

import jax
import jax.numpy as jnp
import numpy as np
from jax import lax
from jax.experimental import pallas as pl
from jax.experimental.pallas import tpu as pltpu

F32 = jnp.float32
BF16 = jnp.bfloat16

D_MODEL = 2048
HEAD_DIM = 128
N_HEADS = 8
WIDTH = N_HEADS * HEAD_DIM
KV_GROUPS = 2
HEADS_PER_GROUP = N_HEADS // KV_GROUPS
KV_WIDTH = KV_GROUPS * HEAD_DIM
N_BRANCHES = 3
CMP_LEN = 32
CMP_STRIDE = 16
CMP_HIDDEN = 256
SEL_LEN = 64
SEL_TOPK = 8
WINDOW = 512
REL_BUCKETS = 32
REL_MAX_DIST = 128
LN_EPS = 1e-5
NEG = -1e30
LOG2E = math.log2(math.e)
Q_PRESCALE = HEAD_DIM ** -0.5 * LOG2E

LANES = 128
SUBLANES = 8
MXU_DEPTH = 256
VMEM_LIMIT = 60 * 1024 * 1024

_REF_LAYOUT = (
    ("fox_q", WIDTH), ("fox_k", WIDTH), ("fox_v", WIDTH), ("fox_f", N_HEADS), ("fox_z", WIDTH),
    ("nsa_q", WIDTH), ("nsa_k_cmp", KV_WIDTH), ("nsa_v_cmp", KV_WIDTH), ("nsa_k_sel", KV_WIDTH),
    ("nsa_v_sel", KV_WIDTH), ("nsa_k_win", KV_WIDTH), ("nsa_v_win", KV_WIDTH),
    ("nsa_gate", N_HEADS * N_BRANCHES), ("nsa_z", WIDTH), ("merge_a", D_MODEL), ("merge_b", D_MODEL),
)
_REF_OFF = {}
_o = 0
for _n, _w in _REF_LAYOUT:
    _REF_OFF[_n] = (_o, _w)
    _o += _w

_MAIN_ORDER = ("fox_q", "fox_k", "fox_v", "fox_z", "nsa_q", "nsa_z", "merge_a", "merge_b",
               "nsa_k_cmp", "nsa_v_cmp", "nsa_k_sel", "nsa_v_sel", "nsa_k_win", "nsa_v_win")
_QUERY_COLS = ("fox_q", "nsa_q")
_MAIN_OFF = {}
_o = 0
for _n in _MAIN_ORDER:
    _MAIN_OFF[_n] = _o
    _o += _REF_OFF[_n][1]
MAIN_COLS = _o
GATE_LANE0 = N_HEADS


def _dot(a, b):
    return jnp.dot(a, b, preferred_element_type=F32)


def _dot_nt(a, b):
    return lax.dot_general(a, b, (((1,), (1,)), ((), ())), preferred_element_type=F32)


def _sigmoid(x):
    return 0.5 + 0.5 * jnp.tanh(0.5 * x)


def _silu(x):
    return x * _sigmoid(x)


PROJ_TN = 512


def _proj_tiles():
    rows, is_query = [], []
    for name in _MAIN_ORDER:
        off, width = _REF_OFF[name]
        start = _MAIN_OFF[name]
        for c in range(start, start + width):
            if c % PROJ_TN == 0:
                assert (off + c - start) % SUBLANES == 0
                rows.append((off + c - start) // SUBLANES)
                is_query.append(int(name in _QUERY_COLS))
    return np.asarray(rows, np.int32), np.asarray(is_query, np.int32)


PROJ_ROWS = 4096
PROJ_RB = 512


def _proj_kernel(rows_ref, isq_ref, x_hbm, wt_ref, wst_ref, o_ref, os_ref,
                 xb_ref, stage_ref, wb_ref, sem):
    i = pl.program_id(0)
    j = pl.program_id(1)
    n_blocks = PROJ_ROWS // PROJ_RB
    wb_ref[...] = wt_ref[...].astype(BF16)
    scale = jnp.where(isq_ref[j] == 1, Q_PRESCALE, 1.0)

    def x_copy(r, slot):
        row0 = pl.multiple_of(i * PROJ_ROWS + r * PROJ_RB, PROJ_RB)
        return pltpu.make_async_copy(x_hbm.at[pl.ds(row0, PROJ_RB), :], stage_ref.at[slot], sem.at[slot])

    def block(r):
        rows = slice(r * PROJ_RB, (r + 1) * PROJ_RB)
        o_ref[rows, :] = (_dot_nt(xb_ref[rows, :], wb_ref[...]) * scale).astype(o_ref.dtype)

    @pl.when(j == 0)
    def _():
        wsb = wst_ref[...].astype(BF16)
        x_copy(0, 0).start()
        for r in range(n_blocks):
            if r + 1 < n_blocks:
                x_copy(r + 1, (r + 1) % 2).start()
            x_copy(r, r % 2).wait()
            rows = slice(r * PROJ_RB, (r + 1) * PROJ_RB)
            xb_ref[rows, :] = stage_ref[r % 2].astype(BF16)
            os_ref[rows, :] = _dot_nt(xb_ref[rows, :], wsb)
            block(r)

    @pl.when(j > 0)
    def _():
        for r in range(n_blocks):
            block(r)


def _proj(x2, w_t, w_small_t):
    m, k = x2.shape
    rows, is_query = _proj_tiles()
    assert MAIN_COLS % PROJ_TN == 0 and len(rows) == MAIN_COLS // PROJ_TN and m % PROJ_ROWS == 0
    grid_spec = pltpu.PrefetchScalarGridSpec(
        num_scalar_prefetch=2,
        grid=(m // PROJ_ROWS, len(rows)),
        in_specs=[
            pl.BlockSpec(memory_space=pl.ANY),
            pl.BlockSpec((pl.Element(PROJ_TN), pl.Element(k)),
                         lambda i, j, rows, isq: (rows[j] * SUBLANES, 0)),
            pl.BlockSpec((LANES, k), lambda i, j, rows, isq: (0, 0)),
        ],
        out_specs=[
            pl.BlockSpec((PROJ_ROWS, PROJ_TN), lambda i, j, rows, isq: (i, j)),
            pl.BlockSpec((PROJ_ROWS, LANES), lambda i, j, rows, isq: (i, 0)),
        ],
        scratch_shapes=[pltpu.VMEM((PROJ_ROWS, k), BF16),
                        pltpu.VMEM((2, PROJ_RB, k), F32),
                        pltpu.VMEM((PROJ_TN, k), BF16),
                        pltpu.SemaphoreType.DMA((2,))],
    )
    return pl.pallas_call(
        _proj_kernel,
        grid_spec=grid_spec,
        out_shape=[jax.ShapeDtypeStruct((m, MAIN_COLS), BF16), jax.ShapeDtypeStruct((m, LANES), F32)],
        compiler_params=pltpu.CompilerParams(
            dimension_semantics=("parallel", "arbitrary"), vmem_limit_bytes=VMEM_LIMIT),
        name="proj",
    )(jnp.asarray(rows), jnp.asarray(is_query), x2, w_t, w_small_t)


_CUM_CHUNK = 256
_N_SPLIT = 3
ONES_LANE = _N_SPLIT * N_HEADS


def _split3(x):
    hi = x.astype(BF16)
    r1 = x - hi.astype(F32)
    mid = r1.astype(BF16)
    lo = (r1 - mid.astype(F32)).astype(BF16)
    return hi, mid, lo


def _gate_kernel(hs_ref, bf_ref, c_ref, g_ref):
    hs = hs_ref[...]
    g_ref[...] = _sigmoid(hs)
    z = hs + bf_ref[...]
    logf = jnp.minimum(z, 0.0) - jnp.log1p(jnp.exp(-jnp.abs(z)))
    n = _CUM_CHUNK
    tri = (lax.broadcasted_iota(jnp.int32, (n, n), 1)
           <= lax.broadcasted_iota(jnp.int32, (n, n), 0)).astype(BF16)
    lane = lax.broadcasted_iota(jnp.int32, (n, LANES), 1)
    carry = jnp.zeros((1, LANES), F32)
    for blk in range(hs.shape[0] // n):
        hi, mid, lo = _split3(logf[blk * n:(blk + 1) * n])
        cb = _dot(tri, hi) + _dot(tri, mid) + _dot(tri, lo) + carry
        carry = cb[n - 1:n, :]
        hi, mid, lo = _split3(cb * LOG2E)
        packed = jnp.where((lane >= ONES_LANE) & (lane < ONES_LANE + _N_SPLIT), 1.0, 0.0)
        for t, term in enumerate((hi, mid, lo)):
            shifted = term.astype(F32) if t == 0 else pltpu.roll(term.astype(F32), t * N_HEADS, 1)
            packed = jnp.where((lane >= t * N_HEADS) & (lane < (t + 1) * N_HEADS), shifted, packed)
        c_ref[blk * n:(blk + 1) * n, :] = packed.astype(c_ref.dtype)


def _gates(h_small, bf_row, batch, seq):
    return pl.pallas_call(
        _gate_kernel,
        grid=(batch,),
        in_specs=[pl.BlockSpec((seq, LANES), lambda b: (b, 0)),
                  pl.BlockSpec((1, LANES), lambda b: (0, 0))],
        out_specs=[pl.BlockSpec((seq, LANES), lambda b: (b, 0)),
                   pl.BlockSpec((seq, LANES), lambda b: (b, 0))],
        out_shape=[jax.ShapeDtypeStruct(h_small.shape, BF16), jax.ShapeDtypeStruct(h_small.shape, F32)],
        compiler_params=pltpu.CompilerParams(dimension_semantics=("parallel",)),
        name="gates",
    )(h_small, bf_row)


_HALF = CMP_LEN // 2


def _gelu_tanh(x):
    return 0.5 * x * (1.0 + jnp.tanh(math.sqrt(2.0 / math.pi) * (x + 0.044715 * (x * x * x))))


def _compress_kernel(raw_ref, pk_ref, pv_ref, w1k_ref, w2k_ref, w1v_ref, w2v_ref, kc_ref, vc_ref, raw32_ref):
    seq = raw_ref.shape[0]
    n_chunks = seq // _HALF
    step = 512
    for slab in range(raw_ref.shape[1] // HEAD_DIM):
        for blk in range(seq // step):
            raw32_ref[slab, blk * step:(blk + 1) * step, :] = (
                raw_ref[blk * step:(blk + 1) * step, slab * HEAD_DIM:(slab + 1) * HEAD_DIM].astype(F32))
    for kv, (pos_ref, w1_ref, w2_ref, out_ref) in enumerate(
            ((pk_ref, w1k_ref, w2k_ref, kc_ref), (pv_ref, w1v_ref, w2v_ref, vc_ref))):
        first = jnp.zeros((KV_GROUPS * n_chunks, CMP_HIDDEN), F32)
        second = jnp.zeros((KV_GROUPS * n_chunks, CMP_HIDDEN), F32)
        for l in range(_HALF):
            a = jnp.concatenate([raw32_ref[kv * KV_GROUPS + g, pl.ds(l, n_chunks, stride=_HALF), :]
                                 for g in range(KV_GROUPS)], axis=0)
            a1 = (a + pos_ref[l:l + 1, :]).astype(BF16)
            a2 = (a + pos_ref[_HALF + l:_HALF + l + 1, :]).astype(BF16)
            first += _dot(a1, w1_ref[l * HEAD_DIM:(l + 1) * HEAD_DIM, :].astype(BF16))
            second += _dot(a2, w1_ref[(_HALF + l) * HEAD_DIM:(_HALF + l + 1) * HEAD_DIM, :].astype(BF16))
        hid = first + jnp.concatenate(
            [pltpu.roll(second[g * n_chunks:(g + 1) * n_chunks], n_chunks - 1, 0) for g in range(KV_GROUPS)],
            axis=0)
        out = _dot(_gelu_tanh(hid).astype(BF16), w2_ref[...].astype(BF16))
        for g in range(KV_GROUPS):
            out_ref[0, :, g * HEAD_DIM:(g + 1) * HEAD_DIM] = (
                out[g * n_chunks:(g + 1) * n_chunks].astype(out_ref.dtype))


def _compress(h_main, pos_k, pos_v, w1k, w2k, w1v, w2v, batch, seq):
    n_chunks = seq // _HALF
    raw_cols = 2 * KV_WIDTH
    assert _MAIN_OFF["nsa_v_cmp"] == _MAIN_OFF["nsa_k_cmp"] + KV_WIDTH
    raw_blk = _MAIN_OFF["nsa_k_cmp"] // raw_cols
    full = lambda shape: pl.BlockSpec(shape, lambda b: (0,) * len(shape), pipeline_mode=pl.Buffered(1))
    return pl.pallas_call(
        _compress_kernel,
        grid=(batch,),
        in_specs=[pl.BlockSpec((seq, raw_cols), lambda b: (b, raw_blk)),
                  full(pos_k.shape), full(pos_v.shape),
                  full(w1k.shape), full(w2k.shape), full(w1v.shape), full(w2v.shape)],
        out_specs=[pl.BlockSpec((1, n_chunks, KV_WIDTH), lambda b: (b, 0, 0)),
                   pl.BlockSpec((1, n_chunks, KV_WIDTH), lambda b: (b, 0, 0))],
        out_shape=[jax.ShapeDtypeStruct((batch, n_chunks, KV_WIDTH), BF16),
                   jax.ShapeDtypeStruct((batch, n_chunks, KV_WIDTH), BF16)],
        scratch_shapes=[pltpu.VMEM((raw_cols // HEAD_DIM, seq, HEAD_DIM), F32)],
        compiler_params=pltpu.CompilerParams(
            dimension_semantics=("parallel",), vmem_limit_bytes=VMEM_LIMIT),
        name="compress",
    )(h_main, pos_k, pos_v, w1k, w2k, w1v, w2v)


def _lane_tile(x, n):
    return x if n == 1 else jnp.concatenate([x] * n, axis=1)


def _lane_fold(p):
    out = p[:, :LANES]
    for t in range(1, p.shape[1] // LANES):
        out = out + p[:, t * LANES:(t + 1) * LANES]
    return out


def _first_weights(s, m_ref, l_ref, rows=slice(None)):
    m = jnp.broadcast_to(jnp.max(s, axis=1, keepdims=True), (s.shape[0], LANES))
    p = jnp.exp2(s - _lane_tile(m, s.shape[1] // LANES))
    m_ref[rows, :] = m
    l_ref[rows, :] = _lane_fold(p)
    return p.astype(BF16)


def _next_weights(s, row_const, m_ref, l_ref, rows=slice(None)):
    m_prev = m_ref[rows, :]
    m_cur = jnp.max(s, axis=1, keepdims=True)
    if row_const is not None:
        m_cur = m_cur + row_const
    m_new = jnp.maximum(m_prev, m_cur)
    alpha = jnp.exp2(m_prev - m_new)
    shift = m_new if row_const is None else m_new - row_const
    p = jnp.exp2(s - _lane_tile(shift, s.shape[1] // LANES))
    l_ref[rows, :] = alpha * l_ref[rows, :] + _lane_fold(p)
    m_ref[rows, :] = m_new
    return alpha, p.astype(BF16)


def _softmax_first(s, v, m_ref, l_ref, acc_ref, rows=slice(None)):
    acc_ref[rows, :] = _dot(_first_weights(s, m_ref, l_ref, rows), v)


def _softmax_update(s, v, row_const, m_ref, l_ref, acc_ref, rows=slice(None)):
    alpha, p = _next_weights(s, row_const, m_ref, l_ref, rows)
    acc_ref[rows, :] = alpha * acc_ref[rows, :] + _dot(p, v)


def _skewed(stages, jobs, before_tick=None):
    state = [None] * len(jobs)
    for tick in range(len(jobs) + len(stages) - 1):
        if before_tick and tick in before_tick:
            before_tick[tick]()
        for k, stage in enumerate(stages):
            j = tick - k
            if 0 <= j < len(jobs):
                state[j] = stage(jobs[j], state[j])


def _softmax_finish(l_ref, acc_ref, rows=slice(None)):
    return acc_ref[rows, :] / jnp.sum(l_ref[rows, :], axis=1, keepdims=True)


def _softmax_once(s, v):
    m = jnp.max(s, axis=1, keepdims=True)
    p = jnp.exp2(s - m)
    return _dot(p.astype(BF16), v) / jnp.sum(_lane_fold(p), axis=1, keepdims=True)


FOX_TQ = 512
FOX_HALF = FOX_TQ // 2


def _fox_routing():
    pk = np.zeros((N_HEADS, LANES, LANES), np.float32)
    for h in range(N_HEADS):
        for t in range(_N_SPLIT):
            pk[h, ONES_LANE, t * N_HEADS + h] = 1.0
            pk[h, t * N_HEADS + h, ONES_LANE + t] = -1.0
    return pk


def _cast_weight_block(step, w_blocks, w_refs, wcast_ref):
    first = 0
    for w_ref, n_blocks in zip(w_refs, w_blocks):
        @pl.when((step >= first) & (step < first + n_blocks))
        def _(w_ref=w_ref):
            wcast_ref[...] = w_ref[...].astype(wcast_ref.dtype)
        first += n_blocks


def _weight_block_specs(weights, n_steps, step_of):
    w_rows = sum(w.shape[0] for w in weights)
    rows_per_step = w_rows // n_steps
    assert w_rows % n_steps == 0 and all(w.shape[0] % rows_per_step == 0 for w in weights)
    w_blocks = tuple(w.shape[0] // rows_per_step for w in weights)
    in_specs, first = [], 0
    for n_blocks in w_blocks:
        in_specs.append(pl.BlockSpec(
            (rows_per_step, D_MODEL),
            lambda b, i, first=first, n_blocks=n_blocks: (jnp.clip(step_of(b, i) - first, 0, n_blocks - 1), 0)))
        first += n_blocks
    out_spec = pl.BlockSpec((rows_per_step, D_MODEL), lambda b, i: (step_of(b, i), 0))
    return w_blocks, in_specs, out_spec, jax.ShapeDtypeStruct((w_rows, D_MODEL), BF16)


def _fox_kernel(q_ref, k_ref, v_ref, z_ref, cq_ref, ck_ref, pk_ref, o_ref,
                kaug_ref, qaug_ref, sz_ref, m_ref, l_ref, acc_ref):
    i = pl.program_id(1)
    tq, half = FOX_TQ, FOX_HALF
    seq = k_ref.shape[0]

    @pl.when(i == 0)
    def _():
        for h in range(N_HEADS):
            for blk in range(seq // tq):
                rows = slice(blk * tq, (blk + 1) * tq)
                kaug_ref[h, rows, :HEAD_DIM] = k_ref[rows, h * HEAD_DIM:(h + 1) * HEAD_DIM]
                kaug_ref[h, rows, HEAD_DIM:] = _dot(ck_ref[rows, :], pk_ref[h]).astype(BF16)

    causal = (lax.broadcasted_iota(jnp.int32, (half, half), 1)
              <= lax.broadcasted_iota(jnp.int32, (half, half), 0))
    diag = pl.multiple_of(i * tq, tq)
    diag2 = pl.multiple_of(i * tq + half, half)
    heads = tuple(range(N_HEADS))
    rows = lambda h: slice(h * tq, (h + 1) * tq)
    top = lambda h: slice(h * tq, h * tq + half)
    bot = lambda h: slice(h * tq + half, (h + 1) * tq)
    cols = lambda h: slice(h * HEAD_DIM, (h + 1) * HEAD_DIM)

    def diag_scores(h, _):
        qaug_ref[rows(h), :HEAD_DIM] = q_ref[:, cols(h)]
        qaug_ref[rows(h), HEAD_DIM:] = cq_ref[...]
        s_left = _dot_nt(qaug_ref[rows(h), :], kaug_ref[h, pl.ds(diag, half), :])
        s_right = _dot_nt(qaug_ref[bot(h), :], kaug_ref[h, pl.ds(diag2, half), :])
        return s_left, s_right

    def diag_weights(h, scores):
        s_left, s_right = scores
        p_top = _first_weights(jnp.where(causal, s_left[:half], NEG), m_ref, l_ref, top(h))
        p_bot = _first_weights(jnp.concatenate([s_left[half:], jnp.where(causal, s_right, NEG)], axis=1),
                               m_ref, l_ref, bot(h))
        return p_top, p_bot

    def diag_values(h, weights):
        p_top, p_bot = weights
        acc_ref[top(h), :] = _dot(p_top, v_ref[pl.ds(diag, half), cols(h)])
        acc_ref[bot(h), :] = _dot(p_bot, v_ref[pl.ds(diag, tq), cols(h)])
        sz_ref[:, cols(h)] = _silu(z_ref[:, cols(h)].astype(F32))

    _skewed((diag_scores, diag_weights, diag_values), heads)

    def body(j, carry):
        off = pl.multiple_of(j * tq, tq)

        def scores(h, _):
            return _dot_nt(qaug_ref[rows(h), :], kaug_ref[h, pl.ds(off, tq), :])

        def weights(h, s):
            return _next_weights(s, None, m_ref, l_ref, rows(h))

        def values(h, rescale_and_weights):
            alpha, p = rescale_and_weights
            acc_ref[rows(h), :] = alpha * acc_ref[rows(h), :] + _dot(p, v_ref[pl.ds(off, tq), cols(h)])

        _skewed((scores, weights, values), heads)
        return carry

    lax.fori_loop(0, i, body, 0)
    for h in heads:
        o_ref[:, cols(h)] = (_softmax_finish(l_ref, acc_ref, rows(h)) * sz_ref[:, cols(h)]).astype(o_ref.dtype)


def _fox(h_main, c_packed, batch, seq):
    tq = FOX_TQ
    nq = seq // tq
    blk = lambda name: _MAIN_OFF[name] // WIDTH
    pk = jnp.asarray(_fox_routing(), BF16)
    full = lambda shape: pl.BlockSpec(shape, lambda b, i: (0,) * len(shape))
    stat = pltpu.VMEM((N_HEADS * tq, LANES), F32)
    return pl.pallas_call(
        _fox_kernel,
        grid=(batch, nq),
        in_specs=[
            pl.BlockSpec((tq, WIDTH), lambda b, i: (b * nq + i, blk("fox_q"))),
            pl.BlockSpec((seq, WIDTH), lambda b, i: (b, blk("fox_k"))),
            pl.BlockSpec((seq, WIDTH), lambda b, i: (b, blk("fox_v"))),
            pl.BlockSpec((tq, WIDTH), lambda b, i: (b * nq + i, blk("fox_z"))),
            pl.BlockSpec((tq, LANES), lambda b, i: (b * nq + i, 0)),
            pl.BlockSpec((seq, LANES), lambda b, i: (b, 0)),
            full(pk.shape),
        ],
        out_specs=pl.BlockSpec((tq, WIDTH), lambda b, i: (b * nq + i, 0)),
        out_shape=jax.ShapeDtypeStruct((batch * seq, WIDTH), BF16),
        scratch_shapes=[pltpu.VMEM((N_HEADS, seq, MXU_DEPTH), BF16),
                        pltpu.VMEM((N_HEADS * tq, MXU_DEPTH), BF16),
                        pltpu.VMEM((tq, WIDTH), F32),
                        stat, stat, stat],
        compiler_params=pltpu.CompilerParams(
            dimension_semantics=("parallel", "arbitrary"), vmem_limit_bytes=VMEM_LIMIT),
        name="fox",
    )(h_main, h_main, h_main, h_main, c_packed, c_packed, pk)


TQ = 256
TK = TQ
WIN_BLOCKS = (WINDOW + TQ) // TK
SEL_NEAR_BLOCKS = 2
FAR_TK = 2 * TK
N_SEL_ROWS = 32
MASK_BUCKET = REL_BUCKETS
GROUP_ROWS = HEADS_PER_GROUP * TQ
T_DIAG, T_PREV, T_WIN2, T_NONE = range(4)
N_GATHERED_TILES = 3


def _bias_lookup(tab_ref, h, idx):
    row = jnp.broadcast_to(tab_ref[h:h + 1, :], idx.shape)
    return jnp.take_along_axis(row, idx, axis=1, mode="promise_in_bounds")


def _nsa_kernel(w_blocks, q_ref, ks_ref, vs_ref, kw_ref, vw_ref, z_ref, kc_ref, vc_ref, cidx_ref, tidx_ref,
                tab_ref, g_ref, et_ref, ov_ref, wa_ref, wb_ref, wo_ref, o_ref, wcast_ref,
                ksaug_ref, qs_ref, qaug_ref, m_ref, l_ref, acc_ref, oc_ref, ow_ref, gs_ref, t_ref, cb_ref):
    i = pl.program_id(1)
    t0 = i * TQ
    _cast_weight_block(pl.program_id(0) * pl.num_programs(1) + i, w_blocks, (wa_ref, wb_ref, wo_ref),
                       wcast_ref)

    @pl.when((pl.program_id(0) == 0) & (i == 0))
    def _():
        for h in range(N_HEADS):
            for d in range(N_GATHERED_TILES):
                for half in range(TK // LANES):
                    cs = slice(half * LANES, (half + 1) * LANES)
                    t_ref[h, d, :, cs] = _bias_lookup(tab_ref, h, tidx_ref[d, :, cs])
            for blk in range(cidx_ref.shape[0] // TQ):
                rows = slice(blk * TQ, (blk + 1) * TQ)
                cb_ref[h, rows, :] = _bias_lookup(tab_ref, h, cidx_ref[rows, :])
            t_ref[h, T_NONE] = jnp.full((TQ, TK), NEG, F32)

    @pl.when(i == 0)
    def _():
        for g in range(KV_GROUPS):
            ksaug_ref[g, :, :HEAD_DIM] = ks_ref[:, g * HEAD_DIM:(g + 1) * HEAD_DIM]
            ksaug_ref[g, :, HEAD_DIM:] = et_ref[...]

    cmaskf = (lax.broadcasted_iota(jnp.int32, (TQ, LANES), 1) * CMP_STRIDE + (CMP_LEN - 1)
              <= t0 + lax.broadcasted_iota(jnp.int32, (TQ, LANES), 0)).astype(F32)
    j_t = lax.broadcasted_iota(jnp.int32, (N_SEL_ROWS, TQ), 0)
    t_t = t0 + lax.broadcasted_iota(jnp.int32, (N_SEL_ROWS, TQ), 1)
    cur_t = t_t // SEL_LEN
    forced_t = (j_t == 0) | (j_t == cur_t) | (j_t == cur_t - 1)
    valid_t = j_t * SEL_LEN <= t_t

    def near_span(n_blocks):
        first = jnp.maximum(i - (n_blocks - 1), 0)
        return first, pl.ds(pl.multiple_of(first * TK, TK), n_blocks * TK)

    win_first, win_keys = near_span(WIN_BLOCKS)
    sel_first, sel_keys = near_span(SEL_NEAR_BLOCKS)

    def near_bias(heads, first, n_blocks):
        def tile(kk):
            dist = i - (first + kk)
            return jnp.where(dist == 0, T_DIAG, jnp.where(dist == 1, T_PREV,
                             jnp.where(dist == 2, T_WIN2, T_NONE)))
        return jnp.concatenate(
            [jnp.concatenate([t_ref[h, tile(kk)] for kk in range(n_blocks)], axis=1) for h in heads],
            axis=0)

    groups = tuple((g, g * HEAD_DIM, tuple(range(g * HEADS_PER_GROUP, (g + 1) * HEADS_PER_GROUP)),
                    slice(g * GROUP_ROWS, (g + 1) * GROUP_ROWS)) for g in range(KV_GROUPS))
    hrows = lambda h: slice(h * TQ, (h + 1) * TQ)
    for h in range(N_HEADS):
        q = q_ref[:, h * HEAD_DIM:(h + 1) * HEAD_DIM]
        qs_ref[hrows(h), :] = q
        qaug_ref[hrows(h), :HEAD_DIM] = q

    chain = {}

    def compressed_branch():
        cbias = jnp.concatenate([cb_ref[h, pl.ds(pl.multiple_of(t0, TQ), TQ), :] for h in range(N_HEADS)],
                                axis=0)
        sc = jnp.concatenate([_dot_nt(qs_ref[grows, :], kc_ref[0, :, glo:glo + HEAD_DIM])
                              for g, glo, heads, grows in groups], axis=0) + cbias
        e = jnp.exp2(sc - jnp.max(sc, axis=1, keepdims=True))
        p = e / jnp.sum(e, axis=1, keepdims=True) * jnp.concatenate([cmaskf] * N_HEADS, axis=0)
        imp_parts = []
        for g, glo, heads, grows in groups:
            oc_ref[grows, :] = _dot(p[grows].astype(BF16), vc_ref[0, :, glo:glo + HEAD_DIM])
            psum = p[hrows(heads[0])]
            for h in heads[1:]:
                psum = psum + p[hrows(h)]
            p_hi = psum.astype(BF16)
            p_lo = (psum - p_hi.astype(F32)).astype(BF16)
            imp_parts.append((_dot_nt(ov_ref[...], p_hi) + _dot_nt(ov_ref[...], p_lo))[:N_SEL_ROWS])
        chain["importance"] = jnp.concatenate(imp_parts, axis=1)

    def block_selection():
        both = lambda a: jnp.concatenate([a] * KV_GROUPS, axis=1)
        j_b = both(j_t)
        x = jnp.where(both(valid_t), jnp.where(both(forced_t), -NEG, chain["importance"]), NEG)
        cnt = jnp.zeros(x.shape, F32)
        for jp in range(N_SEL_ROWS):
            row = x[jp:jp + 1, :]
            beats = (row > x) | ((row == x) & (j_b > jp))
            cnt = cnt + jnp.where(beats, 1.0, 0.0)
        unsel_t = jnp.where(cnt < SEL_TOPK, 0.0, 1.0)
        unsel_t = jnp.concatenate([unsel_t, jnp.zeros((LANES - N_SEL_ROWS, x.shape[1]), F32)], axis=0)
        unsel = unsel_t.T.astype(BF16)
        for h in range(N_HEADS):
            g = h // HEADS_PER_GROUP
            qaug_ref[hrows(h), HEAD_DIM:] = unsel[g * TQ:(g + 1) * TQ]

    def near_scores(job, _):
        branch, (g, glo, heads, grows) = job
        if branch == "window":
            return (_dot_nt(qs_ref[grows, :], kw_ref[win_keys, glo:glo + HEAD_DIM])
                    + near_bias(heads, win_first, WIN_BLOCKS))
        return (_dot_nt(qaug_ref[grows, :], ksaug_ref[g, sel_keys, :])
                + near_bias(heads, sel_first, SEL_NEAR_BLOCKS))

    def near_weights(job, s):
        branch, (g, glo, heads, grows) = job
        if branch == "window":
            p = jnp.exp2(s - jnp.max(s, axis=1, keepdims=True))
            return p.astype(BF16), _lane_fold(p)
        return _first_weights(s, m_ref, l_ref, grows)

    def near_values(job, weights):
        branch, (g, glo, heads, grows) = job
        if branch == "selected":
            acc_ref[grows, :] = _dot(weights, vs_ref[sel_keys, glo:glo + HEAD_DIM])
            return
        p, l = weights
        o_win = _dot(p, vw_ref[win_keys, glo:glo + HEAD_DIM]) / jnp.sum(l, axis=1, keepdims=True)
        for n, h in enumerate(heads):
            gl = GATE_LANE0 + h * N_BRANCHES
            sz = _silu(z_ref[:, h * HEAD_DIM:(h + 1) * HEAD_DIM].astype(F32))
            gs_ref[hrows(h), :] = g_ref[:, gl + 1:gl + 2] * sz
            ow_ref[hrows(h), :] = (g_ref[:, gl:gl + 1] * oc_ref[hrows(h), :]
                                   + g_ref[:, gl + 2:gl + 3] * o_win[n * TQ:(n + 1) * TQ]) * sz

    _skewed((near_scores, near_weights, near_values),
            tuple((branch, group) for branch in ("window", "selected") for group in groups),
            before_tick={1: compressed_branch, 2: block_selection})

    def sel_far(off, width):
        def scores(group, _):
            g, glo, heads, grows = group
            return _dot_nt(qaug_ref[grows, :], ksaug_ref[g, pl.ds(off, width), :])

        def weights(group, s):
            g, glo, heads, grows = group
            return _next_weights(s, None, m_ref, l_ref, grows)

        def values(group, rescale_and_weights):
            g, glo, heads, grows = group
            alpha, p = rescale_and_weights
            acc_ref[grows, :] = alpha * acc_ref[grows, :] + _dot(p, vs_ref[pl.ds(off, width), glo:glo + HEAD_DIM])

        _skewed((scores, weights, values), groups)

    def sel_far_pair(j, carry):
        sel_far(pl.multiple_of(j * FAR_TK, FAR_TK), FAR_TK)
        return carry

    lax.fori_loop(0, sel_first // 2, sel_far_pair, 0)

    @pl.when(sel_first % 2 == 1)
    def _():
        sel_far(pl.multiple_of((sel_first - 1) * TK, TK), TK)

    for h in range(N_HEADS):
        o_ref[:, h * HEAD_DIM:(h + 1) * HEAD_DIM] = (
            ow_ref[hrows(h), :] + gs_ref[hrows(h), :] * _softmax_finish(l_ref, acc_ref, hrows(h))
        ).astype(o_ref.dtype)


def _nsa(h_main, k_cmp, v_cmp, cmp_idx, tile_idx, table, gates, e_t, ov_t, weights, batch, seq):
    nq = seq // TQ
    w_blocks, w_in_specs, w_out_spec, w_out_shape = _weight_block_specs(
        weights, batch * nq, lambda b, i: b * nq + i)
    wblk = lambda name: _MAIN_OFF[name] // WIDTH
    kvblk = lambda name: _MAIN_OFF[name] // KV_WIDTH
    full = lambda shape: pl.BlockSpec(shape, lambda b, i: (0,) * len(shape),
                                      pipeline_mode=pl.Buffered(1))
    kv_spec = lambda name: pl.BlockSpec((seq, KV_WIDTH), lambda b, i: (b, kvblk(name)))
    n_chunks = k_cmp.shape[1]
    stat = pltpu.VMEM((N_HEADS * TQ, LANES), F32)
    return pl.pallas_call(
        functools.partial(_nsa_kernel, w_blocks),
        grid=(batch, nq),
        in_specs=[
            pl.BlockSpec((TQ, WIDTH), lambda b, i: (b * nq + i, wblk("nsa_q"))),
            kv_spec("nsa_k_sel"), kv_spec("nsa_v_sel"), kv_spec("nsa_k_win"), kv_spec("nsa_v_win"),
            pl.BlockSpec((TQ, WIDTH), lambda b, i: (b * nq + i, wblk("nsa_z"))),
            pl.BlockSpec((1, n_chunks, KV_WIDTH), lambda b, i: (b, 0, 0)),
            pl.BlockSpec((1, n_chunks, KV_WIDTH), lambda b, i: (b, 0, 0)),
            full(cmp_idx.shape), full(tile_idx.shape), full(table.shape),
            pl.BlockSpec((TQ, LANES), lambda b, i: (b * nq + i, 0)),
            full(e_t.shape), full(ov_t.shape),
            *w_in_specs,
        ],
        out_specs=[pl.BlockSpec((TQ, WIDTH), lambda b, i: (b * nq + i, 0)), w_out_spec],
        out_shape=[jax.ShapeDtypeStruct((batch * seq, WIDTH), BF16), w_out_shape],
        scratch_shapes=[pltpu.VMEM((KV_GROUPS, seq, MXU_DEPTH), BF16),
                        pltpu.VMEM((N_HEADS * TQ, HEAD_DIM), BF16),
                        pltpu.VMEM((N_HEADS * TQ, MXU_DEPTH), BF16),
                        stat, stat, stat,
                        stat, stat, stat,
                        pltpu.VMEM((N_HEADS, T_NONE + 1, TQ, TK), F32),
                        pltpu.VMEM((N_HEADS, seq, LANES), F32)],
        compiler_params=pltpu.CompilerParams(
            dimension_semantics=("arbitrary", "arbitrary"), vmem_limit_bytes=VMEM_LIMIT),
        name="nsa",
    )(h_main, h_main, h_main, h_main, h_main, h_main, k_cmp, v_cmp, cmp_idx, tile_idx, table, gates,
      e_t, ov_t, *weights)


OUT_TM = 512
OUT_TN = 512
OUT_ROW_GROUPS = 2


def _out_kernel(alpha, ua_ref, ub_ref, ga_ref, gb_ref, x_ref, wa_ref, wb_ref, wo_ref, lg_ref, lb_ref,
                o_ref, merged_ref):
    d_model = o_ref.shape[1]
    chunks = tuple(slice(c * OUT_TN, (c + 1) * OUT_TN) for c in range(d_model // OUT_TN))
    half = o_ref.shape[0] // OUT_ROW_GROUPS

    def merge(rows, _):
        for cs in chunks:
            ya = _dot(ua_ref[rows, :], wa_ref[:, cs])
            yb = _dot(ub_ref[rows, :], wb_ref[:, cs])
            merged_ref[rows, cs] = (_sigmoid(ga_ref[rows, cs].astype(F32)) * ya
                                    + _sigmoid(gb_ref[rows, cs].astype(F32)) * yb).astype(merged_ref.dtype)

    def residual(rows, _):
        total = jnp.zeros((half, LANES), F32)
        for cs in chunks:
            y = alpha * x_ref[rows, cs] + _dot(merged_ref[rows, :], wo_ref[:, cs])
            o_ref[rows, cs] = y
            total = total + _lane_fold(y)
        return total

    def layer_norm(rows, total):
        mu = jnp.sum(total, axis=1, keepdims=True) * (1.0 / d_model)
        sq = jnp.zeros((half, LANES), F32)
        for cs in chunks:
            d = o_ref[rows, cs] - mu
            sq = sq + _lane_fold(d * d)
        rstd = lax.rsqrt(jnp.sum(sq, axis=1, keepdims=True) * (1.0 / d_model) + LN_EPS)
        for cs in chunks:
            o_ref[rows, cs] = (o_ref[rows, cs] - mu) * rstd * lg_ref[:, cs] + lb_ref[:, cs]

    _skewed((merge, residual, layer_norm),
            tuple(slice(r * half, (r + 1) * half) for r in range(OUT_ROW_GROUPS)))


def _out(u_a, u_b, h_main, x2, w_cat, ln_g, ln_b, alpha, tm=OUT_TM):
    m = x2.shape[0]
    mblk = lambda name: _MAIN_OFF[name] // D_MODEL
    const = lambda shape: pl.BlockSpec(shape, lambda i: (0, 0), pipeline_mode=pl.Buffered(1))
    assert w_cat.shape == (2 * WIDTH + D_MODEL, D_MODEL) and D_MODEL == 2 * WIDTH
    w_part = lambda rows, index: pl.BlockSpec((rows, D_MODEL), lambda i: (index, 0),
                                              pipeline_mode=pl.Buffered(1))
    return pl.pallas_call(
        functools.partial(_out_kernel, alpha),
        grid=(m // tm,),
        in_specs=[
            pl.BlockSpec((tm, WIDTH), lambda i: (i, 0)),
            pl.BlockSpec((tm, WIDTH), lambda i: (i, 0)),
            pl.BlockSpec((tm, D_MODEL), lambda i: (i, mblk("merge_a"))),
            pl.BlockSpec((tm, D_MODEL), lambda i: (i, mblk("merge_b"))),
            pl.BlockSpec((tm, D_MODEL), lambda i: (i, 0)),
            w_part(WIDTH, 0), w_part(WIDTH, 1), w_part(D_MODEL, 1),
            const(ln_g.shape), const(ln_b.shape),
        ],
        out_specs=pl.BlockSpec((tm, D_MODEL), lambda i: (i, 0)),
        out_shape=jax.ShapeDtypeStruct((m, D_MODEL), F32),
        scratch_shapes=[pltpu.VMEM((tm, D_MODEL), BF16)],
        compiler_params=pltpu.CompilerParams(
            dimension_semantics=("parallel",), vmem_limit_bytes=VMEM_LIMIT),
        name="out",
    )(u_a, u_b, h_main, h_main, x2, w_cat, w_cat, w_cat, ln_g, ln_b)


def _bucket_np(dist):
    n = np.maximum(dist, 0)
    exact = REL_BUCKETS // 2
    large = exact + (np.log(np.maximum(n, 1).astype(np.float32) / exact)
                     / math.log(REL_MAX_DIST / exact) * (REL_BUCKETS - exact)).astype(np.int32)
    return np.where(n < exact, n, np.minimum(large, REL_BUCKETS - 1)).astype(np.int32)


@functools.lru_cache(maxsize=None)
def _static_tables(seq):
    r = np.arange(TQ)[:, None]
    c = np.arange(TK)[None, :]
    tile_idx = np.stack([_bucket_np(r - c), _bucket_np(TQ + r - c), _bucket_np(2 * TQ + r - c)])
    tile_ok = np.stack([c <= r, np.ones((TQ, TK), bool), (2 * TQ + r - c) < WINDOW])
    tile_idx = np.where(tile_ok, tile_idx, MASK_BUCKET).astype(np.int32)
    t = np.arange(seq)[:, None]
    cblk = np.arange(LANES)[None, :]
    blk_end = cblk * CMP_STRIDE + CMP_LEN - 1
    n_cmp = (seq - CMP_LEN) // CMP_STRIDE + 1
    cmp_idx = np.where((blk_end <= t) & (cblk < n_cmp), _bucket_np(t - blk_end), MASK_BUCKET).astype(np.int32)
    e_t = np.where((np.arange(seq)[:, None] // SEL_LEN) == np.arange(LANES)[None, :], NEG, 0.0)
    cs = (np.arange(LANES) * CMP_STRIDE)[None, :]
    ss = (np.arange(LANES) * SEL_LEN)[:, None]
    ov_t = ((cs < ss + SEL_LEN) & (cs + CMP_LEN > ss)
            & (np.arange(LANES)[None, :] < n_cmp) & (np.arange(LANES)[:, None] < seq // SEL_LEN))
    return tile_idx, cmp_idx, e_t.astype(np.float32), ov_t.astype(np.float32)


def _layer(x, w_in, b_f, cmp_pos_k, cmp_pos_v, cmp_wk1, cmp_wk2, cmp_wv1, cmp_wv2,
           w_a, w_b, w_o, ln_g, ln_b, rel_bias, alpha):
    batch, seq, d_model = x.shape
    assert d_model == D_MODEL and seq % FOX_TQ == 0 and seq >= WIN_BLOCKS * TK and REL_MAX_DIST <= TQ
    assert seq // SEL_LEN == N_SEL_ROWS and seq // CMP_STRIDE == LANES
    x2 = x.reshape(batch * seq, d_model)

    w_t = jnp.swapaxes(w_in, 0, 1)
    trows = lambda name: w_t[_REF_OFF[name][0]:_REF_OFF[name][0] + _REF_OFF[name][1]]
    n_small = N_HEADS + N_HEADS * N_BRANCHES
    w_small_t = jnp.concatenate(
        [trows("fox_f"), trows("nsa_gate"), jnp.zeros((LANES - n_small, d_model), F32)], axis=0)
    bf_row = jnp.concatenate([b_f.astype(F32), jnp.zeros((LANES - N_HEADS,), F32)]).reshape(1, LANES)

    h_main, h_small = _proj(x2, w_t, w_small_t)
    c_col, gates = _gates(h_small, bf_row, batch, seq)
    u_a = _fox(h_main, c_col, batch, seq)

    k_cmp, v_cmp = _compress(h_main, cmp_pos_k, cmp_pos_v, cmp_wk1, cmp_wk2, cmp_wv1, cmp_wv2, batch, seq)

    tile_idx, cmp_idx, e_t, ov_t = _static_tables(seq)
    bias = rel_bias.T.astype(F32)
    table = jnp.concatenate(
        [(bias - bias[:, REL_BUCKETS - 1:]) * LOG2E, jnp.full((N_HEADS, 1), NEG, F32),
         jnp.zeros((N_HEADS, LANES - REL_BUCKETS - 1), F32)], axis=1)
    u_b, w_cat = _nsa(h_main, k_cmp, v_cmp, jnp.asarray(cmp_idx), jnp.asarray(tile_idx), table, gates,
                      jnp.asarray(e_t, BF16), jnp.asarray(ov_t, BF16), (w_a, w_b, w_o), batch, seq)

    out = _out(u_a, u_b, h_main, x2, w_cat, ln_g.reshape(1, d_model), ln_b.reshape(1, d_model), alpha)
    return out.reshape(batch, seq, d_model)


def kernel(x, w_in, b_f, cmp_pos_k, cmp_pos_v, cmp_wk1, cmp_wk2, cmp_wv1, cmp_wv2,
           w_a, w_b, w_o, ln_g, ln_b, rel_bias):
    depth = w_in.shape[0]
    alpha = (2 * depth) ** 0.25
    for layer in range(depth):
        x = _layer(x, w_in[layer], b_f[layer], cmp_pos_k[layer], cmp_pos_v[layer], cmp_wk1[layer],
                   cmp_wk2[layer], cmp_wv1[layer], cmp_wv2[layer], w_a[layer], w_b[layer], w_o[layer],
                   ln_g[layer], ln_b[layer], rel_bias, alpha)
    return x
```

```python
import functools
import math

import jax
import jax.numpy as jnp
import numpy as np
from jax import lax
from jax.experimental import pallas as pl
from jax.experimental.pallas import tpu as pltpu

F32 = jnp.float32
BF16 = jnp.bfloat16

D_MODEL = 2048
HEAD_DIM = 128
N_HEADS = 8
WIDTH = N_HEADS * HEAD_DIM
KV_GROUPS = 2
HEADS_PER_GROUP = N_HEADS // KV_GROUPS
KV_WIDTH = KV_GROUPS * HEAD_DIM
N_BRANCHES = 3
CMP_LEN = 32
CMP_STRIDE = 16
CMP_HIDDEN = 256
SEL_LEN = 64
SEL_TOPK = 8
WINDOW = 512
REL_BUCKETS = 32
REL_MAX_DIST = 128
LN_EPS = 1e-5
NEG = -1e30
LOG2E = math.log2(math.e)
Q_PRESCALE = HEAD_DIM ** -0.5 * LOG2E

LANES = 128
SUBLANES = 8
MXU_DEPTH = 256
VMEM_LIMIT = 60 * 1024 * 1024

_REF_LAYOUT = (
    ("fox_q", WIDTH), ("fox_k", WIDTH), ("fox_v", WIDTH), ("fox_f", N_HEADS), ("fox_z", WIDTH),
    ("nsa_q", WIDTH), ("nsa_k_cmp", KV_WIDTH), ("nsa_v_cmp", KV_WIDTH), ("nsa_k_sel", KV_WIDTH),
    ("nsa_v_sel", KV_WIDTH), ("nsa_k_win", KV_WIDTH), ("nsa_v_win", KV_WIDTH),
    ("nsa_gate", N_HEADS * N_BRANCHES), ("nsa_z", WIDTH), ("merge_a", D_MODEL), ("merge_b", D_MODEL),
)
_REF_OFF = {}
_o = 0
for _n, _w in _REF_LAYOUT:
    _REF_OFF[_n] = (_o, _w)
    _o += _w

_MAIN_ORDER = ("fox_q", "fox_k", "fox_v", "fox_z", "nsa_q", "nsa_z", "merge_a", "merge_b",
               "nsa_k_cmp", "nsa_v_cmp", "nsa_k_sel", "nsa_v_sel", "nsa_k_win", "nsa_v_win")
_QUERY_COLS = ("fox_q", "nsa_q")
_MAIN_OFF = {}
_o = 0
for _n in _MAIN_ORDER:
    _MAIN_OFF[_n] = _o
    _o += _REF_OFF[_n][1]
MAIN_COLS = _o
GATE_LANE0 = N_HEADS


def _dot(a, b):
    return jnp.dot(a, b, preferred_element_type=F32)


def _dot_nt(a, b):
    return lax.dot_general(a, b, (((1,), (1,)), ((), ())), preferred_element_type=F32)


def _sigmoid(x):
    return 0.5 + 0.5 * jnp.tanh(0.5 * x)


def _silu(x):
    return x * _sigmoid(x)


PROJ_TN = 512


def _proj_tiles():
    rows, is_query = [], []
    for name in _MAIN_ORDER:
        off, width = _REF_OFF[name]
        start = _MAIN_OFF[name]
        for c in range(start, start + width):
            if c % PROJ_TN == 0:
                assert (off + c - start) % SUBLANES == 0
                rows.append((off + c - start) // SUBLANES)
                is_query.append(int(name in _QUERY_COLS))
    return np.asarray(rows, np.int32), np.asarray(is_query, np.int32)


PROJ_ROWS = 4096
PROJ_RB = 512


def _proj_kernel(rows_ref, isq_ref, x_hbm, wt_ref, wst_ref, o_ref, os_ref,
                 xb_ref, stage_ref, wb_ref, sem):
    i = pl.program_id(0)
    j = pl.program_id(1)
    n_blocks = PROJ_ROWS // PROJ_RB
    wb_ref[...] = wt_ref[...].astype(BF16)
    scale = jnp.where(isq_ref[j] == 1, Q_PRESCALE, 1.0)

    def x_copy(r, slot):
        row0 = pl.multiple_of(i * PROJ_ROWS + r * PROJ_RB, PROJ_RB)
        return pltpu.make_async_copy(x_hbm.at[pl.ds(row0, PROJ_RB), :], stage_ref.at[slot], sem.at[slot])

    def block(r):
        rows = slice(r * PROJ_RB, (r + 1) * PROJ_RB)
        o_ref[rows, :] = (_dot_nt(xb_ref[rows, :], wb_ref[...]) * scale).astype(o_ref.dtype)

    @pl.when(j == 0)
    def _():
        wsb = wst_ref[...].astype(BF16)
        x_copy(0, 0).start()
        for r in range(n_blocks):
            if r + 1 < n_blocks:
                x_copy(r + 1, (r + 1) % 2).start()
            x_copy(r, r % 2).wait()
            rows = slice(r * PROJ_RB, (r + 1) * PROJ_RB)
            xb_ref[rows, :] = stage_ref[r % 2].astype(BF16)
            os_ref[rows, :] = _dot_nt(xb_ref[rows, :], wsb)
            block(r)

    @pl.when(j > 0)
    def _():
        for r in range(n_blocks):
            block(r)


def _proj(x2, w_t, w_small_t):
    m, k = x2.shape
    rows, is_query = _proj_tiles()
    assert MAIN_COLS % PROJ_TN == 0 and len(rows) == MAIN_COLS // PROJ_TN and m % PROJ_ROWS == 0
    grid_spec = pltpu.PrefetchScalarGridSpec(
        num_scalar_prefetch=2,
        grid=(m // PROJ_ROWS, len(rows)),
        in_specs=[
            pl.BlockSpec(memory_space=pl.ANY),
            pl.BlockSpec((pl.Element(PROJ_TN), pl.Element(k)),
                         lambda i, j, rows, isq: (rows[j] * SUBLANES, 0)),
            pl.BlockSpec((LANES, k), lambda i, j, rows, isq: (0, 0)),
        ],
        out_specs=[
            pl.BlockSpec((PROJ_ROWS, PROJ_TN), lambda i, j, rows, isq: (i, j)),
            pl.BlockSpec((PROJ_ROWS, LANES), lambda i, j, rows, isq: (i, 0)),
        ],
        scratch_shapes=[pltpu.VMEM((PROJ_ROWS, k), BF16),
                        pltpu.VMEM((2, PROJ_RB, k), F32),
                        pltpu.VMEM((PROJ_TN, k), BF16),
                        pltpu.SemaphoreType.DMA((2,))],
    )
    return pl.pallas_call(
        _proj_kernel,
        grid_spec=grid_spec,
        out_shape=[jax.ShapeDtypeStruct((m, MAIN_COLS), BF16), jax.ShapeDtypeStruct((m, LANES), F32)],
        compiler_params=pltpu.CompilerParams(
            dimension_semantics=("parallel", "arbitrary"), vmem_limit_bytes=VMEM_LIMIT),
        name="proj",
    )(jnp.asarray(rows), jnp.asarray(is_query), x2, w_t, w_small_t)


_CUM_CHUNK = 256
_N_SPLIT = 3
ONES_LANE = _N_SPLIT * N_HEADS


def _split3(x):
    hi = x.astype(BF16)
    r1 = x - hi.astype(F32)
    mid = r1.astype(BF16)
    lo = (r1 - mid.astype(F32)).astype(BF16)
    return hi, mid, lo


def _gate_kernel(hs_ref, bf_ref, c_ref, g_ref):
    hs = hs_ref[...]
    g_ref[...] = _sigmoid(hs)
    z = hs + bf_ref[...]
    logf = jnp.minimum(z, 0.0) - jnp.log1p(jnp.exp(-jnp.abs(z)))
    n = _CUM_CHUNK
    tri = (lax.broadcasted_iota(jnp.int32, (n, n), 1)
           <= lax.broadcasted_iota(jnp.int32, (n, n), 0)).astype(BF16)
    lane = lax.broadcasted_iota(jnp.int32, (n, LANES), 1)
    carry = jnp.zeros((1, LANES), F32)
    for blk in range(hs.shape[0] // n):
        hi, mid, lo = _split3(logf[blk * n:(blk + 1) * n])
        cb = _dot(tri, hi) + _dot(tri, mid) + _dot(tri, lo) + carry
        carry = cb[n - 1:n, :]
        hi, mid, lo = _split3(cb * LOG2E)
        packed = jnp.where((lane >= ONES_LANE) & (lane < ONES_LANE + _N_SPLIT), 1.0, 0.0)
        for t, term in enumerate((hi, mid, lo)):
            shifted = term.astype(F32) if t == 0 else pltpu.roll(term.astype(F32), t * N_HEADS, 1)
            packed = jnp.where((lane >= t * N_HEADS) & (lane < (t + 1) * N_HEADS), shifted, packed)
        c_ref[blk * n:(blk + 1) * n, :] = packed.astype(c_ref.dtype)


def _gates(h_small, bf_row, batch, seq):
    return pl.pallas_call(
        _gate_kernel,
        grid=(batch,),
        in_specs=[pl.BlockSpec((seq, LANES), lambda b: (b, 0)),
                  pl.BlockSpec((1, LANES), lambda b: (0, 0))],
        out_specs=[pl.BlockSpec((seq, LANES), lambda b: (b, 0)),
                   pl.BlockSpec((seq, LANES), lambda b: (b, 0))],
        out_shape=[jax.ShapeDtypeStruct(h_small.shape, BF16), jax.ShapeDtypeStruct(h_small.shape, F32)],
        compiler_params=pltpu.CompilerParams(dimension_semantics=("parallel",)),
        name="gates",
    )(h_small, bf_row)


_HALF = CMP_LEN // 2


def _gelu_tanh(x):
    return 0.5 * x * (1.0 + jnp.tanh(math.sqrt(2.0 / math.pi) * (x + 0.044715 * (x * x * x))))


def _compress_kernel(raw_ref, pk_ref, pv_ref, w1k_ref, w2k_ref, w1v_ref, w2v_ref, kc_ref, vc_ref, raw32_ref):
    seq = raw_ref.shape[0]
    n_chunks = seq // _HALF
    step = 512
    for slab in range(raw_ref.shape[1] // HEAD_DIM):
        for blk in range(seq // step):
            raw32_ref[slab, blk * step:(blk + 1) * step, :] = (
                raw_ref[blk * step:(blk + 1) * step, slab * HEAD_DIM:(slab + 1) * HEAD_DIM].astype(F32))
    for kv, (pos_ref, w1_ref, w2_ref, out_ref) in enumerate(
            ((pk_ref, w1k_ref, w2k_ref, kc_ref), (pv_ref, w1v_ref, w2v_ref, vc_ref))):
        first = jnp.zeros((KV_GROUPS * n_chunks, CMP_HIDDEN), F32)
        second = jnp.zeros((KV_GROUPS * n_chunks, CMP_HIDDEN), F32)
        def tokens(l, pos_row):
            a = jnp.concatenate([raw32_ref[kv * KV_GROUPS + g, pl.ds(l, n_chunks, stride=_HALF), :]
                                 for g in range(KV_GROUPS)], axis=0)
            return (a + pos_ref[pos_row:pos_row + 1, :]).astype(BF16)

        for l in range(0, _HALF, 2):
            rows = slice(l * HEAD_DIM, (l + 2) * HEAD_DIM)
            rows2 = slice((_HALF + l) * HEAD_DIM, (_HALF + l + 2) * HEAD_DIM)
            first += _dot(jnp.concatenate([tokens(l, l), tokens(l + 1, l + 1)], axis=1),
                          w1_ref[rows, :].astype(BF16))
            second += _dot(jnp.concatenate([tokens(l, _HALF + l), tokens(l + 1, _HALF + l + 1)], axis=1),
                           w1_ref[rows2, :].astype(BF16))
        hid = first + jnp.concatenate(
            [pltpu.roll(second[g * n_chunks:(g + 1) * n_chunks], n_chunks - 1, 0) for g in range(KV_GROUPS)],
            axis=0)
        out = _dot(_gelu_tanh(hid).astype(BF16), w2_ref[...].astype(BF16))
        for g in range(KV_GROUPS):
            out_ref[0, :, g * HEAD_DIM:(g + 1) * HEAD_DIM] = (
                out[g * n_chunks:(g + 1) * n_chunks].astype(out_ref.dtype))


def _compress(h_main, pos_k, pos_v, w1k, w2k, w1v, w2v, batch, seq):
    n_chunks = seq // _HALF
    raw_cols = 2 * KV_WIDTH
    assert _MAIN_OFF["nsa_v_cmp"] == _MAIN_OFF["nsa_k_cmp"] + KV_WIDTH
    raw_blk = _MAIN_OFF["nsa_k_cmp"] // raw_cols
    full = lambda shape: pl.BlockSpec(shape, lambda b: (0,) * len(shape), pipeline_mode=pl.Buffered(1))
    return pl.pallas_call(
        _compress_kernel,
        grid=(batch,),
        in_specs=[pl.BlockSpec((seq, raw_cols), lambda b: (b, raw_blk)),
                  full(pos_k.shape), full(pos_v.shape),
                  full(w1k.shape), full(w2k.shape), full(w1v.shape), full(w2v.shape)],
        out_specs=[pl.BlockSpec((1, n_chunks, KV_WIDTH), lambda b: (b, 0, 0)),
                   pl.BlockSpec((1, n_chunks, KV_WIDTH), lambda b: (b, 0, 0))],
        out_shape=[jax.ShapeDtypeStruct((batch, n_chunks, KV_WIDTH), BF16),
                   jax.ShapeDtypeStruct((batch, n_chunks, KV_WIDTH), BF16)],
        scratch_shapes=[pltpu.VMEM((raw_cols // HEAD_DIM, seq, HEAD_DIM), F32)],
        compiler_params=pltpu.CompilerParams(
            dimension_semantics=("parallel",), vmem_limit_bytes=VMEM_LIMIT),
        name="compress",
    )(h_main, pos_k, pos_v, w1k, w2k, w1v, w2v)


def _lane_tile(x, n):
    return x if n == 1 else jnp.concatenate([x] * n, axis=1)


def _lane_fold(p):
    out = p[:, :LANES]
    for t in range(1, p.shape[1] // LANES):
        out = out + p[:, t * LANES:(t + 1) * LANES]
    return out


def _first_weights(s, m_ref, l_ref, rows=slice(None)):
    m = jnp.broadcast_to(jnp.max(s, axis=1, keepdims=True), (s.shape[0], LANES))
    p = jnp.exp2(s - _lane_tile(m, s.shape[1] // LANES))
    m_ref[rows, :] = m
    l_ref[rows, :] = _lane_fold(p)
    return p.astype(BF16)


def _next_weights(s, m_ref, l_ref, rows=slice(None)):
    m_prev = m_ref[rows, :]
    m_new = jnp.maximum(m_prev, jnp.max(s, axis=1, keepdims=True))
    alpha = jnp.exp2(m_prev - m_new)
    p = jnp.exp2(s - _lane_tile(m_new, s.shape[1] // LANES))
    l_ref[rows, :] = alpha * l_ref[rows, :] + _lane_fold(p)
    m_ref[rows, :] = m_new
    return alpha, p.astype(BF16)


def _skewed(stages, jobs, before_tick=None):
    state = [None] * len(jobs)
    for tick in range(len(jobs) + len(stages) - 1):
        if before_tick and tick in before_tick:
            before_tick[tick]()
        for k, stage in enumerate(stages):
            j = tick - k
            if 0 <= j < len(jobs):
                state[j] = stage(jobs[j], state[j])


def _softmax_finish(l_ref, acc_ref, rows=slice(None)):
    return acc_ref[rows, :] / jnp.sum(l_ref[rows, :], axis=1, keepdims=True)


FOX_TQ = 512
FOX_HALF = FOX_TQ // 2


def _fox_routing():
    pk = np.zeros((N_HEADS, LANES, LANES), np.float32)
    for h in range(N_HEADS):
        for t in range(_N_SPLIT):
            pk[h, ONES_LANE, t * N_HEADS + h] = 1.0
            pk[h, t * N_HEADS + h, ONES_LANE + t] = -1.0
    return pk


def _cast_weight_block(step, w_blocks, w_refs, wcast_ref):
    first = 0
    for w_ref, n_blocks in zip(w_refs, w_blocks):
        @pl.when((step >= first) & (step < first + n_blocks))
        def _(w_ref=w_ref):
            wcast_ref[...] = w_ref[...].astype(wcast_ref.dtype)
        first += n_blocks


def _weight_block_specs(weights, n_steps, step_of):
    w_rows = sum(w.shape[0] for w in weights)
    rows_per_step = w_rows // n_steps
    assert w_rows % n_steps == 0 and all(w.shape[0] % rows_per_step == 0 for w in weights)
    w_blocks = tuple(w.shape[0] // rows_per_step for w in weights)
    in_specs, first = [], 0
    for n_blocks in w_blocks:
        in_specs.append(pl.BlockSpec(
            (rows_per_step, D_MODEL),
            lambda b, i, first=first, n_blocks=n_blocks: (jnp.clip(step_of(b, i) - first, 0, n_blocks - 1), 0)))
        first += n_blocks
    out_spec = pl.BlockSpec((rows_per_step, D_MODEL), lambda b, i: (step_of(b, i), 0))
    return w_blocks, in_specs, out_spec, jax.ShapeDtypeStruct((w_rows, D_MODEL), BF16)


def _fox_kernel(q_ref, k_ref, v_ref, z_ref, cq_ref, ck_ref, pk_ref, o_ref,
                kaug_ref, qaug_ref, sz_ref, m_ref, l_ref, acc_ref):
    i = pl.program_id(1)
    tq, half = FOX_TQ, FOX_HALF
    seq = k_ref.shape[0]

    @pl.when(i == 0)
    def _():
        for h in range(N_HEADS):
            for blk in range(seq // tq):
                rows = slice(blk * tq, (blk + 1) * tq)
                kaug_ref[h, rows, :HEAD_DIM] = k_ref[rows, h * HEAD_DIM:(h + 1) * HEAD_DIM]
                kaug_ref[h, rows, HEAD_DIM:] = _dot(ck_ref[rows, :], pk_ref[h]).astype(BF16)

    causal = (lax.broadcasted_iota(jnp.int32, (half, half), 1)
              <= lax.broadcasted_iota(jnp.int32, (half, half), 0))
    diag = pl.multiple_of(i * tq, tq)
    diag2 = pl.multiple_of(i * tq + half, half)
    heads = tuple(range(N_HEADS))
    rows = lambda h: slice(h * tq, (h + 1) * tq)
    top = lambda h: slice(h * tq, h * tq + half)
    bot = lambda h: slice(h * tq + half, (h + 1) * tq)
    cols = lambda h: slice(h * HEAD_DIM, (h + 1) * HEAD_DIM)

    def diag_scores(h, _):
        qaug_ref[rows(h), :HEAD_DIM] = q_ref[:, cols(h)]
        qaug_ref[rows(h), HEAD_DIM:] = cq_ref[...]
        s_left = _dot_nt(qaug_ref[rows(h), :], kaug_ref[h, pl.ds(diag, half), :])
        s_right = _dot_nt(qaug_ref[bot(h), :], kaug_ref[h, pl.ds(diag2, half), :])
        return s_left, s_right

    def diag_weights(h, scores):
        s_left, s_right = scores
        p_top = _first_weights(jnp.where(causal, s_left[:half], NEG), m_ref, l_ref, top(h))
        p_bot = _first_weights(jnp.concatenate([s_left[half:], jnp.where(causal, s_right, NEG)], axis=1),
                               m_ref, l_ref, bot(h))
        return p_top, p_bot

    def diag_values(h, weights):
        p_top, p_bot = weights
        acc_ref[top(h), :] = _dot(p_top, v_ref[pl.ds(diag, half), cols(h)])
        acc_ref[bot(h), :] = _dot(p_bot, v_ref[pl.ds(diag, tq), cols(h)])
        sz_ref[:, cols(h)] = _silu(z_ref[:, cols(h)].astype(F32))

    _skewed((diag_scores, diag_weights, diag_values), heads)

    def body(j, carry):
        off = pl.multiple_of(j * tq, tq)

        def scores(h, _):
            return _dot_nt(qaug_ref[rows(h), :], kaug_ref[h, pl.ds(off, tq), :])

        def weights(h, s):
            return _next_weights(s, m_ref, l_ref, rows(h))

        def values(h, rescale_and_weights):
            alpha, p = rescale_and_weights
            acc_ref[rows(h), :] = alpha * acc_ref[rows(h), :] + _dot(p, v_ref[pl.ds(off, tq), cols(h)])

        _skewed((scores, weights, values), heads)
        return carry

    lax.fori_loop(0, i, body, 0)
    for h in heads:
        o_ref[:, cols(h)] = (_softmax_finish(l_ref, acc_ref, rows(h)) * sz_ref[:, cols(h)]).astype(o_ref.dtype)


def _fox(h_main, c_packed, batch, seq):
    tq = FOX_TQ
    nq = seq // tq
    blk = lambda name: _MAIN_OFF[name] // WIDTH
    pk = jnp.asarray(_fox_routing(), BF16)
    full = lambda shape: pl.BlockSpec(shape, lambda b, i: (0,) * len(shape))
    stat = pltpu.VMEM((N_HEADS * tq, LANES), F32)
    return pl.pallas_call(
        _fox_kernel,
        grid=(batch, nq),
        in_specs=[
            pl.BlockSpec((tq, WIDTH), lambda b, i: (b * nq + i, blk("fox_q"))),
            pl.BlockSpec((seq, WIDTH), lambda b, i: (b, blk("fox_k"))),
            pl.BlockSpec((seq, WIDTH), lambda b, i: (b, blk("fox_v"))),
            pl.BlockSpec((tq, WIDTH), lambda b, i: (b * nq + i, blk("fox_z"))),
            pl.BlockSpec((tq, LANES), lambda b, i: (b * nq + i, 0)),
            pl.BlockSpec((seq, LANES), lambda b, i: (b, 0)),
            full(pk.shape),
        ],
        out_specs=pl.BlockSpec((tq, WIDTH), lambda b, i: (b * nq + i, 0)),
        out_shape=jax.ShapeDtypeStruct((batch * seq, WIDTH), BF16),
        scratch_shapes=[pltpu.VMEM((N_HEADS, seq, MXU_DEPTH), BF16),
                        pltpu.VMEM((N_HEADS * tq, MXU_DEPTH), BF16),
                        pltpu.VMEM((tq, WIDTH), F32),
                        stat, stat, stat],
        compiler_params=pltpu.CompilerParams(
            dimension_semantics=("parallel", "arbitrary"), vmem_limit_bytes=VMEM_LIMIT),
        name="fox",
    )(h_main, h_main, h_main, h_main, c_packed, c_packed, pk)


TQ = 256
TK = TQ
WIN_BLOCKS = (WINDOW + TQ) // TK
SEL_NEAR_BLOCKS = 2
FAR_TK = 2 * TK
N_SEL_ROWS = 32
MASK_BUCKET = REL_BUCKETS
GROUP_ROWS = HEADS_PER_GROUP * TQ
T_DIAG, T_PREV, T_WIN2, T_NONE = range(4)
N_GATHERED_TILES = 3


def _bias_lookup(tab_ref, h, idx):
    row = jnp.broadcast_to(tab_ref[h:h + 1, :], idx.shape)
    return jnp.take_along_axis(row, idx, axis=1, mode="promise_in_bounds")


def _nsa_kernel(w_blocks, q_ref, ks_ref, vs_ref, kw_ref, vw_ref, z_ref, kc_ref, vc_ref, cidx_ref, tidx_ref,
                tab_ref, g_ref, et_ref, ov_ref, wa_ref, wb_ref, wo_ref, o_ref, wcast_ref,
                ksaug_ref, qs_ref, qaug_ref, m_ref, l_ref, acc_ref, oc_ref, ow_ref, gs_ref, t_ref, cb_ref):
    i = pl.program_id(1)
    t0 = i * TQ
    _cast_weight_block(pl.program_id(0) * pl.num_programs(1) + i, w_blocks, (wa_ref, wb_ref, wo_ref),
                       wcast_ref)

    @pl.when((pl.program_id(0) == 0) & (i == 0))
    def _():
        for h in range(N_HEADS):
            for d in range(N_GATHERED_TILES):
                for half in range(TK // LANES):
                    cs = slice(half * LANES, (half + 1) * LANES)
                    t_ref[h, d, :, cs] = _bias_lookup(tab_ref, h, tidx_ref[d, :, cs])
            for blk in range(cidx_ref.shape[0] // TQ):
                rows = slice(blk * TQ, (blk + 1) * TQ)
                cb_ref[h, rows, :] = _bias_lookup(tab_ref, h, cidx_ref[rows, :])
            t_ref[h, T_NONE] = jnp.full((TQ, TK), NEG, F32)

    @pl.when(i == 0)
    def _():
        for g in range(KV_GROUPS):
            ksaug_ref[g, :, :HEAD_DIM] = ks_ref[:, g * HEAD_DIM:(g + 1) * HEAD_DIM]
            ksaug_ref[g, :, HEAD_DIM:] = et_ref[...]

    cmaskf = (lax.broadcasted_iota(jnp.int32, (TQ, LANES), 1) * CMP_STRIDE + (CMP_LEN - 1)
              <= t0 + lax.broadcasted_iota(jnp.int32, (TQ, LANES), 0)).astype(F32)
    j_t = lax.broadcasted_iota(jnp.int32, (N_SEL_ROWS, TQ), 0)
    t_t = t0 + lax.broadcasted_iota(jnp.int32, (N_SEL_ROWS, TQ), 1)
    cur_t = t_t // SEL_LEN
    forced_t = (j_t == 0) | (j_t == cur_t) | (j_t == cur_t - 1)
    valid_t = j_t * SEL_LEN <= t_t

    def near_span(n_blocks):
        first = jnp.maximum(i - (n_blocks - 1), 0)
        return first, pl.ds(pl.multiple_of(first * TK, TK), n_blocks * TK)

    win_first, win_keys = near_span(WIN_BLOCKS)
    sel_first, sel_keys = near_span(SEL_NEAR_BLOCKS)

    def near_bias(heads, first, n_blocks):
        def tile(kk):
            dist = i - (first + kk)
            return jnp.where(dist == 0, T_DIAG, jnp.where(dist == 1, T_PREV,
                             jnp.where(dist == 2, T_WIN2, T_NONE)))
        return jnp.concatenate(
            [jnp.concatenate([t_ref[h, tile(kk)] for kk in range(n_blocks)], axis=1) for h in heads],
            axis=0)

    groups = tuple((g, g * HEAD_DIM, tuple(range(g * HEADS_PER_GROUP, (g + 1) * HEADS_PER_GROUP)),
                    slice(g * GROUP_ROWS, (g + 1) * GROUP_ROWS)) for g in range(KV_GROUPS))
    hrows = lambda h: slice(h * TQ, (h + 1) * TQ)
    for h in range(N_HEADS):
        q = q_ref[:, h * HEAD_DIM:(h + 1) * HEAD_DIM]
        qs_ref[hrows(h), :] = q
        qaug_ref[hrows(h), :HEAD_DIM] = q

    chain = {}

    def compressed_branch():
        cbias = jnp.concatenate([cb_ref[h, pl.ds(pl.multiple_of(t0, TQ), TQ), :] for h in range(N_HEADS)],
                                axis=0)
        sc = jnp.concatenate([_dot_nt(qs_ref[grows, :], kc_ref[0, :, glo:glo + HEAD_DIM])
                              for g, glo, heads, grows in groups], axis=0) + cbias
        e = jnp.exp2(sc - jnp.max(sc, axis=1, keepdims=True))
        p = e / jnp.sum(e, axis=1, keepdims=True) * jnp.concatenate([cmaskf] * N_HEADS, axis=0)
        imp_parts = []
        for g, glo, heads, grows in groups:
            oc_ref[grows, :] = _dot(p[grows].astype(BF16), vc_ref[0, :, glo:glo + HEAD_DIM])
            psum = p[hrows(heads[0])]
            for h in heads[1:]:
                psum = psum + p[hrows(h)]
            p_hi = psum.astype(BF16)
            p_lo = (psum - p_hi.astype(F32)).astype(BF16)
            imp_parts.append((_dot_nt(ov_ref[...], p_hi) + _dot_nt(ov_ref[...], p_lo))[:N_SEL_ROWS])
        chain["importance"] = jnp.concatenate(imp_parts, axis=1)

    def block_selection():
        both = lambda a: jnp.concatenate([a] * KV_GROUPS, axis=1)
        j_b = both(j_t)
        x = jnp.where(both(valid_t), jnp.where(both(forced_t), -NEG, chain["importance"]), NEG)
        groups8 = [slice(r, r + SUBLANES) for r in range(0, N_SEL_ROWS, SUBLANES)]
        cnt = [jnp.zeros((SUBLANES, x.shape[1]), F32) for _ in groups8]
        for jp in range(N_SEL_ROWS):
            row = x[jp:jp + 1, :]
            for n, rows8 in enumerate(groups8):
                if rows8.start > jp:
                    beats = row >= x[rows8]
                elif rows8.stop - 1 <= jp:
                    beats = row > x[rows8]
                else:
                    beats = (row > x[rows8]) | ((row == x[rows8]) & (j_b[rows8] > jp))
                cnt[n] = cnt[n] + jnp.where(beats, 1.0, 0.0)
        unsel_t = jnp.where(jnp.concatenate(cnt, axis=0) < SEL_TOPK, 0.0, 1.0)
        unsel_t = jnp.concatenate([unsel_t, jnp.zeros((LANES - N_SEL_ROWS, x.shape[1]), F32)], axis=0)
        unsel = unsel_t.T.astype(BF16)
        for h in range(N_HEADS):
            g = h // HEADS_PER_GROUP
            qaug_ref[hrows(h), HEAD_DIM:] = unsel[g * TQ:(g + 1) * TQ]

    def near_scores(job, _):
        branch, (g, glo, heads, grows) = job
        if branch == "window":
            return (_dot_nt(qs_ref[grows, :], kw_ref[win_keys, glo:glo + HEAD_DIM])
                    + near_bias(heads, win_first, WIN_BLOCKS))
        return (_dot_nt(qaug_ref[grows, :], ksaug_ref[g, sel_keys, :])
                + near_bias(heads, sel_first, SEL_NEAR_BLOCKS))

    def near_weights(job, s):
        branch, (g, glo, heads, grows) = job
        if branch == "window":
            p = jnp.exp2(s - jnp.max(s, axis=1, keepdims=True))
            return p.astype(BF16), _lane_fold(p)
        return _first_weights(s, m_ref, l_ref, grows)

    def near_values(job, weights):
        branch, (g, glo, heads, grows) = job
        if branch == "selected":
            acc_ref[grows, :] = _dot(weights, vs_ref[sel_keys, glo:glo + HEAD_DIM])
            return
        p, l = weights
        o_win = _dot(p, vw_ref[win_keys, glo:glo + HEAD_DIM]) / jnp.sum(l, axis=1, keepdims=True)
        for n, h in enumerate(heads):
            gl = GATE_LANE0 + h * N_BRANCHES
            sz = _silu(z_ref[:, h * HEAD_DIM:(h + 1) * HEAD_DIM].astype(F32))
            gs_ref[hrows(h), :] = g_ref[:, gl + 1:gl + 2] * sz
            ow_ref[hrows(h), :] = (g_ref[:, gl:gl + 1] * oc_ref[hrows(h), :]
                                   + g_ref[:, gl + 2:gl + 3] * o_win[n * TQ:(n + 1) * TQ]) * sz

    _skewed((near_scores, near_weights, near_values),
            tuple((branch, group) for branch in ("window", "selected") for group in groups),
            before_tick={1: compressed_branch, 2: block_selection})

    def sel_far(off, width):
        def scores(group, _):
            g, glo, heads, grows = group
            return _dot_nt(qaug_ref[grows, :], ksaug_ref[g, pl.ds(off, width), :])

        def weights(group, s):
            g, glo, heads, grows = group
            return _next_weights(s, m_ref, l_ref, grows)

        def values(group, rescale_and_weights):
            g, glo, heads, grows = group
            alpha, p = rescale_and_weights
            acc_ref[grows, :] = alpha * acc_ref[grows, :] + _dot(p, vs_ref[pl.ds(off, width), glo:glo + HEAD_DIM])

        _skewed((scores, weights, values), groups)

    def sel_far_pair(j, carry):
        sel_far(pl.multiple_of(j * FAR_TK, FAR_TK), FAR_TK)
        return carry

    lax.fori_loop(0, sel_first // 2, sel_far_pair, 0)

    @pl.when(sel_first % 2 == 1)
    def _():
        sel_far(pl.multiple_of((sel_first - 1) * TK, TK), TK)

    for h in range(N_HEADS):
        o_ref[:, h * HEAD_DIM:(h + 1) * HEAD_DIM] = (
            ow_ref[hrows(h), :] + gs_ref[hrows(h), :] * _softmax_finish(l_ref, acc_ref, hrows(h))
        ).astype(o_ref.dtype)


def _nsa(h_main, k_cmp, v_cmp, cmp_idx, tile_idx, table, gates, e_t, ov_t, weights, batch, seq):
    nq = seq // TQ
    w_blocks, w_in_specs, w_out_spec, w_out_shape = _weight_block_specs(
        weights, batch * nq, lambda b, i: b * nq + i)
    wblk = lambda name: _MAIN_OFF[name] // WIDTH
    kvblk = lambda name: _MAIN_OFF[name] // KV_WIDTH
    full = lambda shape: pl.BlockSpec(shape, lambda b, i: (0,) * len(shape),
                                      pipeline_mode=pl.Buffered(1))
    kv_spec = lambda name: pl.BlockSpec((seq, KV_WIDTH), lambda b, i: (b, kvblk(name)))
    n_chunks = k_cmp.shape[1]
    stat = pltpu.VMEM((N_HEADS * TQ, LANES), F32)
    return pl.pallas_call(
        functools.partial(_nsa_kernel, w_blocks),
        grid=(batch, nq),
        in_specs=[
            pl.BlockSpec((TQ, WIDTH), lambda b, i: (b * nq + i, wblk("nsa_q"))),
            kv_spec("nsa_k_sel"), kv_spec("nsa_v_sel"), kv_spec("nsa_k_win"), kv_spec("nsa_v_win"),
            pl.BlockSpec((TQ, WIDTH), lambda b, i: (b * nq + i, wblk("nsa_z"))),
            pl.BlockSpec((1, n_chunks, KV_WIDTH), lambda b, i: (b, 0, 0)),
            pl.BlockSpec((1, n_chunks, KV_WIDTH), lambda b, i: (b, 0, 0)),
            full(cmp_idx.shape), full(tile_idx.shape), full(table.shape),
            pl.BlockSpec((TQ, LANES), lambda b, i: (b * nq + i, 0)),
            full(e_t.shape), full(ov_t.shape),
            *w_in_specs,
        ],
        out_specs=[pl.BlockSpec((TQ, WIDTH), lambda b, i: (b * nq + i, 0)), w_out_spec],
        out_shape=[jax.ShapeDtypeStruct((batch * seq, WIDTH), BF16), w_out_shape],
        scratch_shapes=[pltpu.VMEM((KV_GROUPS, seq, MXU_DEPTH), BF16),
                        pltpu.VMEM((N_HEADS * TQ, HEAD_DIM), BF16),
                        pltpu.VMEM((N_HEADS * TQ, MXU_DEPTH), BF16),
                        stat, stat, stat,
                        stat, stat, stat,
                        pltpu.VMEM((N_HEADS, T_NONE + 1, TQ, TK), F32),
                        pltpu.VMEM((N_HEADS, seq, LANES), F32)],
        compiler_params=pltpu.CompilerParams(
            dimension_semantics=("arbitrary", "arbitrary"), vmem_limit_bytes=VMEM_LIMIT),
        name="nsa",
    )(h_main, h_main, h_main, h_main, h_main, h_main, k_cmp, v_cmp, cmp_idx, tile_idx, table, gates,
      e_t, ov_t, *weights)


OUT_TM = 512
OUT_TN = 512
OUT_ROW_GROUPS = 2


def _out_kernel(alpha, ua_ref, ub_ref, ga_ref, gb_ref, x_ref, wa_ref, wb_ref, wo_ref, lg_ref, lb_ref,
                o_ref, merged_ref):
    d_model = o_ref.shape[1]
    chunks = tuple(slice(c * OUT_TN, (c + 1) * OUT_TN) for c in range(d_model // OUT_TN))
    half = o_ref.shape[0] // OUT_ROW_GROUPS

    def merge(rows, _):
        for cs in chunks:
            ya = _dot(ua_ref[rows, :], wa_ref[:, cs])
            yb = _dot(ub_ref[rows, :], wb_ref[:, cs])
            merged_ref[rows, cs] = (_sigmoid(ga_ref[rows, cs].astype(F32)) * ya
                                    + _sigmoid(gb_ref[rows, cs].astype(F32)) * yb).astype(merged_ref.dtype)

    def residual(rows, _):
        total = jnp.zeros((half, LANES), F32)
        for cs in chunks:
            y = alpha * x_ref[rows, cs] + _dot(merged_ref[rows, :], wo_ref[:, cs])
            o_ref[rows, cs] = y
            total = total + _lane_fold(y)
        return total

    def layer_norm(rows, total):
        mu = jnp.sum(total, axis=1, keepdims=True) * (1.0 / d_model)
        sq = jnp.zeros((half, LANES), F32)
        for cs in chunks:
            d = o_ref[rows, cs] - mu
            sq = sq + _lane_fold(d * d)
        rstd = lax.rsqrt(jnp.sum(sq, axis=1, keepdims=True) * (1.0 / d_model) + LN_EPS)
        for cs in chunks:
            o_ref[rows, cs] = (o_ref[rows, cs] - mu) * rstd * lg_ref[:, cs] + lb_ref[:, cs]

    _skewed((merge, residual, layer_norm),
            tuple(slice(r * half, (r + 1) * half) for r in range(OUT_ROW_GROUPS)))


def _out(u_a, u_b, h_main, x2, w_cat, ln_g, ln_b, alpha, tm=OUT_TM):
    m = x2.shape[0]
    mblk = lambda name: _MAIN_OFF[name] // D_MODEL
    const = lambda shape: pl.BlockSpec(shape, lambda i: (0, 0), pipeline_mode=pl.Buffered(1))
    assert w_cat.shape == (2 * WIDTH + D_MODEL, D_MODEL) and D_MODEL == 2 * WIDTH
    w_part = lambda rows, index: pl.BlockSpec((rows, D_MODEL), lambda i: (index, 0),
                                              pipeline_mode=pl.Buffered(1))
    return pl.pallas_call(
        functools.partial(_out_kernel, alpha),
        grid=(m // tm,),
        in_specs=[
            pl.BlockSpec((tm, WIDTH), lambda i: (i, 0)),
            pl.BlockSpec((tm, WIDTH), lambda i: (i, 0)),
            pl.BlockSpec((tm, D_MODEL), lambda i: (i, mblk("merge_a"))),
            pl.BlockSpec((tm, D_MODEL), lambda i: (i, mblk("merge_b"))),
            pl.BlockSpec((tm, D_MODEL), lambda i: (i, 0)),
            w_part(WIDTH, 0), w_part(WIDTH, 1), w_part(D_MODEL, 1),
            const(ln_g.shape), const(ln_b.shape),
        ],
        out_specs=pl.BlockSpec((tm, D_MODEL), lambda i: (i, 0)),
        out_shape=jax.ShapeDtypeStruct((m, D_MODEL), F32),
        scratch_shapes=[pltpu.VMEM((tm, D_MODEL), BF16)],
        compiler_params=pltpu.CompilerParams(
            dimension_semantics=("parallel",), vmem_limit_bytes=VMEM_LIMIT),
        name="out",
    )(u_a, u_b, h_main, h_main, x2, w_cat, w_cat, w_cat, ln_g, ln_b)


def _bucket_np(dist):
    n = np.maximum(dist, 0)
    exact = REL_BUCKETS // 2
    large = exact + (np.log(np.maximum(n, 1).astype(np.float32) / exact)
                     / math.log(REL_MAX_DIST / exact) * (REL_BUCKETS - exact)).astype(np.int32)
    return np.where(n < exact, n, np.minimum(large, REL_BUCKETS - 1)).astype(np.int32)


@functools.lru_cache(maxsize=None)
def _static_tables(seq):
    r = np.arange(TQ)[:, None]
    c = np.arange(TK)[None, :]
    tile_idx = np.stack([_bucket_np(r - c), _bucket_np(TQ + r - c), _bucket_np(2 * TQ + r - c)])
    tile_ok = np.stack([c <= r, np.ones((TQ, TK), bool), (2 * TQ + r - c) < WINDOW])
    tile_idx = np.where(tile_ok, tile_idx, MASK_BUCKET).astype(np.int32)
    t = np.arange(seq)[:, None]
    cblk = np.arange(LANES)[None, :]
    blk_end = cblk * CMP_STRIDE + CMP_LEN - 1
    n_cmp = (seq - CMP_LEN) // CMP_STRIDE + 1
    cmp_idx = np.where((blk_end <= t) & (cblk < n_cmp), _bucket_np(t - blk_end), MASK_BUCKET).astype(np.int32)
    e_t = np.where((np.arange(seq)[:, None] // SEL_LEN) == np.arange(LANES)[None, :], NEG, 0.0)
    cs = (np.arange(LANES) * CMP_STRIDE)[None, :]
    ss = (np.arange(LANES) * SEL_LEN)[:, None]
    ov_t = ((cs < ss + SEL_LEN) & (cs + CMP_LEN > ss)
            & (np.arange(LANES)[None, :] < n_cmp) & (np.arange(LANES)[:, None] < seq // SEL_LEN))
    return tile_idx, cmp_idx, e_t.astype(np.float32), ov_t.astype(np.float32)


def _layer(x, w_in, b_f, cmp_pos_k, cmp_pos_v, cmp_wk1, cmp_wk2, cmp_wv1, cmp_wv2,
           w_a, w_b, w_o, ln_g, ln_b, rel_bias, alpha):
    batch, seq, d_model = x.shape
    assert d_model == D_MODEL and seq % FOX_TQ == 0 and seq >= WIN_BLOCKS * TK and REL_MAX_DIST <= TQ
    assert seq // SEL_LEN == N_SEL_ROWS and seq // CMP_STRIDE == LANES
    x2 = x.reshape(batch * seq, d_model)

    w_t = jnp.swapaxes(w_in, 0, 1)
    trows = lambda name: w_t[_REF_OFF[name][0]:_REF_OFF[name][0] + _REF_OFF[name][1]]
    n_small = N_HEADS + N_HEADS * N_BRANCHES
    w_small_t = jnp.concatenate(
        [trows("fox_f"), trows("nsa_gate"), jnp.zeros((LANES - n_small, d_model), F32)], axis=0)
    bf_row = jnp.concatenate([b_f.astype(F32), jnp.zeros((LANES - N_HEADS,), F32)]).reshape(1, LANES)

    h_main, h_small = _proj(x2, w_t, w_small_t)
    c_col, gates = _gates(h_small, bf_row, batch, seq)
    u_a = _fox(h_main, c_col, batch, seq)

    k_cmp, v_cmp = _compress(h_main, cmp_pos_k, cmp_pos_v, cmp_wk1, cmp_wk2, cmp_wv1, cmp_wv2, batch, seq)

    tile_idx, cmp_idx, e_t, ov_t = _static_tables(seq)
    bias = rel_bias.T.astype(F32)
    table = jnp.concatenate(
        [(bias - bias[:, REL_BUCKETS - 1:]) * LOG2E, jnp.full((N_HEADS, 1), NEG, F32),
         jnp.zeros((N_HEADS, LANES - REL_BUCKETS - 1), F32)], axis=1)
    u_b, w_cat = _nsa(h_main, k_cmp, v_cmp, jnp.asarray(cmp_idx), jnp.asarray(tile_idx), table, gates,
                      jnp.asarray(e_t, BF16), jnp.asarray(ov_t, BF16), (w_a, w_b, w_o), batch, seq)

    out = _out(u_a, u_b, h_main, x2, w_cat, ln_g.reshape(1, d_model), ln_b.reshape(1, d_model), alpha)
    return out.reshape(batch, seq, d_model)


def kernel(x, w_in, b_f, cmp_pos_k, cmp_pos_v, cmp_wk1, cmp_wk2, cmp_wv1, cmp_wv2,
           w_a, w_b, w_o, ln_g, ln_b, rel_bias):
    depth = w_in.shape[0]
    alpha = (2 * depth) ** 0.25
    for layer in range(depth):
        x = _layer(x, w_in[layer], b_f[layer], cmp_pos_k[layer], cmp_pos_v[layer], cmp_wk1[layer],
                   cmp_wk2[layer], cmp_wv1[layer], cmp_wv2[layer], w_a[layer], w_b[layer], w_o[layer],
                   ln_g[layer], ln_b[layer], rel_bias, alpha)
    return x
```

```python
import functools
import math

import jax
import jax.numpy as jnp
import numpy as np
from jax import lax
from jax.experimental import pallas as pl
from jax.experimental.pallas import tpu as pltpu

F32 = jnp.float32
BF16 = jnp.bfloat16

D_MODEL = 2048
HEAD_DIM = 128
N_HEADS = 8
WIDTH = N_HEADS * HEAD_DIM
KV_GROUPS = 2
HEADS_PER_GROUP = N_HEADS // KV_GROUPS
KV_WIDTH = KV_GROUPS * HEAD_DIM
N_BRANCHES = 3
CMP_LEN = 32
CMP_STRIDE = 16
CMP_HIDDEN = 256
SEL_LEN = 64
SEL_TOPK = 8
WINDOW = 512
REL_BUCKETS = 32
REL_MAX_DIST = 128
LN_EPS = 1e-5
NEG = -1e30
LOG2E = math.log2(math.e)
Q_PRESCALE = HEAD_DIM ** -0.5 * LOG2E

LANES = 128
SUBLANES = 8
MXU_DEPTH = 256
VMEM_LIMIT = 60 * 1024 * 1024

_REF_LAYOUT = (
    ("fox_q", WIDTH), ("fox_k", WIDTH), ("fox_v", WIDTH), ("fox_f", N_HEADS), ("fox_z", WIDTH),
    ("nsa_q", WIDTH), ("nsa_k_cmp", KV_WIDTH), ("nsa_v_cmp", KV_WIDTH), ("nsa_k_sel", KV_WIDTH),
    ("nsa_v_sel", KV_WIDTH), ("nsa_k_win", KV_WIDTH), ("nsa_v_win", KV_WIDTH),
    ("nsa_gate", N_HEADS * N_BRANCHES), ("nsa_z", WIDTH), ("merge_a", D_MODEL), ("merge_b", D_MODEL),
)
_REF_OFF = {}
_o = 0
for _n, _w in _REF_LAYOUT:
    _REF_OFF[_n] = (_o, _w)
    _o += _w

_MAIN_ORDER = ("fox_q", "fox_k", "fox_v", "fox_z", "nsa_q", "nsa_z", "merge_a", "merge_b",
               "nsa_k_cmp", "nsa_v_cmp", "nsa_k_sel", "nsa_v_sel", "nsa_k_win", "nsa_v_win")
_QUERY_COLS = ("fox_q", "nsa_q")
_MAIN_OFF = {}
_o = 0
for _n in _MAIN_ORDER:
    _MAIN_OFF[_n] = _o
    _o += _REF_OFF[_n][1]
MAIN_COLS = _o
GATE_LANE0 = N_HEADS


def _dot(a, b):
    return jnp.dot(a, b, preferred_element_type=F32)


def _dot_nt(a, b):
    return lax.dot_general(a, b, (((1,), (1,)), ((), ())), preferred_element_type=F32)


def _sigmoid(x):
    return 0.5 + 0.5 * jnp.tanh(0.5 * x)


def _silu(x):
    return x * _sigmoid(x)


PROJ_TN = 512


def _proj_tiles():
    rows, is_query = [], []
    for name in _MAIN_ORDER:
        off, width = _REF_OFF[name]
        start = _MAIN_OFF[name]
        for c in range(start, start + width):
            if c % PROJ_TN == 0:
                assert (off + c - start) % SUBLANES == 0
                rows.append((off + c - start) // SUBLANES)
                is_query.append(int(name in _QUERY_COLS))
    return np.asarray(rows, np.int32), np.asarray(is_query, np.int32)


PROJ_ROWS = 4096
PROJ_RB = 512


def _proj_kernel(rows_ref, isq_ref, x_hbm, wt_ref, wst_ref, o_ref, os_ref,
                 xb_ref, stage_ref, wb_ref, sem):
    i = pl.program_id(0)
    j = pl.program_id(1)
    n_blocks = PROJ_ROWS // PROJ_RB
    wb_ref[...] = wt_ref[...].astype(BF16)
    scale = jnp.where(isq_ref[j] == 1, Q_PRESCALE, 1.0)

    def x_copy(r, slot):
        row0 = pl.multiple_of(i * PROJ_ROWS + r * PROJ_RB, PROJ_RB)
        return pltpu.make_async_copy(x_hbm.at[pl.ds(row0, PROJ_RB), :], stage_ref.at[slot], sem.at[slot])

    def block(r):
        rows = slice(r * PROJ_RB, (r + 1) * PROJ_RB)
        o_ref[rows, :] = (_dot_nt(xb_ref[rows, :], wb_ref[...]) * scale).astype(o_ref.dtype)

    @pl.when(j == 0)
    def _():
        wsb = wst_ref[...].astype(BF16)
        x_copy(0, 0).start()
        for r in range(n_blocks):
            if r + 1 < n_blocks:
                x_copy(r + 1, (r + 1) % 2).start()
            x_copy(r, r % 2).wait()
            rows = slice(r * PROJ_RB, (r + 1) * PROJ_RB)
            xb_ref[rows, :] = stage_ref[r % 2].astype(BF16)
            os_ref[rows, :] = _dot_nt(xb_ref[rows, :], wsb)
            block(r)

    @pl.when(j > 0)
    def _():
        for r in range(n_blocks):
            block(r)


def _proj(x2, w_t, w_small_t):
    m, k = x2.shape
    rows, is_query = _proj_tiles()
    assert MAIN_COLS % PROJ_TN == 0 and len(rows) == MAIN_COLS // PROJ_TN and m % PROJ_ROWS == 0
    grid_spec = pltpu.PrefetchScalarGridSpec(
        num_scalar_prefetch=2,
        grid=(m // PROJ_ROWS, len(rows)),
        in_specs=[
            pl.BlockSpec(memory_space=pl.ANY),
            pl.BlockSpec((pl.Element(PROJ_TN), pl.Element(k)),
                         lambda i, j, rows, isq: (rows[j] * SUBLANES, 0)),
            pl.BlockSpec((LANES, k), lambda i, j, rows, isq: (0, 0)),
        ],
        out_specs=[
            pl.BlockSpec((PROJ_ROWS, PROJ_TN), lambda i, j, rows, isq: (i, j)),
            pl.BlockSpec((PROJ_ROWS, LANES), lambda i, j, rows, isq: (i, 0)),
        ],
        scratch_shapes=[pltpu.VMEM((PROJ_ROWS, k), BF16),
                        pltpu.VMEM((2, PROJ_RB, k), F32),
                        pltpu.VMEM((PROJ_TN, k), BF16),
                        pltpu.SemaphoreType.DMA((2,))],
    )
    return pl.pallas_call(
        _proj_kernel,
        grid_spec=grid_spec,
        out_shape=[jax.ShapeDtypeStruct((m, MAIN_COLS), BF16), jax.ShapeDtypeStruct((m, LANES), F32)],
        compiler_params=pltpu.CompilerParams(
            dimension_semantics=("parallel", "arbitrary"), vmem_limit_bytes=VMEM_LIMIT),
        name="proj",
    )(jnp.asarray(rows), jnp.asarray(is_query), x2, w_t, w_small_t)


_CUM_CHUNK = 256
_N_SPLIT = 3
ONES_LANE = _N_SPLIT * N_HEADS


def _split3(x):
    hi = x.astype(BF16)
    r1 = x - hi.astype(F32)
    mid = r1.astype(BF16)
    lo = (r1 - mid.astype(F32)).astype(BF16)
    return hi, mid, lo


def _gate_kernel(hs_ref, bf_ref, c_ref, g_ref):
    hs = hs_ref[...]
    g_ref[...] = _sigmoid(hs)
    z = hs + bf_ref[...]
    logf = jnp.minimum(z, 0.0) - jnp.log1p(jnp.exp(-jnp.abs(z)))
    n = _CUM_CHUNK
    tri = (lax.broadcasted_iota(jnp.int32, (n, n), 1)
           <= lax.broadcasted_iota(jnp.int32, (n, n), 0)).astype(BF16)
    lane = lax.broadcasted_iota(jnp.int32, (n, LANES), 1)
    carry = jnp.zeros((1, LANES), F32)
    for blk in range(hs.shape[0] // n):
        hi, mid, lo = _split3(logf[blk * n:(blk + 1) * n])
        cb = _dot(tri, hi) + _dot(tri, mid) + _dot(tri, lo) + carry
        carry = cb[n - 1:n, :]
        hi, mid, lo = _split3(cb * LOG2E)
        packed = jnp.where((lane >= ONES_LANE) & (lane < ONES_LANE + _N_SPLIT), 1.0, 0.0)
        for t, term in enumerate((hi, mid, lo)):
            shifted = term.astype(F32) if t == 0 else pltpu.roll(term.astype(F32), t * N_HEADS, 1)
            packed = jnp.where((lane >= t * N_HEADS) & (lane < (t + 1) * N_HEADS), shifted, packed)
        c_ref[blk * n:(blk + 1) * n, :] = packed.astype(c_ref.dtype)


def _gates(h_small, bf_row, batch, seq):
    return pl.pallas_call(
        _gate_kernel,
        grid=(batch,),
        in_specs=[pl.BlockSpec((seq, LANES), lambda b: (b, 0)),
                  pl.BlockSpec((1, LANES), lambda b: (0, 0))],
        out_specs=[pl.BlockSpec((seq, LANES), lambda b: (b, 0)),
                   pl.BlockSpec((seq, LANES), lambda b: (b, 0))],
        out_shape=[jax.ShapeDtypeStruct(h_small.shape, BF16), jax.ShapeDtypeStruct(h_small.shape, F32)],
        compiler_params=pltpu.CompilerParams(dimension_semantics=("parallel",)),
        name="gates",
    )(h_small, bf_row)


_HALF = CMP_LEN // 2


def _gelu_tanh(x):
    return 0.5 * x * (1.0 + jnp.tanh(math.sqrt(2.0 / math.pi) * (x + 0.044715 * (x * x * x))))


def _compress_kernel(raw_ref, pk_ref, pv_ref, w1k_ref, w2k_ref, w1v_ref, w2v_ref, kc_ref, vc_ref, raw32_ref):
    seq = raw_ref.shape[0]
    n_chunks = seq // _HALF
    step = 512
    for slab in range(raw_ref.shape[1] // HEAD_DIM):
        for blk in range(seq // step):
            raw32_ref[slab, blk * step:(blk + 1) * step, :] = (
                raw_ref[blk * step:(blk + 1) * step, slab * HEAD_DIM:(slab + 1) * HEAD_DIM].astype(F32))
    for kv, (pos_ref, w1_ref, w2_ref, out_ref) in enumerate(
            ((pk_ref, w1k_ref, w2k_ref, kc_ref), (pv_ref, w1v_ref, w2v_ref, vc_ref))):
        first = jnp.zeros((KV_GROUPS * n_chunks, CMP_HIDDEN), F32)
        second = jnp.zeros((KV_GROUPS * n_chunks, CMP_HIDDEN), F32)
        def tokens(l, pos_row):
            a = jnp.concatenate([raw32_ref[kv * KV_GROUPS + g, pl.ds(l, n_chunks, stride=_HALF), :]
                                 for g in range(KV_GROUPS)], axis=0)
            return (a + pos_ref[pos_row:pos_row + 1, :]).astype(BF16)

        for l in range(0, _HALF, 2):
            rows = slice(l * HEAD_DIM, (l + 2) * HEAD_DIM)
            rows2 = slice((_HALF + l) * HEAD_DIM, (_HALF + l + 2) * HEAD_DIM)
            first += _dot(jnp.concatenate([tokens(l, l), tokens(l + 1, l + 1)], axis=1),
                          w1_ref[rows, :].astype(BF16))
            second += _dot(jnp.concatenate([tokens(l, _HALF + l), tokens(l + 1, _HALF + l + 1)], axis=1),
                           w1_ref[rows2, :].astype(BF16))
        hid = first + jnp.concatenate(
            [pltpu.roll(second[g * n_chunks:(g + 1) * n_chunks], n_chunks - 1, 0) for g in range(KV_GROUPS)],
            axis=0)
        out = _dot(_gelu_tanh(hid).astype(BF16), w2_ref[...].astype(BF16))
        for g in range(KV_GROUPS):
            out_ref[0, :, g * HEAD_DIM:(g + 1) * HEAD_DIM] = (
                out[g * n_chunks:(g + 1) * n_chunks].astype(out_ref.dtype))


def _compress(h_main, pos_k, pos_v, w1k, w2k, w1v, w2v, batch, seq):
    n_chunks = seq // _HALF
    raw_cols = 2 * KV_WIDTH
    assert _MAIN_OFF["nsa_v_cmp"] == _MAIN_OFF["nsa_k_cmp"] + KV_WIDTH
    raw_blk = _MAIN_OFF["nsa_k_cmp"] // raw_cols
    full = lambda shape: pl.BlockSpec(shape, lambda b: (0,) * len(shape), pipeline_mode=pl.Buffered(1))
    return pl.pallas_call(
        _compress_kernel,
        grid=(batch,),
        in_specs=[pl.BlockSpec((seq, raw_cols), lambda b: (b, raw_blk)),
                  full(pos_k.shape), full(pos_v.shape),
                  full(w1k.shape), full(w2k.shape), full(w1v.shape), full(w2v.shape)],
        out_specs=[pl.BlockSpec((1, n_chunks, KV_WIDTH), lambda b: (b, 0, 0)),
                   pl.BlockSpec((1, n_chunks, KV_WIDTH), lambda b: (b, 0, 0))],
        out_shape=[jax.ShapeDtypeStruct((batch, n_chunks, KV_WIDTH), BF16),
                   jax.ShapeDtypeStruct((batch, n_chunks, KV_WIDTH), BF16)],
        scratch_shapes=[pltpu.VMEM((raw_cols // HEAD_DIM, seq, HEAD_DIM), F32)],
        compiler_params=pltpu.CompilerParams(
            dimension_semantics=("parallel",), vmem_limit_bytes=VMEM_LIMIT),
        name="compress",
    )(h_main, pos_k, pos_v, w1k, w2k, w1v, w2v)


def _lane_tile(x, n):
    return x if n == 1 else jnp.concatenate([x] * n, axis=1)


def _lane_fold(p):
    out = p[:, :LANES]
    for t in range(1, p.shape[1] // LANES):
        out = out + p[:, t * LANES:(t + 1) * LANES]
    return out


def _first_weights(s, m_ref, l_ref, rows=slice(None)):
    m = jnp.broadcast_to(jnp.max(s, axis=1, keepdims=True), (s.shape[0], LANES))
    p = jnp.exp2(s - _lane_tile(m, s.shape[1] // LANES))
    m_ref[rows, :] = m
    l_ref[rows, :] = _lane_fold(p)
    return p.astype(BF16)


def _next_weights(s, m_ref, l_ref, rows=slice(None)):
    m_prev = m_ref[rows, :]
    m_new = jnp.maximum(m_prev, jnp.max(s, axis=1, keepdims=True))
    alpha = jnp.exp2(m_prev - m_new)
    p = jnp.exp2(s - _lane_tile(m_new, s.shape[1] // LANES))
    l_ref[rows, :] = alpha * l_ref[rows, :] + _lane_fold(p)
    m_ref[rows, :] = m_new
    return alpha, p.astype(BF16)


def _skewed(stages, jobs, before_tick=None):
    state = [None] * len(jobs)
    for tick in range(len(jobs) + len(stages) - 1):
        if before_tick and tick in before_tick:
            before_tick[tick]()
        for k, stage in enumerate(stages):
            j = tick - k
            if 0 <= j < len(jobs):
                state[j] = stage(jobs[j], state[j])


def _softmax_finish(l_ref, acc_ref, rows=slice(None)):
    return acc_ref[rows, :] / jnp.sum(l_ref[rows, :], axis=1, keepdims=True)


FOX_TQ = 512
FOX_HALF = FOX_TQ // 2


def _fox_routing():
    pk = np.zeros((N_HEADS, LANES, LANES), np.float32)
    for h in range(N_HEADS):
        for t in range(_N_SPLIT):
            pk[h, ONES_LANE, t * N_HEADS + h] = 1.0
            pk[h, t * N_HEADS + h, ONES_LANE + t] = -1.0
    return pk


def _cast_weight_block(step, w_blocks, w_refs, wcast_ref):
    first = 0
    for w_ref, n_blocks in zip(w_refs, w_blocks):
        @pl.when((step >= first) & (step < first + n_blocks))
        def _(w_ref=w_ref):
            wcast_ref[...] = w_ref[...].astype(wcast_ref.dtype)
        first += n_blocks


def _weight_block_specs(weights, n_steps, step_of):
    w_rows = sum(w.shape[0] for w in weights)
    rows_per_step = w_rows // n_steps
    assert w_rows % n_steps == 0 and all(w.shape[0] % rows_per_step == 0 for w in weights)
    w_blocks = tuple(w.shape[0] // rows_per_step for w in weights)
    in_specs, first = [], 0
    for n_blocks in w_blocks:
        in_specs.append(pl.BlockSpec(
            (rows_per_step, D_MODEL),
            lambda b, i, first=first, n_blocks=n_blocks: (jnp.clip(step_of(b, i) - first, 0, n_blocks - 1), 0)))
        first += n_blocks
    out_spec = pl.BlockSpec((rows_per_step, D_MODEL), lambda b, i: (step_of(b, i), 0))
    return w_blocks, in_specs, out_spec, jax.ShapeDtypeStruct((w_rows, D_MODEL), BF16)


def _fox_kernel(q_ref, k_ref, v_ref, z_ref, cq_ref, ck_ref, pk_ref, o_ref,
                kaug_ref, qaug_ref, sz_ref, m_ref, l_ref, acc_ref):
    i = pl.program_id(1)
    tq, half = FOX_TQ, FOX_HALF
    seq = k_ref.shape[0]

    @pl.when(i == 0)
    def _():
        for h in range(N_HEADS):
            for blk in range(seq // tq):
                rows = slice(blk * tq, (blk + 1) * tq)
                kaug_ref[h, rows, :HEAD_DIM] = k_ref[rows, h * HEAD_DIM:(h + 1) * HEAD_DIM]
                kaug_ref[h, rows, HEAD_DIM:] = _dot(ck_ref[rows, :], pk_ref[h]).astype(BF16)

    causal = (lax.broadcasted_iota(jnp.int32, (half, half), 1)
              <= lax.broadcasted_iota(jnp.int32, (half, half), 0))
    diag = pl.multiple_of(i * tq, tq)
    diag2 = pl.multiple_of(i * tq + half, half)
    heads = tuple(range(N_HEADS))
    rows = lambda h: slice(h * tq, (h + 1) * tq)
    top = lambda h: slice(h * tq, h * tq + half)
    bot = lambda h: slice(h * tq + half, (h + 1) * tq)
    cols = lambda h: slice(h * HEAD_DIM, (h + 1) * HEAD_DIM)

    def diag_scores(h, _):
        qaug_ref[rows(h), :HEAD_DIM] = q_ref[:, cols(h)]
        qaug_ref[rows(h), HEAD_DIM:] = cq_ref[...]
        s_left = _dot_nt(qaug_ref[rows(h), :], kaug_ref[h, pl.ds(diag, half), :])
        s_right = _dot_nt(qaug_ref[bot(h), :], kaug_ref[h, pl.ds(diag2, half), :])
        return s_left, s_right

    def diag_weights(h, scores):
        s_left, s_right = scores
        p_top = _first_weights(jnp.where(causal, s_left[:half], NEG), m_ref, l_ref, top(h))
        p_bot = _first_weights(jnp.concatenate([s_left[half:], jnp.where(causal, s_right, NEG)], axis=1),
                               m_ref, l_ref, bot(h))
        return p_top, p_bot

    def diag_values(h, weights):
        p_top, p_bot = weights
        acc_ref[top(h), :] = _dot(p_top, v_ref[pl.ds(diag, half), cols(h)])
        acc_ref[bot(h), :] = _dot(p_bot, v_ref[pl.ds(diag, tq), cols(h)])
        sz_ref[:, cols(h)] = _silu(z_ref[:, cols(h)].astype(F32))

    _skewed((diag_scores, diag_weights, diag_values), heads)

    def body(j, carry):
        off = pl.multiple_of(j * tq, tq)

        def scores(h, _):
            return _dot_nt(qaug_ref[rows(h), :], kaug_ref[h, pl.ds(off, tq), :])

        def weights(h, s):
            return _next_weights(s, m_ref, l_ref, rows(h))

        def values(h, rescale_and_weights):
            alpha, p = rescale_and_weights
            acc_ref[rows(h), :] = alpha * acc_ref[rows(h), :] + _dot(p, v_ref[pl.ds(off, tq), cols(h)])

        _skewed((scores, weights, values), heads)
        return carry

    lax.fori_loop(0, i, body, 0)
    for h in heads:
        o_ref[:, cols(h)] = (_softmax_finish(l_ref, acc_ref, rows(h)) * sz_ref[:, cols(h)]).astype(o_ref.dtype)


def _fox(h_main, c_packed, batch, seq):
    tq = FOX_TQ
    nq = seq // tq
    blk = lambda name: _MAIN_OFF[name] // WIDTH
    pk = jnp.asarray(_fox_routing(), BF16)
    full = lambda shape: pl.BlockSpec(shape, lambda b, i: (0,) * len(shape))
    stat = pltpu.VMEM((N_HEADS * tq, LANES), F32)
    return pl.pallas_call(
        _fox_kernel,
        grid=(batch, nq),
        in_specs=[
            pl.BlockSpec((tq, WIDTH), lambda b, i: (b * nq + i, blk("fox_q"))),
            pl.BlockSpec((seq, WIDTH), lambda b, i: (b, blk("fox_k"))),
            pl.BlockSpec((seq, WIDTH), lambda b, i: (b, blk("fox_v"))),
            pl.BlockSpec((tq, WIDTH), lambda b, i: (b * nq + i, blk("fox_z"))),
            pl.BlockSpec((tq, LANES), lambda b, i: (b * nq + i, 0)),
            pl.BlockSpec((seq, LANES), lambda b, i: (b, 0)),
            full(pk.shape),
        ],
        out_specs=pl.BlockSpec((tq, WIDTH), lambda b, i: (b * nq + i, 0)),
        out_shape=jax.ShapeDtypeStruct((batch * seq, WIDTH), BF16),
        scratch_shapes=[pltpu.VMEM((N_HEADS, seq, MXU_DEPTH), BF16),
                        pltpu.VMEM((N_HEADS * tq, MXU_DEPTH), BF16),
                        pltpu.VMEM((tq, WIDTH), F32),
                        stat, stat, stat],
        compiler_params=pltpu.CompilerParams(
            dimension_semantics=("parallel", "arbitrary"), vmem_limit_bytes=VMEM_LIMIT),
        name="fox",
    )(h_main, h_main, h_main, h_main, c_packed, c_packed, pk)


TQ = 256
TK = TQ
WIN_BLOCKS = (WINDOW + TQ) // TK
SEL_NEAR_BLOCKS = 2
FAR_TK = 2 * TK
N_SEL_ROWS = 32
MASK_BUCKET = REL_BUCKETS
GROUP_ROWS = HEADS_PER_GROUP * TQ
T_DIAG, T_PREV, T_WIN2, T_NONE = range(4)
N_GATHERED_TILES = 3


def _bias_lookup(tab_ref, h, idx):
    row = jnp.broadcast_to(tab_ref[h:h + 1, :], idx.shape)
    return jnp.take_along_axis(row, idx, axis=1, mode="promise_in_bounds")


def _nsa_kernel(w_blocks, q_ref, ks_ref, vs_ref, kw_ref, vw_ref, z_ref, kc_ref, vc_ref, cidx_ref, tidx_ref,
                tab_ref, g_ref, et_ref, ov_ref, wa_ref, wb_ref, wo_ref, o_ref, wcast_ref,
                ksaug_ref, qs_ref, qaug_ref, m_ref, l_ref, acc_ref, oc_ref, ow_ref, gs_ref, t_ref, cb_ref):
    i = pl.program_id(1)
    t0 = i * TQ
    _cast_weight_block(pl.program_id(0) * pl.num_programs(1) + i, w_blocks, (wa_ref, wb_ref, wo_ref),
                       wcast_ref)

    @pl.when((pl.program_id(0) == 0) & (i == 0))
    def _():
        for h in range(N_HEADS):
            for d in range(N_GATHERED_TILES):
                for half in range(TK // LANES):
                    cs = slice(half * LANES, (half + 1) * LANES)
                    t_ref[h, d, :, cs] = _bias_lookup(tab_ref, h, tidx_ref[d, :, cs])
            for blk in range(cidx_ref.shape[0] // TQ):
                rows = slice(blk * TQ, (blk + 1) * TQ)
                cb_ref[h, rows, :] = _bias_lookup(tab_ref, h, cidx_ref[rows, :])
            t_ref[h, T_NONE] = jnp.full((TQ, TK), NEG, F32)

    @pl.when(i == 0)
    def _():
        for g in range(KV_GROUPS):
            ksaug_ref[g, :, :HEAD_DIM] = ks_ref[:, g * HEAD_DIM:(g + 1) * HEAD_DIM]
            ksaug_ref[g, :, HEAD_DIM:] = et_ref[...]

    cmaskf = (lax.broadcasted_iota(jnp.int32, (TQ, LANES), 1) * CMP_STRIDE + (CMP_LEN - 1)
              <= t0 + lax.broadcasted_iota(jnp.int32, (TQ, LANES), 0)).astype(F32)
    j_t = lax.broadcasted_iota(jnp.int32, (N_SEL_ROWS, TQ), 0)
    t_t = t0 + lax.broadcasted_iota(jnp.int32, (N_SEL_ROWS, TQ), 1)
    cur_t = t_t // SEL_LEN
    forced_t = (j_t == 0) | (j_t == cur_t) | (j_t == cur_t - 1)
    valid_t = j_t * SEL_LEN <= t_t

    def near_span(n_blocks):
        first = jnp.maximum(i - (n_blocks - 1), 0)
        return first, pl.ds(pl.multiple_of(first * TK, TK), n_blocks * TK)

    win_first, win_keys = near_span(WIN_BLOCKS)
    sel_first, sel_keys = near_span(SEL_NEAR_BLOCKS)

    def near_bias(heads, first, n_blocks):
        def tile(kk):
            dist = i - (first + kk)
            return jnp.where(dist == 0, T_DIAG, jnp.where(dist == 1, T_PREV,
                             jnp.where(dist == 2, T_WIN2, T_NONE)))
        return jnp.concatenate(
            [jnp.concatenate([t_ref[h, tile(kk)] for kk in range(n_blocks)], axis=1) for h in heads],
            axis=0)

    groups = tuple((g, g * HEAD_DIM, tuple(range(g * HEADS_PER_GROUP, (g + 1) * HEADS_PER_GROUP)),
                    slice(g * GROUP_ROWS, (g + 1) * GROUP_ROWS)) for g in range(KV_GROUPS))
    hrows = lambda h: slice(h * TQ, (h + 1) * TQ)
    for h in range(N_HEADS):
        q = q_ref[:, h * HEAD_DIM:(h + 1) * HEAD_DIM]
        qs_ref[hrows(h), :] = q
        qaug_ref[hrows(h), :HEAD_DIM] = q

    chain = {}

    def compressed_branch():
        cbias = jnp.concatenate([cb_ref[h, pl.ds(pl.multiple_of(t0, TQ), TQ), :] for h in range(N_HEADS)],
                                axis=0)
        sc = jnp.concatenate([_dot_nt(qs_ref[grows, :], kc_ref[0, :, glo:glo + HEAD_DIM])
                              for g, glo, heads, grows in groups], axis=0) + cbias
        e = jnp.exp2(sc - jnp.max(sc, axis=1, keepdims=True))
        p = e / jnp.sum(e, axis=1, keepdims=True) * jnp.concatenate([cmaskf] * N_HEADS, axis=0)
        imp_parts = []
        for g, glo, heads, grows in groups:
            oc_ref[grows, :] = _dot(p[grows].astype(BF16), vc_ref[0, :, glo:glo + HEAD_DIM])
            psum = p[hrows(heads[0])]
            for h in heads[1:]:
                psum = psum + p[hrows(h)]
            p_hi = psum.astype(BF16)
            p_lo = (psum - p_hi.astype(F32)).astype(BF16)
            imp_parts.append((_dot_nt(ov_ref[...], p_hi) + _dot_nt(ov_ref[...], p_lo))[:N_SEL_ROWS])
        chain["importance"] = jnp.concatenate(imp_parts, axis=1)

    def block_selection():
        both = lambda a: jnp.concatenate([a] * KV_GROUPS, axis=1)
        j_b = both(j_t)
        x = jnp.where(both(valid_t), jnp.where(both(forced_t), -NEG, chain["importance"]), NEG)
        groups8 = [slice(r, r + SUBLANES) for r in range(0, N_SEL_ROWS, SUBLANES)]
        cnt = [jnp.zeros((SUBLANES, x.shape[1]), F32) for _ in groups8]
        for jp in range(N_SEL_ROWS):
            row = x[jp:jp + 1, :]
            for n, rows8 in enumerate(groups8):
                if rows8.start > jp:
                    beats = row >= x[rows8]
                elif rows8.stop - 1 <= jp:
                    beats = row > x[rows8]
                else:
                    beats = (row > x[rows8]) | ((row == x[rows8]) & (j_b[rows8] > jp))
                cnt[n] = cnt[n] + jnp.where(beats, 1.0, 0.0)
        unsel_t = jnp.where(jnp.concatenate(cnt, axis=0) < SEL_TOPK, 0.0, 1.0)
        unsel_t = jnp.concatenate([unsel_t, jnp.zeros((LANES - N_SEL_ROWS, x.shape[1]), F32)], axis=0)
        unsel = unsel_t.T.astype(BF16)
        for h in range(N_HEADS):
            g = h // HEADS_PER_GROUP
            qaug_ref[hrows(h), HEAD_DIM:] = unsel[g * TQ:(g + 1) * TQ]

    def near_scores(job, _):
        branch, (g, glo, heads, grows) = job
        if branch == "window":
            return (_dot_nt(qs_ref[grows, :], kw_ref[win_keys, glo:glo + HEAD_DIM])
                    + near_bias(heads, win_first, WIN_BLOCKS))
        return (_dot_nt(qaug_ref[grows, :], ksaug_ref[g, sel_keys, :])
                + near_bias(heads, sel_first, SEL_NEAR_BLOCKS))

    def near_weights(job, s):
        branch, (g, glo, heads, grows) = job
        if branch == "window":
            p = jnp.exp2(s - jnp.max(s, axis=1, keepdims=True))
            return p.astype(BF16), _lane_fold(p)
        return _first_weights(s, m_ref, l_ref, grows)

    def near_values(job, weights):
        branch, (g, glo, heads, grows) = job
        if branch == "selected":
            acc_ref[grows, :] = _dot(weights, vs_ref[sel_keys, glo:glo + HEAD_DIM])
            return
        p, l = weights
        o_win = _dot(p, vw_ref[win_keys, glo:glo + HEAD_DIM]) / jnp.sum(l, axis=1, keepdims=True)
        for n, h in enumerate(heads):
            gl = GATE_LANE0 + h * N_BRANCHES
            sz = _silu(z_ref[:, h * HEAD_DIM:(h + 1) * HEAD_DIM].astype(F32))
            gs_ref[hrows(h), :] = g_ref[:, gl + 1:gl + 2] * sz
            ow_ref[hrows(h), :] = (g_ref[:, gl:gl + 1] * oc_ref[hrows(h), :]
                                   + g_ref[:, gl + 2:gl + 3] * o_win[n * TQ:(n + 1) * TQ]) * sz

    _skewed((near_scores, near_weights, near_values),
            tuple((branch, group) for branch in ("window", "selected") for group in groups),
            before_tick={0: compressed_branch, 1: block_selection})

    def sel_far(off, width):
        def scores(group, _):
            g, glo, heads, grows = group
            return _dot_nt(qaug_ref[grows, :], ksaug_ref[g, pl.ds(off, width), :])

        def weights(group, s):
            g, glo, heads, grows = group
            return _next_weights(s, m_ref, l_ref, grows)

        def values(group, rescale_and_weights):
            g, glo, heads, grows = group
            alpha, p = rescale_and_weights
            acc_ref[grows, :] = alpha * acc_ref[grows, :] + _dot(p, vs_ref[pl.ds(off, width), glo:glo + HEAD_DIM])

        _skewed((scores, weights, values), groups)

    def sel_far_pair(j, carry):
        sel_far(pl.multiple_of(j * FAR_TK, FAR_TK), FAR_TK)
        return carry

    lax.fori_loop(0, sel_first // 2, sel_far_pair, 0)

    @pl.when(sel_first % 2 == 1)
    def _():
        sel_far(pl.multiple_of((sel_first - 1) * TK, TK), TK)

    for h in range(N_HEADS):
        o_ref[:, h * HEAD_DIM:(h + 1) * HEAD_DIM] = (
            ow_ref[hrows(h), :] + gs_ref[hrows(h), :] * _softmax_finish(l_ref, acc_ref, hrows(h))
        ).astype(o_ref.dtype)


def _nsa(h_main, k_cmp, v_cmp, cmp_idx, tile_idx, table, gates, e_t, ov_t, weights, batch, seq):
    nq = seq // TQ
    w_blocks, w_in_specs, w_out_spec, w_out_shape = _weight_block_specs(
        weights, batch * nq, lambda b, i: b * nq + i)
    wblk = lambda name: _MAIN_OFF[name] // WIDTH
    kvblk = lambda name: _MAIN_OFF[name] // KV_WIDTH
    full = lambda shape: pl.BlockSpec(shape, lambda b, i: (0,) * len(shape),
                                      pipeline_mode=pl.Buffered(1))
    kv_spec = lambda name: pl.BlockSpec((seq, KV_WIDTH), lambda b, i: (b, kvblk(name)))
    n_chunks = k_cmp.shape[1]
    stat = pltpu.VMEM((N_HEADS * TQ, LANES), F32)
    return pl.pallas_call(
        functools.partial(_nsa_kernel, w_blocks),
        grid=(batch, nq),
        in_specs=[
            pl.BlockSpec((TQ, WIDTH), lambda b, i: (b * nq + i, wblk("nsa_q"))),
            kv_spec("nsa_k_sel"), kv_spec("nsa_v_sel"), kv_spec("nsa_k_win"), kv_spec("nsa_v_win"),
            pl.BlockSpec((TQ, WIDTH), lambda b, i: (b * nq + i, wblk("nsa_z"))),
            pl.BlockSpec((1, n_chunks, KV_WIDTH), lambda b, i: (b, 0, 0)),
            pl.BlockSpec((1, n_chunks, KV_WIDTH), lambda b, i: (b, 0, 0)),
            full(cmp_idx.shape), full(tile_idx.shape), full(table.shape),
            pl.BlockSpec((TQ, LANES), lambda b, i: (b * nq + i, 0)),
            full(e_t.shape), full(ov_t.shape),
            *w_in_specs,
        ],
        out_specs=[pl.BlockSpec((TQ, WIDTH), lambda b, i: (b * nq + i, 0)), w_out_spec],
        out_shape=[jax.ShapeDtypeStruct((batch * seq, WIDTH), BF16), w_out_shape],
        scratch_shapes=[pltpu.VMEM((KV_GROUPS, seq, MXU_DEPTH), BF16),
                        pltpu.VMEM((N_HEADS * TQ, HEAD_DIM), BF16),
                        pltpu.VMEM((N_HEADS * TQ, MXU_DEPTH), BF16),
                        stat, stat, stat,
                        stat, stat, stat,
                        pltpu.VMEM((N_HEADS, T_NONE + 1, TQ, TK), F32),
                        pltpu.VMEM((N_HEADS, seq, LANES), F32)],
        compiler_params=pltpu.CompilerParams(
            dimension_semantics=("arbitrary", "arbitrary"), vmem_limit_bytes=VMEM_LIMIT),
        name="nsa",
    )(h_main, h_main, h_main, h_main, h_main, h_main, k_cmp, v_cmp, cmp_idx, tile_idx, table, gates,
      e_t, ov_t, *weights)


OUT_TM = 512
OUT_TN = D_MODEL
OUT_ROW_GROUPS = 2


def _out_kernel(alpha, ua_ref, ub_ref, ga_ref, gb_ref, x_ref, wa_ref, wb_ref, wo_ref, lg_ref, lb_ref,
                o_ref, merged_ref):
    d_model = o_ref.shape[1]
    chunks = tuple(slice(c * OUT_TN, (c + 1) * OUT_TN) for c in range(d_model // OUT_TN))
    half = o_ref.shape[0] // OUT_ROW_GROUPS

    def merge(rows, _):
        for cs in chunks:
            ya = _dot(ua_ref[rows, :], wa_ref[:, cs])
            yb = _dot(ub_ref[rows, :], wb_ref[:, cs])
            merged_ref[rows, cs] = (_sigmoid(ga_ref[rows, cs].astype(F32)) * ya
                                    + _sigmoid(gb_ref[rows, cs].astype(F32)) * yb).astype(merged_ref.dtype)

    def residual(rows, _):
        total = jnp.zeros((half, LANES), F32)
        for cs in chunks:
            y = alpha * x_ref[rows, cs] + _dot(merged_ref[rows, :], wo_ref[:, cs])
            o_ref[rows, cs] = y
            total = total + _lane_fold(y)
        return total

    def layer_norm(rows, total):
        mu = jnp.sum(total, axis=1, keepdims=True) * (1.0 / d_model)
        sq = jnp.zeros((half, LANES), F32)
        for cs in chunks:
            d = o_ref[rows, cs] - mu
            sq = sq + _lane_fold(d * d)
        rstd = lax.rsqrt(jnp.sum(sq, axis=1, keepdims=True) * (1.0 / d_model) + LN_EPS)
        for cs in chunks:
            o_ref[rows, cs] = (o_ref[rows, cs] - mu) * rstd * lg_ref[:, cs] + lb_ref[:, cs]

    _skewed((merge, residual, layer_norm),
            tuple(slice(r * half, (r + 1) * half) for r in range(OUT_ROW_GROUPS)))


def _out(u_a, u_b, h_main, x2, w_cat, ln_g, ln_b, alpha, tm=OUT_TM):
    m = x2.shape[0]
    mblk = lambda name: _MAIN_OFF[name] // D_MODEL
    const = lambda shape: pl.BlockSpec(shape, lambda i: (0, 0), pipeline_mode=pl.Buffered(1))
    assert w_cat.shape == (2 * WIDTH + D_MODEL, D_MODEL) and D_MODEL == 2 * WIDTH
    w_part = lambda rows, index: pl.BlockSpec((rows, D_MODEL), lambda i: (index, 0),
                                              pipeline_mode=pl.Buffered(1))
    return pl.pallas_call(
        functools.partial(_out_kernel, alpha),
        grid=(m // tm,),
        in_specs=[
            pl.BlockSpec((tm, WIDTH), lambda i: (i, 0)),
            pl.BlockSpec((tm, WIDTH), lambda i: (i, 0)),
            pl.BlockSpec((tm, D_MODEL), lambda i: (i, mblk("merge_a"))),
            pl.BlockSpec((tm, D_MODEL), lambda i: (i, mblk("merge_b"))),
            pl.BlockSpec((tm, D_MODEL), lambda i: (i, 0)),
            w_part(WIDTH, 0), w_part(WIDTH, 1), w_part(D_MODEL, 1),
            const(ln_g.shape), const(ln_b.shape),
        ],
        out_specs=pl.BlockSpec((tm, D_MODEL), lambda i: (i, 0)),
        out_shape=jax.ShapeDtypeStruct((m, D_MODEL), F32),
        scratch_shapes=[pltpu.VMEM((tm, D_MODEL), BF16)],
        compiler_params=pltpu.CompilerParams(
            dimension_semantics=("parallel",), vmem_limit_bytes=VMEM_LIMIT),
        name="out",
    )(u_a, u_b, h_main, h_main, x2, w_cat, w_cat, w_cat, ln_g, ln_b)


def _bucket_np(dist):
    n = np.maximum(dist, 0)
    exact = REL_BUCKETS // 2
    large = exact + (np.log(np.maximum(n, 1).astype(np.float32) / exact)
                     / math.log(REL_MAX_DIST / exact) * (REL_BUCKETS - exact)).astype(np.int32)
    return np.where(n < exact, n, np.minimum(large, REL_BUCKETS - 1)).astype(np.int32)


@functools.lru_cache(maxsize=None)
def _static_tables(seq):
    r = np.arange(TQ)[:, None]
    c = np.arange(TK)[None, :]
    tile_idx = np.stack([_bucket_np(r - c), _bucket_np(TQ + r - c), _bucket_np(2 * TQ + r - c)])
    tile_ok = np.stack([c <= r, np.ones((TQ, TK), bool), (2 * TQ + r - c) < WINDOW])
    tile_idx = np.where(tile_ok, tile_idx, MASK_BUCKET).astype(np.int32)
    t = np.arange(seq)[:, None]
    cblk = np.arange(LANES)[None, :]
    blk_end = cblk * CMP_STRIDE + CMP_LEN - 1
    n_cmp = (seq - CMP_LEN) // CMP_STRIDE + 1
    cmp_idx = np.where((blk_end <= t) & (cblk < n_cmp), _bucket_np(t - blk_end), MASK_BUCKET).astype(np.int32)
    e_t = np.where((np.arange(seq)[:, None] // SEL_LEN) == np.arange(LANES)[None, :], NEG, 0.0)
    cs = (np.arange(LANES) * CMP_STRIDE)[None, :]
    ss = (np.arange(LANES) * SEL_LEN)[:, None]
    ov_t = ((cs < ss + SEL_LEN) & (cs + CMP_LEN > ss)
            & (np.arange(LANES)[None, :] < n_cmp) & (np.arange(LANES)[:, None] < seq // SEL_LEN))
    return tile_idx, cmp_idx, e_t.astype(np.float32), ov_t.astype(np.float32)


def _layer(x, w_in, b_f, cmp_pos_k, cmp_pos_v, cmp_wk1, cmp_wk2, cmp_wv1, cmp_wv2,
           w_a, w_b, w_o, ln_g, ln_b, rel_bias, alpha):
    batch, seq, d_model = x.shape
    assert d_model == D_MODEL and seq % FOX_TQ == 0 and seq >= WIN_BLOCKS * TK and REL_MAX_DIST <= TQ
    assert seq // SEL_LEN == N_SEL_ROWS and seq // CMP_STRIDE == LANES
    x2 = x.reshape(batch * seq, d_model)

    w_t = jnp.swapaxes(w_in, 0, 1)
    trows = lambda name: w_t[_REF_OFF[name][0]:_REF_OFF[name][0] + _REF_OFF[name][1]]
    n_small = N_HEADS + N_HEADS * N_BRANCHES
    w_small_t = jnp.concatenate(
        [trows("fox_f"), trows("nsa_gate"), jnp.zeros((LANES - n_small, d_model), F32)], axis=0)
    bf_row = jnp.concatenate([b_f.astype(F32), jnp.zeros((LANES - N_HEADS,), F32)]).reshape(1, LANES)

    h_main, h_small = _proj(x2, w_t, w_small_t)
    c_col, gates = _gates(h_small, bf_row, batch, seq)
    u_a = _fox(h_main, c_col, batch, seq)

    k_cmp, v_cmp = _compress(h_main, cmp_pos_k, cmp_pos_v, cmp_wk1, cmp_wk2, cmp_wv1, cmp_wv2, batch, seq)

    tile_idx, cmp_idx, e_t, ov_t = _static_tables(seq)
    bias = rel_bias.T.astype(F32)
    table = jnp.concatenate(
        [(bias - bias[:, REL_BUCKETS - 1:]) * LOG2E, jnp.full((N_HEADS, 1), NEG, F32),
         jnp.zeros((N_HEADS, LANES - REL_BUCKETS - 1), F32)], axis=1)
    u_b, w_cat = _nsa(h_main, k_cmp, v_cmp, jnp.asarray(cmp_idx), jnp.asarray(tile_idx), table, gates,
                      jnp.asarray(e_t, BF16), jnp.asarray(ov_t, BF16), (w_a, w_b, w_o), batch, seq)

    out = _out(u_a, u_b, h_main, x2, w_cat, ln_g.reshape(1, d_model), ln_b.reshape(1, d_model), alpha)
    return out.reshape(batch, seq, d_model)


def kernel(x, w_in, b_f, cmp_pos_k, cmp_pos_v, cmp_wk1, cmp_wk2, cmp_wv1, cmp_wv2,
           w_a, w_b, w_o, ln_g, ln_b, rel_bias):
    depth = w_in.shape[0]
    alpha = (2 * depth) ** 0.25
    for layer in range(depth):
        x = _layer(x, w_in[layer], b_f[layer], cmp_pos_k[layer], cmp_pos_v[layer], cmp_wk1[layer],
                   cmp_wk2[layer], cmp_wv1[layer], cmp_wv2[layer], w_a[layer], w_b[layer], w_o[layer],
                   ln_g[layer], ln_b[layer], rel_bias, alpha)
    return x
```

```python
import functools
import math

import jax
import jax.numpy as jnp
import numpy as np
from jax import lax
from jax.experimental import pallas as pl
from jax.experimental.pallas import tpu as pltpu

F32 = jnp.float32
BF16 = jnp.bfloat16

D_MODEL = 2048
HEAD_DIM = 128
N_HEADS = 8
WIDTH = N_HEADS * HEAD_DIM
KV_GROUPS = 2
HEADS_PER_GROUP = N_HEADS // KV_GROUPS
KV_WIDTH = KV_GROUPS * HEAD_DIM
N_BRANCHES = 3
CMP_LEN = 32
CMP_STRIDE = 16
CMP_HIDDEN = 256
SEL_LEN = 64
SEL_TOPK = 8
WINDOW = 512
REL_BUCKETS = 32
REL_MAX_DIST = 128
LN_EPS = 1e-5
NEG = -1e30
LOG2E = math.log2(math.e)
Q_PRESCALE = HEAD_DIM ** -0.5 * LOG2E

LANES = 128
SUBLANES = 8
MXU_DEPTH = 256
VMEM_LIMIT = 60 * 1024 * 1024

_REF_LAYOUT = (
    ("fox_q", WIDTH), ("fox_k", WIDTH), ("fox_v", WIDTH), ("fox_f", N_HEADS), ("fox_z", WIDTH),
    ("nsa_q", WIDTH), ("nsa_k_cmp", KV_WIDTH), ("nsa_v_cmp", KV_WIDTH), ("nsa_k_sel", KV_WIDTH),
    ("nsa_v_sel", KV_WIDTH), ("nsa_k_win", KV_WIDTH), ("nsa_v_win", KV_WIDTH),
    ("nsa_gate", N_HEADS * N_BRANCHES), ("nsa_z", WIDTH), ("merge_a", D_MODEL), ("merge_b", D_MODEL),
)
_REF_OFF = {}
_o = 0
for _n, _w in _REF_LAYOUT:
    _REF_OFF[_n] = (_o, _w)
    _o += _w

_MAIN_ORDER = ("fox_q", "fox_k", "fox_v", "fox_z", "nsa_q", "nsa_z", "merge_a", "merge_b",
               "nsa_k_cmp", "nsa_v_cmp", "nsa_k_sel", "nsa_v_sel", "nsa_k_win", "nsa_v_win")
_QUERY_COLS = ("fox_q", "nsa_q")
_MAIN_OFF = {}
_o = 0
for _n in _MAIN_ORDER:
    _MAIN_OFF[_n] = _o
    _o += _REF_OFF[_n][1]
MAIN_COLS = _o
GATE_LANE0 = N_HEADS


def _dot(a, b):
    return jnp.dot(a, b, preferred_element_type=F32)


def _dot_nt(a, b):
    return lax.dot_general(a, b, (((1,), (1,)), ((), ())), preferred_element_type=F32)


def _sigmoid(x):
    return 0.5 + 0.5 * jnp.tanh(0.5 * x)


def _silu(x):
    return x * _sigmoid(x)


PROJ_TN = 512


def _proj_tiles():
    rows, is_query = [], []
    for name in _MAIN_ORDER:
        off, width = _REF_OFF[name]
        start = _MAIN_OFF[name]
        for c in range(start, start + width):
            if c % PROJ_TN == 0:
                assert (off + c - start) % SUBLANES == 0
                rows.append((off + c - start) // SUBLANES)
                is_query.append(int(name in _QUERY_COLS))
    return np.asarray(rows, np.int32), np.asarray(is_query, np.int32)


PROJ_ROWS = 4096
PROJ_RB = 512


def _proj_kernel(rows_ref, isq_ref, x_hbm, wt_ref, wst_ref, o_ref, os_ref,
                 xb_ref, stage_ref, wb_ref, sem):
    i = pl.program_id(0)
    j = pl.program_id(1)
    n_blocks = PROJ_ROWS // PROJ_RB
    wb_ref[...] = wt_ref[...].astype(BF16)
    scale = jnp.where(isq_ref[j] == 1, Q_PRESCALE, 1.0)

    def x_copy(r, slot):
        row0 = pl.multiple_of(i * PROJ_ROWS + r * PROJ_RB, PROJ_RB)
        return pltpu.make_async_copy(x_hbm.at[pl.ds(row0, PROJ_RB), :], stage_ref.at[slot], sem.at[slot])

    def block(r):
        rows = slice(r * PROJ_RB, (r + 1) * PROJ_RB)
        o_ref[rows, :] = (_dot_nt(xb_ref[rows, :], wb_ref[...]) * scale).astype(o_ref.dtype)

    @pl.when(j == 0)
    def _():
        wsb = wst_ref[...].astype(BF16)
        x_copy(0, 0).start()
        for r in range(n_blocks):
            if r + 1 < n_blocks:
                x_copy(r + 1, (r + 1) % 2).start()
            x_copy(r, r % 2).wait()
            rows = slice(r * PROJ_RB, (r + 1) * PROJ_RB)
            xb_ref[rows, :] = stage_ref[r % 2].astype(BF16)
            os_ref[rows, :] = _dot_nt(xb_ref[rows, :], wsb)
            block(r)

    @pl.when(j > 0)
    def _():
        for r in range(n_blocks):
            block(r)


def _proj(x2, w_t, w_small_t):
    m, k = x2.shape
    rows, is_query = _proj_tiles()
    assert MAIN_COLS % PROJ_TN == 0 and len(rows) == MAIN_COLS // PROJ_TN and m % PROJ_ROWS == 0
    grid_spec = pltpu.PrefetchScalarGridSpec(
        num_scalar_prefetch=2,
        grid=(m // PROJ_ROWS, len(rows)),
        in_specs=[
            pl.BlockSpec(memory_space=pl.ANY),
            pl.BlockSpec((pl.Element(PROJ_TN), pl.Element(k)),
                         lambda i, j, rows, isq: (rows[j] * SUBLANES, 0)),
            pl.BlockSpec((LANES, k), lambda i, j, rows, isq: (0, 0)),
        ],
        out_specs=[
            pl.BlockSpec((PROJ_ROWS, PROJ_TN), lambda i, j, rows, isq: (i, j)),
            pl.BlockSpec((PROJ_ROWS, LANES), lambda i, j, rows, isq: (i, 0)),
        ],
        scratch_shapes=[pltpu.VMEM((PROJ_ROWS, k), BF16),
                        pltpu.VMEM((2, PROJ_RB, k), F32),
                        pltpu.VMEM((PROJ_TN, k), BF16),
                        pltpu.SemaphoreType.DMA((2,))],
    )
    return pl.pallas_call(
        _proj_kernel,
        grid_spec=grid_spec,
        out_shape=[jax.ShapeDtypeStruct((m, MAIN_COLS), BF16), jax.ShapeDtypeStruct((m, LANES), F32)],
        compiler_params=pltpu.CompilerParams(
            dimension_semantics=("parallel", "arbitrary"), vmem_limit_bytes=VMEM_LIMIT),
        name="proj",
    )(jnp.asarray(rows), jnp.asarray(is_query), x2, w_t, w_small_t)


_CUM_CHUNK = 256
_N_SPLIT = 3
ONES_LANE = _N_SPLIT * N_HEADS


def _split3(x):
    hi = x.astype(BF16)
    r1 = x - hi.astype(F32)
    mid = r1.astype(BF16)
    lo = (r1 - mid.astype(F32)).astype(BF16)
    return hi, mid, lo


def _gate_kernel(hs_ref, bf_ref, c_ref, g_ref):
    hs = hs_ref[...]
    g_ref[...] = _sigmoid(hs)
    z = hs + bf_ref[...]
    logf = jnp.minimum(z, 0.0) - jnp.log1p(jnp.exp(-jnp.abs(z)))
    n = _CUM_CHUNK
    tri = (lax.broadcasted_iota(jnp.int32, (n, n), 1)
           <= lax.broadcasted_iota(jnp.int32, (n, n), 0)).astype(BF16)
    lane = lax.broadcasted_iota(jnp.int32, (n, LANES), 1)
    carry = jnp.zeros((1, LANES), F32)
    for blk in range(hs.shape[0] // n):
        hi, mid, lo = _split3(logf[blk * n:(blk + 1) * n])
        cb = _dot(tri, hi) + _dot(tri, mid) + _dot(tri, lo) + carry
        carry = cb[n - 1:n, :]
        hi, mid, lo = _split3(cb * LOG2E)
        packed = jnp.where((lane >= ONES_LANE) & (lane < ONES_LANE + _N_SPLIT), 1.0, 0.0)
        for t, term in enumerate((hi, mid, lo)):
            shifted = term.astype(F32) if t == 0 else pltpu.roll(term.astype(F32), t * N_HEADS, 1)
            packed = jnp.where((lane >= t * N_HEADS) & (lane < (t + 1) * N_HEADS), shifted, packed)
        c_ref[blk * n:(blk + 1) * n, :] = packed.astype(c_ref.dtype)


def _gates(h_small, bf_row, batch, seq):
    return pl.pallas_call(
        _gate_kernel,
        grid=(batch,),
        in_specs=[pl.BlockSpec((seq, LANES), lambda b: (b, 0)),
                  pl.BlockSpec((1, LANES), lambda b: (0, 0))],
        out_specs=[pl.BlockSpec((seq, LANES), lambda b: (b, 0)),
                   pl.BlockSpec((seq, LANES), lambda b: (b, 0))],
        out_shape=[jax.ShapeDtypeStruct(h_small.shape, BF16), jax.ShapeDtypeStruct(h_small.shape, F32)],
        compiler_params=pltpu.CompilerParams(dimension_semantics=("parallel",)),
        name="gates",
    )(h_small, bf_row)


_HALF = CMP_LEN // 2


def _gelu_tanh(x):
    return 0.5 * x * (1.0 + jnp.tanh(math.sqrt(2.0 / math.pi) * (x + 0.044715 * (x * x * x))))


def _compress_kernel(raw_ref, pk_ref, pv_ref, w1k_ref, w2k_ref, w1v_ref, w2v_ref, kc_ref, vc_ref, raw32_ref):
    seq = raw_ref.shape[0]
    n_chunks = seq // _HALF
    step = 512
    for slab in range(raw_ref.shape[1] // HEAD_DIM):
        for blk in range(seq // step):
            raw32_ref[slab, blk * step:(blk + 1) * step, :] = (
                raw_ref[blk * step:(blk + 1) * step, slab * HEAD_DIM:(slab + 1) * HEAD_DIM].astype(F32))
    for kv, (pos_ref, w1_ref, w2_ref, out_ref) in enumerate(
            ((pk_ref, w1k_ref, w2k_ref, kc_ref), (pv_ref, w1v_ref, w2v_ref, vc_ref))):
        first = jnp.zeros((KV_GROUPS * n_chunks, CMP_HIDDEN), F32)
        second = jnp.zeros((KV_GROUPS * n_chunks, CMP_HIDDEN), F32)
        def tokens(l, pos_row):
            a = jnp.concatenate([raw32_ref[kv * KV_GROUPS + g, pl.ds(l, n_chunks, stride=_HALF), :]
                                 for g in range(KV_GROUPS)], axis=0)
            return (a + pos_ref[pos_row:pos_row + 1, :]).astype(BF16)

        for l in range(0, _HALF, 2):
            rows = slice(l * HEAD_DIM, (l + 2) * HEAD_DIM)
            rows2 = slice((_HALF + l) * HEAD_DIM, (_HALF + l + 2) * HEAD_DIM)
            first += _dot(jnp.concatenate([tokens(l, l), tokens(l + 1, l + 1)], axis=1),
                          w1_ref[rows, :].astype(BF16))
            second += _dot(jnp.concatenate([tokens(l, _HALF + l), tokens(l + 1, _HALF + l + 1)], axis=1),
                           w1_ref[rows2, :].astype(BF16))
        hid = first + jnp.concatenate(
            [pltpu.roll(second[g * n_chunks:(g + 1) * n_chunks], n_chunks - 1, 0) for g in range(KV_GROUPS)],
            axis=0)
        out = _dot(_gelu_tanh(hid).astype(BF16), w2_ref[...].astype(BF16))
        for g in range(KV_GROUPS):
            out_ref[0, :, g * HEAD_DIM:(g + 1) * HEAD_DIM] = (
                out[g * n_chunks:(g + 1) * n_chunks].astype(out_ref.dtype))


def _compress(h_main, pos_k, pos_v, w1k, w2k, w1v, w2v, batch, seq):
    n_chunks = seq // _HALF
    raw_cols = 2 * KV_WIDTH
    assert _MAIN_OFF["nsa_v_cmp"] == _MAIN_OFF["nsa_k_cmp"] + KV_WIDTH
    raw_blk = _MAIN_OFF["nsa_k_cmp"] // raw_cols
    full = lambda shape: pl.BlockSpec(shape, lambda b: (0,) * len(shape), pipeline_mode=pl.Buffered(1))
    return pl.pallas_call(
        _compress_kernel,
        grid=(batch,),
        in_specs=[pl.BlockSpec((seq, raw_cols), lambda b: (b, raw_blk)),
                  full(pos_k.shape), full(pos_v.shape),
                  full(w1k.shape), full(w2k.shape), full(w1v.shape), full(w2v.shape)],
        out_specs=[pl.BlockSpec((1, n_chunks, KV_WIDTH), lambda b: (b, 0, 0)),
                   pl.BlockSpec((1, n_chunks, KV_WIDTH), lambda b: (b, 0, 0))],
        out_shape=[jax.ShapeDtypeStruct((batch, n_chunks, KV_WIDTH), BF16),
                   jax.ShapeDtypeStruct((batch, n_chunks, KV_WIDTH), BF16)],
        scratch_shapes=[pltpu.VMEM((raw_cols // HEAD_DIM, seq, HEAD_DIM), F32)],
        compiler_params=pltpu.CompilerParams(
            dimension_semantics=("parallel",), vmem_limit_bytes=VMEM_LIMIT),
        name="compress",
    )(h_main, pos_k, pos_v, w1k, w2k, w1v, w2v)


def _lane_tile(x, n):
    return x if n == 1 else jnp.concatenate([x] * n, axis=1)


def _lane_fold(p):
    out = p[:, :LANES]
    for t in range(1, p.shape[1] // LANES):
        out = out + p[:, t * LANES:(t + 1) * LANES]
    return out


def _first_weights(s, m_ref, l_ref, rows=slice(None)):
    m = jnp.broadcast_to(jnp.max(s, axis=1, keepdims=True), (s.shape[0], LANES))
    p = jnp.exp2(s - _lane_tile(m, s.shape[1] // LANES))
    m_ref[rows, :] = m
    l_ref[rows, :] = _lane_fold(p)
    return p.astype(BF16)


def _next_weights(s, m_ref, l_ref, rows=slice(None)):
    m_prev = m_ref[rows, :]
    m_new = jnp.maximum(m_prev, jnp.max(s, axis=1, keepdims=True))
    alpha = jnp.exp2(m_prev - m_new)
    p = jnp.exp2(s - _lane_tile(m_new, s.shape[1] // LANES))
    l_ref[rows, :] = alpha * l_ref[rows, :] + _lane_fold(p)
    m_ref[rows, :] = m_new
    return alpha, p.astype(BF16)


def _skewed(stages, jobs, before_tick=None):
    state = [None] * len(jobs)
    for tick in range(len(jobs) + len(stages) - 1):
        if before_tick and tick in before_tick:
            before_tick[tick]()
        for k, stage in reversed(list(enumerate(stages))):
            j = tick - k
            if 0 <= j < len(jobs):
                state[j] = stage(jobs[j], state[j])


def _softmax_finish(l_ref, acc_ref, rows=slice(None)):
    return acc_ref[rows, :] / jnp.sum(l_ref[rows, :], axis=1, keepdims=True)


FOX_TQ = 512
FOX_HALF = FOX_TQ // 2


def _fox_routing():
    pk = np.zeros((N_HEADS, LANES, LANES), np.float32)
    for h in range(N_HEADS):
        for t in range(_N_SPLIT):
            pk[h, ONES_LANE, t * N_HEADS + h] = 1.0
            pk[h, t * N_HEADS + h, ONES_LANE + t] = -1.0
    return pk


def _cast_weight_block(step, w_blocks, w_refs, wcast_ref):
    first = 0
    for w_ref, n_blocks in zip(w_refs, w_blocks):
        @pl.when((step >= first) & (step < first + n_blocks))
        def _(w_ref=w_ref):
            wcast_ref[...] = w_ref[...].astype(wcast_ref.dtype)
        first += n_blocks


def _weight_block_specs(weights, n_steps, step_of):
    w_rows = sum(w.shape[0] for w in weights)
    rows_per_step = w_rows // n_steps
    assert w_rows % n_steps == 0 and all(w.shape[0] % rows_per_step == 0 for w in weights)
    w_blocks = tuple(w.shape[0] // rows_per_step for w in weights)
    in_specs, first = [], 0
    for n_blocks in w_blocks:
        in_specs.append(pl.BlockSpec(
            (rows_per_step, D_MODEL),
            lambda b, i, first=first, n_blocks=n_blocks: (jnp.clip(step_of(b, i) - first, 0, n_blocks - 1), 0)))
        first += n_blocks
    out_spec = pl.BlockSpec((rows_per_step, D_MODEL), lambda b, i: (step_of(b, i), 0))
    return w_blocks, in_specs, out_spec, jax.ShapeDtypeStruct((w_rows, D_MODEL), BF16)


def _fox_kernel(q_ref, k_ref, v_ref, z_ref, cq_ref, ck_ref, pk_ref, o_ref,
                kaug_ref, qaug_ref, sz_ref, m_ref, l_ref, acc_ref):
    i = pl.program_id(1)
    tq, half = FOX_TQ, FOX_HALF
    seq = k_ref.shape[0]

    @pl.when(i == 0)
    def _():
        for h in range(N_HEADS):
            for blk in range(seq // tq):
                rows = slice(blk * tq, (blk + 1) * tq)
                kaug_ref[h, rows, :HEAD_DIM] = k_ref[rows, h * HEAD_DIM:(h + 1) * HEAD_DIM]
                kaug_ref[h, rows, HEAD_DIM:] = _dot(ck_ref[rows, :], pk_ref[h]).astype(BF16)

    causal = (lax.broadcasted_iota(jnp.int32, (half, half), 1)
              <= lax.broadcasted_iota(jnp.int32, (half, half), 0))
    diag = pl.multiple_of(i * tq, tq)
    diag2 = pl.multiple_of(i * tq + half, half)
    heads = tuple(range(N_HEADS))
    rows = lambda h: slice(h * tq, (h + 1) * tq)
    top = lambda h: slice(h * tq, h * tq + half)
    bot = lambda h: slice(h * tq + half, (h + 1) * tq)
    cols = lambda h: slice(h * HEAD_DIM, (h + 1) * HEAD_DIM)

    def diag_scores(h, _):
        qaug_ref[rows(h), :HEAD_DIM] = q_ref[:, cols(h)]
        qaug_ref[rows(h), HEAD_DIM:] = cq_ref[...]
        s_left = _dot_nt(qaug_ref[rows(h), :], kaug_ref[h, pl.ds(diag, half), :])
        s_right = _dot_nt(qaug_ref[bot(h), :], kaug_ref[h, pl.ds(diag2, half), :])
        return s_left, s_right

    def diag_weights(h, scores):
        s_left, s_right = scores
        p_top = _first_weights(jnp.where(causal, s_left[:half], NEG), m_ref, l_ref, top(h))
        p_bot = _first_weights(jnp.concatenate([s_left[half:], jnp.where(causal, s_right, NEG)], axis=1),
                               m_ref, l_ref, bot(h))
        return p_top, p_bot

    def diag_values(h, weights):
        p_top, p_bot = weights
        acc_ref[top(h), :] = _dot(p_top, v_ref[pl.ds(diag, half), cols(h)])
        acc_ref[bot(h), :] = _dot(p_bot, v_ref[pl.ds(diag, tq), cols(h)])

    def output_gate(h, _):
        sz_ref[:, cols(h)] = _silu(z_ref[:, cols(h)].astype(F32))

    _skewed((output_gate, diag_scores, diag_weights, diag_values), heads)

    def body(j, carry):
        off = pl.multiple_of(j * tq, tq)

        def scores(h, _):
            return _dot_nt(qaug_ref[rows(h), :], kaug_ref[h, pl.ds(off, tq), :])

        def weights(h, s):
            return _next_weights(s, m_ref, l_ref, rows(h))

        def values(h, rescale_and_weights):
            alpha, p = rescale_and_weights
            acc_ref[rows(h), :] = alpha * acc_ref[rows(h), :] + _dot(p, v_ref[pl.ds(off, tq), cols(h)])

        _skewed((scores, weights, values), heads)
        return carry

    lax.fori_loop(0, i, body, 0)
    for h in heads:
        o_ref[:, cols(h)] = (_softmax_finish(l_ref, acc_ref, rows(h)) * sz_ref[:, cols(h)]).astype(o_ref.dtype)


def _fox(h_main, c_packed, batch, seq):
    tq = FOX_TQ
    nq = seq // tq
    blk = lambda name: _MAIN_OFF[name] // WIDTH
    pk = jnp.asarray(_fox_routing(), BF16)
    full = lambda shape: pl.BlockSpec(shape, lambda b, i: (0,) * len(shape))
    stat = pltpu.VMEM((N_HEADS * tq, LANES), F32)
    return pl.pallas_call(
        _fox_kernel,
        grid=(batch, nq),
        in_specs=[
            pl.BlockSpec((tq, WIDTH), lambda b, i: (b * nq + i, blk("fox_q"))),
            pl.BlockSpec((seq, WIDTH), lambda b, i: (b, blk("fox_k"))),
            pl.BlockSpec((seq, WIDTH), lambda b, i: (b, blk("fox_v"))),
            pl.BlockSpec((tq, WIDTH), lambda b, i: (b * nq + i, blk("fox_z"))),
            pl.BlockSpec((tq, LANES), lambda b, i: (b * nq + i, 0)),
            pl.BlockSpec((seq, LANES), lambda b, i: (b, 0)),
            full(pk.shape),
        ],
        out_specs=pl.BlockSpec((tq, WIDTH), lambda b, i: (b * nq + i, 0)),
        out_shape=jax.ShapeDtypeStruct((batch * seq, WIDTH), BF16),
        scratch_shapes=[pltpu.VMEM((N_HEADS, seq, MXU_DEPTH), BF16),
                        pltpu.VMEM((N_HEADS * tq, MXU_DEPTH), BF16),
                        pltpu.VMEM((tq, WIDTH), F32),
                        stat, stat, stat],
        compiler_params=pltpu.CompilerParams(
            dimension_semantics=("parallel", "arbitrary"), vmem_limit_bytes=VMEM_LIMIT),
        name="fox",
    )(h_main, h_main, h_main, h_main, c_packed, c_packed, pk)


TQ = 256
TK = TQ
WIN_BLOCKS = (WINDOW + TQ) // TK
SEL_NEAR_BLOCKS = 2
FAR_TK = 2 * TK
N_SEL_ROWS = 32
MASK_BUCKET = REL_BUCKETS
GROUP_ROWS = HEADS_PER_GROUP * TQ
T_DIAG, T_PREV, T_WIN2, T_NONE = range(4)
N_GATHERED_TILES = 3


def _bias_lookup(tab_ref, h, idx):
    row = jnp.broadcast_to(tab_ref[h:h + 1, :], idx.shape)
    return jnp.take_along_axis(row, idx, axis=1, mode="promise_in_bounds")


def _nsa_kernel(w_blocks, q_ref, ks_ref, vs_ref, kw_ref, vw_ref, z_ref, kc_ref, vc_ref, cidx_ref, tidx_ref,
                tab_ref, g_ref, et_ref, ov_ref, wa_ref, wb_ref, wo_ref, o_ref, wcast_ref,
                ksaug_ref, qs_ref, qaug_ref, m_ref, l_ref, acc_ref, oc_ref, ow_ref, gs_ref, t_ref, cb_ref):
    i = pl.program_id(1)
    t0 = i * TQ
    _cast_weight_block(pl.program_id(0) * pl.num_programs(1) + i, w_blocks, (wa_ref, wb_ref, wo_ref),
                       wcast_ref)

    @pl.when((pl.program_id(0) == 0) & (i == 0))
    def _():
        for h in range(N_HEADS):
            for d in range(N_GATHERED_TILES):
                for half in range(TK // LANES):
                    cs = slice(half * LANES, (half + 1) * LANES)
                    t_ref[h, d, :, cs] = _bias_lookup(tab_ref, h, tidx_ref[d, :, cs])
            for blk in range(cidx_ref.shape[0] // TQ):
                rows = slice(blk * TQ, (blk + 1) * TQ)
                cb_ref[h, rows, :] = _bias_lookup(tab_ref, h, cidx_ref[rows, :])
            t_ref[h, T_NONE] = jnp.full((TQ, TK), NEG, F32)

    @pl.when(i == 0)
    def _():
        for g in range(KV_GROUPS):
            ksaug_ref[g, :, :HEAD_DIM] = ks_ref[:, g * HEAD_DIM:(g + 1) * HEAD_DIM]
            ksaug_ref[g, :, HEAD_DIM:] = et_ref[...]

    cmaskf = (lax.broadcasted_iota(jnp.int32, (TQ, LANES), 1) * CMP_STRIDE + (CMP_LEN - 1)
              <= t0 + lax.broadcasted_iota(jnp.int32, (TQ, LANES), 0)).astype(F32)
    j_t = lax.broadcasted_iota(jnp.int32, (N_SEL_ROWS, TQ), 0)
    t_t = t0 + lax.broadcasted_iota(jnp.int32, (N_SEL_ROWS, TQ), 1)
    cur_t = t_t // SEL_LEN
    forced_t = (j_t == 0) | (j_t == cur_t) | (j_t == cur_t - 1)
    valid_t = j_t * SEL_LEN <= t_t

    def near_span(n_blocks):
        first = jnp.maximum(i - (n_blocks - 1), 0)
        return first, pl.ds(pl.multiple_of(first * TK, TK), n_blocks * TK)

    win_first, win_keys = near_span(WIN_BLOCKS)
    sel_first, sel_keys = near_span(SEL_NEAR_BLOCKS)

    def near_bias(heads, first, n_blocks):
        def tile(kk):
            dist = i - (first + kk)
            return jnp.where(dist == 0, T_DIAG, jnp.where(dist == 1, T_PREV,
                             jnp.where(dist == 2, T_WIN2, T_NONE)))
        return jnp.concatenate(
            [jnp.concatenate([t_ref[h, tile(kk)] for kk in range(n_blocks)], axis=1) for h in heads],
            axis=0)

    groups = tuple((g, g * HEAD_DIM, tuple(range(g * HEADS_PER_GROUP, (g + 1) * HEADS_PER_GROUP)),
                    slice(g * GROUP_ROWS, (g + 1) * GROUP_ROWS)) for g in range(KV_GROUPS))
    hrows = lambda h: slice(h * TQ, (h + 1) * TQ)
    for h in range(N_HEADS):
        q = q_ref[:, h * HEAD_DIM:(h + 1) * HEAD_DIM]
        qs_ref[hrows(h), :] = q
        qaug_ref[hrows(h), :HEAD_DIM] = q

    chain = {}

    def compressed_branch():
        cbias = jnp.concatenate([cb_ref[h, pl.ds(pl.multiple_of(t0, TQ), TQ), :] for h in range(N_HEADS)],
                                axis=0)
        sc = jnp.concatenate([_dot_nt(qs_ref[grows, :], kc_ref[0, :, glo:glo + HEAD_DIM])
                              for g, glo, heads, grows in groups], axis=0) + cbias
        e = jnp.exp2(sc - jnp.max(sc, axis=1, keepdims=True))
        p = e / jnp.sum(e, axis=1, keepdims=True) * jnp.concatenate([cmaskf] * N_HEADS, axis=0)
        imp_parts = []
        for g, glo, heads, grows in groups:
            oc_ref[grows, :] = _dot(p[grows].astype(BF16), vc_ref[0, :, glo:glo + HEAD_DIM])
            psum = p[hrows(heads[0])]
            for h in heads[1:]:
                psum = psum + p[hrows(h)]
            p_hi = psum.astype(BF16)
            p_lo = (psum - p_hi.astype(F32)).astype(BF16)
            imp_parts.append((_dot_nt(ov_ref[...], p_hi) + _dot_nt(ov_ref[...], p_lo))[:N_SEL_ROWS])
        chain["importance"] = jnp.concatenate(imp_parts, axis=1)

    def block_selection():
        both = lambda a: jnp.concatenate([a] * KV_GROUPS, axis=1)
        j_b = both(j_t)
        x = jnp.where(both(valid_t), jnp.where(both(forced_t), -NEG, chain["importance"]), NEG)
        groups8 = [slice(r, r + SUBLANES) for r in range(0, N_SEL_ROWS, SUBLANES)]
        cnt = [jnp.zeros((SUBLANES, x.shape[1]), F32) for _ in groups8]
        for jp in range(N_SEL_ROWS):
            row = x[jp:jp + 1, :]
            for n, rows8 in enumerate(groups8):
                if rows8.start > jp:
                    beats = row >= x[rows8]
                elif rows8.stop - 1 <= jp:
                    beats = row > x[rows8]
                else:
                    beats = (row > x[rows8]) | ((row == x[rows8]) & (j_b[rows8] > jp))
                cnt[n] = cnt[n] + jnp.where(beats, 1.0, 0.0)
        unsel_t = jnp.where(jnp.concatenate(cnt, axis=0) < SEL_TOPK, 0.0, 1.0)
        unsel_t = jnp.concatenate([unsel_t, jnp.zeros((LANES - N_SEL_ROWS, x.shape[1]), F32)], axis=0)
        unsel = unsel_t.T.astype(BF16)
        for h in range(N_HEADS):
            g = h // HEADS_PER_GROUP
            qaug_ref[hrows(h), HEAD_DIM:] = unsel[g * TQ:(g + 1) * TQ]

    def near_scores(job, _):
        branch, (g, glo, heads, grows) = job
        if branch == "window":
            return (_dot_nt(qs_ref[grows, :], kw_ref[win_keys, glo:glo + HEAD_DIM])
                    + near_bias(heads, win_first, WIN_BLOCKS))
        return (_dot_nt(qaug_ref[grows, :], ksaug_ref[g, sel_keys, :])
                + near_bias(heads, sel_first, SEL_NEAR_BLOCKS))

    def near_weights(job, s):
        branch, (g, glo, heads, grows) = job
        if branch == "window":
            p = jnp.exp2(s - jnp.max(s, axis=1, keepdims=True))
            return p.astype(BF16), _lane_fold(p)
        return _first_weights(s, m_ref, l_ref, grows)

    def near_values(job, weights):
        branch, (g, glo, heads, grows) = job
        if branch == "selected":
            acc_ref[grows, :] = _dot(weights, vs_ref[sel_keys, glo:glo + HEAD_DIM])
            return
        p, l = weights
        o_win = _dot(p, vw_ref[win_keys, glo:glo + HEAD_DIM]) / jnp.sum(l, axis=1, keepdims=True)
        for n, h in enumerate(heads):
            gl = GATE_LANE0 + h * N_BRANCHES
            sz = _silu(z_ref[:, h * HEAD_DIM:(h + 1) * HEAD_DIM].astype(F32))
            gs_ref[hrows(h), :] = g_ref[:, gl + 1:gl + 2] * sz
            ow_ref[hrows(h), :] = (g_ref[:, gl:gl + 1] * oc_ref[hrows(h), :]
                                   + g_ref[:, gl + 2:gl + 3] * o_win[n * TQ:(n + 1) * TQ]) * sz

    half_groups = tuple(
        (g, glo, heads[part * 2:(part + 1) * 2], slice(heads[part * 2] * TQ, (heads[part * 2 + 1] + 1) * TQ))
        for g, glo, heads, grows in groups for part in range(HEADS_PER_GROUP // 2))
    _skewed((near_scores, near_weights, near_values),
            tuple((branch, group) for branch in ("window", "selected") for group in half_groups),
            before_tick={0: compressed_branch, 1: block_selection})

    def sel_far(off, width):
        def scores(group, _):
            g, glo, heads, grows = group
            return _dot_nt(qaug_ref[grows, :], ksaug_ref[g, pl.ds(off, width), :])

        def weights(group, s):
            g, glo, heads, grows = group
            return _next_weights(s, m_ref, l_ref, grows)

        def values(group, rescale_and_weights):
            g, glo, heads, grows = group
            alpha, p = rescale_and_weights
            acc_ref[grows, :] = alpha * acc_ref[grows, :] + _dot(p, vs_ref[pl.ds(off, width), glo:glo + HEAD_DIM])

        _skewed((scores, weights, values), groups)

    def sel_far_pair(j, carry):
        sel_far(pl.multiple_of(j * FAR_TK, FAR_TK), FAR_TK)
        return carry

    lax.fori_loop(0, sel_first // 2, sel_far_pair, 0)

    @pl.when(sel_first % 2 == 1)
    def _():
        sel_far(pl.multiple_of((sel_first - 1) * TK, TK), TK)

    for h in range(N_HEADS):
        o_ref[:, h * HEAD_DIM:(h + 1) * HEAD_DIM] = (
            ow_ref[hrows(h), :] + gs_ref[hrows(h), :] * _softmax_finish(l_ref, acc_ref, hrows(h))
        ).astype(o_ref.dtype)


def _nsa(h_main, k_cmp, v_cmp, cmp_idx, tile_idx, table, gates, e_t, ov_t, weights, batch, seq):
    nq = seq // TQ
    w_blocks, w_in_specs, w_out_spec, w_out_shape = _weight_block_specs(
        weights, batch * nq, lambda b, i: b * nq + i)
    wblk = lambda name: _MAIN_OFF[name] // WIDTH
    kvblk = lambda name: _MAIN_OFF[name] // KV_WIDTH
    full = lambda shape: pl.BlockSpec(shape, lambda b, i: (0,) * len(shape),
                                      pipeline_mode=pl.Buffered(1))
    kv_spec = lambda name: pl.BlockSpec((seq, KV_WIDTH), lambda b, i: (b, kvblk(name)))
    n_chunks = k_cmp.shape[1]
    stat = pltpu.VMEM((N_HEADS * TQ, LANES), F32)
    return pl.pallas_call(
        functools.partial(_nsa_kernel, w_blocks),
        grid=(batch, nq),
        in_specs=[
            pl.BlockSpec((TQ, WIDTH), lambda b, i: (b * nq + i, wblk("nsa_q"))),
            kv_spec("nsa_k_sel"), kv_spec("nsa_v_sel"), kv_spec("nsa_k_win"), kv_spec("nsa_v_win"),
            pl.BlockSpec((TQ, WIDTH), lambda b, i: (b * nq + i, wblk("nsa_z"))),
            pl.BlockSpec((1, n_chunks, KV_WIDTH), lambda b, i: (b, 0, 0)),
            pl.BlockSpec((1, n_chunks, KV_WIDTH), lambda b, i: (b, 0, 0)),
            full(cmp_idx.shape), full(tile_idx.shape), full(table.shape),
            pl.BlockSpec((TQ, LANES), lambda b, i: (b * nq + i, 0)),
            full(e_t.shape), full(ov_t.shape),
            *w_in_specs,
        ],
        out_specs=[pl.BlockSpec((TQ, WIDTH), lambda b, i: (b * nq + i, 0)), w_out_spec],
        out_shape=[jax.ShapeDtypeStruct((batch * seq, WIDTH), BF16), w_out_shape],
        scratch_shapes=[pltpu.VMEM((KV_GROUPS, seq, MXU_DEPTH), BF16),
                        pltpu.VMEM((N_HEADS * TQ, HEAD_DIM), BF16),
                        pltpu.VMEM((N_HEADS * TQ, MXU_DEPTH), BF16),
                        stat, stat, stat,
                        stat, stat, stat,
                        pltpu.VMEM((N_HEADS, T_NONE + 1, TQ, TK), F32),
                        pltpu.VMEM((N_HEADS, seq, LANES), F32)],
        compiler_params=pltpu.CompilerParams(
            dimension_semantics=("arbitrary", "arbitrary"), vmem_limit_bytes=VMEM_LIMIT),
        name="nsa",
    )(h_main, h_main, h_main, h_main, h_main, h_main, k_cmp, v_cmp, cmp_idx, tile_idx, table, gates,
      e_t, ov_t, *weights)


OUT_TM = 512
OUT_TN = D_MODEL
OUT_ROW_GROUPS = 2


def _out_kernel(alpha, ua_ref, ub_ref, ga_ref, gb_ref, x_ref, wa_ref, wb_ref, wo_ref, lg_ref, lb_ref,
                o_ref, merged_ref):
    d_model = o_ref.shape[1]
    chunks = tuple(slice(c * OUT_TN, (c + 1) * OUT_TN) for c in range(d_model // OUT_TN))
    half = o_ref.shape[0] // OUT_ROW_GROUPS

    def merge(rows, _):
        for cs in chunks:
            ya = _dot(ua_ref[rows, :], wa_ref[:, cs])
            yb = _dot(ub_ref[rows, :], wb_ref[:, cs])
            merged_ref[rows, cs] = (_sigmoid(ga_ref[rows, cs].astype(F32)) * ya
                                    + _sigmoid(gb_ref[rows, cs].astype(F32)) * yb).astype(merged_ref.dtype)

    def residual(rows, _):
        total = jnp.zeros((half, LANES), F32)
        for cs in chunks:
            y = alpha * x_ref[rows, cs] + _dot(merged_ref[rows, :], wo_ref[:, cs])
            o_ref[rows, cs] = y
            total = total + _lane_fold(y)
        return total

    def layer_norm(rows, total):
        mu = jnp.sum(total, axis=1, keepdims=True) * (1.0 / d_model)
        sq = jnp.zeros((half, LANES), F32)
        for cs in chunks:
            d = o_ref[rows, cs] - mu
            sq = sq + _lane_fold(d * d)
        rstd = lax.rsqrt(jnp.sum(sq, axis=1, keepdims=True) * (1.0 / d_model) + LN_EPS)
        for cs in chunks:
            o_ref[rows, cs] = (o_ref[rows, cs] - mu) * rstd * lg_ref[:, cs] + lb_ref[:, cs]

    _skewed((merge, residual, layer_norm),
            tuple(slice(r * half, (r + 1) * half) for r in range(OUT_ROW_GROUPS)))


def _out(u_a, u_b, h_main, x2, w_cat, ln_g, ln_b, alpha, tm=OUT_TM):
    m = x2.shape[0]
    mblk = lambda name: _MAIN_OFF[name] // D_MODEL
    const = lambda shape: pl.BlockSpec(shape, lambda i: (0, 0), pipeline_mode=pl.Buffered(1))
    assert w_cat.shape == (2 * WIDTH + D_MODEL, D_MODEL) and D_MODEL == 2 * WIDTH
    w_part = lambda rows, index: pl.BlockSpec((rows, D_MODEL), lambda i: (index, 0),
                                              pipeline_mode=pl.Buffered(1))
    return pl.pallas_call(
        functools.partial(_out_kernel, alpha),
        grid=(m // tm,),
        in_specs=[
            pl.BlockSpec((tm, WIDTH), lambda i: (i, 0)),
            pl.BlockSpec((tm, WIDTH), lambda i: (i, 0)),
            pl.BlockSpec((tm, D_MODEL), lambda i: (i, mblk("merge_a"))),
            pl.BlockSpec((tm, D_MODEL), lambda i: (i, mblk("merge_b"))),
            pl.BlockSpec((tm, D_MODEL), lambda i: (i, 0)),
            w_part(WIDTH, 0), w_part(WIDTH, 1), w_part(D_MODEL, 1),
            const(ln_g.shape), const(ln_b.shape),
        ],
        out_specs=pl.BlockSpec((tm, D_MODEL), lambda i: (i, 0)),
        out_shape=jax.ShapeDtypeStruct((m, D_MODEL), F32),
        scratch_shapes=[pltpu.VMEM((tm, D_MODEL), BF16)],
        compiler_params=pltpu.CompilerParams(
            dimension_semantics=("parallel",), vmem_limit_bytes=VMEM_LIMIT),
        name="out",
    )(u_a, u_b, h_main, h_main, x2, w_cat, w_cat, w_cat, ln_g, ln_b)


def _bucket_np(dist):
    n = np.maximum(dist, 0)
    exact = REL_BUCKETS // 2
    large = exact + (np.log(np.maximum(n, 1).astype(np.float32) / exact)
                     / math.log(REL_MAX_DIST / exact) * (REL_BUCKETS - exact)).astype(np.int32)
    return np.where(n < exact, n, np.minimum(large, REL_BUCKETS - 1)).astype(np.int32)


@functools.lru_cache(maxsize=None)
def _static_tables(seq):
    r = np.arange(TQ)[:, None]
    c = np.arange(TK)[None, :]
    tile_idx = np.stack([_bucket_np(r - c), _bucket_np(TQ + r - c), _bucket_np(2 * TQ + r - c)])
    tile_ok = np.stack([c <= r, np.ones((TQ, TK), bool), (2 * TQ + r - c) < WINDOW])
    tile_idx = np.where(tile_ok, tile_idx, MASK_BUCKET).astype(np.int32)
    t = np.arange(seq)[:, None]
    cblk = np.arange(LANES)[None, :]
    blk_end = cblk * CMP_STRIDE + CMP_LEN - 1
    n_cmp = (seq - CMP_LEN) // CMP_STRIDE + 1
    cmp_idx = np.where((blk_end <= t) & (cblk < n_cmp), _bucket_np(t - blk_end), MASK_BUCKET).astype(np.int32)
    e_t = np.where((np.arange(seq)[:, None] // SEL_LEN) == np.arange(LANES)[None, :], NEG, 0.0)
    cs = (np.arange(LANES) * CMP_STRIDE)[None, :]
    ss = (np.arange(LANES) * SEL_LEN)[:, None]
    ov_t = ((cs < ss + SEL_LEN) & (cs + CMP_LEN > ss)
            & (np.arange(LANES)[None, :] < n_cmp) & (np.arange(LANES)[:, None] < seq // SEL_LEN))
    return tile_idx, cmp_idx, e_t.astype(np.float32), ov_t.astype(np.float32)


def _layer(x, w_in, b_f, cmp_pos_k, cmp_pos_v, cmp_wk1, cmp_wk2, cmp_wv1, cmp_wv2,
           w_a, w_b, w_o, ln_g, ln_b, rel_bias, alpha):
    batch, seq, d_model = x.shape
    assert d_model == D_MODEL and seq % FOX_TQ == 0 and seq >= WIN_BLOCKS * TK and REL_MAX_DIST <= TQ
    assert seq // SEL_LEN == N_SEL_ROWS and seq // CMP_STRIDE == LANES
    x2 = x.reshape(batch * seq, d_model)

    w_t = jnp.swapaxes(w_in, 0, 1)
    trows = lambda name: w_t[_REF_OFF[name][0]:_REF_OFF[name][0] + _REF_OFF[name][1]]
    n_small = N_HEADS + N_HEADS * N_BRANCHES
    w_small_t = jnp.concatenate(
        [trows("fox_f"), trows("nsa_gate"), jnp.zeros((LANES - n_small, d_model), F32)], axis=0)
    bf_row = jnp.concatenate([b_f.astype(F32), jnp.zeros((LANES - N_HEADS,), F32)]).reshape(1, LANES)

    h_main, h_small = _proj(x2, w_t, w_small_t)
    c_col, gates = _gates(h_small, bf_row, batch, seq)
    u_a = _fox(h_main, c_col, batch, seq)

    k_cmp, v_cmp = _compress(h_main, cmp_pos_k, cmp_pos_v, cmp_wk1, cmp_wk2, cmp_wv1, cmp_wv2, batch, seq)

    tile_idx, cmp_idx, e_t, ov_t = _static_tables(seq)
    bias = rel_bias.T.astype(F32)
    table = jnp.concatenate(
        [(bias - bias[:, REL_BUCKETS - 1:]) * LOG2E, jnp.full((N_HEADS, 1), NEG, F32),
         jnp.zeros((N_HEADS, LANES - REL_BUCKETS - 1), F32)], axis=1)
    u_b, w_cat = _nsa(h_main, k_cmp, v_cmp, jnp.asarray(cmp_idx), jnp.asarray(tile_idx), table, gates,
                      jnp.asarray(e_t, BF16), jnp.asarray(ov_t, BF16), (w_a, w_b, w_o), batch, seq)

    out = _out(u_a, u_b, h_main, x2, w_cat, ln_g.reshape(1, d_model), ln_b.reshape(1, d_model), alpha)
    return out.reshape(batch, seq, d_model)


def kernel(x, w_in, b_f, cmp_pos_k, cmp_pos_v, cmp_wk1, cmp_wk2, cmp_wv1, cmp_wv2,
           w_a, w_b, w_o, ln_g, ln_b, rel_bias):
    depth = w_in.shape[0]
    alpha = (2 * depth) ** 0.25
    for layer in range(depth):
        x = _layer(x, w_in[layer], b_f[layer], cmp_pos_k[layer], cmp_pos_v[layer], cmp_wk1[layer],
                   cmp_wk2[layer], cmp_wv1[layer], cmp_wv2[layer], w_a[layer], w_b[layer], w_o[layer],
                   ln_g[layer], ln_b[layer], rel_bias, alpha)
    return x
```

```python
import functools
import math

import jax
import jax.numpy as jnp
import numpy as np
from jax import lax
from jax.experimental import pallas as pl
from jax.experimental.pallas import tpu as pltpu

F32 = jnp.float32
BF16 = jnp.bfloat16

D_MODEL = 2048
HEAD_DIM = 128
N_HEADS = 8
WIDTH = N_HEADS * HEAD_DIM
KV_GROUPS = 2
HEADS_PER_GROUP = N_HEADS // KV_GROUPS
KV_WIDTH = KV_GROUPS * HEAD_DIM
N_BRANCHES = 3
CMP_LEN = 32
CMP_STRIDE = 16
CMP_HIDDEN = 256
SEL_LEN = 64
SEL_TOPK = 8
WINDOW = 512
REL_BUCKETS = 32
REL_MAX_DIST = 128
LN_EPS = 1e-5
NEG = -1e30
LOG2E = math.log2(math.e)
Q_PRESCALE = HEAD_DIM ** -0.5 * LOG2E

LANES = 128
SUBLANES = 8
MXU_DEPTH = 256
VMEM_LIMIT = 60 * 1024 * 1024

_REF_LAYOUT = (
    ("fox_q", WIDTH), ("fox_k", WIDTH), ("fox_v", WIDTH), ("fox_f", N_HEADS), ("fox_z", WIDTH),
    ("nsa_q", WIDTH), ("nsa_k_cmp", KV_WIDTH), ("nsa_v_cmp", KV_WIDTH), ("nsa_k_sel", KV_WIDTH),
    ("nsa_v_sel", KV_WIDTH), ("nsa_k_win", KV_WIDTH), ("nsa_v_win", KV_WIDTH),
    ("nsa_gate", N_HEADS * N_BRANCHES), ("nsa_z", WIDTH), ("merge_a", D_MODEL), ("merge_b", D_MODEL),
)
_REF_OFF = {}
_o = 0
for _n, _w in _REF_LAYOUT:
    _REF_OFF[_n] = (_o, _w)
    _o += _w

_MAIN_ORDER = ("fox_q", "fox_k", "fox_v", "fox_z", "nsa_q", "nsa_z", "merge_a", "merge_b",
               "nsa_k_cmp", "nsa_v_cmp", "nsa_k_sel", "nsa_v_sel", "nsa_k_win", "nsa_v_win")
_QUERY_COLS = ("fox_q", "nsa_q")
_MAIN_OFF = {}
_o = 0
for _n in _MAIN_ORDER:
    _MAIN_OFF[_n] = _o
    _o += _REF_OFF[_n][1]
MAIN_COLS = _o
GATE_LANE0 = N_HEADS


def _dot(a, b):
    return jnp.dot(a, b, preferred_element_type=F32)


def _dot_nt(a, b):
    return lax.dot_general(a, b, (((1,), (1,)), ((), ())), preferred_element_type=F32)


def _sigmoid(x):
    return 0.5 + 0.5 * jnp.tanh(0.5 * x)


def _silu(x):
    return x * _sigmoid(x)


PROJ_TN = 512


def _proj_tiles():
    rows, is_query = [], []
    for name in _MAIN_ORDER:
        off, width = _REF_OFF[name]
        start = _MAIN_OFF[name]
        for c in range(start, start + width):
            if c % PROJ_TN == 0:
                assert (off + c - start) % SUBLANES == 0
                rows.append((off + c - start) // SUBLANES)
                is_query.append(int(name in _QUERY_COLS))
    return np.asarray(rows, np.int32), np.asarray(is_query, np.int32)


PROJ_ROWS = 4096
PROJ_RB = 512


def _proj_kernel(rows_ref, isq_ref, x_hbm, wt_ref, wst_ref, o_ref, os_ref,
                 xb_ref, stage_ref, wb_ref, sem):
    i = pl.program_id(0)
    j = pl.program_id(1)
    n_blocks = PROJ_ROWS // PROJ_RB
    wb_ref[...] = wt_ref[...].astype(BF16)
    scale = jnp.where(isq_ref[j] == 1, Q_PRESCALE, 1.0)

    def x_copy(r, slot):
        row0 = pl.multiple_of(i * PROJ_ROWS + r * PROJ_RB, PROJ_RB)
        return pltpu.make_async_copy(x_hbm.at[pl.ds(row0, PROJ_RB), :], stage_ref.at[slot], sem.at[slot])

    def block(r):
        rows = slice(r * PROJ_RB, (r + 1) * PROJ_RB)
        o_ref[rows, :] = (_dot_nt(xb_ref[rows, :], wb_ref[...]) * scale).astype(o_ref.dtype)

    @pl.when(j == 0)
    def _():
        wsb = wst_ref[...].astype(BF16)
        x_copy(0, 0).start()
        for r in range(n_blocks):
            if r + 1 < n_blocks:
                x_copy(r + 1, (r + 1) % 2).start()
            x_copy(r, r % 2).wait()
            rows = slice(r * PROJ_RB, (r + 1) * PROJ_RB)
            xb_ref[rows, :] = stage_ref[r % 2].astype(BF16)
            os_ref[rows, :] = _dot_nt(xb_ref[rows, :], wsb)
            block(r)

    @pl.when(j > 0)
    def _():
        for r in range(n_blocks):
            block(r)


def _proj(x2, w_t, w_small_t):
    m, k = x2.shape
    rows, is_query = _proj_tiles()
    assert MAIN_COLS % PROJ_TN == 0 and len(rows) == MAIN_COLS // PROJ_TN and m % PROJ_ROWS == 0
    grid_spec = pltpu.PrefetchScalarGridSpec(
        num_scalar_prefetch=2,
        grid=(m // PROJ_ROWS, len(rows)),
        in_specs=[
            pl.BlockSpec(memory_space=pl.ANY),
            pl.BlockSpec((pl.Element(PROJ_TN), pl.Element(k)),
                         lambda i, j, rows, isq: (rows[j] * SUBLANES, 0)),
            pl.BlockSpec((LANES, k), lambda i, j, rows, isq: (0, 0)),
        ],
        out_specs=[
            pl.BlockSpec((PROJ_ROWS, PROJ_TN), lambda i, j, rows, isq: (i, j)),
            pl.BlockSpec((PROJ_ROWS, LANES), lambda i, j, rows, isq: (i, 0)),
        ],
        scratch_shapes=[pltpu.VMEM((PROJ_ROWS, k), BF16),
                        pltpu.VMEM((2, PROJ_RB, k), F32),
                        pltpu.VMEM((PROJ_TN, k), BF16),
                        pltpu.SemaphoreType.DMA((2,))],
    )
    return pl.pallas_call(
        _proj_kernel,
        grid_spec=grid_spec,
        out_shape=[jax.ShapeDtypeStruct((m, MAIN_COLS), BF16), jax.ShapeDtypeStruct((m, LANES), F32)],
        compiler_params=pltpu.CompilerParams(
            dimension_semantics=("parallel", "arbitrary"), vmem_limit_bytes=VMEM_LIMIT),
        name="proj",
    )(jnp.asarray(rows), jnp.asarray(is_query), x2, w_t, w_small_t)


_CUM_CHUNK = 256
_N_SPLIT = 3
ONES_LANE = _N_SPLIT * N_HEADS


def _split3(x):
    hi = x.astype(BF16)
    r1 = x - hi.astype(F32)
    mid = r1.astype(BF16)
    lo = (r1 - mid.astype(F32)).astype(BF16)
    return hi, mid, lo


def _gate_kernel(hs_ref, bf_ref, c_ref, g_ref):
    hs = hs_ref[...]
    g_ref[...] = _sigmoid(hs)
    z = hs + bf_ref[...]
    logf = jnp.minimum(z, 0.0) - jnp.log1p(jnp.exp(-jnp.abs(z)))
    n = _CUM_CHUNK
    tri = (lax.broadcasted_iota(jnp.int32, (n, n), 1)
           <= lax.broadcasted_iota(jnp.int32, (n, n), 0)).astype(BF16)
    lane = lax.broadcasted_iota(jnp.int32, (n, LANES), 1)
    carry = jnp.zeros((1, LANES), F32)
    for blk in range(hs.shape[0] // n):
        hi, mid, lo = _split3(logf[blk * n:(blk + 1) * n])
        cb = _dot(tri, hi) + _dot(tri, mid) + _dot(tri, lo) + carry
        carry = cb[n - 1:n, :]
        hi, mid, lo = _split3(cb * LOG2E)
        packed = jnp.where((lane >= ONES_LANE) & (lane < ONES_LANE + _N_SPLIT), 1.0, 0.0)
        for t, term in enumerate((hi, mid, lo)):
            shifted = term.astype(F32) if t == 0 else pltpu.roll(term.astype(F32), t * N_HEADS, 1)
            packed = jnp.where((lane >= t * N_HEADS) & (lane < (t + 1) * N_HEADS), shifted, packed)
        c_ref[blk * n:(blk + 1) * n, :] = packed.astype(c_ref.dtype)


def _gates(h_small, bf_row, batch, seq):
    return pl.pallas_call(
        _gate_kernel,
        grid=(batch,),
        in_specs=[pl.BlockSpec((seq, LANES), lambda b: (b, 0)),
                  pl.BlockSpec((1, LANES), lambda b: (0, 0))],
        out_specs=[pl.BlockSpec((seq, LANES), lambda b: (b, 0)),
                   pl.BlockSpec((seq, LANES), lambda b: (b, 0))],
        out_shape=[jax.ShapeDtypeStruct(h_small.shape, BF16), jax.ShapeDtypeStruct(h_small.shape, F32)],
        compiler_params=pltpu.CompilerParams(dimension_semantics=("parallel",)),
        name="gates",
    )(h_small, bf_row)


_HALF = CMP_LEN // 2


def _gelu_tanh(x):
    return 0.5 * x * (1.0 + jnp.tanh(math.sqrt(2.0 / math.pi) * (x + 0.044715 * (x * x * x))))


def _compress_kernel(raw_ref, pk_ref, pv_ref, w1k_ref, w2k_ref, w1v_ref, w2v_ref, kc_ref, vc_ref, raw32_ref):
    seq = raw_ref.shape[0]
    n_chunks = seq // _HALF
    step = 512
    for slab in range(raw_ref.shape[1] // HEAD_DIM):
        for blk in range(seq // step):
            raw32_ref[slab, blk * step:(blk + 1) * step, :] = (
                raw_ref[blk * step:(blk + 1) * step, slab * HEAD_DIM:(slab + 1) * HEAD_DIM].astype(F32))
    for kv, (pos_ref, w1_ref, w2_ref, out_ref) in enumerate(
            ((pk_ref, w1k_ref, w2k_ref, kc_ref), (pv_ref, w1v_ref, w2v_ref, vc_ref))):
        first = jnp.zeros((KV_GROUPS * n_chunks, CMP_HIDDEN), F32)
        second = jnp.zeros((KV_GROUPS * n_chunks, CMP_HIDDEN), F32)
        def tokens(l, pos_row):
            a = jnp.concatenate([raw32_ref[kv * KV_GROUPS + g, pl.ds(l, n_chunks, stride=_HALF), :]
                                 for g in range(KV_GROUPS)], axis=0)
            return (a + pos_ref[pos_row:pos_row + 1, :]).astype(BF16)

        for l in range(0, _HALF, 2):
            rows = slice(l * HEAD_DIM, (l + 2) * HEAD_DIM)
            rows2 = slice((_HALF + l) * HEAD_DIM, (_HALF + l + 2) * HEAD_DIM)
            first += _dot(jnp.concatenate([tokens(l, l), tokens(l + 1, l + 1)], axis=1),
                          w1_ref[rows, :].astype(BF16))
            second += _dot(jnp.concatenate([tokens(l, _HALF + l), tokens(l + 1, _HALF + l + 1)], axis=1),
                           w1_ref[rows2, :].astype(BF16))
        hid = first + jnp.concatenate(
            [pltpu.roll(second[g * n_chunks:(g + 1) * n_chunks], n_chunks - 1, 0) for g in range(KV_GROUPS)],
            axis=0)
        out = _dot(_gelu_tanh(hid).astype(BF16), w2_ref[...].astype(BF16))
        for g in range(KV_GROUPS):
            out_ref[0, :, g * HEAD_DIM:(g + 1) * HEAD_DIM] = (
                out[g * n_chunks:(g + 1) * n_chunks].astype(out_ref.dtype))


def _compress(h_main, pos_k, pos_v, w1k, w2k, w1v, w2v, batch, seq):
    n_chunks = seq // _HALF
    raw_cols = 2 * KV_WIDTH
    assert _MAIN_OFF["nsa_v_cmp"] == _MAIN_OFF["nsa_k_cmp"] + KV_WIDTH
    raw_blk = _MAIN_OFF["nsa_k_cmp"] // raw_cols
    full = lambda shape: pl.BlockSpec(shape, lambda b: (0,) * len(shape), pipeline_mode=pl.Buffered(1))
    return pl.pallas_call(
        _compress_kernel,
        grid=(batch,),
        in_specs=[pl.BlockSpec((seq, raw_cols), lambda b: (b, raw_blk)),
                  full(pos_k.shape), full(pos_v.shape),
                  full(w1k.shape), full(w2k.shape), full(w1v.shape), full(w2v.shape)],
        out_specs=[pl.BlockSpec((1, n_chunks, KV_WIDTH), lambda b: (b, 0, 0)),
                   pl.BlockSpec((1, n_chunks, KV_WIDTH), lambda b: (b, 0, 0))],
        out_shape=[jax.ShapeDtypeStruct((batch, n_chunks, KV_WIDTH), BF16),
                   jax.ShapeDtypeStruct((batch, n_chunks, KV_WIDTH), BF16)],
        scratch_shapes=[pltpu.VMEM((raw_cols // HEAD_DIM, seq, HEAD_DIM), F32)],
        compiler_params=pltpu.CompilerParams(
            dimension_semantics=("parallel",), vmem_limit_bytes=VMEM_LIMIT),
        name="compress",
    )(h_main, pos_k, pos_v, w1k, w2k, w1v, w2v)


def _lane_tile(x, n):
    return x if n == 1 else jnp.concatenate([x] * n, axis=1)


def _lane_fold(p):
    out = p[:, :LANES]
    for t in range(1, p.shape[1] // LANES):
        out = out + p[:, t * LANES:(t + 1) * LANES]
    return out


def _first_weights(s, m_ref, l_ref, rows=slice(None)):
    m = jnp.broadcast_to(jnp.max(s, axis=1, keepdims=True), (s.shape[0], LANES))
    p = jnp.exp2(s - _lane_tile(m, s.shape[1] // LANES))
    m_ref[rows, :] = m
    l_ref[rows, :] = _lane_fold(p)
    return p.astype(BF16)


def _next_weights(s, m_ref, l_ref, rows=slice(None)):
    m_prev = m_ref[rows, :]
    m_new = jnp.maximum(m_prev, jnp.max(s, axis=1, keepdims=True))
    alpha = jnp.exp2(m_prev - m_new)
    p = jnp.exp2(s - _lane_tile(m_new, s.shape[1] // LANES))
    l_ref[rows, :] = alpha * l_ref[rows, :] + _lane_fold(p)
    m_ref[rows, :] = m_new
    return alpha, p.astype(BF16)


def _skewed(stages, jobs, before_tick=None):
    state = [None] * len(jobs)
    for tick in range(len(jobs) + len(stages) - 1):
        if before_tick and tick in before_tick:
            before_tick[tick]()
        for k, stage in reversed(list(enumerate(stages))):
            j = tick - k
            if 0 <= j < len(jobs):
                state[j] = stage(jobs[j], state[j])


def _softmax_finish(l_ref, acc_ref, rows=slice(None)):
    return acc_ref[rows, :] / jnp.sum(l_ref[rows, :], axis=1, keepdims=True)


FOX_TQ = 512
FOX_HALF = FOX_TQ // 2


def _fox_routing():
    pk = np.zeros((N_HEADS, LANES, LANES), np.float32)
    for h in range(N_HEADS):
        for t in range(_N_SPLIT):
            pk[h, ONES_LANE, t * N_HEADS + h] = 1.0
            pk[h, t * N_HEADS + h, ONES_LANE + t] = -1.0
    return pk


def _cast_weight_block(step, w_blocks, w_refs, wcast_ref):
    first = 0
    for w_ref, n_blocks in zip(w_refs, w_blocks):
        @pl.when((step >= first) & (step < first + n_blocks))
        def _(w_ref=w_ref):
            wcast_ref[...] = w_ref[...].astype(wcast_ref.dtype)
        first += n_blocks


def _weight_block_specs(weights, n_steps, step_of):
    w_rows = sum(w.shape[0] for w in weights)
    rows_per_step = w_rows // n_steps
    assert w_rows % n_steps == 0 and all(w.shape[0] % rows_per_step == 0 for w in weights)
    w_blocks = tuple(w.shape[0] // rows_per_step for w in weights)
    in_specs, first = [], 0
    for n_blocks in w_blocks:
        in_specs.append(pl.BlockSpec(
            (rows_per_step, D_MODEL),
            lambda b, i, first=first, n_blocks=n_blocks: (jnp.clip(step_of(b, i) - first, 0, n_blocks - 1), 0)))
        first += n_blocks
    out_spec = pl.BlockSpec((rows_per_step, D_MODEL), lambda b, i: (step_of(b, i), 0))
    return w_blocks, in_specs, out_spec, jax.ShapeDtypeStruct((w_rows, D_MODEL), BF16)


def _fox_kernel(q_ref, k_ref, v_ref, z_ref, cq_ref, ck_ref, pk_ref, o_ref,
                kaug_ref, qaug_ref, sz_ref, m_ref, l_ref, acc_ref):
    i = pl.program_id(1)
    tq, half = FOX_TQ, FOX_HALF
    seq = k_ref.shape[0]

    @pl.when(i == 0)
    def _():
        for h in range(N_HEADS):
            for blk in range(seq // tq):
                rows = slice(blk * tq, (blk + 1) * tq)
                kaug_ref[h, rows, :HEAD_DIM] = k_ref[rows, h * HEAD_DIM:(h + 1) * HEAD_DIM]
                kaug_ref[h, rows, HEAD_DIM:] = _dot(ck_ref[rows, :], pk_ref[h]).astype(BF16)

    causal = (lax.broadcasted_iota(jnp.int32, (half, half), 1)
              <= lax.broadcasted_iota(jnp.int32, (half, half), 0))
    diag = pl.multiple_of(i * tq, tq)
    diag2 = pl.multiple_of(i * tq + half, half)
    heads = tuple(range(N_HEADS))
    rows = lambda h: slice(h * tq, (h + 1) * tq)
    top = lambda h: slice(h * tq, h * tq + half)
    bot = lambda h: slice(h * tq + half, (h + 1) * tq)
    cols = lambda h: slice(h * HEAD_DIM, (h + 1) * HEAD_DIM)

    def diag_scores(h, _):
        qaug_ref[rows(h), :HEAD_DIM] = q_ref[:, cols(h)]
        qaug_ref[rows(h), HEAD_DIM:] = cq_ref[...]
        s_left = _dot_nt(qaug_ref[rows(h), :], kaug_ref[h, pl.ds(diag, half), :])
        s_right = _dot_nt(qaug_ref[bot(h), :], kaug_ref[h, pl.ds(diag2, half), :])
        return s_left, s_right

    def diag_weights(h, scores):
        s_left, s_right = scores
        p_top = _first_weights(jnp.where(causal, s_left[:half], NEG), m_ref, l_ref, top(h))
        p_bot = _first_weights(jnp.concatenate([s_left[half:], jnp.where(causal, s_right, NEG)], axis=1),
                               m_ref, l_ref, bot(h))
        return p_top, p_bot

    def diag_values(h, weights):
        p_top, p_bot = weights
        acc_ref[top(h), :] = _dot(p_top, v_ref[pl.ds(diag, half), cols(h)])
        acc_ref[bot(h), :] = _dot(p_bot, v_ref[pl.ds(diag, tq), cols(h)])

    def output_gate(h, _):
        sz_ref[:, cols(h)] = _silu(z_ref[:, cols(h)].astype(F32))

    _skewed((output_gate, diag_scores, diag_weights, diag_values), heads)

    def body(j, carry):
        off = pl.multiple_of(j * tq, tq)

        def scores(h, _):
            return _dot_nt(qaug_ref[rows(h), :], kaug_ref[h, pl.ds(off, tq), :])

        def weights(h, s):
            return _next_weights(s, m_ref, l_ref, rows(h))

        def values(h, rescale_and_weights):
            alpha, p = rescale_and_weights
            acc_ref[rows(h), :] = alpha * acc_ref[rows(h), :] + _dot(p, v_ref[pl.ds(off, tq), cols(h)])

        _skewed((scores, weights, values), heads)
        return carry

    lax.fori_loop(0, i, body, 0)
    for h in heads:
        o_ref[:, cols(h)] = (_softmax_finish(l_ref, acc_ref, rows(h)) * sz_ref[:, cols(h)]).astype(o_ref.dtype)


def _fox(h_main, c_packed, batch, seq):
    tq = FOX_TQ
    nq = seq // tq
    blk = lambda name: _MAIN_OFF[name] // WIDTH
    pk = jnp.asarray(_fox_routing(), BF16)
    full = lambda shape: pl.BlockSpec(shape, lambda b, i: (0,) * len(shape))
    stat = pltpu.VMEM((N_HEADS * tq, LANES), F32)
    return pl.pallas_call(
        _fox_kernel,
        grid=(batch, nq),
        in_specs=[
            pl.BlockSpec((tq, WIDTH), lambda b, i: (b * nq + i, blk("fox_q"))),
            pl.BlockSpec((seq, WIDTH), lambda b, i: (b, blk("fox_k"))),
            pl.BlockSpec((seq, WIDTH), lambda b, i: (b, blk("fox_v"))),
            pl.BlockSpec((tq, WIDTH), lambda b, i: (b * nq + i, blk("fox_z"))),
            pl.BlockSpec((tq, LANES), lambda b, i: (b * nq + i, 0)),
            pl.BlockSpec((seq, LANES), lambda b, i: (b, 0)),
            full(pk.shape),
        ],
        out_specs=pl.BlockSpec((tq, WIDTH), lambda b, i: (b * nq + i, 0)),
        out_shape=jax.ShapeDtypeStruct((batch * seq, WIDTH), BF16),
        scratch_shapes=[pltpu.VMEM((N_HEADS, seq, MXU_DEPTH), BF16),
                        pltpu.VMEM((N_HEADS * tq, MXU_DEPTH), BF16),
                        pltpu.VMEM((tq, WIDTH), F32),
                        stat, stat, stat],
        compiler_params=pltpu.CompilerParams(
            dimension_semantics=("parallel", "arbitrary"), vmem_limit_bytes=VMEM_LIMIT),
        name="fox",
    )(h_main, h_main, h_main, h_main, c_packed, c_packed, pk)


TQ = 256
TK = TQ
WIN_BLOCKS = (WINDOW + TQ) // TK
SEL_NEAR_BLOCKS = 2
FAR_TK = 2 * TK
N_SEL_ROWS = 32
MASK_BUCKET = REL_BUCKETS
GROUP_ROWS = HEADS_PER_GROUP * TQ
T_DIAG, T_PREV, T_WIN2, T_NONE = range(4)
N_GATHERED_TILES = 3


def _bias_lookup(tab_ref, h, idx):
    row = jnp.broadcast_to(tab_ref[h:h + 1, :], idx.shape)
    return jnp.take_along_axis(row, idx, axis=1, mode="promise_in_bounds")


def _nsa_kernel(w_blocks, q_ref, ks_ref, vs_ref, kw_ref, vw_ref, z_ref, kc_ref, vc_ref, cidx_ref, tidx_ref,
                tab_ref, g_ref, et_ref, ov_ref, wa_ref, wb_ref, wo_ref, o_ref, wcast_ref,
                ksaug_ref, qs_ref, qaug_ref, m_ref, l_ref, acc_ref, oc_ref, ow_ref, gs_ref, t_ref, cb_ref):
    i = pl.program_id(1)
    t0 = i * TQ
    _cast_weight_block(pl.program_id(0) * pl.num_programs(1) + i, w_blocks, (wa_ref, wb_ref, wo_ref),
                       wcast_ref)

    @pl.when((pl.program_id(0) == 0) & (i == 0))
    def _():
        for h in range(N_HEADS):
            for d in range(N_GATHERED_TILES):
                for half in range(TK // LANES):
                    cs = slice(half * LANES, (half + 1) * LANES)
                    t_ref[h, d, :, cs] = _bias_lookup(tab_ref, h, tidx_ref[d, :, cs])
            for blk in range(cidx_ref.shape[0] // TQ):
                rows = slice(blk * TQ, (blk + 1) * TQ)
                cb_ref[h, rows, :] = _bias_lookup(tab_ref, h, cidx_ref[rows, :])
            t_ref[h, T_NONE] = jnp.full((TQ, TK), NEG, F32)

    @pl.when(i == 0)
    def _():
        for g in range(KV_GROUPS):
            ksaug_ref[g, :, :HEAD_DIM] = ks_ref[:, g * HEAD_DIM:(g + 1) * HEAD_DIM]
            ksaug_ref[g, :, HEAD_DIM:] = et_ref[...]

    cmaskf = (lax.broadcasted_iota(jnp.int32, (TQ, LANES), 1) * CMP_STRIDE + (CMP_LEN - 1)
              <= t0 + lax.broadcasted_iota(jnp.int32, (TQ, LANES), 0)).astype(F32)
    j_t = lax.broadcasted_iota(jnp.int32, (N_SEL_ROWS, TQ), 0)
    t_t = t0 + lax.broadcasted_iota(jnp.int32, (N_SEL_ROWS, TQ), 1)
    cur_t = t_t // SEL_LEN
    forced_t = (j_t == 0) | (j_t == cur_t) | (j_t == cur_t - 1)
    valid_t = j_t * SEL_LEN <= t_t

    def near_span(n_blocks):
        first = jnp.maximum(i - (n_blocks - 1), 0)
        return first, pl.ds(pl.multiple_of(first * TK, TK), n_blocks * TK)

    win_first, win_keys = near_span(WIN_BLOCKS)
    sel_first, sel_keys = near_span(SEL_NEAR_BLOCKS)

    def near_bias(heads, first, n_blocks):
        def tile(kk):
            dist = i - (first + kk)
            return jnp.where(dist == 0, T_DIAG, jnp.where(dist == 1, T_PREV,
                             jnp.where(dist == 2, T_WIN2, T_NONE)))
        return jnp.concatenate(
            [jnp.concatenate([t_ref[h, tile(kk)] for kk in range(n_blocks)], axis=1) for h in heads],
            axis=0)

    groups = tuple((g, g * HEAD_DIM, tuple(range(g * HEADS_PER_GROUP, (g + 1) * HEADS_PER_GROUP)),
                    slice(g * GROUP_ROWS, (g + 1) * GROUP_ROWS)) for g in range(KV_GROUPS))
    hrows = lambda h: slice(h * TQ, (h + 1) * TQ)
    for h in range(N_HEADS):
        q = q_ref[:, h * HEAD_DIM:(h + 1) * HEAD_DIM]
        qs_ref[hrows(h), :] = q
        qaug_ref[hrows(h), :HEAD_DIM] = q

    chain = {}

    def compressed_branch():
        cbias = jnp.concatenate([cb_ref[h, pl.ds(pl.multiple_of(t0, TQ), TQ), :] for h in range(N_HEADS)],
                                axis=0)
        sc = jnp.concatenate([_dot_nt(qs_ref[grows, :], kc_ref[0, :, glo:glo + HEAD_DIM])
                              for g, glo, heads, grows in groups], axis=0) + cbias
        e = jnp.exp2(sc - jnp.max(sc, axis=1, keepdims=True))
        p = e / jnp.sum(e, axis=1, keepdims=True) * jnp.concatenate([cmaskf] * N_HEADS, axis=0)
        imp_parts = []
        for g, glo, heads, grows in groups:
            oc_ref[grows, :] = _dot(p[grows].astype(BF16), vc_ref[0, :, glo:glo + HEAD_DIM])
            psum = p[hrows(heads[0])]
            for h in heads[1:]:
                psum = psum + p[hrows(h)]
            p_hi = psum.astype(BF16)
            p_lo = (psum - p_hi.astype(F32)).astype(BF16)
            imp_parts.append((_dot_nt(ov_ref[...], p_hi) + _dot_nt(ov_ref[...], p_lo))[:N_SEL_ROWS])
        chain["importance"] = jnp.concatenate(imp_parts, axis=1)

    def block_selection():
        both = lambda a: jnp.concatenate([a] * KV_GROUPS, axis=1)
        j_b = both(j_t)
        x = jnp.where(both(valid_t), jnp.where(both(forced_t), -NEG, chain["importance"]), NEG)
        groups8 = [slice(r, r + SUBLANES) for r in range(0, N_SEL_ROWS, SUBLANES)]
        cnt = [jnp.zeros((SUBLANES, x.shape[1]), F32) for _ in groups8]
        for jp in range(N_SEL_ROWS):
            row = x[jp:jp + 1, :]
            for n, rows8 in enumerate(groups8):
                if rows8.start > jp:
                    beats = row >= x[rows8]
                elif rows8.stop - 1 <= jp:
                    beats = row > x[rows8]
                else:
                    beats = (row > x[rows8]) | ((row == x[rows8]) & (j_b[rows8] > jp))
                cnt[n] = cnt[n] + jnp.where(beats, 1.0, 0.0)
        unsel_t = jnp.where(jnp.concatenate(cnt, axis=0) < SEL_TOPK, 0.0, 1.0)
        unsel_t = jnp.concatenate([unsel_t, jnp.zeros((LANES - N_SEL_ROWS, x.shape[1]), F32)], axis=0)
        unsel = unsel_t.T.astype(BF16)
        for h in range(N_HEADS):
            g = h // HEADS_PER_GROUP
            qaug_ref[hrows(h), HEAD_DIM:] = unsel[g * TQ:(g + 1) * TQ]

    def near_scores(job, _):
        branch, (g, glo, heads, grows) = job
        if branch == "window":
            return (_dot_nt(qs_ref[grows, :], kw_ref[win_keys, glo:glo + HEAD_DIM])
                    + near_bias(heads, win_first, WIN_BLOCKS))
        return (_dot_nt(qaug_ref[grows, :], ksaug_ref[g, sel_keys, :])
                + near_bias(heads, sel_first, SEL_NEAR_BLOCKS))

    def near_weights(job, s):
        branch, (g, glo, heads, grows) = job
        if branch == "window":
            p = jnp.exp2(s - jnp.max(s, axis=1, keepdims=True))
            return p.astype(BF16), _lane_fold(p)
        return _first_weights(s, m_ref, l_ref, grows)

    def near_values(job, weights):
        branch, (g, glo, heads, grows) = job
        if branch == "selected":
            acc_ref[grows, :] = _dot(weights, vs_ref[sel_keys, glo:glo + HEAD_DIM])
            return
        p, l = weights
        return _dot(p, vw_ref[win_keys, glo:glo + HEAD_DIM]) / jnp.sum(l, axis=1, keepdims=True)

    def near_gating(job, o_win):
        branch, (g, glo, heads, grows) = job
        if branch == "selected":
            return
        for n, h in enumerate(heads):
            gl = GATE_LANE0 + h * N_BRANCHES
            sz = _silu(z_ref[:, h * HEAD_DIM:(h + 1) * HEAD_DIM].astype(F32))
            gs_ref[hrows(h), :] = g_ref[:, gl + 1:gl + 2] * sz
            ow_ref[hrows(h), :] = (g_ref[:, gl:gl + 1] * oc_ref[hrows(h), :]
                                   + g_ref[:, gl + 2:gl + 3] * o_win[n * TQ:(n + 1) * TQ]) * sz

    half_groups = tuple(
        (g, glo, heads[part * 2:(part + 1) * 2], slice(heads[part * 2] * TQ, (heads[part * 2 + 1] + 1) * TQ))
        for g, glo, heads, grows in groups for part in range(HEADS_PER_GROUP // 2))
    _skewed((near_scores, near_weights, near_values, near_gating),
            tuple((branch, group) for branch in ("window", "selected") for group in half_groups),
            before_tick={0: compressed_branch, 2: block_selection})

    def sel_far(off, width):
        def scores(group, _):
            g, glo, heads, grows = group
            return _dot_nt(qaug_ref[grows, :], ksaug_ref[g, pl.ds(off, width), :])

        def weights(group, s):
            g, glo, heads, grows = group
            return _next_weights(s, m_ref, l_ref, grows)

        def values(group, rescale_and_weights):
            g, glo, heads, grows = group
            alpha, p = rescale_and_weights
            acc_ref[grows, :] = alpha * acc_ref[grows, :] + _dot(p, vs_ref[pl.ds(off, width), glo:glo + HEAD_DIM])

        _skewed((scores, weights, values), groups)

    def sel_far_pair(j, carry):
        sel_far(pl.multiple_of(j * FAR_TK, FAR_TK), FAR_TK)
        return carry

    lax.fori_loop(0, sel_first // 2, sel_far_pair, 0)

    @pl.when(sel_first % 2 == 1)
    def _():
        sel_far(pl.multiple_of((sel_first - 1) * TK, TK), TK)

    for h in range(N_HEADS):
        o_ref[:, h * HEAD_DIM:(h + 1) * HEAD_DIM] = (
            ow_ref[hrows(h), :] + gs_ref[hrows(h), :] * _softmax_finish(l_ref, acc_ref, hrows(h))
        ).astype(o_ref.dtype)


def _nsa(h_main, k_cmp, v_cmp, cmp_idx, tile_idx, table, gates, e_t, ov_t, weights, batch, seq):
    nq = seq // TQ
    w_blocks, w_in_specs, w_out_spec, w_out_shape = _weight_block_specs(
        weights, batch * nq, lambda b, i: b * nq + i)
    wblk = lambda name: _MAIN_OFF[name] // WIDTH
    kvblk = lambda name: _MAIN_OFF[name] // KV_WIDTH
    full = lambda shape: pl.BlockSpec(shape, lambda b, i: (0,) * len(shape),
                                      pipeline_mode=pl.Buffered(1))
    kv_spec = lambda name: pl.BlockSpec((seq, KV_WIDTH), lambda b, i: (b, kvblk(name)))
    n_chunks = k_cmp.shape[1]
    stat = pltpu.VMEM((N_HEADS * TQ, LANES), F32)
    return pl.pallas_call(
        functools.partial(_nsa_kernel, w_blocks),
        grid=(batch, nq),
        in_specs=[
            pl.BlockSpec((TQ, WIDTH), lambda b, i: (b * nq + i, wblk("nsa_q"))),
            kv_spec("nsa_k_sel"), kv_spec("nsa_v_sel"), kv_spec("nsa_k_win"), kv_spec("nsa_v_win"),
            pl.BlockSpec((TQ, WIDTH), lambda b, i: (b * nq + i, wblk("nsa_z"))),
            pl.BlockSpec((1, n_chunks, KV_WIDTH), lambda b, i: (b, 0, 0)),
            pl.BlockSpec((1, n_chunks, KV_WIDTH), lambda b, i: (b, 0, 0)),
            full(cmp_idx.shape), full(tile_idx.shape), full(table.shape),
            pl.BlockSpec((TQ, LANES), lambda b, i: (b * nq + i, 0)),
            full(e_t.shape), full(ov_t.shape),
            *w_in_specs,
        ],
        out_specs=[pl.BlockSpec((TQ, WIDTH), lambda b, i: (b * nq + i, 0)), w_out_spec],
        out_shape=[jax.ShapeDtypeStruct((batch * seq, WIDTH), BF16), w_out_shape],
        scratch_shapes=[pltpu.VMEM((KV_GROUPS, seq, MXU_DEPTH), BF16),
                        pltpu.VMEM((N_HEADS * TQ, HEAD_DIM), BF16),
                        pltpu.VMEM((N_HEADS * TQ, MXU_DEPTH), BF16),
                        stat, stat, stat,
                        stat, stat, stat,
                        pltpu.VMEM((N_HEADS, T_NONE + 1, TQ, TK), F32),
                        pltpu.VMEM((N_HEADS, seq, LANES), F32)],
        compiler_params=pltpu.CompilerParams(
            dimension_semantics=("arbitrary", "arbitrary"), vmem_limit_bytes=VMEM_LIMIT),
        name="nsa",
    )(h_main, h_main, h_main, h_main, h_main, h_main, k_cmp, v_cmp, cmp_idx, tile_idx, table, gates,
      e_t, ov_t, *weights)


OUT_TM = 512
OUT_TN = D_MODEL
OUT_ROW_GROUPS = 2


def _out_kernel(alpha, ua_ref, ub_ref, ga_ref, gb_ref, x_ref, wa_ref, wb_ref, wo_ref, lg_ref, lb_ref,
                o_ref, merged_ref):
    d_model = o_ref.shape[1]
    chunks = tuple(slice(c * OUT_TN, (c + 1) * OUT_TN) for c in range(d_model // OUT_TN))
    half = o_ref.shape[0] // OUT_ROW_GROUPS

    def merge(rows, _):
        for cs in chunks:
            ya = _dot(ua_ref[rows, :], wa_ref[:, cs])
            yb = _dot(ub_ref[rows, :], wb_ref[:, cs])
            merged_ref[rows, cs] = (_sigmoid(ga_ref[rows, cs].astype(F32)) * ya
                                    + _sigmoid(gb_ref[rows, cs].astype(F32)) * yb).astype(merged_ref.dtype)

    def residual(rows, _):
        total = jnp.zeros((half, LANES), F32)
        for cs in chunks:
            y = alpha * x_ref[rows, cs] + _dot(merged_ref[rows, :], wo_ref[:, cs])
            o_ref[rows, cs] = y
            total = total + _lane_fold(y)
        return total

    def layer_norm(rows, total):
        mu = jnp.sum(total, axis=1, keepdims=True) * (1.0 / d_model)
        sq = jnp.zeros((half, LANES), F32)
        for cs in chunks:
            d = o_ref[rows, cs] - mu
            sq = sq + _lane_fold(d * d)
        rstd = lax.rsqrt(jnp.sum(sq, axis=1, keepdims=True) * (1.0 / d_model) + LN_EPS)
        for cs in chunks:
            o_ref[rows, cs] = (o_ref[rows, cs] - mu) * rstd * lg_ref[:, cs] + lb_ref[:, cs]

    _skewed((merge, residual, layer_norm),
            tuple(slice(r * half, (r + 1) * half) for r in range(OUT_ROW_GROUPS)))


def _out(u_a, u_b, h_main, x2, w_cat, ln_g, ln_b, alpha, tm=OUT_TM):
    m = x2.shape[0]
    mblk = lambda name: _MAIN_OFF[name] // D_MODEL
    const = lambda shape: pl.BlockSpec(shape, lambda i: (0, 0), pipeline_mode=pl.Buffered(1))
    assert w_cat.shape == (2 * WIDTH + D_MODEL, D_MODEL) and D_MODEL == 2 * WIDTH
    w_part = lambda rows, index: pl.BlockSpec((rows, D_MODEL), lambda i: (index, 0),
                                              pipeline_mode=pl.Buffered(1))
    return pl.pallas_call(
        functools.partial(_out_kernel, alpha),
        grid=(m // tm,),
        in_specs=[
            pl.BlockSpec((tm, WIDTH), lambda i: (i, 0)),
            pl.BlockSpec((tm, WIDTH), lambda i: (i, 0)),
            pl.BlockSpec((tm, D_MODEL), lambda i: (i, mblk("merge_a"))),
            pl.BlockSpec((tm, D_MODEL), lambda i: (i, mblk("merge_b"))),
            pl.BlockSpec((tm, D_MODEL), lambda i: (i, 0)),
            w_part(WIDTH, 0), w_part(WIDTH, 1), w_part(D_MODEL, 1),
            const(ln_g.shape), const(ln_b.shape),
        ],
        out_specs=pl.BlockSpec((tm, D_MODEL), lambda i: (i, 0)),
        out_shape=jax.ShapeDtypeStruct((m, D_MODEL), F32),
        scratch_shapes=[pltpu.VMEM((tm, D_MODEL), BF16)],
        compiler_params=pltpu.CompilerParams(
            dimension_semantics=("parallel",), vmem_limit_bytes=VMEM_LIMIT),
        name="out",
    )(u_a, u_b, h_main, h_main, x2, w_cat, w_cat, w_cat, ln_g, ln_b)


def _bucket_np(dist):
    n = np.maximum(dist, 0)
    exact = REL_BUCKETS // 2
    large = exact + (np.log(np.maximum(n, 1).astype(np.float32) / exact)
                     / math.log(REL_MAX_DIST / exact) * (REL_BUCKETS - exact)).astype(np.int32)
    return np.where(n < exact, n, np.minimum(large, REL_BUCKETS - 1)).astype(np.int32)


@functools.lru_cache(maxsize=None)
def _static_tables(seq):
    r = np.arange(TQ)[:, None]
    c = np.arange(TK)[None, :]
    tile_idx = np.stack([_bucket_np(r - c), _bucket_np(TQ + r - c), _bucket_np(2 * TQ + r - c)])
    tile_ok = np.stack([c <= r, np.ones((TQ, TK), bool), (2 * TQ + r - c) < WINDOW])
    tile_idx = np.where(tile_ok, tile_idx, MASK_BUCKET).astype(np.int32)
    t = np.arange(seq)[:, None]
    cblk = np.arange(LANES)[None, :]
    blk_end = cblk * CMP_STRIDE + CMP_LEN - 1
    n_cmp = (seq - CMP_LEN) // CMP_STRIDE + 1
    cmp_idx = np.where((blk_end <= t) & (cblk < n_cmp), _bucket_np(t - blk_end), MASK_BUCKET).astype(np.int32)
    e_t = np.where((np.arange(seq)[:, None] // SEL_LEN) == np.arange(LANES)[None, :], NEG, 0.0)
    cs = (np.arange(LANES) * CMP_STRIDE)[None, :]
    ss = (np.arange(LANES) * SEL_LEN)[:, None]
    ov_t = ((cs < ss + SEL_LEN) & (cs + CMP_LEN > ss)
            & (np.arange(LANES)[None, :] < n_cmp) & (np.arange(LANES)[:, None] < seq // SEL_LEN))
    return tile_idx, cmp_idx, e_t.astype(np.float32), ov_t.astype(np.float32)


def _layer(x, w_in, b_f, cmp_pos_k, cmp_pos_v, cmp_wk1, cmp_wk2, cmp_wv1, cmp_wv2,
           w_a, w_b, w_o, ln_g, ln_b, rel_bias, alpha):
    batch, seq, d_model = x.shape
    assert d_model == D_MODEL and seq % FOX_TQ == 0 and seq >= WIN_BLOCKS * TK and REL_MAX_DIST <= TQ
    assert seq // SEL_LEN == N_SEL_ROWS and seq // CMP_STRIDE == LANES
    x2 = x.reshape(batch * seq, d_model)

    w_t = jnp.swapaxes(w_in, 0, 1)
    trows = lambda name: w_t[_REF_OFF[name][0]:_REF_OFF[name][0] + _REF_OFF[name][1]]
    n_small = N_HEADS + N_HEADS * N_BRANCHES
    w_small_t = jnp.concatenate(
        [trows("fox_f"), trows("nsa_gate"), jnp.zeros((LANES - n_small, d_model), F32)], axis=0)
    bf_row = jnp.concatenate([b_f.astype(F32), jnp.zeros((LANES - N_HEADS,), F32)]).reshape(1, LANES)

    h_main, h_small = _proj(x2, w_t, w_small_t)
    c_col, gates = _gates(h_small, bf_row, batch, seq)
    u_a = _fox(h_main, c_col, batch, seq)

    k_cmp, v_cmp = _compress(h_main, cmp_pos_k, cmp_pos_v, cmp_wk1, cmp_wk2, cmp_wv1, cmp_wv2, batch, seq)

    tile_idx, cmp_idx, e_t, ov_t = _static_tables(seq)
    bias = rel_bias.T.astype(F32)
    table = jnp.concatenate(
        [(bias - bias[:, REL_BUCKETS - 1:]) * LOG2E, jnp.full((N_HEADS, 1), NEG, F32),
         jnp.zeros((N_HEADS, LANES - REL_BUCKETS - 1), F32)], axis=1)
    u_b, w_cat = _nsa(h_main, k_cmp, v_cmp, jnp.asarray(cmp_idx), jnp.asarray(tile_idx), table, gates,
                      jnp.asarray(e_t, BF16), jnp.asarray(ov_t, BF16), (w_a, w_b, w_o), batch, seq)

    out = _out(u_a, u_b, h_main, x2, w_cat, ln_g.reshape(1, d_model), ln_b.reshape(1, d_model), alpha)
    return out.reshape(batch, seq, d_model)


def kernel(x, w_in, b_f, cmp_pos_k, cmp_pos_v, cmp_wk1, cmp_wk2, cmp_wv1, cmp_wv2,
           w_a, w_b, w_o, ln_g, ln_b, rel_bias):
    depth = w_in.shape[0]
    alpha = (2 * depth) ** 0.25
    for layer in range(depth):
        x = _layer(x, w_in[layer], b_f[layer], cmp_pos_k[layer], cmp_pos_v[layer], cmp_wk1[layer],
                   cmp_wk2[layer], cmp_wv1[layer], cmp_wv2[layer], w_a[layer], w_b[layer], w_o[layer],
                   ln_g[layer], ln_b[layer], rel_bias, alpha)
    return x
```

```python
import functools
import math

import jax
import jax.numpy as jnp
import numpy as np
from jax import lax
from jax.experimental import pallas as pl
from jax.experimental.pallas import tpu as pltpu

F32 = jnp.float32
BF16 = jnp.bfloat16

D_MODEL = 2048
HEAD_DIM = 128
N_HEADS = 8
WIDTH = N_HEADS * HEAD_DIM
KV_GROUPS = 2
HEADS_PER_GROUP = N_HEADS // KV_GROUPS
KV_WIDTH = KV_GROUPS * HEAD_DIM
N_BRANCHES = 3
CMP_LEN = 32
CMP_STRIDE = 16
CMP_HIDDEN = 256
SEL_LEN = 64
SEL_TOPK = 8
WINDOW = 512
REL_BUCKETS = 32
REL_MAX_DIST = 128
LN_EPS = 1e-5
NEG = -1e30
LOG2E = math.log2(math.e)
Q_PRESCALE = HEAD_DIM ** -0.5 * LOG2E

LANES = 128
SUBLANES = 8
MXU_DEPTH = 256
VMEM_LIMIT = 60 * 1024 * 1024

_REF_LAYOUT = (
    ("fox_q", WIDTH), ("fox_k", WIDTH), ("fox_v", WIDTH), ("fox_f", N_HEADS), ("fox_z", WIDTH),
    ("nsa_q", WIDTH), ("nsa_k_cmp", KV_WIDTH), ("nsa_v_cmp", KV_WIDTH), ("nsa_k_sel", KV_WIDTH),
    ("nsa_v_sel", KV_WIDTH), ("nsa_k_win", KV_WIDTH), ("nsa_v_win", KV_WIDTH),
    ("nsa_gate", N_HEADS * N_BRANCHES), ("nsa_z", WIDTH), ("merge_a", D_MODEL), ("merge_b", D_MODEL),
)
_REF_OFF = {}
_o = 0
for _n, _w in _REF_LAYOUT:
    _REF_OFF[_n] = (_o, _w)
    _o += _w

_MAIN_ORDER = ("fox_q", "fox_k", "fox_v", "fox_z", "nsa_q", "nsa_z", "merge_a", "merge_b",
               "nsa_k_cmp", "nsa_v_cmp", "nsa_k_sel", "nsa_v_sel", "nsa_k_win", "nsa_v_win")
_QUERY_COLS = ("fox_q", "nsa_q")
_MAIN_OFF = {}
_o = 0
for _n in _MAIN_ORDER:
    _MAIN_OFF[_n] = _o
    _o += _REF_OFF[_n][1]
MAIN_COLS = _o
GATE_LANE0 = N_HEADS


def _dot(a, b):
    return jnp.dot(a, b, preferred_element_type=F32)


def _dot_nt(a, b):
    return lax.dot_general(a, b, (((1,), (1,)), ((), ())), preferred_element_type=F32)


def _sigmoid(x):
    return 0.5 + 0.5 * jnp.tanh(0.5 * x)


def _silu(x):
    return x * _sigmoid(x)


PROJ_TN = 512


def _proj_tiles():
    rows, is_query = [], []
    for name in _MAIN_ORDER:
        off, width = _REF_OFF[name]
        start = _MAIN_OFF[name]
        for c in range(start, start + width):
            if c % PROJ_TN == 0:
                assert (off + c - start) % SUBLANES == 0
                rows.append((off + c - start) // SUBLANES)
                is_query.append(int(name in _QUERY_COLS))
    return np.asarray(rows, np.int32), np.asarray(is_query, np.int32)


PROJ_ROWS = 4096
PROJ_RB = 512


def _proj_kernel(rows_ref, isq_ref, x_hbm, wt_ref, wst_ref, o_ref, os_ref,
                 xb_ref, stage_ref, wb_ref, sem):
    i = pl.program_id(0)
    j = pl.program_id(1)
    n_blocks = PROJ_ROWS // PROJ_RB
    scale = jnp.where(isq_ref[j] == 1, Q_PRESCALE, 1.0)
    halves = (slice(0, PROJ_TN // 2), slice(PROJ_TN // 2, PROJ_TN))

    def x_copy(r, slot):
        row0 = pl.multiple_of(i * PROJ_ROWS + r * PROJ_RB, PROJ_RB)
        return pltpu.make_async_copy(x_hbm.at[pl.ds(row0, PROJ_RB), :], stage_ref.at[slot], sem.at[slot])

    def block(r):
        rows = slice(r * PROJ_RB, (r + 1) * PROJ_RB)
        for cs in halves:
            if r == 0:
                wb_ref[cs, :] = wt_ref[cs, :].astype(BF16)
            o_ref[rows, cs] = (_dot_nt(xb_ref[rows, :], wb_ref[cs, :]) * scale).astype(o_ref.dtype)

    @pl.when(j == 0)
    def _():
        wsb = wst_ref[...].astype(BF16)
        x_copy(0, 0).start()
        for r in range(n_blocks):
            if r + 1 < n_blocks:
                x_copy(r + 1, (r + 1) % 2).start()
            x_copy(r, r % 2).wait()
            rows = slice(r * PROJ_RB, (r + 1) * PROJ_RB)
            xb_ref[rows, :] = stage_ref[r % 2].astype(BF16)
            os_ref[rows, :] = _dot_nt(xb_ref[rows, :], wsb)
            block(r)

    @pl.when(j > 0)
    def _():
        for r in range(n_blocks):
            block(r)


def _proj(x2, w_t, w_small_t):
    m, k = x2.shape
    rows, is_query = _proj_tiles()
    assert MAIN_COLS % PROJ_TN == 0 and len(rows) == MAIN_COLS // PROJ_TN and m % PROJ_ROWS == 0
    grid_spec = pltpu.PrefetchScalarGridSpec(
        num_scalar_prefetch=2,
        grid=(m // PROJ_ROWS, len(rows)),
        in_specs=[
            pl.BlockSpec(memory_space=pl.ANY),
            pl.BlockSpec((pl.Element(PROJ_TN), pl.Element(k)),
                         lambda i, j, rows, isq: (rows[j] * SUBLANES, 0)),
            pl.BlockSpec((LANES, k), lambda i, j, rows, isq: (0, 0)),
        ],
        out_specs=[
            pl.BlockSpec((PROJ_ROWS, PROJ_TN), lambda i, j, rows, isq: (i, j)),
            pl.BlockSpec((PROJ_ROWS, LANES), lambda i, j, rows, isq: (i, 0)),
        ],
        scratch_shapes=[pltpu.VMEM((PROJ_ROWS, k), BF16),
                        pltpu.VMEM((2, PROJ_RB, k), F32),
                        pltpu.VMEM((PROJ_TN, k), BF16),
                        pltpu.SemaphoreType.DMA((2,))],
    )
    return pl.pallas_call(
        _proj_kernel,
        grid_spec=grid_spec,
        out_shape=[jax.ShapeDtypeStruct((m, MAIN_COLS), BF16), jax.ShapeDtypeStruct((m, LANES), F32)],
        compiler_params=pltpu.CompilerParams(
            dimension_semantics=("parallel", "arbitrary"), vmem_limit_bytes=VMEM_LIMIT),
        name="proj",
    )(jnp.asarray(rows), jnp.asarray(is_query), x2, w_t, w_small_t)


_CUM_CHUNK = 256
_N_SPLIT = 3
ONES_LANE = _N_SPLIT * N_HEADS


def _split3(x):
    hi = x.astype(BF16)
    r1 = x - hi.astype(F32)
    mid = r1.astype(BF16)
    lo = (r1 - mid.astype(F32)).astype(BF16)
    return hi, mid, lo


def _gate_kernel(hs_ref, bf_ref, c_ref, g_ref):
    hs = hs_ref[...]
    g_ref[...] = _sigmoid(hs)
    z = hs + bf_ref[...]
    logf = jnp.minimum(z, 0.0) - jnp.log1p(jnp.exp(-jnp.abs(z)))
    n = _CUM_CHUNK
    tri = (lax.broadcasted_iota(jnp.int32, (n, n), 1)
           <= lax.broadcasted_iota(jnp.int32, (n, n), 0)).astype(BF16)
    lane = lax.broadcasted_iota(jnp.int32, (n, LANES), 1)
    carry = jnp.zeros((1, LANES), F32)
    for blk in range(hs.shape[0] // n):
        hi, mid, lo = _split3(logf[blk * n:(blk + 1) * n])
        cb = _dot(tri, hi) + _dot(tri, mid) + _dot(tri, lo) + carry
        carry = cb[n - 1:n, :]
        hi, mid, lo = _split3(cb * LOG2E)
        packed = jnp.where((lane >= ONES_LANE) & (lane < ONES_LANE + _N_SPLIT), 1.0, 0.0)
        for t, term in enumerate((hi, mid, lo)):
            shifted = term.astype(F32) if t == 0 else pltpu.roll(term.astype(F32), t * N_HEADS, 1)
            packed = jnp.where((lane >= t * N_HEADS) & (lane < (t + 1) * N_HEADS), shifted, packed)
        c_ref[blk * n:(blk + 1) * n, :] = packed.astype(c_ref.dtype)


def _gates(h_small, bf_row, batch, seq):
    return pl.pallas_call(
        _gate_kernel,
        grid=(batch,),
        in_specs=[pl.BlockSpec((seq, LANES), lambda b: (b, 0)),
                  pl.BlockSpec((1, LANES), lambda b: (0, 0))],
        out_specs=[pl.BlockSpec((seq, LANES), lambda b: (b, 0)),
                   pl.BlockSpec((seq, LANES), lambda b: (b, 0))],
        out_shape=[jax.ShapeDtypeStruct(h_small.shape, BF16), jax.ShapeDtypeStruct(h_small.shape, F32)],
        compiler_params=pltpu.CompilerParams(dimension_semantics=("parallel",)),
        name="gates",
    )(h_small, bf_row)


_HALF = CMP_LEN // 2


def _gelu_tanh(x):
    return 0.5 * x * (1.0 + jnp.tanh(math.sqrt(2.0 / math.pi) * (x + 0.044715 * (x * x * x))))


def _compress_kernel(raw_ref, pk_ref, pv_ref, w1k_ref, w2k_ref, w1v_ref, w2v_ref, kc_ref, vc_ref, raw32_ref):
    seq = raw_ref.shape[0]
    n_chunks = seq // _HALF
    step = 512
    for slab in range(raw_ref.shape[1] // HEAD_DIM):
        for blk in range(seq // step):
            raw32_ref[slab, blk * step:(blk + 1) * step, :] = (
                raw_ref[blk * step:(blk + 1) * step, slab * HEAD_DIM:(slab + 1) * HEAD_DIM].astype(F32))
    for kv, (pos_ref, w1_ref, w2_ref, out_ref) in enumerate(
            ((pk_ref, w1k_ref, w2k_ref, kc_ref), (pv_ref, w1v_ref, w2v_ref, vc_ref))):
        first = jnp.zeros((KV_GROUPS * n_chunks, CMP_HIDDEN), F32)
        second = jnp.zeros((KV_GROUPS * n_chunks, CMP_HIDDEN), F32)
        def tokens(l, pos_row):
            a = jnp.concatenate([raw32_ref[kv * KV_GROUPS + g, pl.ds(l, n_chunks, stride=_HALF), :]
                                 for g in range(KV_GROUPS)], axis=0)
            return (a + pos_ref[pos_row:pos_row + 1, :]).astype(BF16)

        for l in range(0, _HALF, 2):
            rows = slice(l * HEAD_DIM, (l + 2) * HEAD_DIM)
            rows2 = slice((_HALF + l) * HEAD_DIM, (_HALF + l + 2) * HEAD_DIM)
            first += _dot(jnp.concatenate([tokens(l, l), tokens(l + 1, l + 1)], axis=1),
                          w1_ref[rows, :].astype(BF16))
            second += _dot(jnp.concatenate([tokens(l, _HALF + l), tokens(l + 1, _HALF + l + 1)], axis=1),
                           w1_ref[rows2, :].astype(BF16))
        hid = first + jnp.concatenate(
            [pltpu.roll(second[g * n_chunks:(g + 1) * n_chunks], n_chunks - 1, 0) for g in range(KV_GROUPS)],
            axis=0)
        out = _dot(_gelu_tanh(hid).astype(BF16), w2_ref[...].astype(BF16))
        for g in range(KV_GROUPS):
            out_ref[0, :, g * HEAD_DIM:(g + 1) * HEAD_DIM] = (
                out[g * n_chunks:(g + 1) * n_chunks].astype(out_ref.dtype))


def _compress(h_main, pos_k, pos_v, w1k, w2k, w1v, w2v, batch, seq):
    n_chunks = seq // _HALF
    raw_cols = 2 * KV_WIDTH
    assert _MAIN_OFF["nsa_v_cmp"] == _MAIN_OFF["nsa_k_cmp"] + KV_WIDTH
    raw_blk = _MAIN_OFF["nsa_k_cmp"] // raw_cols
    full = lambda shape: pl.BlockSpec(shape, lambda b: (0,) * len(shape), pipeline_mode=pl.Buffered(1))
    return pl.pallas_call(
        _compress_kernel,
        grid=(batch,),
        in_specs=[pl.BlockSpec((seq, raw_cols), lambda b: (b, raw_blk)),
                  full(pos_k.shape), full(pos_v.shape),
                  full(w1k.shape), full(w2k.shape), full(w1v.shape), full(w2v.shape)],
        out_specs=[pl.BlockSpec((1, n_chunks, KV_WIDTH), lambda b: (b, 0, 0)),
                   pl.BlockSpec((1, n_chunks, KV_WIDTH), lambda b: (b, 0, 0))],
        out_shape=[jax.ShapeDtypeStruct((batch, n_chunks, KV_WIDTH), BF16),
                   jax.ShapeDtypeStruct((batch, n_chunks, KV_WIDTH), BF16)],
        scratch_shapes=[pltpu.VMEM((raw_cols // HEAD_DIM, seq, HEAD_DIM), F32)],
        compiler_params=pltpu.CompilerParams(
            dimension_semantics=("parallel",), vmem_limit_bytes=VMEM_LIMIT),
        name="compress",
    )(h_main, pos_k, pos_v, w1k, w2k, w1v, w2v)


def _lane_tile(x, n):
    return x if n == 1 else jnp.concatenate([x] * n, axis=1)


def _lane_fold(p):
    out = p[:, :LANES]
    for t in range(1, p.shape[1] // LANES):
        out = out + p[:, t * LANES:(t + 1) * LANES]
    return out


def _first_weights(s, m_ref, l_ref, rows=slice(None)):
    m = jnp.broadcast_to(jnp.max(s, axis=1, keepdims=True), (s.shape[0], LANES))
    p = jnp.exp2(s - _lane_tile(m, s.shape[1] // LANES))
    m_ref[rows, :] = m
    l_ref[rows, :] = _lane_fold(p)
    return p.astype(BF16)


def _next_weights(s, m_ref, l_ref, rows=slice(None)):
    m_prev = m_ref[rows, :]
    m_new = jnp.maximum(m_prev, jnp.max(s, axis=1, keepdims=True))
    alpha = jnp.exp2(m_prev - m_new)
    p = jnp.exp2(s - _lane_tile(m_new, s.shape[1] // LANES))
    l_ref[rows, :] = alpha * l_ref[rows, :] + _lane_fold(p)
    m_ref[rows, :] = m_new
    return alpha, p.astype(BF16)


def _skewed(stages, jobs, before_tick=None):
    state = [None] * len(jobs)
    for tick in range(len(jobs) + len(stages) - 1):
        if before_tick and tick in before_tick:
            before_tick[tick]()
        for k, stage in reversed(list(enumerate(stages))):
            j = tick - k
            if 0 <= j < len(jobs):
                state[j] = stage(jobs[j], state[j])


def _softmax_finish(l_ref, acc_ref, rows=slice(None)):
    return acc_ref[rows, :] / jnp.sum(l_ref[rows, :], axis=1, keepdims=True)


FOX_TQ = 512
FOX_HALF = FOX_TQ // 2


def _fox_routing():
    pk = np.zeros((N_HEADS, LANES, LANES), np.float32)
    for h in range(N_HEADS):
        for t in range(_N_SPLIT):
            pk[h, ONES_LANE, t * N_HEADS + h] = 1.0
            pk[h, t * N_HEADS + h, ONES_LANE + t] = -1.0
    return pk


def _cast_weight_block(step, w_blocks, w_refs, wcast_ref):
    first = 0
    for w_ref, n_blocks in zip(w_refs, w_blocks):
        @pl.when((step >= first) & (step < first + n_blocks))
        def _(w_ref=w_ref):
            wcast_ref[...] = w_ref[...].astype(wcast_ref.dtype)
        first += n_blocks


def _weight_block_specs(weights, n_steps, step_of):
    w_rows = sum(w.shape[0] for w in weights)
    rows_per_step = w_rows // n_steps
    assert w_rows % n_steps == 0 and all(w.shape[0] % rows_per_step == 0 for w in weights)
    w_blocks = tuple(w.shape[0] // rows_per_step for w in weights)
    in_specs, first = [], 0
    for n_blocks in w_blocks:
        in_specs.append(pl.BlockSpec(
            (rows_per_step, D_MODEL),
            lambda b, i, first=first, n_blocks=n_blocks: (jnp.clip(step_of(b, i) - first, 0, n_blocks - 1), 0)))
        first += n_blocks
    out_spec = pl.BlockSpec((rows_per_step, D_MODEL), lambda b, i: (step_of(b, i), 0))
    return w_blocks, in_specs, out_spec, jax.ShapeDtypeStruct((w_rows, D_MODEL), BF16)


def _fox_kernel(q_ref, k_ref, v_ref, z_ref, cq_ref, ck_ref, pk_ref, o_ref,
                kaug_ref, qaug_ref, sz_ref, m_ref, l_ref, acc_ref):
    i = pl.program_id(1)
    tq, half = FOX_TQ, FOX_HALF
    seq = k_ref.shape[0]

    @pl.when(i == 0)
    def _():
        for h in range(N_HEADS):
            for blk in range(seq // tq):
                rows = slice(blk * tq, (blk + 1) * tq)
                kaug_ref[h, rows, :HEAD_DIM] = k_ref[rows, h * HEAD_DIM:(h + 1) * HEAD_DIM]
                kaug_ref[h, rows, HEAD_DIM:] = _dot(ck_ref[rows, :], pk_ref[h]).astype(BF16)

    causal = (lax.broadcasted_iota(jnp.int32, (half, half), 1)
              <= lax.broadcasted_iota(jnp.int32, (half, half), 0))
    diag = pl.multiple_of(i * tq, tq)
    diag2 = pl.multiple_of(i * tq + half, half)
    heads = tuple(range(N_HEADS))
    rows = lambda h: slice(h * tq, (h + 1) * tq)
    top = lambda h: slice(h * tq, h * tq + half)
    bot = lambda h: slice(h * tq + half, (h + 1) * tq)
    cols = lambda h: slice(h * HEAD_DIM, (h + 1) * HEAD_DIM)

    def diag_scores(h, _):
        qaug_ref[rows(h), :HEAD_DIM] = q_ref[:, cols(h)]
        qaug_ref[rows(h), HEAD_DIM:] = cq_ref[...]
        s_left = _dot_nt(qaug_ref[rows(h), :], kaug_ref[h, pl.ds(diag, half), :])
        s_right = _dot_nt(qaug_ref[bot(h), :], kaug_ref[h, pl.ds(diag2, half), :])
        return s_left, s_right

    def diag_weights(h, scores):
        s_left, s_right = scores
        p_top = _first_weights(jnp.where(causal, s_left[:half], NEG), m_ref, l_ref, top(h))
        p_bot = _first_weights(jnp.concatenate([s_left[half:], jnp.where(causal, s_right, NEG)], axis=1),
                               m_ref, l_ref, bot(h))
        return p_top, p_bot

    def diag_values(h, weights):
        p_top, p_bot = weights
        acc_ref[top(h), :] = _dot(p_top, v_ref[pl.ds(diag, half), cols(h)])
        acc_ref[bot(h), :] = _dot(p_bot, v_ref[pl.ds(diag, tq), cols(h)])

    def output_gate(h, _):
        sz_ref[:, cols(h)] = _silu(z_ref[:, cols(h)].astype(F32))

    _skewed((output_gate, diag_scores, diag_weights, diag_values), heads)

    def body(j, carry):
        off = pl.multiple_of(j * tq, tq)

        def scores(h, _):
            return _dot_nt(qaug_ref[rows(h), :], kaug_ref[h, pl.ds(off, tq), :])

        def weights(h, s):
            return _next_weights(s, m_ref, l_ref, rows(h))

        def values(h, rescale_and_weights):
            alpha, p = rescale_and_weights
            acc_ref[rows(h), :] = alpha * acc_ref[rows(h), :] + _dot(p, v_ref[pl.ds(off, tq), cols(h)])

        _skewed((scores, weights, values), heads)
        return carry

    lax.fori_loop(0, i, body, 0)
    for h in heads:
        o_ref[:, cols(h)] = (_softmax_finish(l_ref, acc_ref, rows(h)) * sz_ref[:, cols(h)]).astype(o_ref.dtype)


def _fox(h_main, c_packed, batch, seq):
    tq = FOX_TQ
    nq = seq // tq
    blk = lambda name: _MAIN_OFF[name] // WIDTH
    pk = jnp.asarray(_fox_routing(), BF16)
    full = lambda shape: pl.BlockSpec(shape, lambda b, i: (0,) * len(shape))
    stat = pltpu.VMEM((N_HEADS * tq, LANES), F32)
    return pl.pallas_call(
        _fox_kernel,
        grid=(batch, nq),
        in_specs=[
            pl.BlockSpec((tq, WIDTH), lambda b, i: (b * nq + i, blk("fox_q"))),
            pl.BlockSpec((seq, WIDTH), lambda b, i: (b, blk("fox_k"))),
            pl.BlockSpec((seq, WIDTH), lambda b, i: (b, blk("fox_v"))),
            pl.BlockSpec((tq, WIDTH), lambda b, i: (b * nq + i, blk("fox_z"))),
            pl.BlockSpec((tq, LANES), lambda b, i: (b * nq + i, 0)),
            pl.BlockSpec((seq, LANES), lambda b, i: (b, 0)),
            full(pk.shape),
        ],
        out_specs=pl.BlockSpec((tq, WIDTH), lambda b, i: (b * nq + i, 0)),
        out_shape=jax.ShapeDtypeStruct((batch * seq, WIDTH), BF16),
        scratch_shapes=[pltpu.VMEM((N_HEADS, seq, MXU_DEPTH), BF16),
                        pltpu.VMEM((N_HEADS * tq, MXU_DEPTH), BF16),
                        pltpu.VMEM((tq, WIDTH), F32),
                        stat, stat, stat],
        compiler_params=pltpu.CompilerParams(
            dimension_semantics=("parallel", "arbitrary"), vmem_limit_bytes=VMEM_LIMIT),
        name="fox",
    )(h_main, h_main, h_main, h_main, c_packed, c_packed, pk)


TQ = 256
TK = TQ
WIN_BLOCKS = (WINDOW + TQ) // TK
SEL_NEAR_BLOCKS = 2
FAR_TK = 2 * TK
N_SEL_ROWS = 32
MASK_BUCKET = REL_BUCKETS
GROUP_ROWS = HEADS_PER_GROUP * TQ
T_DIAG, T_PREV, T_WIN2, T_NONE = range(4)
N_GATHERED_TILES = 3


def _bias_lookup(tab_ref, h, idx):
    row = jnp.broadcast_to(tab_ref[h:h + 1, :], idx.shape)
    return jnp.take_along_axis(row, idx, axis=1, mode="promise_in_bounds")


def _nsa_kernel(w_blocks, q_ref, ks_ref, vs_ref, kw_ref, vw_ref, z_ref, kc_ref, vc_ref, cidx_ref, tidx_ref,
                tab_ref, g_ref, et_ref, ov_ref, wa_ref, wb_ref, wo_ref, o_ref, wcast_ref,
                ksaug_ref, qs_ref, qaug_ref, m_ref, l_ref, acc_ref, oc_ref, ow_ref, gs_ref, t_ref, cb_ref):
    i = pl.program_id(1)
    t0 = i * TQ
    _cast_weight_block(pl.program_id(0) * pl.num_programs(1) + i, w_blocks, (wa_ref, wb_ref, wo_ref),
                       wcast_ref)

    @pl.when((pl.program_id(0) == 0) & (i == 0))
    def _():
        for h in range(N_HEADS):
            for d in range(N_GATHERED_TILES):
                for half in range(TK // LANES):
                    cs = slice(half * LANES, (half + 1) * LANES)
                    t_ref[h, d, :, cs] = _bias_lookup(tab_ref, h, tidx_ref[d, :, cs])
            for blk in range(cidx_ref.shape[0] // TQ):
                rows = slice(blk * TQ, (blk + 1) * TQ)
                cb_ref[h, rows, :] = _bias_lookup(tab_ref, h, cidx_ref[rows, :])
            t_ref[h, T_NONE] = jnp.full((TQ, TK), NEG, F32)

    @pl.when(i == 0)
    def _():
        for g in range(KV_GROUPS):
            ksaug_ref[g, :, :HEAD_DIM] = ks_ref[:, g * HEAD_DIM:(g + 1) * HEAD_DIM]
            ksaug_ref[g, :, HEAD_DIM:] = et_ref[...]

    cmaskf = (lax.broadcasted_iota(jnp.int32, (TQ, LANES), 1) * CMP_STRIDE + (CMP_LEN - 1)
              <= t0 + lax.broadcasted_iota(jnp.int32, (TQ, LANES), 0)).astype(F32)
    j_t = lax.broadcasted_iota(jnp.int32, (N_SEL_ROWS, TQ), 0)
    t_t = t0 + lax.broadcasted_iota(jnp.int32, (N_SEL_ROWS, TQ), 1)
    cur_t = t_t // SEL_LEN
    forced_t = (j_t == 0) | (j_t == cur_t) | (j_t == cur_t - 1)
    valid_t = j_t * SEL_LEN <= t_t

    def near_span(n_blocks):
        first = jnp.maximum(i - (n_blocks - 1), 0)
        return first, pl.ds(pl.multiple_of(first * TK, TK), n_blocks * TK)

    win_first, win_keys = near_span(WIN_BLOCKS)
    sel_first, sel_keys = near_span(SEL_NEAR_BLOCKS)

    def near_bias(heads, first, n_blocks):
        def tile(kk):
            dist = i - (first + kk)
            return jnp.where(dist == 0, T_DIAG, jnp.where(dist == 1, T_PREV,
                             jnp.where(dist == 2, T_WIN2, T_NONE)))
        return jnp.concatenate(
            [jnp.concatenate([t_ref[h, tile(kk)] for kk in range(n_blocks)], axis=1) for h in heads],
            axis=0)

    groups = tuple((g, g * HEAD_DIM, tuple(range(g * HEADS_PER_GROUP, (g + 1) * HEADS_PER_GROUP)),
                    slice(g * GROUP_ROWS, (g + 1) * GROUP_ROWS)) for g in range(KV_GROUPS))
    hrows = lambda h: slice(h * TQ, (h + 1) * TQ)
    for h in range(N_HEADS):
        q = q_ref[:, h * HEAD_DIM:(h + 1) * HEAD_DIM]
        qs_ref[hrows(h), :] = q
        qaug_ref[hrows(h), :HEAD_DIM] = q

    chain = {}

    def compressed_branch():
        cbias = jnp.concatenate([cb_ref[h, pl.ds(pl.multiple_of(t0, TQ), TQ), :] for h in range(N_HEADS)],
                                axis=0)
        sc = jnp.concatenate([_dot_nt(qs_ref[grows, :], kc_ref[0, :, glo:glo + HEAD_DIM])
                              for g, glo, heads, grows in groups], axis=0) + cbias
        e = jnp.exp2(sc - jnp.max(sc, axis=1, keepdims=True))
        p = e / jnp.sum(e, axis=1, keepdims=True) * jnp.concatenate([cmaskf] * N_HEADS, axis=0)
        imp_parts = []
        for g, glo, heads, grows in groups:
            oc_ref[grows, :] = _dot(p[grows].astype(BF16), vc_ref[0, :, glo:glo + HEAD_DIM])
            psum = p[hrows(heads[0])]
            for h in heads[1:]:
                psum = psum + p[hrows(h)]
            p_hi = psum.astype(BF16)
            p_lo = (psum - p_hi.astype(F32)).astype(BF16)
            imp_parts.append((_dot_nt(ov_ref[...], p_hi) + _dot_nt(ov_ref[...], p_lo))[:N_SEL_ROWS])
        chain["importance"] = jnp.concatenate(imp_parts, axis=1)

    def block_selection():
        both = lambda a: jnp.concatenate([a] * KV_GROUPS, axis=1)
        j_b = both(j_t)
        x = jnp.where(both(valid_t), jnp.where(both(forced_t), -NEG, chain["importance"]), NEG)
        groups8 = [slice(r, r + SUBLANES) for r in range(0, N_SEL_ROWS, SUBLANES)]
        cnt = [jnp.zeros((SUBLANES, x.shape[1]), F32) for _ in groups8]
        for jp in range(N_SEL_ROWS):
            row = x[jp:jp + 1, :]
            for n, rows8 in enumerate(groups8):
                if rows8.start > jp:
                    beats = row >= x[rows8]
                elif rows8.stop - 1 <= jp:
                    beats = row > x[rows8]
                else:
                    beats = (row > x[rows8]) | ((row == x[rows8]) & (j_b[rows8] > jp))
                cnt[n] = cnt[n] + jnp.where(beats, 1.0, 0.0)
        unsel_t = jnp.where(jnp.concatenate(cnt, axis=0) < SEL_TOPK, 0.0, 1.0)
        unsel_t = jnp.concatenate([unsel_t, jnp.zeros((LANES - N_SEL_ROWS, x.shape[1]), F32)], axis=0)
        unsel = unsel_t.T.astype(BF16)
        for h in range(N_HEADS):
            g = h // HEADS_PER_GROUP
            qaug_ref[hrows(h), HEAD_DIM:] = unsel[g * TQ:(g + 1) * TQ]

    def near_scores(job, _):
        branch, (g, glo, heads, grows) = job
        if branch == "window":
            return (_dot_nt(qs_ref[grows, :], kw_ref[win_keys, glo:glo + HEAD_DIM])
                    + near_bias(heads, win_first, WIN_BLOCKS))
        return (_dot_nt(qaug_ref[grows, :], ksaug_ref[g, sel_keys, :])
                + near_bias(heads, sel_first, SEL_NEAR_BLOCKS))

    def near_weights(job, s):
        branch, (g, glo, heads, grows) = job
        if branch == "window":
            p = jnp.exp2(s - jnp.max(s, axis=1, keepdims=True))
            return p.astype(BF16), _lane_fold(p)
        return _first_weights(s, m_ref, l_ref, grows)

    def near_values(job, weights):
        branch, (g, glo, heads, grows) = job
        if branch == "selected":
            acc_ref[grows, :] = _dot(weights, vs_ref[sel_keys, glo:glo + HEAD_DIM])
            return
        p, l = weights
        return _dot(p, vw_ref[win_keys, glo:glo + HEAD_DIM]) / jnp.sum(l, axis=1, keepdims=True)

    def near_gating(job, o_win):
        branch, (g, glo, heads, grows) = job
        if branch == "selected":
            return
        for n, h in enumerate(heads):
            gl = GATE_LANE0 + h * N_BRANCHES
            sz = _silu(z_ref[:, h * HEAD_DIM:(h + 1) * HEAD_DIM].astype(F32))
            gs_ref[hrows(h), :] = g_ref[:, gl + 1:gl + 2] * sz
            ow_ref[hrows(h), :] = (g_ref[:, gl:gl + 1] * oc_ref[hrows(h), :]
                                   + g_ref[:, gl + 2:gl + 3] * o_win[n * TQ:(n + 1) * TQ]) * sz

    half_groups = tuple(
        (g, glo, heads[part * 2:(part + 1) * 2], slice(heads[part * 2] * TQ, (heads[part * 2 + 1] + 1) * TQ))
        for g, glo, heads, grows in groups for part in range(HEADS_PER_GROUP // 2))
    _skewed((near_scores, near_weights, near_values, near_gating),
            tuple((branch, group) for branch in ("window", "selected") for group in half_groups),
            before_tick={0: compressed_branch, 2: block_selection})

    def sel_far(off, width):
        def scores(group, _):
            g, glo, heads, grows = group
            return _dot_nt(qaug_ref[grows, :], ksaug_ref[g, pl.ds(off, width), :])

        def weights(group, s):
            g, glo, heads, grows = group
            return _next_weights(s, m_ref, l_ref, grows)

        def values(group, rescale_and_weights):
            g, glo, heads, grows = group
            alpha, p = rescale_and_weights
            acc_ref[grows, :] = alpha * acc_ref[grows, :] + _dot(p, vs_ref[pl.ds(off, width), glo:glo + HEAD_DIM])

        _skewed((scores, weights, values), groups)

    def sel_far_pair(j, carry):
        sel_far(pl.multiple_of(j * FAR_TK, FAR_TK), FAR_TK)
        return carry

    lax.fori_loop(0, sel_first // 2, sel_far_pair, 0)

    @pl.when(sel_first % 2 == 1)
    def _():
        sel_far(pl.multiple_of((sel_first - 1) * TK, TK), TK)

    for h in range(N_HEADS):
        o_ref[:, h * HEAD_DIM:(h + 1) * HEAD_DIM] = (
            ow_ref[hrows(h), :] + gs_ref[hrows(h), :] * _softmax_finish(l_ref, acc_ref, hrows(h))
        ).astype(o_ref.dtype)


def _nsa(h_main, k_cmp, v_cmp, cmp_idx, tile_idx, table, gates, e_t, ov_t, weights, batch, seq):
    nq = seq // TQ
    w_blocks, w_in_specs, w_out_spec, w_out_shape = _weight_block_specs(
        weights, batch * nq, lambda b, i: b * nq + i)
    wblk = lambda name: _MAIN_OFF[name] // WIDTH
    kvblk = lambda name: _MAIN_OFF[name] // KV_WIDTH
    full = lambda shape: pl.BlockSpec(shape, lambda b, i: (0,) * len(shape),
                                      pipeline_mode=pl.Buffered(1))
    kv_spec = lambda name: pl.BlockSpec((seq, KV_WIDTH), lambda b, i: (b, kvblk(name)))
    n_chunks = k_cmp.shape[1]
    stat = pltpu.VMEM((N_HEADS * TQ, LANES), F32)
    return pl.pallas_call(
        functools.partial(_nsa_kernel, w_blocks),
        grid=(batch, nq),
        in_specs=[
            pl.BlockSpec((TQ, WIDTH), lambda b, i: (b * nq + i, wblk("nsa_q"))),
            kv_spec("nsa_k_sel"), kv_spec("nsa_v_sel"), kv_spec("nsa_k_win"), kv_spec("nsa_v_win"),
            pl.BlockSpec((TQ, WIDTH), lambda b, i: (b * nq + i, wblk("nsa_z"))),
            pl.BlockSpec((1, n_chunks, KV_WIDTH), lambda b, i: (b, 0, 0)),
            pl.BlockSpec((1, n_chunks, KV_WIDTH), lambda b, i: (b, 0, 0)),
            full(cmp_idx.shape), full(tile_idx.shape), full(table.shape),
            pl.BlockSpec((TQ, LANES), lambda b, i: (b * nq + i, 0)),
            full(e_t.shape), full(ov_t.shape),
            *w_in_specs,
        ],
        out_specs=[pl.BlockSpec((TQ, WIDTH), lambda b, i: (b * nq + i, 0)), w_out_spec],
        out_shape=[jax.ShapeDtypeStruct((batch * seq, WIDTH), BF16), w_out_shape],
        scratch_shapes=[pltpu.VMEM((KV_GROUPS, seq, MXU_DEPTH), BF16),
                        pltpu.VMEM((N_HEADS * TQ, HEAD_DIM), BF16),
                        pltpu.VMEM((N_HEADS * TQ, MXU_DEPTH), BF16),
                        stat, stat, stat,
                        stat, stat, stat,
                        pltpu.VMEM((N_HEADS, T_NONE + 1, TQ, TK), F32),
                        pltpu.VMEM((N_HEADS, seq, LANES), F32)],
        compiler_params=pltpu.CompilerParams(
            dimension_semantics=("arbitrary", "arbitrary"), vmem_limit_bytes=VMEM_LIMIT),
        name="nsa",
    )(h_main, h_main, h_main, h_main, h_main, h_main, k_cmp, v_cmp, cmp_idx, tile_idx, table, gates,
      e_t, ov_t, *weights)


OUT_TM = 512
OUT_TN = D_MODEL
OUT_ROW_GROUPS = 2


def _out_kernel(alpha, ua_ref, ub_ref, ga_ref, gb_ref, x_ref, wa_ref, wb_ref, wo_ref, lg_ref, lb_ref,
                o_ref, merged_ref):
    d_model = o_ref.shape[1]
    chunks = tuple(slice(c * OUT_TN, (c + 1) * OUT_TN) for c in range(d_model // OUT_TN))
    half = o_ref.shape[0] // OUT_ROW_GROUPS

    def merge(rows, _):
        for cs in chunks:
            ya = _dot(ua_ref[rows, :], wa_ref[:, cs])
            yb = _dot(ub_ref[rows, :], wb_ref[:, cs])
            merged_ref[rows, cs] = (_sigmoid(ga_ref[rows, cs].astype(F32)) * ya
                                    + _sigmoid(gb_ref[rows, cs].astype(F32)) * yb).astype(merged_ref.dtype)

    def residual(rows, _):
        total = jnp.zeros((half, LANES), F32)
        for cs in chunks:
            y = alpha * x_ref[rows, cs] + _dot(merged_ref[rows, :], wo_ref[:, cs])
            o_ref[rows, cs] = y
            total = total + _lane_fold(y)
        return total

    def layer_norm(rows, total):
        mu = jnp.sum(total, axis=1, keepdims=True) * (1.0 / d_model)
        sq = jnp.zeros((half, LANES), F32)
        for cs in chunks:
            d = o_ref[rows, cs] - mu
            sq = sq + _lane_fold(d * d)
        rstd = lax.rsqrt(jnp.sum(sq, axis=1, keepdims=True) * (1.0 / d_model) + LN_EPS)
        for cs in chunks:
            o_ref[rows, cs] = (o_ref[rows, cs] - mu) * rstd * lg_ref[:, cs] + lb_ref[:, cs]

    _skewed((merge, residual, layer_norm),
            tuple(slice(r * half, (r + 1) * half) for r in range(OUT_ROW_GROUPS)))


def _out(u_a, u_b, h_main, x2, w_cat, ln_g, ln_b, alpha, tm=OUT_TM):
    m = x2.shape[0]
    mblk = lambda name: _MAIN_OFF[name] // D_MODEL
    const = lambda shape: pl.BlockSpec(shape, lambda i: (0, 0), pipeline_mode=pl.Buffered(1))
    assert w_cat.shape == (2 * WIDTH + D_MODEL, D_MODEL) and D_MODEL == 2 * WIDTH
    w_part = lambda rows, index: pl.BlockSpec((rows, D_MODEL), lambda i: (index, 0),
                                              pipeline_mode=pl.Buffered(1))
    return pl.pallas_call(
        functools.partial(_out_kernel, alpha),
        grid=(m // tm,),
        in_specs=[
            pl.BlockSpec((tm, WIDTH), lambda i: (i, 0)),
            pl.BlockSpec((tm, WIDTH), lambda i: (i, 0)),
            pl.BlockSpec((tm, D_MODEL), lambda i: (i, mblk("merge_a"))),
            pl.BlockSpec((tm, D_MODEL), lambda i: (i, mblk("merge_b"))),
            pl.BlockSpec((tm, D_MODEL), lambda i: (i, 0)),
            w_part(WIDTH, 0), w_part(WIDTH, 1), w_part(D_MODEL, 1),
            const(ln_g.shape), const(ln_b.shape),
        ],
        out_specs=pl.BlockSpec((tm, D_MODEL), lambda i: (i, 0)),
        out_shape=jax.ShapeDtypeStruct((m, D_MODEL), F32),
        scratch_shapes=[pltpu.VMEM((tm, D_MODEL), BF16)],
        compiler_params=pltpu.CompilerParams(
            dimension_semantics=("parallel",), vmem_limit_bytes=VMEM_LIMIT),
        name="out",
    )(u_a, u_b, h_main, h_main, x2, w_cat, w_cat, w_cat, ln_g, ln_b)


def _bucket_np(dist):
    n = np.maximum(dist, 0)
    exact = REL_BUCKETS // 2
    large = exact + (np.log(np.maximum(n, 1).astype(np.float32) / exact)
                     / math.log(REL_MAX_DIST / exact) * (REL_BUCKETS - exact)).astype(np.int32)
    return np.where(n < exact, n, np.minimum(large, REL_BUCKETS - 1)).astype(np.int32)


@functools.lru_cache(maxsize=None)
def _static_tables(seq):
    r = np.arange(TQ)[:, None]
    c = np.arange(TK)[None, :]
    tile_idx = np.stack([_bucket_np(r - c), _bucket_np(TQ + r - c), _bucket_np(2 * TQ + r - c)])
    tile_ok = np.stack([c <= r, np.ones((TQ, TK), bool), (2 * TQ + r - c) < WINDOW])
    tile_idx = np.where(tile_ok, tile_idx, MASK_BUCKET).astype(np.int32)
    t = np.arange(seq)[:, None]
    cblk = np.arange(LANES)[None, :]
    blk_end = cblk * CMP_STRIDE + CMP_LEN - 1
    n_cmp = (seq - CMP_LEN) // CMP_STRIDE + 1
    cmp_idx = np.where((blk_end <= t) & (cblk < n_cmp), _bucket_np(t - blk_end), MASK_BUCKET).astype(np.int32)
    e_t = np.where((np.arange(seq)[:, None] // SEL_LEN) == np.arange(LANES)[None, :], NEG, 0.0)
    cs = (np.arange(LANES) * CMP_STRIDE)[None, :]
    ss = (np.arange(LANES) * SEL_LEN)[:, None]
    ov_t = ((cs < ss + SEL_LEN) & (cs + CMP_LEN > ss)
            & (np.arange(LANES)[None, :] < n_cmp) & (np.arange(LANES)[:, None] < seq // SEL_LEN))
    return tile_idx, cmp_idx, e_t.astype(np.float32), ov_t.astype(np.float32)


def _layer(x, w_in, b_f, cmp_pos_k, cmp_pos_v, cmp_wk1, cmp_wk2, cmp_wv1, cmp_wv2,
           w_a, w_b, w_o, ln_g, ln_b, rel_bias, alpha):
    batch, seq, d_model = x.shape
    assert d_model == D_MODEL and seq % FOX_TQ == 0 and seq >= WIN_BLOCKS * TK and REL_MAX_DIST <= TQ
    assert seq // SEL_LEN == N_SEL_ROWS and seq // CMP_STRIDE == LANES
    x2 = x.reshape(batch * seq, d_model)

    w_t = jnp.swapaxes(w_in, 0, 1)
    trows = lambda name: w_t[_REF_OFF[name][0]:_REF_OFF[name][0] + _REF_OFF[name][1]]
    n_small = N_HEADS + N_HEADS * N_BRANCHES
    w_small_t = jnp.concatenate(
        [trows("fox_f"), trows("nsa_gate"), jnp.zeros((LANES - n_small, d_model), F32)], axis=0)
    bf_row = jnp.concatenate([b_f.astype(F32), jnp.zeros((LANES - N_HEADS,), F32)]).reshape(1, LANES)

    h_main, h_small = _proj(x2, w_t, w_small_t)
    c_col, gates = _gates(h_small, bf_row, batch, seq)
    u_a = _fox(h_main, c_col, batch, seq)

    k_cmp, v_cmp = _compress(h_main, cmp_pos_k, cmp_pos_v, cmp_wk1, cmp_wk2, cmp_wv1, cmp_wv2, batch, seq)

    tile_idx, cmp_idx, e_t, ov_t = _static_tables(seq)
    bias = rel_bias.T.astype(F32)
    table = jnp.concatenate(
        [(bias - bias[:, REL_BUCKETS - 1:]) * LOG2E, jnp.full((N_HEADS, 1), NEG, F32),
         jnp.zeros((N_HEADS, LANES - REL_BUCKETS - 1), F32)], axis=1)
    u_b, w_cat = _nsa(h_main, k_cmp, v_cmp, jnp.asarray(cmp_idx), jnp.asarray(tile_idx), table, gates,
                      jnp.asarray(e_t, BF16), jnp.asarray(ov_t, BF16), (w_a, w_b, w_o), batch, seq)

    out = _out(u_a, u_b, h_main, x2, w_cat, ln_g.reshape(1, d_model), ln_b.reshape(1, d_model), alpha)
    return out.reshape(batch, seq, d_model)


def kernel(x, w_in, b_f, cmp_pos_k, cmp_pos_v, cmp_wk1, cmp_wk2, cmp_wv1, cmp_wv2,
           w_a, w_b, w_o, ln_g, ln_b, rel_bias):
    depth = w_in.shape[0]
    alpha = (2 * depth) ** 0.25
    for layer in range(depth):
        x = _layer(x, w_in[layer], b_f[layer], cmp_pos_k[layer], cmp_pos_v[layer], cmp_wk1[layer],
                   cmp_wk2[layer], cmp_wv1[layer], cmp_wv2[layer], w_a[layer], w_b[layer], w_o[layer],
                   ln_g[layer], ln_b[layer], rel_bias, alpha)
    return x
```

```python
import functools
import math

import jax
import jax.numpy as jnp
import numpy as np
from jax import lax
from jax.experimental import pallas as pl
from jax.experimental.pallas import tpu as pltpu

F32 = jnp.float32
BF16 = jnp.bfloat16

D_MODEL = 2048
HEAD_DIM = 128
N_HEADS = 8
WIDTH = N_HEADS * HEAD_DIM
KV_GROUPS = 2
HEADS_PER_GROUP = N_HEADS // KV_GROUPS
KV_WIDTH = KV_GROUPS * HEAD_DIM
N_BRANCHES = 3
CMP_LEN = 32
CMP_STRIDE = 16
CMP_HIDDEN = 256
SEL_LEN = 64
SEL_TOPK = 8
WINDOW = 512
REL_BUCKETS = 32
REL_MAX_DIST = 128
LN_EPS = 1e-5
NEG = -1e30
LOG2E = math.log2(math.e)
Q_PRESCALE = HEAD_DIM ** -0.5 * LOG2E

LANES = 128
SUBLANES = 8
MXU_DEPTH = 256
VMEM_LIMIT = 60 * 1024 * 1024

_REF_LAYOUT = (
    ("fox_q", WIDTH), ("fox_k", WIDTH), ("fox_v", WIDTH), ("fox_f", N_HEADS), ("fox_z", WIDTH),
    ("nsa_q", WIDTH), ("nsa_k_cmp", KV_WIDTH), ("nsa_v_cmp", KV_WIDTH), ("nsa_k_sel", KV_WIDTH),
    ("nsa_v_sel", KV_WIDTH), ("nsa_k_win", KV_WIDTH), ("nsa_v_win", KV_WIDTH),
    ("nsa_gate", N_HEADS * N_BRANCHES), ("nsa_z", WIDTH), ("merge_a", D_MODEL), ("merge_b", D_MODEL),
)
_REF_OFF = {}
_o = 0
for _n, _w in _REF_LAYOUT:
    _REF_OFF[_n] = (_o, _w)
    _o += _w

_MAIN_ORDER = ("fox_q", "fox_k", "fox_v", "fox_z", "nsa_q", "nsa_z", "merge_a", "merge_b",
               "nsa_k_cmp", "nsa_v_cmp", "nsa_k_sel", "nsa_v_sel", "nsa_k_win", "nsa_v_win")
_QUERY_COLS = ("fox_q", "nsa_q")
_MAIN_OFF = {}
_o = 0
for _n in _MAIN_ORDER:
    _MAIN_OFF[_n] = _o
    _o += _REF_OFF[_n][1]
MAIN_COLS = _o
GATE_LANE0 = N_HEADS


def _dot(a, b):
    return jnp.dot(a, b, preferred_element_type=F32)


def _dot_nt(a, b):
    return lax.dot_general(a, b, (((1,), (1,)), ((), ())), preferred_element_type=F32)


def _sigmoid(x):
    return 0.5 + 0.5 * jnp.tanh(0.5 * x)


def _silu(x):
    return x * _sigmoid(x)


PROJ_TN = 512


def _proj_tiles():
    rows, is_query = [], []
    for name in _MAIN_ORDER:
        off, width = _REF_OFF[name]
        start = _MAIN_OFF[name]
        for c in range(start, start + width):
            if c % PROJ_TN == 0:
                assert (off + c - start) % SUBLANES == 0
                rows.append((off + c - start) // SUBLANES)
                is_query.append(int(name in _QUERY_COLS))
    return np.asarray(rows, np.int32), np.asarray(is_query, np.int32)


PROJ_ROWS = 4096
PROJ_RB = 512


def _proj_kernel(rows_ref, isq_ref, x_hbm, wt_ref, wst_ref, o_ref, os_ref,
                 xb_ref, stage_ref, wb_ref, sem):
    i = pl.program_id(0)
    j = pl.program_id(1)
    n_blocks = PROJ_ROWS // PROJ_RB
    scale = jnp.where(isq_ref[j] == 1, Q_PRESCALE, 1.0)
    halves = (slice(0, PROJ_TN // 2), slice(PROJ_TN // 2, PROJ_TN))

    def x_copy(r, slot):
        row0 = pl.multiple_of(i * PROJ_ROWS + r * PROJ_RB, PROJ_RB)
        return pltpu.make_async_copy(x_hbm.at[pl.ds(row0, PROJ_RB), :], stage_ref.at[slot], sem.at[slot])

    def block(r):
        rows = slice(r * PROJ_RB, (r + 1) * PROJ_RB)
        for cs in halves:
            if r == 0:
                wb_ref[cs, :] = wt_ref[cs, :].astype(BF16)
            o_ref[rows, cs] = (_dot_nt(xb_ref[rows, :], wb_ref[cs, :]) * scale).astype(o_ref.dtype)

    @pl.when(j == 0)
    def _():
        wsb = wst_ref[...].astype(BF16)
        x_copy(0, 0).start()
        for r in range(n_blocks):
            if r + 1 < n_blocks:
                x_copy(r + 1, (r + 1) % 2).start()
            x_copy(r, r % 2).wait()
            rows = slice(r * PROJ_RB, (r + 1) * PROJ_RB)
            xb_ref[rows, :] = stage_ref[r % 2].astype(BF16)
            os_ref[rows, :] = _dot_nt(xb_ref[rows, :], wsb)
            block(r)

    @pl.when(j > 0)
    def _():
        for r in range(n_blocks):
            block(r)


def _proj(x2, w_t, w_small_t):
    m, k = x2.shape
    rows, is_query = _proj_tiles()
    assert MAIN_COLS % PROJ_TN == 0 and len(rows) == MAIN_COLS // PROJ_TN and m % PROJ_ROWS == 0
    grid_spec = pltpu.PrefetchScalarGridSpec(
        num_scalar_prefetch=2,
        grid=(m // PROJ_ROWS, len(rows)),
        in_specs=[
            pl.BlockSpec(memory_space=pl.ANY),
            pl.BlockSpec((pl.Element(PROJ_TN), pl.Element(k)),
                         lambda i, j, rows, isq: (rows[j] * SUBLANES, 0)),
            pl.BlockSpec((LANES, k), lambda i, j, rows, isq: (0, 0)),
        ],
        out_specs=[
            pl.BlockSpec((PROJ_ROWS, PROJ_TN), lambda i, j, rows, isq: (i, j)),
            pl.BlockSpec((PROJ_ROWS, LANES), lambda i, j, rows, isq: (i, 0)),
        ],
        scratch_shapes=[pltpu.VMEM((PROJ_ROWS, k), BF16),
                        pltpu.VMEM((2, PROJ_RB, k), F32),
                        pltpu.VMEM((PROJ_TN, k), BF16),
                        pltpu.SemaphoreType.DMA((2,))],
    )
    return pl.pallas_call(
        _proj_kernel,
        grid_spec=grid_spec,
        out_shape=[jax.ShapeDtypeStruct((m, MAIN_COLS), BF16), jax.ShapeDtypeStruct((m, LANES), F32)],
        compiler_params=pltpu.CompilerParams(
            dimension_semantics=("parallel", "arbitrary"), vmem_limit_bytes=VMEM_LIMIT),
        name="proj",
    )(jnp.asarray(rows), jnp.asarray(is_query), x2, w_t, w_small_t)


_CUM_CHUNK = 256
_N_SPLIT = 3
ONES_LANE = _N_SPLIT * N_HEADS


def _split3(x):
    hi = x.astype(BF16)
    r1 = x - hi.astype(F32)
    mid = r1.astype(BF16)
    lo = (r1 - mid.astype(F32)).astype(BF16)
    return hi, mid, lo


def _gate_kernel(hs_ref, bf_ref, c_ref, g_ref):
    hs = hs_ref[...]
    g_ref[...] = _sigmoid(hs)
    z = hs + bf_ref[...]
    logf = jnp.minimum(z, 0.0) - jnp.log1p(jnp.exp(-jnp.abs(z)))
    n = _CUM_CHUNK
    tri = (lax.broadcasted_iota(jnp.int32, (n, n), 1)
           <= lax.broadcasted_iota(jnp.int32, (n, n), 0)).astype(BF16)
    lane = lax.broadcasted_iota(jnp.int32, (n, LANES), 1)
    carry = jnp.zeros((1, LANES), F32)
    for blk in range(hs.shape[0] // n):
        parts = _dot(tri, jnp.concatenate(_split3(logf[blk * n:(blk + 1) * n]), axis=1))
        cb = parts[:, :LANES] + parts[:, LANES:2 * LANES] + parts[:, 2 * LANES:] + carry
        carry = cb[n - 1:n, :]
        hi, mid, lo = _split3(cb * LOG2E)
        packed = jnp.where((lane >= ONES_LANE) & (lane < ONES_LANE + _N_SPLIT), 1.0, 0.0)
        for t, term in enumerate((hi, mid, lo)):
            shifted = term.astype(F32) if t == 0 else pltpu.roll(term.astype(F32), t * N_HEADS, 1)
            packed = jnp.where((lane >= t * N_HEADS) & (lane < (t + 1) * N_HEADS), shifted, packed)
        c_ref[blk * n:(blk + 1) * n, :] = packed.astype(c_ref.dtype)


def _gates(h_small, bf_row, batch, seq):
    return pl.pallas_call(
        _gate_kernel,
        grid=(batch,),
        in_specs=[pl.BlockSpec((seq, LANES), lambda b: (b, 0)),
                  pl.BlockSpec((1, LANES), lambda b: (0, 0))],
        out_specs=[pl.BlockSpec((seq, LANES), lambda b: (b, 0)),
                   pl.BlockSpec((seq, LANES), lambda b: (b, 0))],
        out_shape=[jax.ShapeDtypeStruct(h_small.shape, BF16), jax.ShapeDtypeStruct(h_small.shape, F32)],
        compiler_params=pltpu.CompilerParams(dimension_semantics=("parallel",)),
        name="gates",
    )(h_small, bf_row)


_HALF = CMP_LEN // 2


def _gelu_tanh(x):
    return 0.5 * x * (1.0 + jnp.tanh(math.sqrt(2.0 / math.pi) * (x + 0.044715 * (x * x * x))))


def _compress_kernel(raw_ref, pk_ref, pv_ref, w1k_ref, w2k_ref, w1v_ref, w2v_ref, kc_ref, vc_ref, raw32_ref):
    seq = raw_ref.shape[0]
    n_chunks = seq // _HALF
    step = 512
    for slab in range(raw_ref.shape[1] // HEAD_DIM):
        for blk in range(seq // step):
            raw32_ref[slab, blk * step:(blk + 1) * step, :] = (
                raw_ref[blk * step:(blk + 1) * step, slab * HEAD_DIM:(slab + 1) * HEAD_DIM].astype(F32))
    for kv, (pos_ref, w1_ref, w2_ref, out_ref) in enumerate(
            ((pk_ref, w1k_ref, w2k_ref, kc_ref), (pv_ref, w1v_ref, w2v_ref, vc_ref))):
        first = jnp.zeros((KV_GROUPS * n_chunks, CMP_HIDDEN), F32)
        second = jnp.zeros((KV_GROUPS * n_chunks, CMP_HIDDEN), F32)
        def tokens(l, pos_row):
            a = jnp.concatenate([raw32_ref[kv * KV_GROUPS + g, pl.ds(l, n_chunks, stride=_HALF), :]
                                 for g in range(KV_GROUPS)], axis=0)
            return (a + pos_ref[pos_row:pos_row + 1, :]).astype(BF16)

        for l in range(0, _HALF, 2):
            rows = slice(l * HEAD_DIM, (l + 2) * HEAD_DIM)
            rows2 = slice((_HALF + l) * HEAD_DIM, (_HALF + l + 2) * HEAD_DIM)
            first += _dot(jnp.concatenate([tokens(l, l), tokens(l + 1, l + 1)], axis=1),
                          w1_ref[rows, :].astype(BF16))
            second += _dot(jnp.concatenate([tokens(l, _HALF + l), tokens(l + 1, _HALF + l + 1)], axis=1),
                           w1_ref[rows2, :].astype(BF16))
        hid = first + jnp.concatenate(
            [pltpu.roll(second[g * n_chunks:(g + 1) * n_chunks], n_chunks - 1, 0) for g in range(KV_GROUPS)],
            axis=0)
        out = _dot(_gelu_tanh(hid).astype(BF16), w2_ref[...].astype(BF16))
        for g in range(KV_GROUPS):
            out_ref[0, :, g * HEAD_DIM:(g + 1) * HEAD_DIM] = (
                out[g * n_chunks:(g + 1) * n_chunks].astype(out_ref.dtype))


def _compress(h_main, pos_k, pos_v, w1k, w2k, w1v, w2v, batch, seq):
    n_chunks = seq // _HALF
    raw_cols = 2 * KV_WIDTH
    assert _MAIN_OFF["nsa_v_cmp"] == _MAIN_OFF["nsa_k_cmp"] + KV_WIDTH
    raw_blk = _MAIN_OFF["nsa_k_cmp"] // raw_cols
    full = lambda shape: pl.BlockSpec(shape, lambda b: (0,) * len(shape), pipeline_mode=pl.Buffered(1))
    return pl.pallas_call(
        _compress_kernel,
        grid=(batch,),
        in_specs=[pl.BlockSpec((seq, raw_cols), lambda b: (b, raw_blk)),
                  full(pos_k.shape), full(pos_v.shape),
                  full(w1k.shape), full(w2k.shape), full(w1v.shape), full(w2v.shape)],
        out_specs=[pl.BlockSpec((1, n_chunks, KV_WIDTH), lambda b: (b, 0, 0)),
                   pl.BlockSpec((1, n_chunks, KV_WIDTH), lambda b: (b, 0, 0))],
        out_shape=[jax.ShapeDtypeStruct((batch, n_chunks, KV_WIDTH), BF16),
                   jax.ShapeDtypeStruct((batch, n_chunks, KV_WIDTH), BF16)],
        scratch_shapes=[pltpu.VMEM((raw_cols // HEAD_DIM, seq, HEAD_DIM), F32)],
        compiler_params=pltpu.CompilerParams(
            dimension_semantics=("parallel",), vmem_limit_bytes=VMEM_LIMIT),
        name="compress",
    )(h_main, pos_k, pos_v, w1k, w2k, w1v, w2v)


def _lane_tile(x, n):
    return x if n == 1 else jnp.concatenate([x] * n, axis=1)


def _lane_fold(p):
    out = p[:, :LANES]
    for t in range(1, p.shape[1] // LANES):
        out = out + p[:, t * LANES:(t + 1) * LANES]
    return out


def _first_weights(s, m_ref, l_ref, rows=slice(None)):
    m = jnp.broadcast_to(jnp.max(s, axis=1, keepdims=True), (s.shape[0], LANES))
    p = jnp.exp2(s - _lane_tile(m, s.shape[1] // LANES))
    m_ref[rows, :] = m
    l_ref[rows, :] = _lane_fold(p)
    return p.astype(BF16)


def _next_weights(s, m_ref, l_ref, rows=slice(None)):
    m_prev = m_ref[rows, :]
    m_new = jnp.maximum(m_prev, jnp.max(s, axis=1, keepdims=True))
    alpha = jnp.exp2(m_prev - m_new)
    p = jnp.exp2(s - _lane_tile(m_new, s.shape[1] // LANES))
    l_ref[rows, :] = alpha * l_ref[rows, :] + _lane_fold(p)
    m_ref[rows, :] = m_new
    return alpha, p.astype(BF16)


def _skewed(stages, jobs, before_tick=None):
    state = [None] * len(jobs)
    for tick in range(len(jobs) + len(stages) - 1):
        if before_tick and tick in before_tick:
            before_tick[tick]()
        for k, stage in reversed(list(enumerate(stages))):
            j = tick - k
            if 0 <= j < len(jobs):
                state[j] = stage(jobs[j], state[j])


def _softmax_finish(l_ref, acc_ref, rows=slice(None)):
    return acc_ref[rows, :] / jnp.sum(l_ref[rows, :], axis=1, keepdims=True)


FOX_TQ = 512
FOX_HALF = FOX_TQ // 2


def _fox_routing():
    pk = np.zeros((N_HEADS, LANES, LANES), np.float32)
    for h in range(N_HEADS):
        for t in range(_N_SPLIT):
            pk[h, ONES_LANE, t * N_HEADS + h] = 1.0
            pk[h, t * N_HEADS + h, ONES_LANE + t] = -1.0
    return pk


def _cast_weight_block(step, w_blocks, w_refs, wcast_ref):
    first = 0
    for w_ref, n_blocks in zip(w_refs, w_blocks):
        @pl.when((step >= first) & (step < first + n_blocks))
        def _(w_ref=w_ref):
            wcast_ref[...] = w_ref[...].astype(wcast_ref.dtype)
        first += n_blocks


def _weight_block_specs(weights, n_steps, step_of):
    w_rows = sum(w.shape[0] for w in weights)
    rows_per_step = w_rows // n_steps
    assert w_rows % n_steps == 0 and all(w.shape[0] % rows_per_step == 0 for w in weights)
    w_blocks = tuple(w.shape[0] // rows_per_step for w in weights)
    in_specs, first = [], 0
    for n_blocks in w_blocks:
        in_specs.append(pl.BlockSpec(
            (rows_per_step, D_MODEL),
            lambda b, i, first=first, n_blocks=n_blocks: (jnp.clip(step_of(b, i) - first, 0, n_blocks - 1), 0)))
        first += n_blocks
    out_spec = pl.BlockSpec((rows_per_step, D_MODEL), lambda b, i: (step_of(b, i), 0))
    return w_blocks, in_specs, out_spec, jax.ShapeDtypeStruct((w_rows, D_MODEL), BF16)


def _fox_kernel(q_ref, k_ref, v_ref, z_ref, cq_ref, ck_ref, pk_ref, o_ref,
                kaug_ref, qaug_ref, sz_ref, m_ref, l_ref, acc_ref):
    i = pl.program_id(1)
    tq, half = FOX_TQ, FOX_HALF
    seq = k_ref.shape[0]

    @pl.when(i == 0)
    def _():
        for h in range(N_HEADS):
            for blk in range(seq // tq):
                rows = slice(blk * tq, (blk + 1) * tq)
                kaug_ref[h, rows, :HEAD_DIM] = k_ref[rows, h * HEAD_DIM:(h + 1) * HEAD_DIM]
                kaug_ref[h, rows, HEAD_DIM:] = _dot(ck_ref[rows, :], pk_ref[h]).astype(BF16)

    causal = (lax.broadcasted_iota(jnp.int32, (half, half), 1)
              <= lax.broadcasted_iota(jnp.int32, (half, half), 0))
    diag = pl.multiple_of(i * tq, tq)
    diag2 = pl.multiple_of(i * tq + half, half)
    heads = tuple(range(N_HEADS))
    rows = lambda h: slice(h * tq, (h + 1) * tq)
    top = lambda h: slice(h * tq, h * tq + half)
    bot = lambda h: slice(h * tq + half, (h + 1) * tq)
    cols = lambda h: slice(h * HEAD_DIM, (h + 1) * HEAD_DIM)

    def diag_scores(h, _):
        qaug_ref[rows(h), :HEAD_DIM] = q_ref[:, cols(h)]
        qaug_ref[rows(h), HEAD_DIM:] = cq_ref[...]
        s_left = _dot_nt(qaug_ref[rows(h), :], kaug_ref[h, pl.ds(diag, half), :])
        s_right = _dot_nt(qaug_ref[bot(h), :], kaug_ref[h, pl.ds(diag2, half), :])
        return s_left, s_right

    def diag_weights(h, scores):
        s_left, s_right = scores
        p_top = _first_weights(jnp.where(causal, s_left[:half], NEG), m_ref, l_ref, top(h))
        p_bot = _first_weights(jnp.concatenate([s_left[half:], jnp.where(causal, s_right, NEG)], axis=1),
                               m_ref, l_ref, bot(h))
        return p_top, p_bot

    def diag_values(h, weights):
        p_top, p_bot = weights
        acc_ref[top(h), :] = _dot(p_top, v_ref[pl.ds(diag, half), cols(h)])
        acc_ref[bot(h), :] = _dot(p_bot, v_ref[pl.ds(diag, tq), cols(h)])

    def output_gate(h, _):
        sz_ref[:, cols(h)] = _silu(z_ref[:, cols(h)].astype(F32))

    _skewed((output_gate, diag_scores, diag_weights, diag_values), heads)

    def body(j, carry):
        off = pl.multiple_of(j * tq, tq)

        def scores(h, _):
            return _dot_nt(qaug_ref[rows(h), :], kaug_ref[h, pl.ds(off, tq), :])

        def weights(h, s):
            return _next_weights(s, m_ref, l_ref, rows(h))

        def values(h, rescale_and_weights):
            alpha, p = rescale_and_weights
            acc_ref[rows(h), :] = alpha * acc_ref[rows(h), :] + _dot(p, v_ref[pl.ds(off, tq), cols(h)])

        _skewed((scores, weights, values), heads)
        return carry

    lax.fori_loop(0, i, body, 0)
    for h in heads:
        o_ref[:, cols(h)] = (_softmax_finish(l_ref, acc_ref, rows(h)) * sz_ref[:, cols(h)]).astype(o_ref.dtype)


def _fox(h_main, c_packed, batch, seq):
    tq = FOX_TQ
    nq = seq // tq
    blk = lambda name: _MAIN_OFF[name] // WIDTH
    pk = jnp.asarray(_fox_routing(), BF16)
    full = lambda shape: pl.BlockSpec(shape, lambda b, i: (0,) * len(shape))
    stat = pltpu.VMEM((N_HEADS * tq, LANES), F32)
    return pl.pallas_call(
        _fox_kernel,
        grid=(batch, nq),
        in_specs=[
            pl.BlockSpec((tq, WIDTH), lambda b, i: (b * nq + i, blk("fox_q"))),
            pl.BlockSpec((seq, WIDTH), lambda b, i: (b, blk("fox_k"))),
            pl.BlockSpec((seq, WIDTH), lambda b, i: (b, blk("fox_v"))),
            pl.BlockSpec((tq, WIDTH), lambda b, i: (b * nq + i, blk("fox_z"))),
            pl.BlockSpec((tq, LANES), lambda b, i: (b * nq + i, 0)),
            pl.BlockSpec((seq, LANES), lambda b, i: (b, 0)),
            full(pk.shape),
        ],
        out_specs=pl.BlockSpec((tq, WIDTH), lambda b, i: (b * nq + i, 0)),
        out_shape=jax.ShapeDtypeStruct((batch * seq, WIDTH), BF16),
        scratch_shapes=[pltpu.VMEM((N_HEADS, seq, MXU_DEPTH), BF16),
                        pltpu.VMEM((N_HEADS * tq, MXU_DEPTH), BF16),
                        pltpu.VMEM((tq, WIDTH), F32),
                        stat, stat, stat],
        compiler_params=pltpu.CompilerParams(
            dimension_semantics=("parallel", "arbitrary"), vmem_limit_bytes=VMEM_LIMIT),
        name="fox",
    )(h_main, h_main, h_main, h_main, c_packed, c_packed, pk)


TQ = 256
TK = TQ
WIN_BLOCKS = (WINDOW + TQ) // TK
SEL_NEAR_BLOCKS = 2
FAR_TK = 2 * TK
N_SEL_ROWS = 32
MASK_BUCKET = REL_BUCKETS
GROUP_ROWS = HEADS_PER_GROUP * TQ
T_DIAG, T_PREV, T_WIN2, T_NONE = range(4)
N_GATHERED_TILES = 3


def _bias_lookup(tab_ref, h, idx):
    row = jnp.broadcast_to(tab_ref[h:h + 1, :], idx.shape)
    return jnp.take_along_axis(row, idx, axis=1, mode="promise_in_bounds")


def _nsa_kernel(w_blocks, q_ref, ks_ref, vs_ref, kw_ref, vw_ref, z_ref, kc_ref, vc_ref, cidx_ref, tidx_ref,
                tab_ref, g_ref, et_ref, ov_ref, wa_ref, wb_ref, wo_ref, o_ref, wcast_ref,
                ksaug_ref, qs_ref, qaug_ref, m_ref, l_ref, acc_ref, oc_ref, ow_ref, gs_ref, t_ref, cb_ref):
    i = pl.program_id(1)
    t0 = i * TQ
    _cast_weight_block(pl.program_id(0) * pl.num_programs(1) + i, w_blocks, (wa_ref, wb_ref, wo_ref),
                       wcast_ref)

    @pl.when((pl.program_id(0) == 0) & (i == 0))
    def _():
        bucket_tiles = _static_tables(cidx_ref.shape[0])[0]
        for h in range(N_HEADS):
            far = jnp.broadcast_to(tab_ref[h:h + 1, REL_BUCKETS - 1:REL_BUCKETS], (LANES, LANES))
            for d in range(N_GATHERED_TILES):
                for rs in (slice(r, r + LANES) for r in range(0, TQ, LANES)):
                    for cs in (slice(c, c + LANES) for c in range(0, TK, LANES)):
                        if np.isin(bucket_tiles[d, rs, cs], (REL_BUCKETS - 1, MASK_BUCKET)).all():
                            t_ref[h, d, rs, cs] = jnp.where(tidx_ref[d, rs, cs] == MASK_BUCKET, NEG, far)
                        else:
                            t_ref[h, d, rs, cs] = _bias_lookup(tab_ref, h, tidx_ref[d, rs, cs])
            for blk in range(cidx_ref.shape[0] // TQ):
                rows = slice(blk * TQ, (blk + 1) * TQ)
                cb_ref[h, rows, :] = _bias_lookup(tab_ref, h, cidx_ref[rows, :])
            t_ref[h, T_NONE] = jnp.full((TQ, TK), NEG, F32)

    @pl.when(i == 0)
    def _():
        for g in range(KV_GROUPS):
            ksaug_ref[g, :, :HEAD_DIM] = ks_ref[:, g * HEAD_DIM:(g + 1) * HEAD_DIM]
            ksaug_ref[g, :, HEAD_DIM:] = et_ref[...]

    cmaskf = (lax.broadcasted_iota(jnp.int32, (TQ, LANES), 1) * CMP_STRIDE + (CMP_LEN - 1)
              <= t0 + lax.broadcasted_iota(jnp.int32, (TQ, LANES), 0)).astype(F32)
    j_t = lax.broadcasted_iota(jnp.int32, (N_SEL_ROWS, TQ), 0)
    t_t = t0 + lax.broadcasted_iota(jnp.int32, (N_SEL_ROWS, TQ), 1)
    cur_t = t_t // SEL_LEN
    forced_t = (j_t == 0) | (j_t == cur_t) | (j_t == cur_t - 1)
    valid_t = j_t * SEL_LEN <= t_t

    def near_span(n_blocks):
        first = jnp.maximum(i - (n_blocks - 1), 0)
        return first, pl.ds(pl.multiple_of(first * TK, TK), n_blocks * TK)

    win_first, win_keys = near_span(WIN_BLOCKS)
    sel_first, sel_keys = near_span(SEL_NEAR_BLOCKS)

    def near_bias(heads, first, n_blocks):
        def tile(kk):
            dist = i - (first + kk)
            return jnp.where(dist == 0, T_DIAG, jnp.where(dist == 1, T_PREV,
                             jnp.where(dist == 2, T_WIN2, T_NONE)))
        return jnp.concatenate(
            [jnp.concatenate([t_ref[h, tile(kk)] for kk in range(n_blocks)], axis=1) for h in heads],
            axis=0)

    groups = tuple((g, g * HEAD_DIM, tuple(range(g * HEADS_PER_GROUP, (g + 1) * HEADS_PER_GROUP)),
                    slice(g * GROUP_ROWS, (g + 1) * GROUP_ROWS)) for g in range(KV_GROUPS))
    hrows = lambda h: slice(h * TQ, (h + 1) * TQ)
    for h in range(N_HEADS):
        q = q_ref[:, h * HEAD_DIM:(h + 1) * HEAD_DIM]
        qs_ref[hrows(h), :] = q
        qaug_ref[hrows(h), :HEAD_DIM] = q

    chain = {}

    def compressed_branch():
        cbias = jnp.concatenate([cb_ref[h, pl.ds(pl.multiple_of(t0, TQ), TQ), :] for h in range(N_HEADS)],
                                axis=0)
        sc = jnp.concatenate([_dot_nt(qs_ref[grows, :], kc_ref[0, :, glo:glo + HEAD_DIM])
                              for g, glo, heads, grows in groups], axis=0) + cbias
        e = jnp.exp2(sc - jnp.max(sc, axis=1, keepdims=True))
        p = e / jnp.sum(e, axis=1, keepdims=True) * jnp.concatenate([cmaskf] * N_HEADS, axis=0)
        imp_parts = []
        for g, glo, heads, grows in groups:
            oc_ref[grows, :] = _dot(p[grows].astype(BF16), vc_ref[0, :, glo:glo + HEAD_DIM])
            psum = p[hrows(heads[0])]
            for h in heads[1:]:
                psum = psum + p[hrows(h)]
            p_hi = psum.astype(BF16)
            p_lo = (psum - p_hi.astype(F32)).astype(BF16)
            imp_parts.append((_dot_nt(ov_ref[...], p_hi) + _dot_nt(ov_ref[...], p_lo))[:N_SEL_ROWS])
        chain["importance"] = jnp.concatenate(imp_parts, axis=1)

    def block_selection():
        both = lambda a: jnp.concatenate([a] * KV_GROUPS, axis=1)
        j_b = both(j_t)
        x = jnp.where(both(valid_t), jnp.where(both(forced_t), -NEG, chain["importance"]), NEG)
        groups8 = [slice(r, r + SUBLANES) for r in range(0, N_SEL_ROWS, SUBLANES)]
        cnt = [jnp.zeros((SUBLANES, x.shape[1]), F32) for _ in groups8]
        for jp in range(N_SEL_ROWS):
            row = x[jp:jp + 1, :]
            for n, rows8 in enumerate(groups8):
                if rows8.start > jp:
                    beats = row >= x[rows8]
                elif rows8.stop - 1 <= jp:
                    beats = row > x[rows8]
                else:
                    beats = (row > x[rows8]) | ((row == x[rows8]) & (j_b[rows8] > jp))
                cnt[n] = cnt[n] + jnp.where(beats, 1.0, 0.0)
        unsel_t = jnp.where(jnp.concatenate(cnt, axis=0) < SEL_TOPK, 0.0, 1.0)
        unsel_t = jnp.concatenate([unsel_t, jnp.zeros((LANES - N_SEL_ROWS, x.shape[1]), F32)], axis=0)
        unsel = unsel_t.T.astype(BF16)
        for h in range(N_HEADS):
            g = h // HEADS_PER_GROUP
            qaug_ref[hrows(h), HEAD_DIM:] = unsel[g * TQ:(g + 1) * TQ]

    def near_scores(job, _):
        branch, (g, glo, heads, grows) = job
        if branch == "window":
            return (_dot_nt(qs_ref[grows, :], kw_ref[win_keys, glo:glo + HEAD_DIM])
                    + near_bias(heads, win_first, WIN_BLOCKS))
        return (_dot_nt(qaug_ref[grows, :], ksaug_ref[g, sel_keys, :])
                + near_bias(heads, sel_first, SEL_NEAR_BLOCKS))

    def near_weights(job, s):
        branch, (g, glo, heads, grows) = job
        if branch == "window":
            p = jnp.exp2(s - jnp.max(s, axis=1, keepdims=True))
            return p.astype(BF16), _lane_fold(p)
        return _first_weights(s, m_ref, l_ref, grows)

    def near_values(job, weights):
        branch, (g, glo, heads, grows) = job
        if branch == "selected":
            acc_ref[grows, :] = _dot(weights, vs_ref[sel_keys, glo:glo + HEAD_DIM])
            return
        p, l = weights
        return _dot(p, vw_ref[win_keys, glo:glo + HEAD_DIM]) / jnp.sum(l, axis=1, keepdims=True)

    def near_gating(job, o_win):
        branch, (g, glo, heads, grows) = job
        if branch == "selected":
            return
        for n, h in enumerate(heads):
            gl = GATE_LANE0 + h * N_BRANCHES
            sz = _silu(z_ref[:, h * HEAD_DIM:(h + 1) * HEAD_DIM].astype(F32))
            gs_ref[hrows(h), :] = g_ref[:, gl + 1:gl + 2] * sz
            ow_ref[hrows(h), :] = (g_ref[:, gl:gl + 1] * oc_ref[hrows(h), :]
                                   + g_ref[:, gl + 2:gl + 3] * o_win[n * TQ:(n + 1) * TQ]) * sz

    half_groups = tuple(
        (g, glo, heads[part * 2:(part + 1) * 2], slice(heads[part * 2] * TQ, (heads[part * 2 + 1] + 1) * TQ))
        for g, glo, heads, grows in groups for part in range(HEADS_PER_GROUP // 2))
    _skewed((near_scores, near_weights, near_values, near_gating),
            tuple((branch, group) for branch in ("window", "selected") for group in half_groups),
            before_tick={0: compressed_branch, 2: block_selection})

    def sel_far(off, width):
        def scores(group, _):
            g, glo, heads, grows = group
            return _dot_nt(qaug_ref[grows, :], ksaug_ref[g, pl.ds(off, width), :])

        def weights(group, s):
            g, glo, heads, grows = group
            return _next_weights(s, m_ref, l_ref, grows)

        def values(group, rescale_and_weights):
            g, glo, heads, grows = group
            alpha, p = rescale_and_weights
            acc_ref[grows, :] = alpha * acc_ref[grows, :] + _dot(p, vs_ref[pl.ds(off, width), glo:glo + HEAD_DIM])

        _skewed((scores, weights, values), groups)

    def sel_far_pair(j, carry):
        sel_far(pl.multiple_of(j * FAR_TK, FAR_TK), FAR_TK)
        return carry

    lax.fori_loop(0, sel_first // 2, sel_far_pair, 0)

    @pl.when(sel_first % 2 == 1)
    def _():
        sel_far(pl.multiple_of((sel_first - 1) * TK, TK), TK)

    for h in range(N_HEADS):
        o_ref[:, h * HEAD_DIM:(h + 1) * HEAD_DIM] = (
            ow_ref[hrows(h), :] + gs_ref[hrows(h), :] * _softmax_finish(l_ref, acc_ref, hrows(h))
        ).astype(o_ref.dtype)


def _nsa(h_main, k_cmp, v_cmp, cmp_idx, tile_idx, table, gates, e_t, ov_t, weights, batch, seq):
    nq = seq // TQ
    w_blocks, w_in_specs, w_out_spec, w_out_shape = _weight_block_specs(
        weights, batch * nq, lambda b, i: b * nq + i)
    wblk = lambda name: _MAIN_OFF[name] // WIDTH
    kvblk = lambda name: _MAIN_OFF[name] // KV_WIDTH
    full = lambda shape: pl.BlockSpec(shape, lambda b, i: (0,) * len(shape),
                                      pipeline_mode=pl.Buffered(1))
    kv_spec = lambda name: pl.BlockSpec((seq, KV_WIDTH), lambda b, i: (b, kvblk(name)))
    n_chunks = k_cmp.shape[1]
    stat = pltpu.VMEM((N_HEADS * TQ, LANES), F32)
    return pl.pallas_call(
        functools.partial(_nsa_kernel, w_blocks),
        grid=(batch, nq),
        in_specs=[
            pl.BlockSpec((TQ, WIDTH), lambda b, i: (b * nq + i, wblk("nsa_q"))),
            kv_spec("nsa_k_sel"), kv_spec("nsa_v_sel"), kv_spec("nsa_k_win"), kv_spec("nsa_v_win"),
            pl.BlockSpec((TQ, WIDTH), lambda b, i: (b * nq + i, wblk("nsa_z"))),
            pl.BlockSpec((1, n_chunks, KV_WIDTH), lambda b, i: (b, 0, 0)),
            pl.BlockSpec((1, n_chunks, KV_WIDTH), lambda b, i: (b, 0, 0)),
            full(cmp_idx.shape), full(tile_idx.shape), full(table.shape),
            pl.BlockSpec((TQ, LANES), lambda b, i: (b * nq + i, 0)),
            full(e_t.shape), full(ov_t.shape),
            *w_in_specs,
        ],
        out_specs=[pl.BlockSpec((TQ, WIDTH), lambda b, i: (b * nq + i, 0)), w_out_spec],
        out_shape=[jax.ShapeDtypeStruct((batch * seq, WIDTH), BF16), w_out_shape],
        scratch_shapes=[pltpu.VMEM((KV_GROUPS, seq, MXU_DEPTH), BF16),
                        pltpu.VMEM((N_HEADS * TQ, HEAD_DIM), BF16),
                        pltpu.VMEM((N_HEADS * TQ, MXU_DEPTH), BF16),
                        stat, stat, stat,
                        stat, stat, stat,
                        pltpu.VMEM((N_HEADS, T_NONE + 1, TQ, TK), F32),
                        pltpu.VMEM((N_HEADS, seq, LANES), F32)],
        compiler_params=pltpu.CompilerParams(
            dimension_semantics=("arbitrary", "arbitrary"), vmem_limit_bytes=VMEM_LIMIT),
        name="nsa",
    )(h_main, h_main, h_main, h_main, h_main, h_main, k_cmp, v_cmp, cmp_idx, tile_idx, table, gates,
      e_t, ov_t, *weights)


OUT_TM = 512
OUT_TN = D_MODEL
OUT_ROW_GROUPS = 2


def _out_kernel(alpha, ua_ref, ub_ref, ga_ref, gb_ref, x_ref, wa_ref, wb_ref, wo_ref, lg_ref, lb_ref,
                o_ref, merged_ref):
    d_model = o_ref.shape[1]
    chunks = tuple(slice(c * OUT_TN, (c + 1) * OUT_TN) for c in range(d_model // OUT_TN))
    half = o_ref.shape[0] // OUT_ROW_GROUPS

    def merge(rows, _):
        for cs in chunks:
            ya = _dot(ua_ref[rows, :], wa_ref[:, cs])
            yb = _dot(ub_ref[rows, :], wb_ref[:, cs])
            merged_ref[rows, cs] = (_sigmoid(ga_ref[rows, cs].astype(F32)) * ya
                                    + _sigmoid(gb_ref[rows, cs].astype(F32)) * yb).astype(merged_ref.dtype)

    def residual(rows, _):
        total = jnp.zeros((half, LANES), F32)
        for cs in chunks:
            y = alpha * x_ref[rows, cs] + _dot(merged_ref[rows, :], wo_ref[:, cs])
            o_ref[rows, cs] = y
            total = total + _lane_fold(y)
        return total

    def layer_norm(rows, total):
        mu = jnp.sum(total, axis=1, keepdims=True) * (1.0 / d_model)
        sq = jnp.zeros((half, LANES), F32)
        for cs in chunks:
            d = o_ref[rows, cs] - mu
            sq = sq + _lane_fold(d * d)
        rstd = lax.rsqrt(jnp.sum(sq, axis=1, keepdims=True) * (1.0 / d_model) + LN_EPS)
        for cs in chunks:
            o_ref[rows, cs] = (o_ref[rows, cs] - mu) * rstd * lg_ref[:, cs] + lb_ref[:, cs]

    _skewed((merge, residual, layer_norm),
            tuple(slice(r * half, (r + 1) * half) for r in range(OUT_ROW_GROUPS)))


def _out(u_a, u_b, h_main, x2, w_cat, ln_g, ln_b, alpha, tm=OUT_TM):
    m = x2.shape[0]
    mblk = lambda name: _MAIN_OFF[name] // D_MODEL
    const = lambda shape: pl.BlockSpec(shape, lambda i: (0, 0), pipeline_mode=pl.Buffered(1))
    assert w_cat.shape == (2 * WIDTH + D_MODEL, D_MODEL) and D_MODEL == 2 * WIDTH
    w_part = lambda rows, index: pl.BlockSpec((rows, D_MODEL), lambda i: (index, 0),
                                              pipeline_mode=pl.Buffered(1))
    return pl.pallas_call(
        functools.partial(_out_kernel, alpha),
        grid=(m // tm,),
        in_specs=[
            pl.BlockSpec((tm, WIDTH), lambda i: (i, 0)),
            pl.BlockSpec((tm, WIDTH), lambda i: (i, 0)),
            pl.BlockSpec((tm, D_MODEL), lambda i: (i, mblk("merge_a"))),
            pl.BlockSpec((tm, D_MODEL), lambda i: (i, mblk("merge_b"))),
            pl.BlockSpec((tm, D_MODEL), lambda i: (i, 0)),
            w_part(WIDTH, 0), w_part(WIDTH, 1), w_part(D_MODEL, 1),
            const(ln_g.shape), const(ln_b.shape),
        ],
        out_specs=pl.BlockSpec((tm, D_MODEL), lambda i: (i, 0)),
        out_shape=jax.ShapeDtypeStruct((m, D_MODEL), F32),
        scratch_shapes=[pltpu.VMEM((tm, D_MODEL), BF16)],
        compiler_params=pltpu.CompilerParams(
            dimension_semantics=("parallel",), vmem_limit_bytes=VMEM_LIMIT),
        name="out",
    )(u_a, u_b, h_main, h_main, x2, w_cat, w_cat, w_cat, ln_g, ln_b)


def _bucket_np(dist):
    n = np.maximum(dist, 0)
    exact = REL_BUCKETS // 2
    large = exact + (np.log(np.maximum(n, 1).astype(np.float32) / exact)
                     / math.log(REL_MAX_DIST / exact) * (REL_BUCKETS - exact)).astype(np.int32)
    return np.where(n < exact, n, np.minimum(large, REL_BUCKETS - 1)).astype(np.int32)


@functools.lru_cache(maxsize=None)
def _static_tables(seq):
    r = np.arange(TQ)[:, None]
    c = np.arange(TK)[None, :]
    tile_idx = np.stack([_bucket_np(r - c), _bucket_np(TQ + r - c), _bucket_np(2 * TQ + r - c)])
    tile_ok = np.stack([c <= r, np.ones((TQ, TK), bool), (2 * TQ + r - c) < WINDOW])
    tile_idx = np.where(tile_ok, tile_idx, MASK_BUCKET).astype(np.int32)
    t = np.arange(seq)[:, None]
    cblk = np.arange(LANES)[None, :]
    blk_end = cblk * CMP_STRIDE + CMP_LEN - 1
    n_cmp = (seq - CMP_LEN) // CMP_STRIDE + 1
    cmp_idx = np.where((blk_end <= t) & (cblk < n_cmp), _bucket_np(t - blk_end), MASK_BUCKET).astype(np.int32)
    e_t = np.where((np.arange(seq)[:, None] // SEL_LEN) == np.arange(LANES)[None, :], NEG, 0.0)
    cs = (np.arange(LANES) * CMP_STRIDE)[None, :]
    ss = (np.arange(LANES) * SEL_LEN)[:, None]
    ov_t = ((cs < ss + SEL_LEN) & (cs + CMP_LEN > ss)
            & (np.arange(LANES)[None, :] < n_cmp) & (np.arange(LANES)[:, None] < seq // SEL_LEN))
    return tile_idx, cmp_idx, e_t.astype(np.float32), ov_t.astype(np.float32)


def _layer(x, w_in, b_f, cmp_pos_k, cmp_pos_v, cmp_wk1, cmp_wk2, cmp_wv1, cmp_wv2,
           w_a, w_b, w_o, ln_g, ln_b, rel_bias, alpha):
    batch, seq, d_model = x.shape
    assert d_model == D_MODEL and seq % FOX_TQ == 0 and seq >= WIN_BLOCKS * TK and REL_MAX_DIST <= TQ
    assert seq // SEL_LEN == N_SEL_ROWS and seq // CMP_STRIDE == LANES
    x2 = x.reshape(batch * seq, d_model)

    w_t = jnp.swapaxes(w_in, 0, 1)
    trows = lambda name: w_t[_REF_OFF[name][0]:_REF_OFF[name][0] + _REF_OFF[name][1]]
    n_small = N_HEADS + N_HEADS * N_BRANCHES
    w_small_t = jnp.concatenate(
        [trows("fox_f"), trows("nsa_gate"), jnp.zeros((LANES - n_small, d_model), F32)], axis=0)
    bf_row = jnp.concatenate([b_f.astype(F32), jnp.zeros((LANES - N_HEADS,), F32)]).reshape(1, LANES)

    h_main, h_small = _proj(x2, w_t, w_small_t)
    c_col, gates = _gates(h_small, bf_row, batch, seq)
    u_a = _fox(h_main, c_col, batch, seq)

    k_cmp, v_cmp = _compress(h_main, cmp_pos_k, cmp_pos_v, cmp_wk1, cmp_wk2, cmp_wv1, cmp_wv2, batch, seq)

    tile_idx, cmp_idx, e_t, ov_t = _static_tables(seq)
    bias = rel_bias.T.astype(F32)
    table = jnp.concatenate(
        [(bias - bias[:, REL_BUCKETS - 1:]) * LOG2E, jnp.full((N_HEADS, 1), NEG, F32),
         jnp.zeros((N_HEADS, LANES - REL_BUCKETS - 1), F32)], axis=1)
    u_b, w_cat = _nsa(h_main, k_cmp, v_cmp, jnp.asarray(cmp_idx), jnp.asarray(tile_idx), table, gates,
                      jnp.asarray(e_t, BF16), jnp.asarray(ov_t, BF16), (w_a, w_b, w_o), batch, seq)

    out = _out(u_a, u_b, h_main, x2, w_cat, ln_g.reshape(1, d_model), ln_b.reshape(1, d_model), alpha)
    return out.reshape(batch, seq, d_model)


def kernel(x, w_in, b_f, cmp_pos_k, cmp_pos_v, cmp_wk1, cmp_wk2, cmp_wv1, cmp_wv2,
           w_a, w_b, w_o, ln_g, ln_b, rel_bias):
    depth = w_in.shape[0]
    alpha = (2 * depth) ** 0.25
    for layer in range(depth):
        x = _layer(x, w_in[layer], b_f[layer], cmp_pos_k[layer], cmp_pos_v[layer], cmp_wk1[layer],
                   cmp_wk2[layer], cmp_wv1[layer], cmp_wv2[layer], w_a[layer], w_b[layer], w_o[layer],
                   ln_g[layer], ln_b[layer], rel_bias, alpha)
    return x
```
